```python
import functools
import jax, jax.numpy as jnp
from jax import lax
import numpy as np

D_MODEL = 1024
BATCH = 8
SEQ = 2048
DEPTH = 1
DEC_BATCH = 32
DEC_SEQ = 8
PAST_LEN = 8192
PAGE_SIZE = 128

ATT_HEADS = 8
ATT_KV_HEADS = 2
HEAD_DIM = 64
ROPE_DIM = HEAD_DIM // 4
ROPE_THETA = 500000.0
IDX_HEADS = 8
IDX_DIM = 64
IDX_ROPE_DIM = IDX_DIM // 4
TOPK_MAX = 256
Q_BLOCK = 128
RET_HEADS = 8
RET_DK = 64
RET_DV = 128
RET_THETA = 10000.0
RET_CHUNK = 128
N_EXPERTS = 32
TOP_K = 4
D_FF = D_MODEL
SWIGLU_LIMIT = 7.0
SWIGLU_ALPHA = 1.702
MOE_BLOCK = 128
NORM_EPS = 1e-6
GN_EPS = 1e-5
ATT_OUT = ATT_HEADS * HEAD_DIM
RET_OUT = RET_HEADS * RET_DV
IN_SPLITS = (ATT_HEADS * HEAD_DIM, ATT_KV_HEADS * HEAD_DIM, ATT_KV_HEADS * HEAD_DIM,
             IDX_HEADS * IDX_DIM, IDX_DIM, IDX_HEADS,
             RET_HEADS * RET_DK, RET_HEADS * RET_DK, RET_OUT, RET_OUT,
             D_MODEL, D_MODEL)
D_IN = sum(IN_SPLITS)

kernel_name = 'hybrid_dsa_retention_moe_step'


def _rmsnorm(x, g):
    xf = x.astype(jnp.float32)
    y = xf * lax.rsqrt(jnp.mean(xf * xf, axis=-1, keepdims=True) + NORM_EPS)
    return (y * g.astype(jnp.float32)).astype(x.dtype)


def _rope(x, pos, rot_dim, theta):
    half = rot_dim // 2
    inv = theta ** (-jnp.arange(half, dtype=jnp.float32) * (2.0 / rot_dim))
    ang = pos.astype(jnp.float32)[:, None] * inv[None, :]
    cos = jnp.cos(ang)[:, None, :]
    sin = jnp.sin(ang)[:, None, :]
    xr = x[..., :rot_dim].astype(jnp.float32)
    x1, x2 = xr[..., :half], xr[..., half:]
    rot = jnp.concatenate([x1 * cos - x2 * sin, x2 * cos + x1 * sin], axis=-1).astype(x.dtype)
    return jnp.concatenate([rot, x[..., rot_dim:]], axis=-1)


def _index_scores(iq, ik, iw):
    dots = jnp.einsum('bthi,bsi->bths', iq, ik, preferred_element_type=jnp.float32) * (IDX_DIM ** -0.5)
    w = iw.astype(jnp.float32) * (IDX_HEADS ** -0.5)
    return jnp.einsum('bths,bth->bts', jax.nn.relu(dots), w)


def _sparse_attend(q, k_sel, v_sel, valid):
    b, t, h, d = q.shape
    qg = q.reshape(b, t, ATT_KV_HEADS, h // ATT_KV_HEADS, d)
    s = jnp.einsum('btjgd,btnjd->btjgn', qg, k_sel, preferred_element_type=jnp.float32) * (HEAD_DIM ** -0.5)
    s = jnp.where(valid[:, :, None, None, :], s, -jnp.inf)
    p = jax.nn.softmax(s, axis=-1).astype(v_sel.dtype)
    o = jnp.einsum('btjgn,btnjd->btjgd', p, v_sel)
    return o.reshape(b, t, h * d)


def _dsa_prompt(q, k, v, iq, ik, iw):
    b, s = q.shape[:2]
    topk = min(TOPK_MAX, s // 4)
    nb = s // Q_BLOCK
    key_pos = jnp.arange(s)
    gather = jax.vmap(lambda a, i: a[i])

    def block(args):
        qb, iqb, iwb, qpos = args
        scores = _index_scores(iqb, ik, iwb)
        causal = key_pos[None, None, :] <= qpos[None, :, None]
        scores = jnp.where(causal, scores, -jnp.inf)
        _, idx = lax.top_k(scores, topk)
        valid = idx <= qpos[None, :, None]
        return _sparse_attend(qb, gather(k, idx), gather(v, idx), valid)

    to_blocks = lambda a: a.reshape(b, nb, Q_BLOCK, *a.shape[2:]).swapaxes(0, 1)
    out = lax.map(block, (to_blocks(q), to_blocks(iq), to_blocks(iw), jnp.arange(s).reshape(nb, Q_BLOCK)))
    return out.swapaxes(0, 1).reshape(b, s, ATT_OUT)


def _dsa_sample(q, k, v, iq, ik, iw, cache_k, cache_v, cache_ik, page_table):
    db, t = q.shape[:2]
    n_pages = PAST_LEN // PAGE_SIZE
    n_keys = PAST_LEN + t
    topk = min(TOPK_MAX, n_keys // 4)
    ik_past = cache_ik[page_table].reshape(db, n_pages * PAGE_SIZE, IDX_DIM)
    ik_all = jnp.concatenate([ik_past, ik.astype(ik_past.dtype)], axis=1)
    scores = _index_scores(iq, ik_all, iw)
    qpos = PAST_LEN + jnp.arange(t)
    causal = jnp.arange(n_keys)[None, None, :] <= qpos[None, :, None]
    scores = jnp.where(causal, scores, -jnp.inf)
    _, idx = lax.top_k(scores, topk)
    valid = idx <= qpos[None, :, None]
    in_past = idx < PAST_LEN
    pidx = jnp.minimum(idx, PAST_LEN - 1)
    phys_page = jnp.take_along_axis(page_table, (pidx // PAGE_SIZE).reshape(db, -1), axis=1).reshape(idx.shape)
    phys_row = phys_page * PAGE_SIZE + pidx % PAGE_SIZE
    nidx = jnp.clip(idx - PAST_LEN, 0, t - 1)
    gather = jax.vmap(lambda a, i: a[i])
    flat_k = cache_k.reshape(-1, ATT_KV_HEADS, HEAD_DIM)
    flat_v = cache_v.reshape(-1, ATT_KV_HEADS, HEAD_DIM)
    sel = in_past[..., None, None]
    k_sel = jnp.where(sel, flat_k[phys_row], gather(k, nidx).astype(flat_k.dtype))
    v_sel = jnp.where(sel, flat_v[phys_row], gather(v, nidx).astype(flat_v.dtype))
    return _sparse_attend(q, k_sel.astype(q.dtype), v_sel.astype(q.dtype), valid)


def _ret_log_gamma():
    return jnp.log(1.0 - 2.0 ** (-5.0 - jnp.arange(RET_HEADS, dtype=jnp.float32)))


def _retention_chunk(state, q, k, v, log_gamma):
    c = q.shape[1]
    i = jnp.arange(c, dtype=jnp.float32)
    diff = i[:, None] - i[None, :]
    decay = jnp.where(diff >= 0, jnp.exp(jnp.maximum(diff, 0.0)[None] * log_gamma[:, None, None]), 0.0)
    qf = q.astype(jnp.float32)
    kf = k.astype(jnp.float32) * (RET_DK ** -0.5)
    vf = v.astype(jnp.float32)
    scores = jnp.einsum('bihd,bjhd->bhij', qf, kf) * decay[None]
    o = jnp.einsum('bhij,bjhv->bihv', scores, vf)
    q_decay = jnp.exp((i + 1.0)[:, None] * log_gamma[None, :])
    o = o + jnp.einsum('bihd,bhdv->bihv', qf * q_decay[None, :, :, None], state)
    k_decay = jnp.exp((c - 1.0 - i)[:, None] * log_gamma[None, :])
    new_state = (state * jnp.exp(c * log_gamma)[None, :, None, None]
                 + jnp.einsum('bjhd,bjhv->bhdv', kf * k_decay[None, :, :, None], vf))
    return new_state, o


def _retention_prompt(q, k, v):
    b, s = q.shape[:2]
    nc = s // RET_CHUNK
    lg = _ret_log_gamma()
    to_chunks = lambda a: a.reshape(b, nc, RET_CHUNK, *a.shape[2:]).swapaxes(0, 1)
    s0 = jnp.zeros((b, RET_HEADS, RET_DK, RET_DV), jnp.float32)
    s_final, o = lax.scan(lambda st, xs: _retention_chunk(st, xs[0], xs[1], xs[2], lg), s0,
                          (to_chunks(q), to_chunks(k), to_chunks(v)))
    return o.swapaxes(0, 1).reshape(b, s, RET_HEADS, RET_DV), s_final


def _retention_sample(q, k, v, state):
    new_state, o = _retention_chunk(state.astype(jnp.float32), q, k, v, _ret_log_gamma())
    return o, new_state


def _head_groupnorm(o):
    mu = jnp.mean(o, axis=-1, keepdims=True)
    var = jnp.mean(jnp.square(o - mu), axis=-1, keepdims=True)
    return (o - mu) * lax.rsqrt(var + GN_EPS)


def _moe(h, w_router, b_router, w_gate_up, b_gate_up, w_down, b_down):
    lead = h.shape[:-1]
    x = h.reshape(-1, D_MODEL)
    n_tok = x.shape[0]
    logits = jnp.einsum('td,de->te', x, w_router, preferred_element_type=jnp.float32) + b_router.astype(jnp.float32)
    top_val, top_idx = lax.top_k(logits, TOP_K)
    gates = jax.nn.softmax(top_val, axis=-1)
    n_asg = n_tok * TOP_K
    e_f = top_idx.reshape(-1)
    tok_f = jnp.arange(n_asg, dtype=jnp.int32) // TOP_K
    w_f = gates.reshape(-1)
    order = jnp.argsort(e_f)
    e_s, tok_s, w_s = e_f[order], tok_f[order], w_f[order]
    counts = jnp.bincount(e_f, length=N_EXPERTS)
    padded = (counts + MOE_BLOCK - 1) // MOE_BLOCK * MOE_BLOCK
    start = jnp.cumsum(counts) - counts
    pend = jnp.cumsum(padded)
    pstart = pend - padded
    dest = pstart[e_s] + jnp.arange(n_asg) - start[e_s]
    n_blocks = -(-n_asg // MOE_BLOCK) + N_EXPERTS
    n_rows = n_blocks * MOE_BLOCK
    row_tok = jnp.zeros((n_rows,), jnp.int32).at[dest].set(tok_s)
    row_w = jnp.zeros((n_rows,), jnp.float32).at[dest].set(w_s)
    blk_expert = jnp.minimum(jnp.searchsorted(pend, jnp.arange(n_blocks) * MOE_BLOCK, side='right'), N_EXPERTS - 1)

    def expert_block(args):
        toks, e = args
        gu = x[toks] @ w_gate_up[e] + b_gate_up[e]
        g = jnp.minimum(gu[:, :D_FF], SWIGLU_LIMIT)
        u = jnp.clip(gu[:, D_FF:], -SWIGLU_LIMIT, SWIGLU_LIMIT)
        act = (u + 1.0) * (g * jax.nn.sigmoid(SWIGLU_ALPHA * g))
        return act @ w_down[e] + b_down[e]

    out = lax.map(expert_block, (row_tok.reshape(n_blocks, MOE_BLOCK), blk_expert))
    out = out.reshape(n_rows, D_MODEL).astype(jnp.float32) * row_w[:, None]
    y = jax.ops.segment_sum(out, row_tok, num_segments=n_tok)
    return y.astype(h.dtype).reshape(*lead, D_MODEL)


def _layer(x, c, pos, attend, retain, g_mix, g_ffn, w_ada, b_ada, w_in, w_pa, w_pb, w_o,
           w_router, b_router, w_gate_up, b_gate_up, w_down, b_down):
    b, s, _ = x.shape
    mod = (jnp.einsum('bd,de->be', jax.nn.silu(c), w_ada) + b_ada)[:, None, :]
    sh1, sc1, gt1, sh2, sc2, gt2 = jnp.split(mod, 6, axis=-1)
    h = _rmsnorm(x, g_mix) * (1.0 + sc1) + sh1
    z = jnp.einsum('bsd,de->bse', h, w_in)
    q, k, v, iq, ik, iw, rq, rk, rv, rg, ga, gb = jnp.split(z, np.cumsum(IN_SPLITS)[:-1].tolist(), axis=-1)
    q = _rope(q.reshape(b, s, ATT_HEADS, HEAD_DIM), pos, ROPE_DIM, ROPE_THETA)
    k = _rope(k.reshape(b, s, ATT_KV_HEADS, HEAD_DIM), pos, ROPE_DIM, ROPE_THETA)
    v = v.reshape(b, s, ATT_KV_HEADS, HEAD_DIM)
    iq = _rope(iq.reshape(b, s, IDX_HEADS, IDX_DIM), pos, IDX_ROPE_DIM, ROPE_THETA)
    ik = _rope(ik.reshape(b, s, 1, IDX_DIM), pos, IDX_ROPE_DIM, ROPE_THETA)[:, :, 0]
    rq = _rope(rq.reshape(b, s, RET_HEADS, RET_DK), pos, RET_DK, RET_THETA)
    rk = _rope(rk.reshape(b, s, RET_HEADS, RET_DK), pos, RET_DK, RET_THETA)
    rv = rv.reshape(b, s, RET_HEADS, RET_DV)
    a_out = attend(q, k, v, iq, ik, iw)
    r_out, r_state = retain(rq, rk, rv)
    r_out = _head_groupnorm(r_out).reshape(b, s, RET_OUT).astype(x.dtype) * jax.nn.silu(rg)
    merged = jax.nn.sigmoid(ga) * (a_out @ w_pa) + jax.nn.sigmoid(gb) * (r_out @ w_pb)
    x = x + gt1 * (merged @ w_o)
    h2 = _rmsnorm(x, g_ffn) * (1.0 + sc2) + sh2
    x = x + gt2 * _moe(h2, w_router, b_router, w_gate_up, b_gate_up, w_down, b_down)
    return x, (k, v, ik, r_state)


def setup_inputs(seed: int = 0) -> dict:
    key = jax.random.key(seed)
    ks = jax.random.split(key, 26)
    f32 = jnp.float32
    nrm = lambda kk, shape, scale: scale * jax.random.normal(kk, shape, f32)
    n_pages = PAST_LEN // PAGE_SIZE
    n_used = DEC_BATCH * n_pages
    n_pool = n_used + max(1, n_used // 4)
    page_table = jax.random.permutation(ks[7], n_pool)[:n_used].reshape(DEC_BATCH, n_pages).astype(jnp.int32)
    return {
        'x_prompt': nrm(ks[0], (BATCH, SEQ, D_MODEL), 1.0),
        'x_sample': nrm(ks[1], (DEC_BATCH, DEC_SEQ, D_MODEL), 1.0),
        'cache_k': nrm(ks[2], (DEPTH, n_pool, PAGE_SIZE, ATT_KV_HEADS, HEAD_DIM), 1.0),
        'cache_v': nrm(ks[3], (DEPTH, n_pool, PAGE_SIZE, ATT_KV_HEADS, HEAD_DIM), 1.0),
        'cache_ik': nrm(ks[4], (DEPTH, n_pool, PAGE_SIZE, IDX_DIM), 1.0),
        'state_ret': nrm(ks[5], (DEPTH, DEC_BATCH, RET_HEADS, RET_DK, RET_DV), 0.3),
        'page_table': page_table,
        'c_prompt': nrm(ks[6], (BATCH, D_MODEL), 1.0),
        'c_sample': nrm(ks[8], (DEC_BATCH, D_MODEL), 1.0),
        'norm_mix_g': 1.0 + nrm(ks[9], (DEPTH, D_MODEL), 0.01),
        'norm_ffn_g': 1.0 + nrm(ks[10], (DEPTH, D_MODEL), 0.01),
        'norm_final_g': 1.0 + nrm(ks[11], (D_MODEL,), 0.01),
        'w_ada': nrm(ks[12], (DEPTH, D_MODEL, 6 * D_MODEL), 0.5 * D_MODEL ** -0.5),
        'b_ada': nrm(ks[13], (DEPTH, 6 * D_MODEL), 0.01),
        'w_in': nrm(ks[14], (DEPTH, D_MODEL, D_IN), D_MODEL ** -0.5),
        'w_branch_a': nrm(ks[15], (DEPTH, ATT_OUT, D_MODEL), ATT_OUT ** -0.5),
        'w_branch_b': nrm(ks[16], (DEPTH, RET_OUT, D_MODEL), RET_OUT ** -0.5),
        'w_out': nrm(ks[17], (DEPTH, D_MODEL, D_MODEL), D_MODEL ** -0.5),
        'w_router': nrm(ks[18], (DEPTH, D_MODEL, N_EXPERTS), D_MODEL ** -0.5),
        'b_router': nrm(ks[19], (DEPTH, N_EXPERTS), 0.01),
        'w_gate_up': nrm(ks[20], (DEPTH, N_EXPERTS, D_MODEL, 2 * D_FF), D_MODEL ** -0.5),
        'b_gate_up': nrm(ks[21], (DEPTH, N_EXPERTS, 2 * D_FF), 0.01),
        'w_down': nrm(ks[22], (DEPTH, N_EXPERTS, D_FF, D_MODEL), D_FF ** -0.5),
        'b_down': nrm(ks[23], (DEPTH, N_EXPERTS, D_MODEL), 0.01),
    }


def reference(x_prompt, x_sample, cache_k, cache_v, cache_ik, state_ret, page_table, c_prompt, c_sample,
              norm_mix_g, norm_ffn_g, norm_final_g, w_ada, b_ada, w_in, w_branch_a, w_branch_b, w_out,
              w_router, b_router, w_gate_up, b_gate_up, w_down, b_down):
    pos_p = jnp.arange(x_prompt.shape[1])
    pos_s = PAST_LEN + jnp.arange(x_sample.shape[1])
    hp, hs = x_prompt, x_sample
    k_p, v_p, ik_p, r_p = [], [], [], []
    k_s, v_s, ik_s, r_s = [], [], [], []
    for l in range(DEPTH):
        lw = (norm_mix_g[l], norm_ffn_g[l], w_ada[l], b_ada[l], w_in[l], w_branch_a[l], w_branch_b[l], w_out[l],
              w_router[l], b_router[l], w_gate_up[l], b_gate_up[l], w_down[l], b_down[l])
        hp, (kk, vv, ii, rr) = _layer(hp, c_prompt, pos_p, _dsa_prompt, _retention_prompt, *lw)
        k_p.append(kk); v_p.append(vv); ik_p.append(ii); r_p.append(rr)
        attend_s = functools.partial(_dsa_sample, cache_k=cache_k[l], cache_v=cache_v[l],
                                     cache_ik=cache_ik[l], page_table=page_table)
        retain_s = functools.partial(_retention_sample, state=state_ret[l])
        hs, (kk, vv, ii, rr) = _layer(hs, c_sample, pos_s, attend_s, retain_s, *lw)
        k_s.append(kk); v_s.append(vv); ik_s.append(ii); r_s.append(rr)
    y_prompt = _rmsnorm(hp, norm_final_g)
    y_sample = _rmsnorm(hs, norm_final_g)
    return (y_prompt, y_sample, jnp.stack(k_p), jnp.stack(v_p), jnp.stack(ik_p), jnp.stack(r_p),
            jnp.stack(k_s), jnp.stack(v_s), jnp.stack(ik_s), jnp.stack(r_s))
```

```python
import functools

import jax
import jax.numpy as jnp
import numpy as np
from jax import lax
from jax.experimental import pallas as pl
from jax.experimental.pallas import tpu as pltpu

F32 = jnp.float32
BF16 = jnp.bfloat16
I32 = jnp.int32

D_MODEL = 1024
PAST_LEN = 8192
PAGE_SIZE = 128
ATT_HEADS = 8
ATT_KV_HEADS = 2
HEAD_DIM = 64
ROPE_DIM = HEAD_DIM // 4
ROPE_THETA = 500000.0
IDX_HEADS = 8
IDX_DIM = 64
IDX_ROPE_DIM = IDX_DIM // 4
TOPK_MAX = 256
RET_HEADS = 8
RET_DK = 64
RET_DV = 128
RET_THETA = 10000.0
RET_CHUNK = 128
N_EXPERTS = 32
TOP_K = 4
D_FF = D_MODEL
SWIGLU_LIMIT = 7.0
SWIGLU_ALPHA = 1.702
NORM_EPS = 1e-6
GN_EPS = 1e-5
ATT_OUT = ATT_HEADS * HEAD_DIM
RET_OUT = RET_HEADS * RET_DV
IN_SPLITS = (ATT_HEADS * HEAD_DIM, ATT_KV_HEADS * HEAD_DIM, ATT_KV_HEADS * HEAD_DIM,
             IDX_HEADS * IDX_DIM, IDX_DIM, IDX_HEADS,
             RET_HEADS * RET_DK, RET_HEADS * RET_DK, RET_OUT, RET_OUT, D_MODEL, D_MODEL)

LANES = 128
MASK_NEG = -1e30
INT_MIN = -2 ** 31
VMEM_LIMIT = 56 * 1024 * 1024

TOK_BLOCK = 256
Q_BLOCK = 128
MOE_ROWS = 256

_W_GROUPS = (("q", 1024), ("k", 128), ("v", 128), ("iq", 1024), ("ikw", 128), ("rq", 512), ("rk", 512),
             ("rv", 1024), ("rg", 1024), ("ga", 1024), ("gb", 1024))
_W_OFF = {}
_off = 0
for _n, _w in _W_GROUPS:
    _W_OFF[_n] = (_off, _w)
    _off += _w
W_COLS = _off


def _cparams(*sem):
    return pltpu.CompilerParams(dimension_semantics=sem, vmem_limit_bytes=VMEM_LIMIT)


def _adaln_kernel(c_ref, w_ref, b_ref, o_ref):
    c = c_ref[...]
    s = c * jax.nn.sigmoid(c)
    o_ref[...] = jnp.dot(s, w_ref[...], preferred_element_type=F32, precision=lax.Precision.HIGHEST) + b_ref[...]


def _adaln(c_all, w_ada, b_ada):
    n = c_all.shape[0]
    nb = 1536
    return pl.pallas_call(
        _adaln_kernel,
        grid=(6 * D_MODEL // nb,),
        in_specs=[pl.BlockSpec((n, D_MODEL), lambda j: (0, 0)),
                  pl.BlockSpec((D_MODEL, nb), lambda j: (0, j)),
                  pl.BlockSpec((1, nb), lambda j: (0, j))],
        out_specs=pl.BlockSpec((n, nb), lambda j: (0, j)),
        out_shape=jax.ShapeDtypeStruct((n, 6 * D_MODEL), F32),
        compiler_params=_cparams("arbitrary"),
        name="adaln",
    )(c_all, w_ada, b_ada.reshape(1, -1))


def _rope_slab(z, c, sa, sb, half):
    return z * c + pltpu.roll(z, LANES - half, 1) * sa + pltpu.roll(z, half, 1) * sb


def _inproj_kernel(x_ref, mod_ref, g_ref, w_ref, ca_ref, saa_ref, sba_ref, cr_ref, sar_ref, sbr_ref,
                   q_ref, k_ref, v_ref, kb_ref, vb_ref, iq_ref, ikw_ref, ikb_ref,
                   rq_ref, rk_ref, rv_ref, sg_ref, sga_ref, sgb_ref):
    x = x_ref[...]
    ms = jnp.mean(x * x, axis=-1, keepdims=True)
    y = x * lax.rsqrt(ms + NORM_EPS) * g_ref[...]
    h = (y * (1.0 + mod_ref[:, D_MODEL:2 * D_MODEL]) + mod_ref[:, 0:D_MODEL]).astype(BF16)

    def proj(name, s):
        c0 = _W_OFF[name][0] + s * LANES
        return jnp.dot(h, w_ref[:, c0:c0 + LANES], preferred_element_type=F32)

    ca, saa, sba = ca_ref[...], saa_ref[...], sba_ref[...]
    cr, sar, sbr = cr_ref[...], sar_ref[...], sbr_ref[...]
    att_half, ret_half = ROPE_DIM // 2, RET_DK // 2
    lane = lax.broadcasted_iota(I32, (x.shape[0], LANES), 1)

    for s in range(8):
        sl = slice(s * LANES, (s + 1) * LANES)
        q_ref[:, sl] = (_rope_slab(proj("q", s), ca, saa, sba, att_half) * 0.125).astype(BF16)
        iq_ref[:, sl] = (_rope_slab(proj("iq", s), ca, saa, sba, att_half) * 0.125).astype(BF16)
    kk = _rope_slab(proj("k", 0), ca, saa, sba, att_half)
    k_ref[...] = kk
    kb_ref[...] = kk.astype(BF16)
    vv = proj("v", 0)
    v_ref[...] = vv
    vb_ref[...] = vv.astype(BF16)
    z = proj("ikw", 0)
    zr = _rope_slab(z, ca, saa, sba, att_half)
    ikw_ref[...] = jnp.where(lane < IDX_DIM, zr, z)
    ikb_ref[...] = jnp.where(lane < IDX_DIM, zr, 0.0).astype(BF16)
    for s in range(4):
        sl = slice(s * LANES, (s + 1) * LANES)
        rq_ref[:, sl] = _rope_slab(proj("rq", s), cr, sar, sbr, ret_half).astype(BF16)
        rk_ref[:, sl] = (_rope_slab(proj("rk", s), cr, sar, sbr, ret_half) * 0.125).astype(BF16)
    for s in range(8):
        sl = slice(s * LANES, (s + 1) * LANES)
        rv_ref[:, sl] = proj("rv", s).astype(BF16)
        g = proj("rg", s)
        sg_ref[:, sl] = (g * jax.nn.sigmoid(g)).astype(BF16)
        sga_ref[:, sl] = jax.nn.sigmoid(proj("ga", s)).astype(BF16)
        sgb_ref[:, sl] = jax.nn.sigmoid(proj("gb", s)).astype(BF16)


def _inproj(x, mod3, g, w_packed, tabs_att, tabs_ret, blocks_per_mod, tab_blocks):
    t = x.shape[0]
    tm = TOK_BLOCK
    nblk = t // tm
    mod_rows = mod3.shape[1]
    tab_spec = pl.BlockSpec((tm, LANES), lambda i: (i % tab_blocks, 0))
    row = lambda n: pl.BlockSpec((tm, n), lambda i: (i, 0))
    out_defs = [(1024, BF16), (128, F32), (128, F32), (128, BF16), (128, BF16), (1024, BF16), (128, F32),
                (128, BF16), (512, BF16), (512, BF16), (1024, BF16), (1024, BF16), (1024, BF16), (1024, BF16)]
    return pl.pallas_call(
        _inproj_kernel,
        grid=(nblk,),
        in_specs=[row(D_MODEL),
                  pl.BlockSpec((None, mod_rows, 6 * D_MODEL), lambda i: (i // blocks_per_mod, 0, 0)),
                  pl.BlockSpec((1, D_MODEL), lambda i: (0, 0)),
                  pl.BlockSpec((D_MODEL, W_COLS), lambda i: (0, 0))] + [tab_spec] * 6,
        out_specs=[row(n) for n, _ in out_defs],
        out_shape=[jax.ShapeDtypeStruct((t, n), d) for n, d in out_defs],
        compiler_params=_cparams("parallel"),
        name="inproj",
    )(x, mod3, g, w_packed, *tabs_att, *tabs_ret)


def _sortable_key(score):
    bits = pltpu.bitcast(score, I32)
    return jnp.where(bits < 0, bits ^ jnp.int32(0x7FFFFFFF), bits)


def _kth_largest_key(key_ref, k):
    rows = key_ref.shape[0]

    def count_ge(c):
        return jnp.sum(jnp.where(key_ref[...] >= c, 1.0, 0.0), axis=1, keepdims=True)

    n0 = count_ge(jnp.zeros((rows, 1), I32))
    prefix = jnp.where(n0 >= k, jnp.int32(0), jnp.int32(INT_MIN))

    def body(b, prefix):
        cand = prefix | lax.shift_left(jnp.int32(1), jnp.int32(30) - b)
        return jnp.where(count_ge(cand) >= k, cand, prefix)

    return lax.fori_loop(0, 31, body, prefix)


def _topk_bias(key_ref, bias_ref, tri_ref, causal_fn, k):
    rows, n = key_ref.shape
    thr = _kth_largest_key(key_ref, k)
    n_gt = jnp.sum(jnp.where(key_ref[...] > thr, 1.0, 0.0), axis=1, keepdims=True)
    need = k - n_gt

    def chunk(c, run):
        sl = pl.ds(pl.multiple_of(c * LANES, LANES), LANES)
        kc = key_ref[:, sl]
        eq = kc == thr
        eqf = jnp.where(eq, 1.0, 0.0)
        before = jnp.dot(eqf.astype(BF16), tri_ref[...], preferred_element_type=F32) + run
        take = jnp.logical_or(kc > thr, jnp.logical_and(eq, before < need))
        bias_ref[:, sl] = jnp.where(jnp.logical_and(take, causal_fn(c)), 0.0, MASK_NEG)
        return run + jnp.sum(eqf, axis=1, keepdims=True)

    lax.fori_loop(0, n // LANES, chunk, jnp.zeros((rows, 1), F32))


def _dsa_prompt_kernel(q_ref, iq_ref, ikw_ref, kb_ref, vb_ref, ikb_ref, tri_ref, o_ref, key_ref, bias_ref):
    qb, s_len = key_ref.shape
    i = pl.program_id(1)
    w = ikw_ref[:, IDX_DIM:IDX_DIM + IDX_HEADS] * (IDX_HEADS ** -0.5)
    nt = (((1,), (1,)), ((), ()))
    ikb = ikb_ref[...]
    acc = jnp.zeros((qb, s_len), F32)
    for h in range(IDX_HEADS):
        d = lax.dot_general(iq_ref[:, h * LANES:(h + 1) * LANES], ikb, nt, preferred_element_type=F32)
        acc = acc + jnp.maximum(d, 0.0) * w[:, h:h + 1]
    qpos = i * qb + lax.broadcasted_iota(I32, (qb, s_len), 0)
    kpos = lax.broadcasted_iota(I32, (qb, s_len), 1)
    key_ref[...] = _sortable_key(jnp.where(kpos <= qpos, acc, -jnp.inf))

    def causal_fn(c):
        qp = i * qb + lax.broadcasted_iota(I32, (qb, LANES), 0)
        kp = c * LANES + lax.broadcasted_iota(I32, (qb, LANES), 1)
        return kp <= qp

    _topk_bias(key_ref, bias_ref, tri_ref, causal_fn, min(TOPK_MAX, s_len // 4))

    kb = kb_ref[...]
    vb = vb_ref[...]
    lane = lax.broadcasted_iota(I32, (qb, LANES), 1)
    heads = []
    for h in range(ATT_HEADS):
        s = lax.dot_general(q_ref[:, h * LANES:(h + 1) * LANES], kb, nt, preferred_element_type=F32) + bias_ref[...]
        m = jnp.max(s, axis=1, keepdims=True)
        p = jnp.exp(s - m)
        l = jnp.sum(p, axis=1, keepdims=True)
        heads.append(jnp.dot(p.astype(BF16), vb, preferred_element_type=F32) / l)
    group = ATT_HEADS // ATT_KV_HEADS
    for pp in range(ATT_HEADS // 2):
        a, b = heads[2 * pp], heads[2 * pp + 1]
        if (2 * pp) // group == 0:
            slab = jnp.where(lane < HEAD_DIM, a, pltpu.roll(b, HEAD_DIM, 1))
        else:
            slab = jnp.where(lane < HEAD_DIM, pltpu.roll(a, HEAD_DIM, 1), b)
        o_ref[:, pp * LANES:(pp + 1) * LANES] = slab.astype(BF16)


def _dsa_prompt(q, iq, ikw, kb, vb, ikb, tri, batch, seq):
    nqb = seq // Q_BLOCK
    qrow = lambda n: pl.BlockSpec((Q_BLOCK, n), lambda b, i: (b * nqb + i, 0))
    keys = pl.BlockSpec((seq, LANES), lambda b, i: (b, 0))
    return pl.pallas_call(
        _dsa_prompt_kernel,
        grid=(batch, nqb),
        in_specs=[qrow(1024), qrow(1024), qrow(LANES), keys, keys, keys,
                  pl.BlockSpec((LANES, LANES), lambda b, i: (0, 0))],
        out_specs=qrow(ATT_OUT),
        out_shape=jax.ShapeDtypeStruct((batch * seq, ATT_OUT), BF16),
        scratch_shapes=[pltpu.VMEM((Q_BLOCK, seq), I32), pltpu.VMEM((Q_BLOCK, seq), F32)],
        compiler_params=_cparams("parallel", "arbitrary"),
        name="dsa_prompt",
    )(q, iq, ikw, kb, vb, ikb, tri)


def _dsa_sample_kernel(pt_ref, qs_ref, iqs_ref, ws_ref, knew_ref, vnew_ref, iknew_ref, cik_hbm, ck_hbm, cv_hbm,
                       tri_ref, o_ref, ikbuf, kbuf, vbuf, sems, key_ref, bias_ref):
    db = pl.program_id(0)
    n_pages = ikbuf.shape[0]
    t = key_ref.shape[0]
    n_past = n_pages * PAGE_SIZE

    def copies(p):
        pg = pt_ref[db, p]
        return (pltpu.make_async_copy(cik_hbm.at[pg], ikbuf.at[p], sems.at[0]),
                pltpu.make_async_copy(ck_hbm.at[pg], kbuf.at[p], sems.at[1]),
                pltpu.make_async_copy(cv_hbm.at[pg], vbuf.at[p], sems.at[2]))

    def start(p, carry):
        for cp in copies(p):
            cp.start()
        return carry

    def wait(p, carry):
        for cp in copies(p):
            cp.wait()
        return carry

    lax.fori_loop(0, n_pages, start, 0)
    lax.fori_loop(0, n_pages, wait, 0)

    nt = (((1,), (1,)), ((), ()))
    ik_past = ikbuf[...].reshape(n_past, IDX_DIM).astype(BF16)
    iqs = iqs_ref[...]
    wcol = ws_ref[...]
    d_past = jnp.maximum(lax.dot_general(iqs, ik_past, nt, preferred_element_type=F32), 0.0) * wcol
    d_new = jnp.maximum(lax.dot_general(iqs, iknew_ref[...], nt, preferred_element_type=F32), 0.0) * wcol
    s_past = d_past[0:t]
    s_new = d_new[0:t]
    for h in range(1, IDX_HEADS):
        s_past = s_past + d_past[h * t:(h + 1) * t]
        s_new = s_new + d_new[h * t:(h + 1) * t]
    row = lax.broadcasted_iota(I32, (t, LANES), 0)
    lane = lax.broadcasted_iota(I32, (t, LANES), 1)
    new_ok = lane <= row
    key_ref[:, 0:n_past] = _sortable_key(s_past)
    key_ref[:, n_past:n_past + LANES] = _sortable_key(jnp.where(new_ok, s_new, -jnp.inf))

    def causal_fn(c):
        return jnp.logical_or(c < n_pages, new_ok)

    _topk_bias(key_ref, bias_ref, tri_ref, causal_fn, min(TOPK_MAX, (n_past + t) // 4))

    qs = qs_ref[...]
    reps = qs.shape[0] // t
    bias = jnp.concatenate([bias_ref[...]] * reps, axis=0)
    k_past = kbuf[...].reshape(n_past, LANES).astype(BF16)
    v_past = vbuf[...].reshape(n_past, LANES).astype(BF16)
    sp = lax.dot_general(qs, k_past, nt, preferred_element_type=F32) + bias[:, 0:n_past]
    sn = lax.dot_general(qs, knew_ref[...], nt, preferred_element_type=F32) + bias[:, n_past:n_past + LANES]
    m = jnp.maximum(jnp.max(sp, axis=1, keepdims=True), jnp.max(sn, axis=1, keepdims=True))
    pp = jnp.exp(sp - m)
    pn = jnp.exp(sn - m)
    l = jnp.sum(pp, axis=1, keepdims=True) + jnp.sum(pn, axis=1, keepdims=True)
    o = (jnp.dot(pp.astype(BF16), v_past, preferred_element_type=F32)
         + jnp.dot(pn.astype(BF16), vnew_ref[...], preferred_element_type=F32))
    o_ref[...] = o / l


def _dsa_sample(page_table, qs, iqs, ws, knew, vnew, iknew, cache_ik, cache_k, cache_v, tri, t):
    db, n_pages = page_table.shape
    rows = qs.shape[1]
    per_db = lambda r, n: pl.BlockSpec((None, r, n), lambda b, pt: (b, 0, 0))
    any_spec = pl.BlockSpec(memory_space=pl.ANY)
    n_keys = n_pages * PAGE_SIZE + LANES
    grid_spec = pltpu.PrefetchScalarGridSpec(
        num_scalar_prefetch=1,
        grid=(db,),
        in_specs=[per_db(rows, LANES), per_db(rows, IDX_DIM), per_db(rows, 1), per_db(LANES, LANES),
                  per_db(LANES, LANES), per_db(LANES, IDX_DIM), any_spec, any_spec, any_spec,
                  pl.BlockSpec((LANES, LANES), lambda b, pt: (0, 0))],
        out_specs=per_db(rows, LANES),
        scratch_shapes=[pltpu.VMEM((n_pages, PAGE_SIZE, IDX_DIM), F32),
                        pltpu.VMEM((n_pages, PAGE_SIZE, LANES), F32),
                        pltpu.VMEM((n_pages, PAGE_SIZE, LANES), F32),
                        pltpu.SemaphoreType.DMA((3,)),
                        pltpu.VMEM((t, n_keys), I32),
                        pltpu.VMEM((t, n_keys), F32)])
    return pl.pallas_call(
        _dsa_sample_kernel,
        grid_spec=grid_spec,
        out_shape=jax.ShapeDtypeStruct((db, rows, LANES), F32),
        compiler_params=_cparams("arbitrary"),
        name="dsa_sample",
    )(page_table, qs, iqs, ws, knew, vnew, iknew, cache_ik, cache_k, cache_v, tri)


def _retention_kernel(has_init, rq_ref, rk_ref, rv_ref, sg_ref, decay_ref, qdec_ref, kdec_ref, gst_ref, *rest):
    if has_init:
        init_ref, o_ref, st_ref, state = rest
    else:
        o_ref, st_ref, state = rest
    c = pl.program_id(1)

    @pl.when(c == 0)
    def _():
        if has_init:
            state[...] = init_ref[...]
        else:
            state[...] = jnp.zeros_like(state)

    nt = (((1,), (1,)), ((), ()))
    tn = (((0,), (0,)), ((), ()))
    rows = rq_ref.shape[0]
    lane = lax.broadcasted_iota(I32, (rows, LANES), 1)
    for p in range(RET_HEADS // 2):
        sl = slice(p * LANES, (p + 1) * LANES)
        qp = rq_ref[:, sl]
        kp = rk_ref[:, sl]
        qd = (qp.astype(F32) * qdec_ref[:, sl]).astype(BF16)
        kd = (kp.astype(F32) * kdec_ref[:, sl]).astype(BF16)
        s_old = state[p]
        s_old_b = s_old.astype(BF16)
        s_new = s_old * gst_ref[p]
        for e in range(2):
            h = 2 * p + e
            hs = slice(h * LANES, (h + 1) * LANES)
            mine = jnp.where((lane >= e * RET_DK) & (lane < (e + 1) * RET_DK), 1.0, 0.0).astype(BF16)
            sc = lax.dot_general(qp * mine, kp, nt, preferred_element_type=F32) * decay_ref[h]
            vh = rv_ref[:, hs]
            o = (jnp.dot(sc.astype(BF16), vh, preferred_element_type=F32)
                 + jnp.dot(qd * mine, s_old_b, preferred_element_type=F32))
            s_new = s_new + lax.dot_general(kd * mine, vh, tn, preferred_element_type=F32)
            mu = jnp.mean(o, axis=-1, keepdims=True)
            var = jnp.mean(jnp.square(o - mu), axis=-1, keepdims=True)
            on = (o - mu) * lax.rsqrt(var + GN_EPS)
            o_ref[:, hs] = (on * sg_ref[:, hs].astype(F32)).astype(BF16)
        state[p] = s_new

    @pl.when(c == pl.num_programs(1) - 1)
    def _():
        st_ref[...] = state[...]


def _retention(rq, rk, rv, sg, tables, init, batch, n_chunks):
    decay, qdec, kdec, gst = tables
    cr = RET_CHUNK
    rowspec = lambda n: pl.BlockSpec((cr, n), lambda b, c: (b * n_chunks + c, 0))
    const = lambda shape: pl.BlockSpec(shape, lambda b, c: (0,) * len(shape))
    st_spec = pl.BlockSpec((None, RET_HEADS // 2, LANES, LANES), lambda b, c: (b, 0, 0, 0))
    in_specs = [rowspec(512), rowspec(512), rowspec(RET_OUT), rowspec(RET_OUT),
                const(decay.shape), const(qdec.shape), const(kdec.shape), const(gst.shape)]
    args = [rq, rk, rv, sg, decay, qdec, kdec, gst]
    if init is not None:
        in_specs.append(st_spec)
        args.append(init)
    return pl.pallas_call(
        functools.partial(_retention_kernel, init is not None),
        grid=(batch, n_chunks),
        in_specs=in_specs,
        out_specs=[rowspec(RET_OUT), st_spec],
        out_shape=[jax.ShapeDtypeStruct((batch * n_chunks * cr, RET_OUT), BF16),
                   jax.ShapeDtypeStruct((batch, RET_HEADS // 2, LANES, LANES), F32)],
        scratch_shapes=[pltpu.VMEM((RET_HEADS // 2, LANES, LANES), F32)],
        compiler_params=_cparams("parallel", "arbitrary"),
        name="retention",
    )(*args)


def _retention_tables(c_eff):
    lg = jnp.log(1.0 - 2.0 ** (-5.0 - jnp.arange(RET_HEADS, dtype=F32)))
    i = jnp.arange(RET_CHUNK, dtype=F32)
    diff = i[:, None] - i[None, :]
    decay = jnp.where(diff >= 0, jnp.exp(jnp.maximum(diff, 0.0)[None] * lg[:, None, None]), 0.0)
    q_decay = jnp.exp((i + 1.0)[:, None] * lg[None, :])
    k_decay = jnp.exp((c_eff - 1.0 - i)[:, None] * lg[None, :])
    qdec = jnp.repeat(q_decay, RET_DK, axis=1)
    kdec = jnp.repeat(k_decay, RET_DK, axis=1)
    g_state = jnp.exp(c_eff * lg)
    gst = jnp.broadcast_to(jnp.repeat(g_state, RET_DK).reshape(RET_HEADS // 2, LANES, 1),
                           (RET_HEADS // 2, LANES, LANES))
    return decay, qdec, kdec, gst


def _post_kernel(a_ref, r_ref, sga_ref, sgb_ref, x_ref, mod_ref, wpa_ref, wpb_ref, wo_ref, g_ref, wr_ref, br_ref,
                 x1_ref, h2_ref, idx_ref, gate_ref):
    pa = jnp.dot(a_ref[...], wpa_ref[...], preferred_element_type=F32)
    pb = jnp.dot(r_ref[...], wpb_ref[...], preferred_element_type=F32)
    merged = sga_ref[...].astype(F32) * pa + sgb_ref[...].astype(F32) * pb
    gt1 = mod_ref[:, 2 * D_MODEL:3 * D_MODEL]
    x1 = x_ref[...] + gt1 * jnp.dot(merged.astype(BF16), wo_ref[...], preferred_element_type=F32)
    x1_ref[...] = x1
    ms = jnp.mean(x1 * x1, axis=-1, keepdims=True)
    y = x1 * lax.rsqrt(ms + NORM_EPS) * g_ref[...]
    h2 = y * (1.0 + mod_ref[:, 4 * D_MODEL:5 * D_MODEL]) + mod_ref[:, 3 * D_MODEL:4 * D_MODEL]
    h2_ref[...] = h2.astype(BF16)
    logits = jnp.dot(h2, wr_ref[...], preferred_element_type=F32, precision=lax.Precision.HIGHEST) + br_ref[...]
    lane = lax.broadcasted_iota(I32, logits.shape, 1).astype(F32)
    idx_out = jnp.zeros(logits.shape, F32)
    val_out = jnp.zeros(logits.shape, F32)
    top = None
    for j in range(TOP_K):
        m = jnp.max(logits, axis=1, keepdims=True)
        am = jnp.min(jnp.where(logits == m, lane, float(LANES)), axis=1, keepdims=True)
        if j == 0:
            top = m
        idx_out = jnp.where(lane == j, am, idx_out)
        val_out = jnp.where(lane == j, jnp.exp(m - top), val_out)
        logits = jnp.where(lane == am, -jnp.inf, logits)
    idx_ref[...] = idx_out.astype(I32)
    gate_ref[...] = val_out / jnp.sum(val_out, axis=1, keepdims=True)


def _post(a, r, sga, sgb, x, mod3, wpa, wpb, wo, g, wr, br, blocks_per_mod):
    t = x.shape[0]
    tm = TOK_BLOCK
    mod_rows = mod3.shape[1]
    row = lambda n: pl.BlockSpec((tm, n), lambda i: (i, 0))
    const = lambda a_: pl.BlockSpec(a_.shape, lambda i: (0,) * a_.ndim)
    return pl.pallas_call(
        _post_kernel,
        grid=(t // tm,),
        in_specs=[row(ATT_OUT), row(RET_OUT), row(D_MODEL), row(D_MODEL), row(D_MODEL),
                  pl.BlockSpec((None, mod_rows, 6 * D_MODEL), lambda i: (i // blocks_per_mod, 0, 0)),
                  const(wpa), const(wpb), const(wo), const(g), const(wr), const(br)],
        out_specs=[row(D_MODEL), row(D_MODEL), row(LANES), row(LANES)],
        out_shape=[jax.ShapeDtypeStruct((t, D_MODEL), F32), jax.ShapeDtypeStruct((t, D_MODEL), BF16),
                   jax.ShapeDtypeStruct((t, LANES), I32), jax.ShapeDtypeStruct((t, LANES), F32)],
        compiler_params=_cparams("parallel"),
        name="post",
    )(a, r, sga, sgb, x, mod3, wpa, wpb, wo, g, wr, br)


def _moe_kernel(be_ref, first_ref, nused_ref, x_ref, wgu_ref, bgu_ref, wd_ref, bd_ref, o_ref, wgu_b, wd_b):
    i = pl.program_id(0)

    @pl.when(first_ref[i] == 1)
    def _():
        wgu_b[...] = wgu_ref[...].astype(BF16)
        wd_b[...] = wd_ref[...].astype(BF16)

    @pl.when(i < nused_ref[0])
    def _():
        gu = jnp.dot(x_ref[...], wgu_b[...], preferred_element_type=F32) + bgu_ref[...]
        g = jnp.minimum(gu[:, :D_FF], SWIGLU_LIMIT)
        u = jnp.clip(gu[:, D_FF:], -SWIGLU_LIMIT, SWIGLU_LIMIT)
        act = (u + 1.0) * (g * jax.nn.sigmoid(SWIGLU_ALPHA * g))
        o_ref[...] = jnp.dot(act.astype(BF16), wd_b[...], preferred_element_type=F32) + bd_ref[...]

    @pl.when(i >= nused_ref[0])
    def _():
        o_ref[...] = jnp.zeros_like(o_ref)


def _moe(blk_expert, blk_first, n_used, x_sorted, w_gate_up, b_gate_up, w_down, b_down):
    n_rows = x_sorted.shape[0]
    grid_spec = pltpu.PrefetchScalarGridSpec(
        num_scalar_prefetch=3,
        grid=(n_rows // MOE_ROWS,),
        in_specs=[pl.BlockSpec((MOE_ROWS, D_MODEL), lambda i, be, bf, nu: (i, 0)),
                  pl.BlockSpec((None, D_MODEL, 2 * D_FF), lambda i, be, bf, nu: (be[i], 0, 0)),
                  pl.BlockSpec((None, 1, 2 * D_FF), lambda i, be, bf, nu: (be[i], 0, 0)),
                  pl.BlockSpec((None, D_FF, D_MODEL), lambda i, be, bf, nu: (be[i], 0, 0)),
                  pl.BlockSpec((None, 1, D_MODEL), lambda i, be, bf, nu: (be[i], 0, 0))],
        out_specs=pl.BlockSpec((MOE_ROWS, D_MODEL), lambda i, be, bf, nu: (i, 0)),
        scratch_shapes=[pltpu.VMEM((D_MODEL, 2 * D_FF), BF16), pltpu.VMEM((D_FF, D_MODEL), BF16)])
    return pl.pallas_call(
        _moe_kernel,
        grid_spec=grid_spec,
        out_shape=jax.ShapeDtypeStruct((n_rows, D_MODEL), F32),
        compiler_params=_cparams("arbitrary"),
        name="moe",
    )(blk_expert, blk_first, n_used, x_sorted, w_gate_up, b_gate_up.reshape(N_EXPERTS, 1, -1),
      w_down, b_down.reshape(N_EXPERTS, 1, -1))


def _final_kernel(x1_ref, mod_ref, gate_ref, e_ref, g_ref, y_ref):
    gates = gate_ref[...]
    moe = e_ref[0] * gates[:, 0:1]
    for j in range(1, TOP_K):
        moe = moe + e_ref[j] * gates[:, j:j + 1]
    x2 = x1_ref[...] + mod_ref[:, 5 * D_MODEL:6 * D_MODEL] * moe
    ms = jnp.mean(x2 * x2, axis=-1, keepdims=True)
    y_ref[...] = x2 * lax.rsqrt(ms + NORM_EPS) * g_ref[...]


def _final(x1, mod3, gates, rows4, g, blocks_per_mod, row_block_offset):
    t = x1.shape[0]
    tm = TOK_BLOCK
    mod_rows = mod3.shape[1]
    row = lambda n: pl.BlockSpec((tm, n), lambda i: (i, 0))
    return pl.pallas_call(
        _final_kernel,
        grid=(t // tm,),
        in_specs=[row(D_MODEL),
                  pl.BlockSpec((None, mod_rows, 6 * D_MODEL), lambda i: (i // blocks_per_mod, 0, 0)),
                  row(LANES),
                  pl.BlockSpec((TOP_K, tm, D_MODEL), lambda i: (0, i + row_block_offset, 0)),
                  pl.BlockSpec((1, D_MODEL), lambda i: (0, 0))],
        out_specs=row(D_MODEL),
        out_shape=jax.ShapeDtypeStruct((t, D_MODEL), F32),
        compiler_params=_cparams("parallel"),
        name="final",
    )(x1, mod3, gates, rows4, g)


def _rope_tables(pos, rot_dim, theta, head_dim):
    half = rot_dim // 2
    inv = theta ** (-jnp.arange(half, dtype=F32) * (2.0 / rot_dim))
    ang = pos.astype(F32)[:, None] * inv[None, :]
    cos, sin = jnp.cos(ang), jnp.sin(ang)
    n = pos.shape[0]
    rest = head_dim - rot_dim
    zh = jnp.zeros((n, half), F32)
    c = jnp.concatenate([cos, cos, jnp.ones((n, rest), F32)], axis=1)
    sa = jnp.concatenate([-sin, zh, jnp.zeros((n, rest), F32)], axis=1)
    sb = jnp.concatenate([zh, sin, jnp.zeros((n, rest), F32)], axis=1)
    rep = LANES // head_dim
    return tuple(jnp.tile(a, (1, rep)) for a in (c, sa, sb))


def _pack_w_in(w_in):
    offs = np.cumsum((0,) + IN_SPLITS)
    part = lambda j: w_in[:, offs[j]:offs[j + 1]]
    zero = lambda n: jnp.zeros((D_MODEL, n), w_in.dtype)
    group = ATT_HEADS // ATT_KV_HEADS
    cols = []
    wq = part(0)
    for h in range(ATT_HEADS):
        wh = wq[:, h * HEAD_DIM:(h + 1) * HEAD_DIM]
        cols += [wh, zero(HEAD_DIM)] if h // group == 0 else [zero(HEAD_DIM), wh]
    cols += [part(1), part(2)]
    wiq = part(3)
    for h in range(IDX_HEADS):
        cols += [wiq[:, h * IDX_DIM:(h + 1) * IDX_DIM], zero(LANES - IDX_DIM)]
    cols += [part(4), part(5), zero(LANES - IDX_DIM - IDX_HEADS)]
    cols += [part(j) for j in range(6, 12)]
    return jnp.concatenate(cols, axis=1).astype(BF16)


def _heads_major(a, db, t, width):
    heads = a.shape[1] // width
    return a.reshape(db, t, heads, width).transpose(0, 2, 1, 3).reshape(db, heads * t, width)


def _pad_rows(a, db, t, rows):
    return jnp.pad(a.reshape(db, t, -1), ((0, 0), (0, rows - t), (0, 0)))


def kernel(x_prompt, x_sample, cache_k, cache_v, cache_ik, state_ret, page_table, c_prompt, c_sample, norm_mix_g, norm_ffn_g, norm_final_g, w_ada, b_ada, w_in, w_branch_a, w_branch_b, w_out, w_router, b_router, w_gate_up, b_gate_up, w_down, b_down):
    batch, seq, _ = x_prompt.shape
    db, dt, _ = x_sample.shape
    assert w_in.shape[0] == 1, "one layer"
    tp, ts = batch * seq, db * dt
    xp = x_prompt.reshape(tp, D_MODEL)
    xs = x_sample.reshape(ts, D_MODEL)

    mod = _adaln(jnp.concatenate([c_prompt, c_sample], axis=0), w_ada[0], b_ada[0])
    mod_p = mod[:batch].reshape(batch, 1, 6 * D_MODEL)
    mod_s = jnp.repeat(mod[batch:], dt, axis=0).reshape(ts // TOK_BLOCK, TOK_BLOCK, 6 * D_MODEL)
    bpm_p = seq // TOK_BLOCK

    w_packed = _pack_w_in(w_in[0])
    pos_p = jnp.arange(seq)
    pos_s = PAST_LEN + (jnp.arange(TOK_BLOCK) % dt)
    g_mix = norm_mix_g[0].reshape(1, D_MODEL)
    outs_p = _inproj(xp, mod_p, g_mix, w_packed, _rope_tables(pos_p, ROPE_DIM, ROPE_THETA, HEAD_DIM),
                     _rope_tables(pos_p, RET_DK, RET_THETA, RET_DK), bpm_p, seq // TOK_BLOCK)
    outs_s = _inproj(xs, mod_s, g_mix, w_packed, _rope_tables(pos_s, ROPE_DIM, ROPE_THETA, HEAD_DIM),
                     _rope_tables(pos_s, RET_DK, RET_THETA, RET_DK), 1, 1)
    (q_p, k_p, v_p, kb_p, vb_p, iq_p, ikw_p, ikb_p, rq_p, rk_p, rv_p, sg_p, sga_p, sgb_p) = outs_p
    (q_s, k_s, v_s, kb_s, vb_s, iq_s, ikw_s, ikb_s, rq_s, rk_s, rv_s, sg_s, sga_s, sgb_s) = outs_s

    tri = (jnp.arange(LANES)[:, None] < jnp.arange(LANES)[None, :]).astype(BF16)

    a_p = _dsa_prompt(q_p, iq_p, ikw_p, kb_p, vb_p, ikb_p, tri, batch, seq)
    qs = _heads_major(q_s, db, dt, LANES)
    iqs = _heads_major(iq_s, db, dt, LANES)[:, :, :IDX_DIM]
    ws = _heads_major(ikw_s[:, IDX_DIM:IDX_DIM + IDX_HEADS] * (IDX_HEADS ** -0.5), db, dt, 1)
    o_s = _dsa_sample(page_table, qs, iqs, ws, _pad_rows(kb_s, db, dt, LANES), _pad_rows(vb_s, db, dt, LANES),
                      _pad_rows(ikb_s[:, :IDX_DIM], db, dt, LANES), cache_ik[0],
                      cache_k[0].reshape(-1, PAGE_SIZE, LANES), cache_v[0].reshape(-1, PAGE_SIZE, LANES), tri, dt)
    group = ATT_HEADS // ATT_KV_HEADS
    o_s = o_s.reshape(db, ATT_HEADS, dt, ATT_KV_HEADS, HEAD_DIM)
    a_s = jnp.stack([o_s[:, h, :, h // group, :] for h in range(ATT_HEADS)], axis=2)
    a_s = a_s.reshape(ts, ATT_OUT).astype(BF16)

    r_p, st_p = _retention(rq_p, rk_p, rv_p, sg_p, _retention_tables(float(RET_CHUNK)), None, batch,
                           seq // RET_CHUNK)
    pad = lambda a: _pad_rows(a, db, dt, RET_CHUNK).reshape(db * RET_CHUNK, -1)
    r_s, st_s = _retention(pad(rq_s), pad(rk_s), pad(rv_s), pad(sg_s), _retention_tables(float(dt)),
                           state_ret[0].reshape(db, RET_HEADS // 2, LANES, LANES), db, 1)
    r_s = r_s.reshape(db, RET_CHUNK, RET_OUT)[:, :dt].reshape(ts, RET_OUT)

    wr = jnp.pad(w_router[0], ((0, 0), (0, LANES - N_EXPERTS)))
    br = jnp.concatenate([b_router[0], jnp.full((LANES - N_EXPERTS,), -jnp.inf, F32)]).reshape(1, LANES)
    post_w = (w_branch_a[0].astype(BF16), w_branch_b[0].astype(BF16), w_out[0].astype(BF16),
              norm_ffn_g[0].reshape(1, D_MODEL), wr, br)
    x1_p, h2_p, idx_p, gate_p = _post(a_p, r_p, sga_p, sgb_p, xp, mod_p, *post_w, bpm_p)
    x1_s, h2_s, idx_s, gate_s = _post(a_s, r_s, sga_s, sgb_s, xs, mod_s, *post_w, 1)

    n_tok = tp + ts
    n_asg = n_tok * TOP_K
    h2 = jnp.concatenate([h2_p, h2_s], axis=0)
    e_f = jnp.concatenate([idx_p[:, :TOP_K], idx_s[:, :TOP_K]], axis=0).reshape(-1)
    counts = jnp.bincount(e_f, length=N_EXPERTS)
    padded = (counts + MOE_ROWS - 1) // MOE_ROWS * MOE_ROWS
    start = jnp.cumsum(counts) - counts
    pend = jnp.cumsum(padded)
    pstart = pend - padded
    order = jnp.argsort(e_f)
    e_s = e_f[order]
    dest = (pstart[e_s] + jnp.arange(n_asg) - start[e_s]).astype(I32)
    n_blocks = -(-n_asg // MOE_ROWS) + N_EXPERTS
    n_rows = n_blocks * MOE_ROWS
    row_tok = jnp.zeros((n_rows,), I32).at[dest].set((order // TOP_K).astype(I32))
    asg_row = jnp.zeros((n_asg,), I32).at[order].set(dest)
    blk_expert = jnp.minimum(jnp.searchsorted(pend, jnp.arange(n_blocks) * MOE_ROWS, side='right'),
                             N_EXPERTS - 1).astype(I32)
    blk_first = jnp.concatenate([jnp.ones((1,), I32), (blk_expert[1:] != blk_expert[:-1]).astype(I32)])
    n_used = (pend[-1] // MOE_ROWS).astype(I32).reshape(1)
    x_sorted = h2[row_tok]
    rows_out = _moe(blk_expert, blk_first, n_used, x_sorted, w_gate_up[0], b_gate_up[0], w_down[0], b_down[0])
    rows4 = rows_out[asg_row.reshape(n_tok, TOP_K).T]

    g_final = norm_final_g.reshape(1, D_MODEL)
    y_p = _final(x1_p, mod_p, gate_p, rows4, g_final, bpm_p, 0)
    y_s = _final(x1_s, mod_s, gate_s, rows4, g_final, 1, tp // TOK_BLOCK)

    kv_shape = lambda b, s: (1, b, s, ATT_KV_HEADS, HEAD_DIM)
    st_shape = lambda b: (1, b, RET_HEADS, RET_DK, RET_DV)
    return (y_p.reshape(batch, seq, D_MODEL), y_s.reshape(db, dt, D_MODEL),
            k_p.reshape(kv_shape(batch, seq)), v_p.reshape(kv_shape(batch, seq)),
            ikw_p[:, :IDX_DIM].reshape(1, batch, seq, IDX_DIM), st_p.reshape(st_shape(batch)),
            k_s.reshape(kv_shape(db, dt)), v_s.reshape(kv_shape(db, dt)),
            ikw_s[:, :IDX_DIM].reshape(1, db, dt, IDX_DIM), st_s.reshape(st_shape(db)))
```

```python
import functools

import jax
import jax.numpy as jnp
import numpy as np
from jax import lax
from jax.experimental import pallas as pl
from jax.experimental.pallas import tpu as pltpu

F32 = jnp.float32
BF16 = jnp.bfloat16
I32 = jnp.int32

D_MODEL = 1024
PAST_LEN = 8192
PAGE_SIZE = 128
ATT_HEADS = 8
ATT_KV_HEADS = 2
HEAD_DIM = 64
ROPE_DIM = HEAD_DIM // 4
ROPE_THETA = 500000.0
IDX_HEADS = 8
IDX_DIM = 64
IDX_ROPE_DIM = IDX_DIM // 4
TOPK_MAX = 256
RET_HEADS = 8
RET_DK = 64
RET_DV = 128
RET_THETA = 10000.0
RET_CHUNK = 128
N_EXPERTS = 32
TOP_K = 4
D_FF = D_MODEL
SWIGLU_LIMIT = 7.0
SWIGLU_ALPHA = 1.702
NORM_EPS = 1e-6
GN_EPS = 1e-5
ATT_OUT = ATT_HEADS * HEAD_DIM
RET_OUT = RET_HEADS * RET_DV
IN_SPLITS = (ATT_HEADS * HEAD_DIM, ATT_KV_HEADS * HEAD_DIM, ATT_KV_HEADS * HEAD_DIM,
             IDX_HEADS * IDX_DIM, IDX_DIM, IDX_HEADS,
             RET_HEADS * RET_DK, RET_HEADS * RET_DK, RET_OUT, RET_OUT, D_MODEL, D_MODEL)

LANES = 128
MASK_NEG = -1e30
FLT_MAX = 3.4028234663852886e38
SELECT_UNROLL = 4
CAUSAL_VARIANTS = 4
IDX_KEY_CHUNK = 256
VMEM_LIMIT = 56 * 1024 * 1024

TOK_BLOCK = 256
Q_BLOCK = 128
MOE_ROWS = 256

_W_GROUPS = (("q", 1024), ("k", 128), ("v", 128), ("iq", 1024), ("ikw", 128), ("rq", 512), ("rk", 512),
             ("rv", 1024), ("rg", 1024), ("ga", 1024), ("gb", 1024))
_W_OFF = {}
_off = 0
for _n, _w in _W_GROUPS:
    _W_OFF[_n] = (_off, _w)
    _off += _w
W_COLS = _off


def _cparams(*sem):
    return pltpu.CompilerParams(dimension_semantics=sem, vmem_limit_bytes=VMEM_LIMIT)


def _adaln_kernel(c_ref, w_ref, b_ref, o_ref):
    c = c_ref[...]
    s = c * jax.nn.sigmoid(c)
    o_ref[...] = jnp.dot(s, w_ref[...], preferred_element_type=F32, precision=lax.Precision.HIGHEST) + b_ref[...]


def _adaln(c_all, w_ada, b_ada):
    n = c_all.shape[0]
    nb = 1536
    return pl.pallas_call(
        _adaln_kernel,
        grid=(6 * D_MODEL // nb,),
        in_specs=[pl.BlockSpec((n, D_MODEL), lambda j: (0, 0)),
                  pl.BlockSpec((D_MODEL, nb), lambda j: (0, j)),
                  pl.BlockSpec((1, nb), lambda j: (0, j))],
        out_specs=pl.BlockSpec((n, nb), lambda j: (0, j)),
        out_shape=jax.ShapeDtypeStruct((n, 6 * D_MODEL), F32),
        compiler_params=_cparams("arbitrary"),
        name="adaln",
    )(c_all, w_ada, b_ada.reshape(1, -1))


def _rope_slab(z, c, sa, sb, half):
    return z * c + pltpu.roll(z, LANES - half, 1) * sa + pltpu.roll(z, half, 1) * sb


def _inproj_kernel(x_ref, mod_ref, g_ref, w_ref, ca_ref, saa_ref, sba_ref, cr_ref, sar_ref, sbr_ref,
                   q_ref, k_ref, v_ref, kb_ref, vb_ref, iq_ref, ikw_ref, ikb_ref,
                   rq_ref, rk_ref, rv_ref, sg_ref, sga_ref, sgb_ref):
    x = x_ref[...]
    ms = jnp.mean(x * x, axis=-1, keepdims=True)
    y = x * lax.rsqrt(ms + NORM_EPS) * g_ref[...]
    h = (y * (1.0 + mod_ref[:, D_MODEL:2 * D_MODEL]) + mod_ref[:, 0:D_MODEL]).astype(BF16)

    def proj(name, s):
        c0 = _W_OFF[name][0] + s * LANES
        return jnp.dot(h, w_ref[:, c0:c0 + LANES], preferred_element_type=F32)

    ca, saa, sba = ca_ref[...], saa_ref[...], sba_ref[...]
    cr, sar, sbr = cr_ref[...], sar_ref[...], sbr_ref[...]
    att_half, ret_half = ROPE_DIM // 2, RET_DK // 2
    lane = lax.broadcasted_iota(I32, (x.shape[0], LANES), 1)

    for s in range(8):
        sl = slice(s * LANES, (s + 1) * LANES)
        q_ref[:, sl] = (_rope_slab(proj("q", s), ca, saa, sba, att_half) * 0.125).astype(BF16)
        iq_ref[:, sl] = (_rope_slab(proj("iq", s), ca, saa, sba, att_half) * 0.125).astype(BF16)
    kk = _rope_slab(proj("k", 0), ca, saa, sba, att_half)
    k_ref[...] = kk
    kb_ref[...] = kk.astype(BF16)
    vv = proj("v", 0)
    v_ref[...] = vv
    vb_ref[...] = vv.astype(BF16)
    z = proj("ikw", 0)
    zr = _rope_slab(z, ca, saa, sba, att_half)
    ikw_ref[...] = jnp.where(lane < IDX_DIM, zr, z)
    ikb_ref[...] = jnp.where(lane < IDX_DIM, zr, 0.0).astype(BF16)
    for s in range(4):
        sl = slice(s * LANES, (s + 1) * LANES)
        rq_ref[:, sl] = _rope_slab(proj("rq", s), cr, sar, sbr, ret_half).astype(BF16)
        rk_ref[:, sl] = (_rope_slab(proj("rk", s), cr, sar, sbr, ret_half) * 0.125).astype(BF16)
    for s in range(8):
        sl = slice(s * LANES, (s + 1) * LANES)
        rv_ref[:, sl] = proj("rv", s).astype(BF16)
        g = proj("rg", s)
        sg_ref[:, sl] = (g * jax.nn.sigmoid(g)).astype(BF16)
        sga_ref[:, sl] = jax.nn.sigmoid(proj("ga", s)).astype(BF16)
        sgb_ref[:, sl] = jax.nn.sigmoid(proj("gb", s)).astype(BF16)


def _inproj(x, mod3, g, w_packed, tabs_att, tabs_ret, blocks_per_mod, tab_blocks):
    t = x.shape[0]
    tm = TOK_BLOCK
    nblk = t // tm
    mod_rows = mod3.shape[1]
    tab_spec = pl.BlockSpec((tm, LANES), lambda i: (i % tab_blocks, 0))
    row = lambda n: pl.BlockSpec((tm, n), lambda i: (i, 0))
    out_defs = [(1024, BF16), (128, F32), (128, F32), (128, BF16), (128, BF16), (1024, BF16), (128, F32),
                (128, BF16), (512, BF16), (512, BF16), (1024, BF16), (1024, BF16), (1024, BF16), (1024, BF16)]
    return pl.pallas_call(
        _inproj_kernel,
        grid=(nblk,),
        in_specs=[row(D_MODEL),
                  pl.BlockSpec((None, mod_rows, 6 * D_MODEL), lambda i: (i // blocks_per_mod, 0, 0)),
                  pl.BlockSpec((1, D_MODEL), lambda i: (0, 0)),
                  pl.BlockSpec((D_MODEL, W_COLS), lambda i: (0, 0))] + [tab_spec] * 6,
        out_specs=[row(n) for n, _ in out_defs],
        out_shape=[jax.ShapeDtypeStruct((t, n), d) for n, d in out_defs],
        compiler_params=_cparams("parallel"),
        name="inproj",
    )(x, mod3, g, w_packed, *tabs_att, *tabs_ret)


def _count(score_ref, n, pred):
    acc = jnp.zeros((score_ref.shape[0], LANES), F32)
    for c in range(n // LANES):
        acc = acc + jnp.where(pred(score_ref[:, c * LANES:(c + 1) * LANES]), 1.0, 0.0)
    return jnp.sum(acc, axis=1, keepdims=True)


def _kth_largest(score_ref, n, k):
    sc = score_ref[:, :n]
    finite = sc > -jnp.inf
    n_fin = jnp.sum(jnp.where(finite, 1.0, 0.0), axis=1, keepdims=True)
    mx = jnp.max(sc, axis=1, keepdims=True)
    lo = jnp.min(jnp.where(finite, sc, jnp.inf), axis=1, keepdims=True)
    hi = mx + (jnp.abs(mx) * 2.0 ** -20 + 2.0 ** -100)
    small = n_fin <= k
    done = jnp.where(small, 1.0, 0.0)

    def cond(state):
        return jnp.min(state[2]) < 0.5

    def body(state):
        lo, hi, done = state
        for _ in range(SELECT_UNROLL):
            mid = 0.5 * lo + 0.5 * hi
            cnt = _count(score_ref, n, lambda s: s >= mid)
            stuck = jnp.logical_or(mid <= lo, mid >= hi)
            live = jnp.logical_and(done < 0.5, jnp.logical_not(stuck))
            ge = cnt >= k
            lo = jnp.where(jnp.logical_and(live, ge), mid, lo)
            hi = jnp.where(jnp.logical_and(live, jnp.logical_not(ge)), mid, hi)
            done = jnp.where(jnp.logical_or(stuck, cnt == k), 1.0, done)
        return lo, hi, done

    lo, _, _ = lax.while_loop(cond, body, (lo, hi, done))
    return jnp.where(small, -FLT_MAX, lo)


def _topk_bias(score_ref, bias_ref, tri_ref, n, k):
    rows = score_ref.shape[0]
    thr = _kth_largest(score_ref, n, k)
    need = k - _count(score_ref, n, lambda s: s > thr)
    n_eq = _count(score_ref, n, lambda s: s == thr)
    has_ties = jnp.max(jnp.where(need < n_eq, 1.0, 0.0)) > 0.5

    @pl.when(jnp.logical_not(has_ties))
    def _():
        for c in range(n // LANES):
            sl = slice(c * LANES, (c + 1) * LANES)
            bias_ref[:, sl] = jnp.where(score_ref[:, sl] >= thr, 0.0, MASK_NEG)

    @pl.when(has_ties)
    def _():
        run = jnp.zeros((rows, 1), F32)
        for c in range(n // LANES):
            sl = slice(c * LANES, (c + 1) * LANES)
            sc = score_ref[:, sl]
            eq = sc == thr
            eqf = jnp.where(eq, 1.0, 0.0)
            before = jnp.dot(eqf.astype(BF16), tri_ref[...], preferred_element_type=F32) + run
            take = jnp.logical_or(sc > thr, jnp.logical_and(eq, before < need))
            bias_ref[:, sl] = jnp.where(take, 0.0, MASK_NEG)
            run = run + jnp.sum(eqf, axis=1, keepdims=True)


def _dsa_prompt_kernel(q_ref, iq_ref, ikw_ref, kb_ref, vb_ref, ikb_ref, tri_ref, o_ref, score_ref, bias_ref):
    qb, s_len = score_ref.shape
    i = pl.program_id(1)
    nqb = s_len // qb
    per_variant = nqb // CAUSAL_VARIANTS
    for v in range(CAUSAL_VARIANTS):
        pl.when(i // per_variant == v)(
            functools.partial(_dsa_prompt_body, q_ref, iq_ref, ikw_ref, kb_ref, vb_ref, ikb_ref, tri_ref, o_ref,
                              score_ref, bias_ref, (v + 1) * per_variant * qb, min(TOPK_MAX, s_len // 4)))


def _dsa_prompt_body(q_ref, iq_ref, ikw_ref, kb_ref, vb_ref, ikb_ref, tri_ref, o_ref, score_ref, bias_ref,
                     n_keys, topk):
    qb = score_ref.shape[0]
    i = pl.program_id(1)
    w = ikw_ref[:, IDX_DIM:IDX_DIM + IDX_HEADS] * (IDX_HEADS ** -0.5)
    nt = (((1,), (1,)), ((), ()))
    kc = IDX_KEY_CHUNK
    qpos = i * qb + lax.broadcasted_iota(I32, (qb, kc), 0)
    for c in range(n_keys // kc):
        ikc = ikb_ref[c * kc:(c + 1) * kc, :]
        acc = jnp.zeros((qb, kc), F32)
        for h in range(IDX_HEADS):
            d = lax.dot_general(iq_ref[:, h * LANES:(h + 1) * LANES], ikc, nt, preferred_element_type=F32)
            acc = acc + jnp.maximum(d, 0.0) * w[:, h:h + 1]
        kpos = c * kc + lax.broadcasted_iota(I32, (qb, kc), 1)
        score_ref[:, c * kc:(c + 1) * kc] = jnp.where(kpos <= qpos, acc, -jnp.inf)

    _topk_bias(score_ref, bias_ref, tri_ref, n_keys, topk)

    kb = kb_ref[0:n_keys, :]
    vb = vb_ref[0:n_keys, :]
    bias = bias_ref[:, 0:n_keys]
    lane = lax.broadcasted_iota(I32, (qb, LANES), 1)
    heads = []
    for h in range(ATT_HEADS):
        s = lax.dot_general(q_ref[:, h * LANES:(h + 1) * LANES], kb, nt, preferred_element_type=F32) + bias
        m = jnp.max(s, axis=1, keepdims=True)
        p = jnp.exp(s - m)
        l = jnp.sum(p, axis=1, keepdims=True)
        heads.append(jnp.dot(p.astype(BF16), vb, preferred_element_type=F32) / l)
    group = ATT_HEADS // ATT_KV_HEADS
    for pp in range(ATT_HEADS // 2):
        a, b = heads[2 * pp], heads[2 * pp + 1]
        if (2 * pp) // group == 0:
            slab = jnp.where(lane < HEAD_DIM, a, pltpu.roll(b, HEAD_DIM, 1))
        else:
            slab = jnp.where(lane < HEAD_DIM, pltpu.roll(a, HEAD_DIM, 1), b)
        o_ref[:, pp * LANES:(pp + 1) * LANES] = slab.astype(BF16)


def _dsa_prompt(q, iq, ikw, kb, vb, ikb, tri, batch, seq):
    nqb = seq // Q_BLOCK
    qrow = lambda n: pl.BlockSpec((Q_BLOCK, n), lambda b, i: (b * nqb + i, 0))
    keys = pl.BlockSpec((seq, LANES), lambda b, i: (b, 0))
    return pl.pallas_call(
        _dsa_prompt_kernel,
        grid=(batch, nqb),
        in_specs=[qrow(1024), qrow(1024), qrow(LANES), keys, keys, keys,
                  pl.BlockSpec((LANES, LANES), lambda b, i: (0, 0))],
        out_specs=qrow(ATT_OUT),
        out_shape=jax.ShapeDtypeStruct((batch * seq, ATT_OUT), BF16),
        scratch_shapes=[pltpu.VMEM((Q_BLOCK, seq), F32), pltpu.VMEM((Q_BLOCK, seq), F32)],
        compiler_params=_cparams("parallel", "arbitrary"),
        name="dsa_prompt",
    )(q, iq, ikw, kb, vb, ikb, tri)


def _dsa_sample_kernel(pt_ref, qs_ref, iqs_ref, ws_ref, knew_ref, vnew_ref, iknew_ref, cik_hbm, ck_hbm, cv_hbm,
                       tri_ref, o_ref, ikbuf, kbuf, vbuf, sems, key_ref, bias_ref):
    db = pl.program_id(0)
    n_pages = ikbuf.shape[0]
    t = key_ref.shape[0]
    n_past = n_pages * PAGE_SIZE

    def copies(p):
        pg = pt_ref[db, p]
        return (pltpu.make_async_copy(cik_hbm.at[pg], ikbuf.at[p], sems.at[0]),
                pltpu.make_async_copy(ck_hbm.at[pg], kbuf.at[p], sems.at[1]),
                pltpu.make_async_copy(cv_hbm.at[pg], vbuf.at[p], sems.at[2]))

    def start(p, carry):
        for cp in copies(p):
            cp.start()
        return carry

    def wait(p, carry):
        for cp in copies(p):
            cp.wait()
        return carry

    lax.fori_loop(0, n_pages, start, 0)
    lax.fori_loop(0, n_pages, wait, 0)

    nt = (((1,), (1,)), ((), ()))
    ik_past = ikbuf[...].reshape(n_past, IDX_DIM).astype(BF16)
    iqs = iqs_ref[...]
    wcol = ws_ref[...]
    d_past = jnp.maximum(lax.dot_general(iqs, ik_past, nt, preferred_element_type=F32), 0.0) * wcol
    d_new = jnp.maximum(lax.dot_general(iqs, iknew_ref[...], nt, preferred_element_type=F32), 0.0) * wcol
    s_past = d_past[0:t]
    s_new = d_new[0:t]
    for h in range(1, IDX_HEADS):
        s_past = s_past + d_past[h * t:(h + 1) * t]
        s_new = s_new + d_new[h * t:(h + 1) * t]
    row = lax.broadcasted_iota(I32, (t, LANES), 0)
    lane = lax.broadcasted_iota(I32, (t, LANES), 1)
    new_ok = lane <= row
    key_ref[:, 0:n_past] = s_past
    key_ref[:, n_past:n_past + LANES] = jnp.where(new_ok, s_new, -jnp.inf)
    _topk_bias(key_ref, bias_ref, tri_ref, n_past + LANES, min(TOPK_MAX, (n_past + t) // 4))

    qs = qs_ref[...]
    reps = qs.shape[0] // t
    bias = jnp.concatenate([bias_ref[...]] * reps, axis=0)
    k_past = kbuf[...].reshape(n_past, LANES).astype(BF16)
    v_past = vbuf[...].reshape(n_past, LANES).astype(BF16)
    sp = lax.dot_general(qs, k_past, nt, preferred_element_type=F32) + bias[:, 0:n_past]
    sn = lax.dot_general(qs, knew_ref[...], nt, preferred_element_type=F32) + bias[:, n_past:n_past + LANES]
    m = jnp.maximum(jnp.max(sp, axis=1, keepdims=True), jnp.max(sn, axis=1, keepdims=True))
    pp = jnp.exp(sp - m)
    pn = jnp.exp(sn - m)
    l = jnp.sum(pp, axis=1, keepdims=True) + jnp.sum(pn, axis=1, keepdims=True)
    o = (jnp.dot(pp.astype(BF16), v_past, preferred_element_type=F32)
         + jnp.dot(pn.astype(BF16), vnew_ref[...], preferred_element_type=F32))
    o_ref[...] = o / l


def _dsa_sample(page_table, qs, iqs, ws, knew, vnew, iknew, cache_ik, cache_k, cache_v, tri, t):
    db, n_pages = page_table.shape
    rows = qs.shape[1]
    per_db = lambda r, n: pl.BlockSpec((None, r, n), lambda b, pt: (b, 0, 0))
    any_spec = pl.BlockSpec(memory_space=pl.ANY)
    n_keys = n_pages * PAGE_SIZE + LANES
    grid_spec = pltpu.PrefetchScalarGridSpec(
        num_scalar_prefetch=1,
        grid=(db,),
        in_specs=[per_db(rows, LANES), per_db(rows, IDX_DIM), per_db(rows, 1), per_db(LANES, LANES),
                  per_db(LANES, LANES), per_db(LANES, IDX_DIM), any_spec, any_spec, any_spec,
                  pl.BlockSpec((LANES, LANES), lambda b, pt: (0, 0))],
        out_specs=per_db(rows, LANES),
        scratch_shapes=[pltpu.VMEM((n_pages, PAGE_SIZE, IDX_DIM), F32),
                        pltpu.VMEM((n_pages, PAGE_SIZE, LANES), F32),
                        pltpu.VMEM((n_pages, PAGE_SIZE, LANES), F32),
                        pltpu.SemaphoreType.DMA((3,)),
                        pltpu.VMEM((t, n_keys), F32),
                        pltpu.VMEM((t, n_keys), F32)])
    return pl.pallas_call(
        _dsa_sample_kernel,
        grid_spec=grid_spec,
        out_shape=jax.ShapeDtypeStruct((db, rows, LANES), F32),
        compiler_params=_cparams("arbitrary"),
        name="dsa_sample",
    )(page_table, qs, iqs, ws, knew, vnew, iknew, cache_ik, cache_k, cache_v, tri)


def _retention_kernel(has_init, rq_ref, rk_ref, rv_ref, sg_ref, decay_ref, qdec_ref, kdec_ref, gst_ref, *rest):
    if has_init:
        init_ref, o_ref, st_ref, state = rest
    else:
        o_ref, st_ref, state = rest
    c = pl.program_id(1)

    @pl.when(c == 0)
    def _():
        if has_init:
            state[...] = init_ref[...]
        else:
            state[...] = jnp.zeros_like(state)

    nt = (((1,), (1,)), ((), ()))
    tn = (((0,), (0,)), ((), ()))
    rows = rq_ref.shape[0]
    lane = lax.broadcasted_iota(I32, (rows, LANES), 1)
    for p in range(RET_HEADS // 2):
        sl = slice(p * LANES, (p + 1) * LANES)
        qp = rq_ref[:, sl]
        kp = rk_ref[:, sl]
        qd = (qp.astype(F32) * qdec_ref[:, sl]).astype(BF16)
        kd = (kp.astype(F32) * kdec_ref[:, sl]).astype(BF16)
        s_old = state[p]
        s_old_b = s_old.astype(BF16)
        s_new = s_old * gst_ref[p]
        for e in range(2):
            h = 2 * p + e
            hs = slice(h * LANES, (h + 1) * LANES)
            mine = jnp.where((lane >= e * RET_DK) & (lane < (e + 1) * RET_DK), 1.0, 0.0).astype(BF16)
            sc = lax.dot_general(qp * mine, kp, nt, preferred_element_type=F32) * decay_ref[h]
            vh = rv_ref[:, hs]
            o = (jnp.dot(sc.astype(BF16), vh, preferred_element_type=F32)
                 + jnp.dot(qd * mine, s_old_b, preferred_element_type=F32))
            s_new = s_new + lax.dot_general(kd * mine, vh, tn, preferred_element_type=F32)
            mu = jnp.mean(o, axis=-1, keepdims=True)
            var = jnp.mean(jnp.square(o - mu), axis=-1, keepdims=True)
            on = (o - mu) * lax.rsqrt(var + GN_EPS)
            o_ref[:, hs] = (on * sg_ref[:, hs].astype(F32)).astype(BF16)
        state[p] = s_new

    @pl.when(c == pl.num_programs(1) - 1)
    def _():
        st_ref[...] = state[...]


def _retention(rq, rk, rv, sg, tables, init, batch, n_chunks):
    decay, qdec, kdec, gst = tables
    cr = RET_CHUNK
    rowspec = lambda n: pl.BlockSpec((cr, n), lambda b, c: (b * n_chunks + c, 0))
    const = lambda shape: pl.BlockSpec(shape, lambda b, c: (0,) * len(shape))
    st_spec = pl.BlockSpec((None, RET_HEADS // 2, LANES, LANES), lambda b, c: (b, 0, 0, 0))
    in_specs = [rowspec(512), rowspec(512), rowspec(RET_OUT), rowspec(RET_OUT),
                const(decay.shape), const(qdec.shape), const(kdec.shape), const(gst.shape)]
    args = [rq, rk, rv, sg, decay, qdec, kdec, gst]
    if init is not None:
        in_specs.append(st_spec)
        args.append(init)
    return pl.pallas_call(
        functools.partial(_retention_kernel, init is not None),
        grid=(batch, n_chunks),
        in_specs=in_specs,
        out_specs=[rowspec(RET_OUT), st_spec],
        out_shape=[jax.ShapeDtypeStruct((batch * n_chunks * cr, RET_OUT), BF16),
                   jax.ShapeDtypeStruct((batch, RET_HEADS // 2, LANES, LANES), F32)],
        scratch_shapes=[pltpu.VMEM((RET_HEADS // 2, LANES, LANES), F32)],
        compiler_params=_cparams("parallel", "arbitrary"),
        name="retention",
    )(*args)


def _retention_tables(c_eff):
    lg = jnp.log(1.0 - 2.0 ** (-5.0 - jnp.arange(RET_HEADS, dtype=F32)))
    i = jnp.arange(RET_CHUNK, dtype=F32)
    diff = i[:, None] - i[None, :]
    decay = jnp.where(diff >= 0, jnp.exp(jnp.maximum(diff, 0.0)[None] * lg[:, None, None]), 0.0)
    q_decay = jnp.exp((i + 1.0)[:, None] * lg[None, :])
    k_decay = jnp.exp((c_eff - 1.0 - i)[:, None] * lg[None, :])
    qdec = jnp.repeat(q_decay, RET_DK, axis=1)
    kdec = jnp.repeat(k_decay, RET_DK, axis=1)
    g_state = jnp.exp(c_eff * lg)
    gst = jnp.broadcast_to(jnp.repeat(g_state, RET_DK).reshape(RET_HEADS // 2, LANES, 1),
                           (RET_HEADS // 2, LANES, LANES))
    return decay, qdec, kdec, gst


def _pack_bf16_pairs(lo, hi):
    return pltpu.pack_elementwise([lo, hi], packed_dtype=BF16)


def _unpack_bf16_pairs(words):
    return tuple(pltpu.unpack_elementwise(words, index=j, packed_dtype=BF16, unpacked_dtype=F32).astype(BF16)
                 for j in range(2))


def _post_kernel(a_ref, r_ref, sga_ref, sgb_ref, x_ref, mod_ref, wpa_ref, wpb_ref, wo_ref, g_ref, wr_ref, br_ref,
                 ltri_ref, cnt0_ref, x1_ref, h2w_ref, idx_ref, gate_ref, rank_ref, cnt_ref, cnt):
    @pl.when(pl.program_id(0) == 0)
    def _():
        cnt[...] = cnt0_ref[...]

    pa = jnp.dot(a_ref[...], wpa_ref[...], preferred_element_type=F32)
    pb = jnp.dot(r_ref[...], wpb_ref[...], preferred_element_type=F32)
    merged = sga_ref[...].astype(F32) * pa + sgb_ref[...].astype(F32) * pb
    gt1 = mod_ref[:, 2 * D_MODEL:3 * D_MODEL]
    x1 = x_ref[...] + gt1 * jnp.dot(merged.astype(BF16), wo_ref[...], preferred_element_type=F32)
    x1_ref[...] = x1
    ms = jnp.mean(x1 * x1, axis=-1, keepdims=True)
    y = x1 * lax.rsqrt(ms + NORM_EPS) * g_ref[...]
    h2 = y * (1.0 + mod_ref[:, 4 * D_MODEL:5 * D_MODEL]) + mod_ref[:, 3 * D_MODEL:4 * D_MODEL]
    half = D_MODEL // 2
    h2w_ref[...] = _pack_bf16_pairs(h2[:, :half], h2[:, half:])
    logits = jnp.dot(h2, wr_ref[...], preferred_element_type=F32, precision=lax.Precision.HIGHEST) + br_ref[...]
    lane = lax.broadcasted_iota(I32, logits.shape, 1).astype(F32)
    idx_out = jnp.zeros(logits.shape, F32)
    val_out = jnp.zeros(logits.shape, F32)
    chosen = []
    top = None
    for j in range(TOP_K):
        m = jnp.max(logits, axis=1, keepdims=True)
        am = jnp.min(jnp.where(logits == m, lane, float(LANES)), axis=1, keepdims=True)
        if j == 0:
            top = m
        idx_out = jnp.where(lane == j, am, idx_out)
        val_out = jnp.where(lane == j, jnp.exp(m - top), val_out)
        chosen.append(lane == am)
        logits = jnp.where(chosen[-1], -jnp.inf, logits)
    idx_ref[...] = idx_out.astype(I32)
    gate_ref[...] = val_out / jnp.sum(val_out, axis=1, keepdims=True)
    onehot = jnp.where(chosen[0] | chosen[1] | chosen[2] | chosen[3], 1.0, 0.0)
    before = jnp.dot(ltri_ref[...], onehot.astype(BF16), preferred_element_type=F32) + cnt[...]
    rank_out = jnp.zeros(logits.shape, F32)
    for j in range(TOP_K):
        rj = jnp.sum(jnp.where(chosen[j], before, 0.0), axis=1, keepdims=True)
        rank_out = jnp.where(lane == j, rj, rank_out)
    rank_ref[...] = rank_out.astype(I32)
    cnt[...] = cnt[...] + jnp.sum(onehot, axis=0, keepdims=True)
    cnt_ref[...] = cnt[...]


def _post(a, r, sga, sgb, x, mod3, wpa, wpb, wo, g, wr, br, ltri, cnt0, blocks_per_mod):
    t = x.shape[0]
    tm = TOK_BLOCK
    mod_rows = mod3.shape[1]
    row = lambda n: pl.BlockSpec((tm, n), lambda i: (i, 0))
    const = lambda a_: pl.BlockSpec(a_.shape, lambda i: (0,) * a_.ndim)
    return pl.pallas_call(
        _post_kernel,
        grid=(t // tm,),
        in_specs=[row(ATT_OUT), row(RET_OUT), row(D_MODEL), row(D_MODEL), row(D_MODEL),
                  pl.BlockSpec((None, mod_rows, 6 * D_MODEL), lambda i: (i // blocks_per_mod, 0, 0)),
                  const(wpa), const(wpb), const(wo), const(g), const(wr), const(br), const(ltri), const(cnt0)],
        out_specs=[row(D_MODEL), row(D_MODEL // 2), row(LANES), row(LANES), row(LANES),
                   pl.BlockSpec((1, LANES), lambda i: (0, 0))],
        out_shape=[jax.ShapeDtypeStruct((t, D_MODEL), F32), jax.ShapeDtypeStruct((t, D_MODEL // 2), jnp.uint32),
                   jax.ShapeDtypeStruct((t, LANES), I32), jax.ShapeDtypeStruct((t, LANES), F32),
                   jax.ShapeDtypeStruct((t, LANES), I32), jax.ShapeDtypeStruct((1, LANES), F32)],
        scratch_shapes=[pltpu.VMEM((1, LANES), F32)],
        compiler_params=_cparams("arbitrary"),
        name="post",
    )(a, r, sga, sgb, x, mod3, wpa, wpb, wo, g, wr, br, ltri, cnt0)


def _dispatch_kernel(dest_ref, h2w_ref, xs_in, xs_out, sem):
    del xs_in
    i = pl.program_id(0)
    tm = h2w_ref.shape[0]

    def row_copy(r, d):
        return pltpu.make_async_copy(h2w_ref.at[pl.ds(r, 1)], xs_out.at[pl.ds(d, 1)], sem)

    def start(r, carry):
        base = (i * tm + r) * TOP_K
        for j in range(TOP_K):
            row_copy(r, dest_ref[base + j]).start()
        return carry

    lax.fori_loop(0, tm, start, 0)
    for j in range(TOP_K):
        pltpu.make_async_copy(h2w_ref, xs_out.at[pl.ds(0, tm)], sem).wait()


def _dispatch(dest_flat, h2w, x_sorted):
    t = h2w.shape[0]
    tm = TOK_BLOCK
    grid_spec = pltpu.PrefetchScalarGridSpec(
        num_scalar_prefetch=1,
        grid=(t // tm,),
        in_specs=[pl.BlockSpec((tm, D_MODEL // 2), lambda i, d: (i, 0)), pl.BlockSpec(memory_space=pl.ANY)],
        out_specs=pl.BlockSpec(memory_space=pl.ANY),
        scratch_shapes=[pltpu.SemaphoreType.DMA(())])
    return pl.pallas_call(
        _dispatch_kernel,
        grid_spec=grid_spec,
        out_shape=jax.ShapeDtypeStruct(x_sorted.shape, x_sorted.dtype),
        input_output_aliases={2: 0},
        compiler_params=_cparams("arbitrary"),
        name="dispatch",
    )(dest_flat, h2w, x_sorted)


def _moe_kernel(be_ref, first_ref, nused_ref, x_ref, wgu_ref, bgu_ref, wd_ref, bd_ref, o_ref, wgu_b, wd_b):
    i = pl.program_id(0)

    @pl.when(first_ref[i] == 1)
    def _():
        wgu_b[...] = wgu_ref[...].astype(BF16)
        wd_b[...] = wd_ref[...].astype(BF16)

    @pl.when(i < nused_ref[0])
    def _():
        x = jnp.concatenate(_unpack_bf16_pairs(x_ref[...]), axis=1)
        gu = jnp.dot(x, wgu_b[...], preferred_element_type=F32) + bgu_ref[...]
        g = jnp.minimum(gu[:, :D_FF], SWIGLU_LIMIT)
        u = jnp.clip(gu[:, D_FF:], -SWIGLU_LIMIT, SWIGLU_LIMIT)
        act = (u + 1.0) * (g * jax.nn.sigmoid(SWIGLU_ALPHA * g))
        o_ref[...] = jnp.dot(act.astype(BF16), wd_b[...], preferred_element_type=F32) + bd_ref[...]

    @pl.when(i >= nused_ref[0])
    def _():
        o_ref[...] = jnp.zeros_like(o_ref)


def _moe(blk_expert, blk_first, n_used, x_sorted, w_gate_up, b_gate_up, w_down, b_down):
    n_rows = x_sorted.shape[0]
    grid_spec = pltpu.PrefetchScalarGridSpec(
        num_scalar_prefetch=3,
        grid=(n_rows // MOE_ROWS,),
        in_specs=[pl.BlockSpec((MOE_ROWS, D_MODEL // 2), lambda i, be, bf, nu: (i, 0)),
                  pl.BlockSpec((None, D_MODEL, 2 * D_FF), lambda i, be, bf, nu: (be[i], 0, 0)),
                  pl.BlockSpec((None, 1, 2 * D_FF), lambda i, be, bf, nu: (be[i], 0, 0)),
                  pl.BlockSpec((None, D_FF, D_MODEL), lambda i, be, bf, nu: (be[i], 0, 0)),
                  pl.BlockSpec((None, 1, D_MODEL), lambda i, be, bf, nu: (be[i], 0, 0))],
        out_specs=pl.BlockSpec((MOE_ROWS, D_MODEL), lambda i, be, bf, nu: (i, 0)),
        scratch_shapes=[pltpu.VMEM((D_MODEL, 2 * D_FF), BF16), pltpu.VMEM((D_FF, D_MODEL), BF16)])
    return pl.pallas_call(
        _moe_kernel,
        grid_spec=grid_spec,
        out_shape=jax.ShapeDtypeStruct((n_rows, D_MODEL), F32),
        compiler_params=_cparams("arbitrary"),
        name="moe",
    )(blk_expert, blk_first, n_used, x_sorted, w_gate_up, b_gate_up.reshape(N_EXPERTS, 1, -1),
      w_down, b_down.reshape(N_EXPERTS, 1, -1))


def _final_kernel(dest_ref, x1_ref, mod_ref, gate_ref, g_ref, rows_hbm, y_ref, buf, sems):
    i = pl.program_id(0)
    n = pl.num_programs(0)
    tm = x1_ref.shape[0]

    def issue(blk, slot):
        def body(r, carry):
            base = (blk * tm + r) * TOP_K
            for j in range(TOP_K):
                pltpu.make_async_copy(rows_hbm.at[pl.ds(dest_ref[base + j], 1)], buf.at[slot, j, pl.ds(r, 1)],
                                      sems.at[slot]).start()
            return carry
        lax.fori_loop(0, tm, body, 0)

    @pl.when(i == 0)
    def _():
        issue(0, 0)

    @pl.when(i + 1 < n)
    def _():
        issue(i + 1, (i + 1) % 2)

    slot = i % 2
    for j in range(TOP_K):
        pltpu.make_async_copy(rows_hbm.at[pl.ds(0, tm)], buf.at[slot, j], sems.at[slot]).wait()
    gates = gate_ref[...]
    moe = buf[slot, 0] * gates[:, 0:1]
    for j in range(1, TOP_K):
        moe = moe + buf[slot, j] * gates[:, j:j + 1]
    x2 = x1_ref[...] + mod_ref[:, 5 * D_MODEL:6 * D_MODEL] * moe
    ms = jnp.mean(x2 * x2, axis=-1, keepdims=True)
    y_ref[...] = x2 * lax.rsqrt(ms + NORM_EPS) * g_ref[...]


def _final(dest_flat, x1, mod3, gates, rows_out, g, blocks_per_mod):
    t = x1.shape[0]
    tm = TOK_BLOCK
    mod_rows = mod3.shape[1]
    row = lambda n: pl.BlockSpec((tm, n), lambda i, d: (i, 0))
    grid_spec = pltpu.PrefetchScalarGridSpec(
        num_scalar_prefetch=1,
        grid=(t // tm,),
        in_specs=[row(D_MODEL),
                  pl.BlockSpec((None, mod_rows, 6 * D_MODEL), lambda i, d: (i // blocks_per_mod, 0, 0)),
                  row(LANES),
                  pl.BlockSpec((1, D_MODEL), lambda i, d: (0, 0)),
                  pl.BlockSpec(memory_space=pl.ANY)],
        out_specs=row(D_MODEL),
        scratch_shapes=[pltpu.VMEM((2, TOP_K, tm, D_MODEL), F32), pltpu.SemaphoreType.DMA((2,))])
    return pl.pallas_call(
        _final_kernel,
        grid_spec=grid_spec,
        out_shape=jax.ShapeDtypeStruct((t, D_MODEL), F32),
        compiler_params=_cparams("arbitrary"),
        name="final",
    )(dest_flat, x1, mod3, gates, g, rows_out)


def _rope_tables(pos, rot_dim, theta, head_dim):
    half = rot_dim // 2
    inv = theta ** (-jnp.arange(half, dtype=F32) * (2.0 / rot_dim))
    ang = pos.astype(F32)[:, None] * inv[None, :]
    cos, sin = jnp.cos(ang), jnp.sin(ang)
    n = pos.shape[0]
    rest = head_dim - rot_dim
    zh = jnp.zeros((n, half), F32)
    c = jnp.concatenate([cos, cos, jnp.ones((n, rest), F32)], axis=1)
    sa = jnp.concatenate([-sin, zh, jnp.zeros((n, rest), F32)], axis=1)
    sb = jnp.concatenate([zh, sin, jnp.zeros((n, rest), F32)], axis=1)
    rep = LANES // head_dim
    return tuple(jnp.tile(a, (1, rep)) for a in (c, sa, sb))


def _pack_w_in(w_in):
    offs = np.cumsum((0,) + IN_SPLITS)
    part = lambda j: w_in[:, offs[j]:offs[j + 1]]
    zero = lambda n: jnp.zeros((D_MODEL, n), w_in.dtype)
    group = ATT_HEADS // ATT_KV_HEADS
    cols = []
    wq = part(0)
    for h in range(ATT_HEADS):
        wh = wq[:, h * HEAD_DIM:(h + 1) * HEAD_DIM]
        cols += [wh, zero(HEAD_DIM)] if h // group == 0 else [zero(HEAD_DIM), wh]
    cols += [part(1), part(2)]
    wiq = part(3)
    for h in range(IDX_HEADS):
        cols += [wiq[:, h * IDX_DIM:(h + 1) * IDX_DIM], zero(LANES - IDX_DIM)]
    cols += [part(4), part(5), zero(LANES - IDX_DIM - IDX_HEADS)]
    cols += [part(j) for j in range(6, 12)]
    return jnp.concatenate(cols, axis=1).astype(BF16)


def _heads_major(a, db, t, width):
    heads = a.shape[1] // width
    return a.reshape(db, t, heads, width).transpose(0, 2, 1, 3).reshape(db, heads * t, width)


def _pad_rows(a, db, t, rows):
    return jnp.pad(a.reshape(db, t, -1), ((0, 0), (0, rows - t), (0, 0)))


def kernel(x_prompt, x_sample, cache_k, cache_v, cache_ik, state_ret, page_table, c_prompt, c_sample, norm_mix_g, norm_ffn_g, norm_final_g, w_ada, b_ada, w_in, w_branch_a, w_branch_b, w_out, w_router, b_router, w_gate_up, b_gate_up, w_down, b_down):
    batch, seq, _ = x_prompt.shape
    db, dt, _ = x_sample.shape
    assert w_in.shape[0] == 1, "one layer"
    tp, ts = batch * seq, db * dt
    xp = x_prompt.reshape(tp, D_MODEL)
    xs = x_sample.reshape(ts, D_MODEL)

    mod = _adaln(jnp.concatenate([c_prompt, c_sample], axis=0), w_ada[0], b_ada[0])
    mod_p = mod[:batch].reshape(batch, 1, 6 * D_MODEL)
    mod_s = jnp.repeat(mod[batch:], dt, axis=0).reshape(ts // TOK_BLOCK, TOK_BLOCK, 6 * D_MODEL)
    bpm_p = seq // TOK_BLOCK

    w_packed = _pack_w_in(w_in[0])
    pos_p = jnp.arange(seq)
    pos_s = PAST_LEN + (jnp.arange(TOK_BLOCK) % dt)
    g_mix = norm_mix_g[0].reshape(1, D_MODEL)
    outs_p = _inproj(xp, mod_p, g_mix, w_packed, _rope_tables(pos_p, ROPE_DIM, ROPE_THETA, HEAD_DIM),
                     _rope_tables(pos_p, RET_DK, RET_THETA, RET_DK), bpm_p, seq // TOK_BLOCK)
    outs_s = _inproj(xs, mod_s, g_mix, w_packed, _rope_tables(pos_s, ROPE_DIM, ROPE_THETA, HEAD_DIM),
                     _rope_tables(pos_s, RET_DK, RET_THETA, RET_DK), 1, 1)
    (q_p, k_p, v_p, kb_p, vb_p, iq_p, ikw_p, ikb_p, rq_p, rk_p, rv_p, sg_p, sga_p, sgb_p) = outs_p
    (q_s, k_s, v_s, kb_s, vb_s, iq_s, ikw_s, ikb_s, rq_s, rk_s, rv_s, sg_s, sga_s, sgb_s) = outs_s

    tri = (jnp.arange(LANES)[:, None] < jnp.arange(LANES)[None, :]).astype(BF16)

    a_p = _dsa_prompt(q_p, iq_p, ikw_p, kb_p, vb_p, ikb_p, tri, batch, seq)
    qs = _heads_major(q_s, db, dt, LANES)
    iqs = _heads_major(iq_s, db, dt, LANES)[:, :, :IDX_DIM]
    ws = _heads_major(ikw_s[:, IDX_DIM:IDX_DIM + IDX_HEADS] * (IDX_HEADS ** -0.5), db, dt, 1)
    o_s = _dsa_sample(page_table, qs, iqs, ws, _pad_rows(kb_s, db, dt, LANES), _pad_rows(vb_s, db, dt, LANES),
                      _pad_rows(ikb_s[:, :IDX_DIM], db, dt, LANES), cache_ik[0],
                      cache_k[0].reshape(-1, PAGE_SIZE, LANES), cache_v[0].reshape(-1, PAGE_SIZE, LANES), tri, dt)
    group = ATT_HEADS // ATT_KV_HEADS
    o_s = o_s.reshape(db, ATT_HEADS, dt, ATT_KV_HEADS, HEAD_DIM)
    a_s = jnp.stack([o_s[:, h, :, h // group, :] for h in range(ATT_HEADS)], axis=2)
    a_s = a_s.reshape(ts, ATT_OUT).astype(BF16)

    r_p, st_p = _retention(rq_p, rk_p, rv_p, sg_p, _retention_tables(float(RET_CHUNK)), None, batch,
                           seq // RET_CHUNK)
    pad = lambda a: _pad_rows(a, db, dt, RET_CHUNK).reshape(db * RET_CHUNK, -1)
    r_s, st_s = _retention(pad(rq_s), pad(rk_s), pad(rv_s), pad(sg_s), _retention_tables(float(dt)),
                           state_ret[0].reshape(db, RET_HEADS // 2, LANES, LANES), db, 1)
    r_s = r_s.reshape(db, RET_CHUNK, RET_OUT)[:, :dt].reshape(ts, RET_OUT)

    wr = jnp.pad(w_router[0], ((0, 0), (0, LANES - N_EXPERTS)))
    br = jnp.concatenate([b_router[0], jnp.full((LANES - N_EXPERTS,), -jnp.inf, F32)]).reshape(1, LANES)
    post_w = (w_branch_a[0].astype(BF16), w_branch_b[0].astype(BF16), w_out[0].astype(BF16),
              norm_ffn_g[0].reshape(1, D_MODEL), wr, br)
    ar = jnp.arange(TOK_BLOCK)
    ltri = (ar[None, :] < ar[:, None]).astype(BF16)
    x1_p, h2w_p, idx_p, gate_p, rank_p, cnt_p = _post(a_p, r_p, sga_p, sgb_p, xp, mod_p, *post_w, ltri,
                                                      jnp.zeros((1, LANES), F32), bpm_p)
    x1_s, h2w_s, idx_s, gate_s, rank_s, cnt_all = _post(a_s, r_s, sga_s, sgb_s, xs, mod_s, *post_w, ltri, cnt_p, 1)

    n_asg = (tp + ts) * TOP_K
    counts = cnt_all[0, :N_EXPERTS].astype(I32)
    padded = (counts + MOE_ROWS - 1) // MOE_ROWS * MOE_ROWS
    pend = jnp.cumsum(padded)
    pstart = pend - padded
    dest_p = (pstart[idx_p[:, :TOP_K]] + rank_p[:, :TOP_K]).reshape(-1)
    dest_s = (pstart[idx_s[:, :TOP_K]] + rank_s[:, :TOP_K]).reshape(-1)
    n_blocks = -(-n_asg // MOE_ROWS) + N_EXPERTS
    blk_start = jnp.arange(n_blocks, dtype=I32) * MOE_ROWS
    blk_expert = jnp.minimum(jnp.sum((blk_start[:, None] >= pend[None, :]).astype(I32), axis=1), N_EXPERTS - 1)
    blk_first = jnp.concatenate([jnp.ones((1,), I32), (blk_expert[1:] != blk_expert[:-1]).astype(I32)])
    n_used = (pend[-1] // MOE_ROWS).astype(I32).reshape(1)
    x_sorted = jnp.zeros((n_blocks * MOE_ROWS, D_MODEL // 2), jnp.uint32)
    x_sorted = _dispatch(dest_p, h2w_p, x_sorted)
    x_sorted = _dispatch(dest_s, h2w_s, x_sorted)
    rows_out = _moe(blk_expert, blk_first, n_used, x_sorted, w_gate_up[0], b_gate_up[0], w_down[0], b_down[0])

    g_final = norm_final_g.reshape(1, D_MODEL)
    y_p = _final(dest_p, x1_p, mod_p, gate_p, rows_out, g_final, bpm_p)
    y_s = _final(dest_s, x1_s, mod_s, gate_s, rows_out, g_final, 1)

    kv_shape = lambda b, s: (1, b, s, ATT_KV_HEADS, HEAD_DIM)
    st_shape = lambda b: (1, b, RET_HEADS, RET_DK, RET_DV)
    return (y_p.reshape(batch, seq, D_MODEL), y_s.reshape(db, dt, D_MODEL),
            k_p.reshape(kv_shape(batch, seq)), v_p.reshape(kv_shape(batch, seq)),
            ikw_p[:, :IDX_DIM].reshape(1, batch, seq, IDX_DIM), st_p.reshape(st_shape(batch)),
            k_s.reshape(kv_shape(db, dt)), v_s.reshape(kv_shape(db, dt)),
            ikw_s[:, :IDX_DIM].reshape(1, db, dt, IDX_DIM), st_s.reshape(st_shape(db)))
```

```python
import functools

import jax
import jax.numpy as jnp
import numpy as np
from jax import lax
from jax.experimental import pallas as pl
from jax.experimental.pallas import tpu as pltpu

F32 = jnp.float32
BF16 = jnp.bfloat16
I32 = jnp.int32

D_MODEL = 1024
PAST_LEN = 8192
PAGE_SIZE = 128
ATT_HEADS = 8
ATT_KV_HEADS = 2
HEAD_DIM = 64
ROPE_DIM = HEAD_DIM // 4
ROPE_THETA = 500000.0
IDX_HEADS = 8
IDX_DIM = 64
IDX_ROPE_DIM = IDX_DIM // 4
TOPK_MAX = 256
RET_HEADS = 8
RET_DK = 64
RET_DV = 128
RET_THETA = 10000.0
RET_CHUNK = 128
N_EXPERTS = 32
TOP_K = 4
D_FF = D_MODEL
SWIGLU_LIMIT = 7.0
SWIGLU_ALPHA = 1.702
NORM_EPS = 1e-6
GN_EPS = 1e-5
ATT_OUT = ATT_HEADS * HEAD_DIM
RET_OUT = RET_HEADS * RET_DV
IN_SPLITS = (ATT_HEADS * HEAD_DIM, ATT_KV_HEADS * HEAD_DIM, ATT_KV_HEADS * HEAD_DIM,
             IDX_HEADS * IDX_DIM, IDX_DIM, IDX_HEADS,
             RET_HEADS * RET_DK, RET_HEADS * RET_DK, RET_OUT, RET_OUT, D_MODEL, D_MODEL)

LANES = 128
MASK_NEG = -1e30
FLT_MAX = 3.4028234663852886e38
SELECT_UNROLL = 4
CAUSAL_VARIANTS = 4
IDX_KEY_CHUNK = 256
VMEM_LIMIT = 56 * 1024 * 1024

TOK_BLOCK = 256
Q_BLOCK = 128
MOE_ROWS = 512

_W_GROUPS = (("q", 1024), ("iq", 1024), ("kvi", 384), ("rq", 512), ("rk", 512),
             ("rv", 1024), ("rg", 1024), ("ga", 1024), ("gb", 1024))
PROJ_COLS = 512
_W_OFF = {}
_off = 0
for _n, _w in _W_GROUPS:
    _W_OFF[_n] = (_off, _w)
    _off += _w
W_COLS = _off


def _cparams(*sem):
    return pltpu.CompilerParams(dimension_semantics=sem, vmem_limit_bytes=VMEM_LIMIT)


def _adaln_kernel(c_ref, w_ref, b_ref, o_ref):
    c = c_ref[...]
    s = c * jax.nn.sigmoid(c)
    o_ref[...] = jnp.dot(s, w_ref[...], preferred_element_type=F32, precision=lax.Precision.HIGHEST) + b_ref[...]


def _adaln(c_all, w_ada, b_ada):
    n = c_all.shape[0]
    nb = 1536
    return pl.pallas_call(
        _adaln_kernel,
        grid=(6 * D_MODEL // nb,),
        in_specs=[pl.BlockSpec((n, D_MODEL), lambda j: (0, 0)),
                  pl.BlockSpec((D_MODEL, nb), lambda j: (0, j)),
                  pl.BlockSpec((1, nb), lambda j: (0, j))],
        out_specs=pl.BlockSpec((n, nb), lambda j: (0, j)),
        out_shape=jax.ShapeDtypeStruct((n, 6 * D_MODEL), F32),
        compiler_params=_cparams("arbitrary"),
        name="adaln",
    )(c_all, w_ada, b_ada.reshape(1, -1))


def _rope_slab(z, c, sa, sb, half):
    return z * c + pltpu.roll(z, LANES - half, 1) * sa + pltpu.roll(z, half, 1) * sb


def _inproj_kernel(x_ref, mod_ref, g_ref, w_ref, ca_ref, saa_ref, sba_ref, cr_ref, sar_ref, sbr_ref,
                   q_ref, k_ref, v_ref, kb_ref, vb_ref, iq_ref, ikw_ref, ikb_ref,
                   rq_ref, rk_ref, rv_ref, sg_ref, sga_ref, sgb_ref):
    x = x_ref[...]
    ms = jnp.mean(x * x, axis=-1, keepdims=True)
    y = x * lax.rsqrt(ms + NORM_EPS) * g_ref[...]
    h = (y * (1.0 + mod_ref[:, D_MODEL:2 * D_MODEL]) + mod_ref[:, 0:D_MODEL]).astype(BF16)

    def slabs(name):
        c0, width = _W_OFF[name]
        step = min(width, PROJ_COLS)
        for j in range(width // step):
            z = jnp.dot(h, w_ref[:, c0 + j * step:c0 + (j + 1) * step], preferred_element_type=F32)
            for s in range(step // LANES):
                yield j * (step // LANES) + s, z[:, s * LANES:(s + 1) * LANES]

    ca, saa, sba = ca_ref[...], saa_ref[...], sba_ref[...]
    cr, sar, sbr = cr_ref[...], sar_ref[...], sbr_ref[...]
    att_half, ret_half = ROPE_DIM // 2, RET_DK // 2
    lane = lax.broadcasted_iota(I32, (x.shape[0], LANES), 1)
    sl = lambda s: slice(s * LANES, (s + 1) * LANES)

    for s, z in slabs("q"):
        q_ref[:, sl(s)] = (_rope_slab(z, ca, saa, sba, att_half) * 0.125).astype(BF16)
    for s, z in slabs("iq"):
        iq_ref[:, sl(s)] = (_rope_slab(z, ca, saa, sba, att_half) * 0.125).astype(BF16)
    (_, zk), (_, zv), (_, zi) = slabs("kvi")
    kk = _rope_slab(zk, ca, saa, sba, att_half)
    k_ref[...] = kk
    kb_ref[...] = kk.astype(BF16)
    v_ref[...] = zv
    vb_ref[...] = zv.astype(BF16)
    zr = _rope_slab(zi, ca, saa, sba, att_half)
    ikw_ref[...] = jnp.where(lane < IDX_DIM, zr, zi)
    ikb_ref[...] = jnp.where(lane < IDX_DIM, zr, 0.0).astype(BF16)
    for s, z in slabs("rq"):
        rq_ref[:, sl(s)] = _rope_slab(z, cr, sar, sbr, ret_half).astype(BF16)
    for s, z in slabs("rk"):
        rk_ref[:, sl(s)] = (_rope_slab(z, cr, sar, sbr, ret_half) * 0.125).astype(BF16)
    for s, z in slabs("rv"):
        rv_ref[:, sl(s)] = z.astype(BF16)
    for s, z in slabs("rg"):
        sg_ref[:, sl(s)] = (z * jax.nn.sigmoid(z)).astype(BF16)
    for s, z in slabs("ga"):
        sga_ref[:, sl(s)] = jax.nn.sigmoid(z).astype(BF16)
    for s, z in slabs("gb"):
        sgb_ref[:, sl(s)] = jax.nn.sigmoid(z).astype(BF16)


def _inproj(x, mod3, g, w_packed, tabs_att, tabs_ret, blocks_per_mod, tab_blocks):
    t = x.shape[0]
    tm = TOK_BLOCK
    nblk = t // tm
    mod_rows = mod3.shape[1]
    tab_spec = pl.BlockSpec((tm, LANES), lambda i: (i % tab_blocks, 0))
    row = lambda n: pl.BlockSpec((tm, n), lambda i: (i, 0))
    out_defs = [(1024, BF16), (128, F32), (128, F32), (128, BF16), (128, BF16), (1024, BF16), (128, F32),
                (128, BF16), (512, BF16), (512, BF16), (1024, BF16), (1024, BF16), (1024, BF16), (1024, BF16)]
    return pl.pallas_call(
        _inproj_kernel,
        grid=(nblk,),
        in_specs=[row(D_MODEL),
                  pl.BlockSpec((None, mod_rows, 6 * D_MODEL), lambda i: (i // blocks_per_mod, 0, 0)),
                  pl.BlockSpec((1, D_MODEL), lambda i: (0, 0)),
                  pl.BlockSpec((D_MODEL, W_COLS), lambda i: (0, 0))] + [tab_spec] * 6,
        out_specs=[row(n) for n, _ in out_defs],
        out_shape=[jax.ShapeDtypeStruct((t, n), d) for n, d in out_defs],
        compiler_params=_cparams("parallel"),
        name="inproj",
    )(x, mod3, g, w_packed, *tabs_att, *tabs_ret)


def _count(score_ref, n, pred):
    acc = jnp.zeros((score_ref.shape[0], LANES), F32)
    for c in range(n // LANES):
        acc = acc + jnp.where(pred(score_ref[:, c * LANES:(c + 1) * LANES]), 1.0, 0.0)
    return jnp.sum(acc, axis=1, keepdims=True)


def _kth_largest(score_ref, n, k):
    sc = score_ref[:, :n]
    finite = sc > -jnp.inf
    n_fin = jnp.sum(jnp.where(finite, 1.0, 0.0), axis=1, keepdims=True)
    n_pos = jnp.sum(jnp.where(sc > 0.0, 1.0, 0.0), axis=1, keepdims=True)
    n_nonneg = jnp.sum(jnp.where(sc >= 0.0, 1.0, 0.0), axis=1, keepdims=True)
    mx = jnp.max(sc, axis=1, keepdims=True)
    mn = jnp.min(jnp.where(finite, sc, jnp.inf), axis=1, keepdims=True)
    small = n_fin <= k
    positive = n_pos >= k
    at_zero = jnp.logical_and(jnp.logical_not(positive), n_nonneg >= k)
    lo = jnp.where(positive, 0.0, mn)
    hi = jnp.where(positive, mx + (jnp.abs(mx) * 2.0 ** -20 + 2.0 ** -100), 0.0)
    lo = jnp.where(at_zero, 0.0, lo)
    done = jnp.where(jnp.logical_or(small, at_zero), 1.0, 0.0)

    def cond(state):
        return jnp.min(state[2]) < 0.5

    def body(state):
        lo, hi, done = state
        for _ in range(SELECT_UNROLL):
            mid = 0.5 * lo + 0.5 * hi
            cnt = _count(score_ref, n, lambda s: s >= mid)
            stuck = jnp.logical_or(mid <= lo, mid >= hi)
            live = jnp.logical_and(done < 0.5, jnp.logical_not(stuck))
            ge = cnt >= k
            lo = jnp.where(jnp.logical_and(live, ge), mid, lo)
            hi = jnp.where(jnp.logical_and(live, jnp.logical_not(ge)), mid, hi)
            done = jnp.where(jnp.logical_or(stuck, cnt == k), 1.0, done)
        return lo, hi, done

    lo, _, _ = lax.while_loop(cond, body, (lo, hi, done))
    return jnp.where(small, -FLT_MAX, lo)


def _topk_bias(score_ref, bias_ref, tri_ref, n, k):
    rows = score_ref.shape[0]
    thr = _kth_largest(score_ref, n, k)
    need = k - _count(score_ref, n, lambda s: s > thr)
    n_eq = _count(score_ref, n, lambda s: s == thr)
    has_ties = jnp.max(jnp.where(need < n_eq, 1.0, 0.0)) > 0.5

    @pl.when(jnp.logical_not(has_ties))
    def _():
        for c in range(n // LANES):
            sl = slice(c * LANES, (c + 1) * LANES)
            bias_ref[:, sl] = jnp.where(score_ref[:, sl] >= thr, 0.0, MASK_NEG)

    @pl.when(has_ties)
    def _():
        run = jnp.zeros((rows, 1), F32)
        for c in range(n // LANES):
            sl = slice(c * LANES, (c + 1) * LANES)
            sc = score_ref[:, sl]
            eq = sc == thr
            eqf = jnp.where(eq, 1.0, 0.0)
            before = jnp.dot(eqf.astype(BF16), tri_ref[...], preferred_element_type=F32) + run
            take = jnp.logical_or(sc > thr, jnp.logical_and(eq, before < need))
            bias_ref[:, sl] = jnp.where(take, 0.0, MASK_NEG)
            run = run + jnp.sum(eqf, axis=1, keepdims=True)


def _dsa_prompt_kernel(q_ref, iq_ref, ikw_ref, kb_ref, vb_ref, ikb_ref, tri_ref, o_ref, score_ref, bias_ref):
    qb, s_len = score_ref.shape
    i = pl.program_id(1)
    nqb = s_len // qb
    per_variant = nqb // CAUSAL_VARIANTS
    for v in range(CAUSAL_VARIANTS):
        pl.when(i // per_variant == v)(
            functools.partial(_dsa_prompt_body, q_ref, iq_ref, ikw_ref, kb_ref, vb_ref, ikb_ref, tri_ref, o_ref,
                              score_ref, bias_ref, (v + 1) * per_variant * qb, min(TOPK_MAX, s_len // 4)))


def _dsa_prompt_body(q_ref, iq_ref, ikw_ref, kb_ref, vb_ref, ikb_ref, tri_ref, o_ref, score_ref, bias_ref,
                     n_keys, topk):
    qb = score_ref.shape[0]
    i = pl.program_id(1)
    w = ikw_ref[:, IDX_DIM:IDX_DIM + IDX_HEADS] * (IDX_HEADS ** -0.5)
    nt = (((1,), (1,)), ((), ()))
    kc = IDX_KEY_CHUNK
    qpos = i * qb + lax.broadcasted_iota(I32, (qb, kc), 0)
    for c in range(n_keys // kc):
        ikc = ikb_ref[c * kc:(c + 1) * kc, :]
        acc = jnp.zeros((qb, kc), F32)
        for h in range(IDX_HEADS):
            d = lax.dot_general(iq_ref[:, h * LANES:(h + 1) * LANES], ikc, nt, preferred_element_type=F32)
            acc = acc + jnp.maximum(d, 0.0) * w[:, h:h + 1]
        kpos = c * kc + lax.broadcasted_iota(I32, (qb, kc), 1)
        score_ref[:, c * kc:(c + 1) * kc] = jnp.where(kpos <= qpos, acc, -jnp.inf)

    _topk_bias(score_ref, bias_ref, tri_ref, n_keys, topk)

    kb = kb_ref[0:n_keys, :]
    vb = vb_ref[0:n_keys, :]
    bias = bias_ref[:, 0:n_keys]
    lane = lax.broadcasted_iota(I32, (qb, LANES), 1)
    heads = []
    for h in range(ATT_HEADS):
        s = lax.dot_general(q_ref[:, h * LANES:(h + 1) * LANES], kb, nt, preferred_element_type=F32) + bias
        m = jnp.max(s, axis=1, keepdims=True)
        p = jnp.exp(s - m)
        l = jnp.sum(p, axis=1, keepdims=True)
        heads.append(jnp.dot(p.astype(BF16), vb, preferred_element_type=F32) / l)
    group = ATT_HEADS // ATT_KV_HEADS
    for pp in range(ATT_HEADS // 2):
        a, b = heads[2 * pp], heads[2 * pp + 1]
        if (2 * pp) // group == 0:
            slab = jnp.where(lane < HEAD_DIM, a, pltpu.roll(b, HEAD_DIM, 1))
        else:
            slab = jnp.where(lane < HEAD_DIM, pltpu.roll(a, HEAD_DIM, 1), b)
        o_ref[:, pp * LANES:(pp + 1) * LANES] = slab.astype(BF16)


def _dsa_prompt(q, iq, ikw, kb, vb, ikb, tri, batch, seq):
    nqb = seq // Q_BLOCK
    qrow = lambda n: pl.BlockSpec((Q_BLOCK, n), lambda b, i: (b * nqb + i, 0))
    keys = pl.BlockSpec((seq, LANES), lambda b, i: (b, 0))
    return pl.pallas_call(
        _dsa_prompt_kernel,
        grid=(batch, nqb),
        in_specs=[qrow(1024), qrow(1024), qrow(LANES), keys, keys, keys,
                  pl.BlockSpec((LANES, LANES), lambda b, i: (0, 0))],
        out_specs=qrow(ATT_OUT),
        out_shape=jax.ShapeDtypeStruct((batch * seq, ATT_OUT), BF16),
        scratch_shapes=[pltpu.VMEM((Q_BLOCK, seq), F32), pltpu.VMEM((Q_BLOCK, seq), F32)],
        compiler_params=_cparams("parallel", "arbitrary"),
        name="dsa_prompt",
    )(q, iq, ikw, kb, vb, ikb, tri)


def _dsa_sample_kernel(pt_ref, qs_ref, iqs_ref, ws_ref, knew_ref, vnew_ref, iknew_ref, cik_hbm, ck_hbm, cv_hbm,
                       tri_ref, o_ref, ikbuf, kbuf, vbuf, sems, key_ref, bias_ref):
    db = pl.program_id(0)
    n_pages = ikbuf.shape[1]
    t = key_ref.shape[0]
    n_past = n_pages * PAGE_SIZE
    bufs = ((cik_hbm, ikbuf), (ck_hbm, kbuf), (cv_hbm, vbuf))

    def fetch(req, slot):
        def body(p, carry):
            pg = pt_ref[req, p]
            for j, (src, dst) in enumerate(bufs):
                pltpu.make_async_copy(src.at[pg], dst.at[slot, p], sems.at[slot, j]).start()
            return carry
        lax.fori_loop(0, n_pages, body, 0)

    @pl.when(db == 0)
    def _():
        fetch(0, 0)

    @pl.when(db + 1 < pl.num_programs(0))
    def _():
        fetch(db + 1, (db + 1) % 2)

    slot = db % 2
    for j, (src, dst) in enumerate(bufs):
        pltpu.make_async_copy(src.at[pl.ds(0, n_pages)], dst.at[slot], sems.at[slot, j]).wait()

    nt = (((1,), (1,)), ((), ()))
    ik_past = ikbuf[slot].reshape(n_past, IDX_DIM)
    iqs = iqs_ref[...]
    wcol = ws_ref[...]
    d_past = jnp.maximum(lax.dot_general(iqs, ik_past, nt, preferred_element_type=F32), 0.0) * wcol
    d_new = jnp.maximum(lax.dot_general(iqs, iknew_ref[...], nt, preferred_element_type=F32), 0.0) * wcol
    s_past = d_past[0:t]
    s_new = d_new[0:t]
    for h in range(1, IDX_HEADS):
        s_past = s_past + d_past[h * t:(h + 1) * t]
        s_new = s_new + d_new[h * t:(h + 1) * t]
    row = lax.broadcasted_iota(I32, (t, LANES), 0)
    lane = lax.broadcasted_iota(I32, (t, LANES), 1)
    new_ok = lane <= row
    key_ref[:, 0:n_past] = s_past
    key_ref[:, n_past:n_past + LANES] = jnp.where(new_ok, s_new, -jnp.inf)
    _topk_bias(key_ref, bias_ref, tri_ref, n_past + LANES, min(TOPK_MAX, (n_past + t) // 4))

    qs = qs_ref[...]
    reps = qs.shape[0] // t
    bias = jnp.concatenate([bias_ref[...]] * reps, axis=0)
    k_past = kbuf[slot].reshape(n_past, LANES)
    v_past = vbuf[slot].reshape(n_past, LANES)
    sp = lax.dot_general(qs, k_past, nt, preferred_element_type=F32) + bias[:, 0:n_past]
    sn = lax.dot_general(qs, knew_ref[...], nt, preferred_element_type=F32) + bias[:, n_past:n_past + LANES]
    m = jnp.maximum(jnp.max(sp, axis=1, keepdims=True), jnp.max(sn, axis=1, keepdims=True))
    pp = jnp.exp(sp - m)
    pn = jnp.exp(sn - m)
    l = jnp.sum(pp, axis=1, keepdims=True) + jnp.sum(pn, axis=1, keepdims=True)
    o = (jnp.dot(pp.astype(BF16), v_past, preferred_element_type=F32)
         + jnp.dot(pn.astype(BF16), vnew_ref[...], preferred_element_type=F32))
    o_ref[...] = o / l


def _dsa_sample(page_table, qs, iqs, ws, knew, vnew, iknew, cache_ik, cache_k, cache_v, tri, t):
    db, n_pages = page_table.shape
    rows = qs.shape[1]
    per_db = lambda r, n: pl.BlockSpec((None, r, n), lambda b, pt: (b, 0, 0))
    any_spec = pl.BlockSpec(memory_space=pl.ANY)
    n_keys = n_pages * PAGE_SIZE + LANES
    grid_spec = pltpu.PrefetchScalarGridSpec(
        num_scalar_prefetch=1,
        grid=(db,),
        in_specs=[per_db(rows, LANES), per_db(rows, IDX_DIM), per_db(rows, 1), per_db(LANES, LANES),
                  per_db(LANES, LANES), per_db(LANES, IDX_DIM), any_spec, any_spec, any_spec,
                  pl.BlockSpec((LANES, LANES), lambda b, pt: (0, 0))],
        out_specs=per_db(rows, LANES),
        scratch_shapes=[pltpu.VMEM((2, n_pages, PAGE_SIZE, IDX_DIM), BF16),
                        pltpu.VMEM((2, n_pages, PAGE_SIZE, LANES), BF16),
                        pltpu.VMEM((2, n_pages, PAGE_SIZE, LANES), BF16),
                        pltpu.SemaphoreType.DMA((2, 3)),
                        pltpu.VMEM((t, n_keys), F32),
                        pltpu.VMEM((t, n_keys), F32)])
    return pl.pallas_call(
        _dsa_sample_kernel,
        grid_spec=grid_spec,
        out_shape=jax.ShapeDtypeStruct((db, rows, LANES), F32),
        compiler_params=_cparams("arbitrary"),
        name="dsa_sample",
    )(page_table, qs, iqs, ws, knew, vnew, iknew, cache_ik, cache_k, cache_v, tri)


def _retention_kernel(has_init, rq_ref, rk_ref, rv_ref, sg_ref, decay_ref, qdec_ref, kdec_ref, gst_ref, *rest):
    if has_init:
        init_ref, o_ref, st_ref, state = rest
    else:
        o_ref, st_ref, state = rest
    c = pl.program_id(1)

    @pl.when(c == 0)
    def _():
        if has_init:
            state[...] = init_ref[...]
        else:
            state[...] = jnp.zeros_like(state)

    nt = (((1,), (1,)), ((), ()))
    tn = (((0,), (0,)), ((), ()))
    rows = rq_ref.shape[0]
    lane = lax.broadcasted_iota(I32, (rows, LANES), 1)
    for p in range(RET_HEADS // 2):
        sl = slice(p * LANES, (p + 1) * LANES)
        qp = rq_ref[:, sl]
        kp = rk_ref[:, sl]
        qd = (qp.astype(F32) * qdec_ref[:, sl]).astype(BF16)
        kd = (kp.astype(F32) * kdec_ref[:, sl]).astype(BF16)
        s_old = state[p]
        s_old_b = s_old.astype(BF16)
        s_new = s_old * gst_ref[p]
        for e in range(2):
            h = 2 * p + e
            hs = slice(h * LANES, (h + 1) * LANES)
            mine = jnp.where((lane >= e * RET_DK) & (lane < (e + 1) * RET_DK), 1.0, 0.0).astype(BF16)
            sc = lax.dot_general(qp * mine, kp, nt, preferred_element_type=F32) * decay_ref[h]
            vh = rv_ref[:, hs]
            o = (jnp.dot(sc.astype(BF16), vh, preferred_element_type=F32)
                 + jnp.dot(qd * mine, s_old_b, preferred_element_type=F32))
            s_new = s_new + lax.dot_general(kd * mine, vh, tn, preferred_element_type=F32)
            mu = jnp.mean(o, axis=-1, keepdims=True)
            var = jnp.mean(jnp.square(o - mu), axis=-1, keepdims=True)
            on = (o - mu) * lax.rsqrt(var + GN_EPS)
            o_ref[:, hs] = (on * sg_ref[:, hs].astype(F32)).astype(BF16)
        state[p] = s_new

    @pl.when(c == pl.num_programs(1) - 1)
    def _():
        st_ref[...] = state[...]


def _retention(rq, rk, rv, sg, tables, init, batch, n_chunks):
    decay, qdec, kdec, gst = tables
    cr = RET_CHUNK
    rowspec = lambda n: pl.BlockSpec((cr, n), lambda b, c: (b * n_chunks + c, 0))
    const = lambda shape: pl.BlockSpec(shape, lambda b, c: (0,) * len(shape))
    st_spec = pl.BlockSpec((None, RET_HEADS // 2, LANES, LANES), lambda b, c: (b, 0, 0, 0))
    in_specs = [rowspec(512), rowspec(512), rowspec(RET_OUT), rowspec(RET_OUT),
                const(decay.shape), const(qdec.shape), const(kdec.shape), const(gst.shape)]
    args = [rq, rk, rv, sg, decay, qdec, kdec, gst]
    if init is not None:
        in_specs.append(st_spec)
        args.append(init)
    return pl.pallas_call(
        functools.partial(_retention_kernel, init is not None),
        grid=(batch, n_chunks),
        in_specs=in_specs,
        out_specs=[rowspec(RET_OUT), st_spec],
        out_shape=[jax.ShapeDtypeStruct((batch * n_chunks * cr, RET_OUT), BF16),
                   jax.ShapeDtypeStruct((batch, RET_HEADS // 2, LANES, LANES), F32)],
        scratch_shapes=[pltpu.VMEM((RET_HEADS // 2, LANES, LANES), F32)],
        compiler_params=_cparams("parallel", "arbitrary"),
        name="retention",
    )(*args)


def _retention_tables(c_eff):
    lg = jnp.log(1.0 - 2.0 ** (-5.0 - jnp.arange(RET_HEADS, dtype=F32)))
    i = jnp.arange(RET_CHUNK, dtype=F32)
    diff = i[:, None] - i[None, :]
    decay = jnp.where(diff >= 0, jnp.exp(jnp.maximum(diff, 0.0)[None] * lg[:, None, None]), 0.0)
    q_decay = jnp.exp((i + 1.0)[:, None] * lg[None, :])
    k_decay = jnp.exp((c_eff - 1.0 - i)[:, None] * lg[None, :])
    qdec = jnp.repeat(q_decay, RET_DK, axis=1)
    kdec = jnp.repeat(k_decay, RET_DK, axis=1)
    g_state = jnp.exp(c_eff * lg)
    gst = jnp.broadcast_to(jnp.repeat(g_state, RET_DK).reshape(RET_HEADS // 2, LANES, 1),
                           (RET_HEADS // 2, LANES, LANES))
    return decay, qdec, kdec, gst


def _pack_bf16_pairs(lo, hi):
    return pltpu.pack_elementwise([lo, hi], packed_dtype=BF16)


def _unpack_bf16_pairs(words):
    return tuple(pltpu.unpack_elementwise(words, index=j, packed_dtype=BF16, unpacked_dtype=F32).astype(BF16)
                 for j in range(2))


def _post_kernel(a_ref, r_ref, sga_ref, sgb_ref, x_ref, mod_ref, wpa_ref, wpb_ref, wo_ref, g_ref, wr_ref, br_ref,
                 ltri_ref, cnt0_ref, x1_ref, h2w_ref, idx_ref, gate_ref, rank_ref, cnt_ref, cnt):
    @pl.when(pl.program_id(0) == 0)
    def _():
        cnt[...] = cnt0_ref[...]

    pa = jnp.dot(a_ref[...], wpa_ref[...], preferred_element_type=F32)
    pb = jnp.dot(r_ref[...], wpb_ref[...], preferred_element_type=F32)
    merged = sga_ref[...].astype(F32) * pa + sgb_ref[...].astype(F32) * pb
    gt1 = mod_ref[:, 2 * D_MODEL:3 * D_MODEL]
    x1 = x_ref[...] + gt1 * jnp.dot(merged.astype(BF16), wo_ref[...], preferred_element_type=F32)
    x1_ref[...] = x1
    ms = jnp.mean(x1 * x1, axis=-1, keepdims=True)
    y = x1 * lax.rsqrt(ms + NORM_EPS) * g_ref[...]
    h2 = y * (1.0 + mod_ref[:, 4 * D_MODEL:5 * D_MODEL]) + mod_ref[:, 3 * D_MODEL:4 * D_MODEL]
    half = D_MODEL // 2
    h2w_ref[...] = _pack_bf16_pairs(h2[:, :half], h2[:, half:])
    logits = jnp.dot(h2, wr_ref[...], preferred_element_type=F32, precision=lax.Precision.HIGHEST) + br_ref[...]
    lane = lax.broadcasted_iota(I32, logits.shape, 1).astype(F32)
    idx_out = jnp.zeros(logits.shape, F32)
    val_out = jnp.zeros(logits.shape, F32)
    chosen = []
    top = None
    for j in range(TOP_K):
        m = jnp.max(logits, axis=1, keepdims=True)
        am = jnp.min(jnp.where(logits == m, lane, float(LANES)), axis=1, keepdims=True)
        if j == 0:
            top = m
        idx_out = jnp.where(lane == j, am, idx_out)
        val_out = jnp.where(lane == j, jnp.exp(m - top), val_out)
        chosen.append(lane == am)
        logits = jnp.where(chosen[-1], -jnp.inf, logits)
    idx_ref[...] = idx_out.astype(I32)
    gate_ref[...] = val_out / jnp.sum(val_out, axis=1, keepdims=True)
    onehot = jnp.where(chosen[0] | chosen[1] | chosen[2] | chosen[3], 1.0, 0.0)
    before = jnp.dot(ltri_ref[...], onehot.astype(BF16), preferred_element_type=F32) + cnt[...]
    rank_out = jnp.zeros(logits.shape, F32)
    for j in range(TOP_K):
        rj = jnp.sum(jnp.where(chosen[j], before, 0.0), axis=1, keepdims=True)
        rank_out = jnp.where(lane == j, rj, rank_out)
    rank_ref[...] = rank_out.astype(I32)
    cnt[...] = cnt[...] + jnp.sum(onehot, axis=0, keepdims=True)
    cnt_ref[...] = cnt[...]


def _post(a, r, sga, sgb, x, mod3, wpa, wpb, wo, g, wr, br, ltri, cnt0, blocks_per_mod):
    t = x.shape[0]
    tm = TOK_BLOCK
    mod_rows = mod3.shape[1]
    row = lambda n: pl.BlockSpec((tm, n), lambda i: (i, 0))
    const = lambda a_: pl.BlockSpec(a_.shape, lambda i: (0,) * a_.ndim)
    return pl.pallas_call(
        _post_kernel,
        grid=(t // tm,),
        in_specs=[row(ATT_OUT), row(RET_OUT), row(D_MODEL), row(D_MODEL), row(D_MODEL),
                  pl.BlockSpec((None, mod_rows, 6 * D_MODEL), lambda i: (i // blocks_per_mod, 0, 0)),
                  const(wpa), const(wpb), const(wo), const(g), const(wr), const(br), const(ltri), const(cnt0)],
        out_specs=[row(D_MODEL), row(D_MODEL // 2), row(LANES), row(LANES), row(LANES),
                   pl.BlockSpec((1, LANES), lambda i: (0, 0))],
        out_shape=[jax.ShapeDtypeStruct((t, D_MODEL), F32), jax.ShapeDtypeStruct((t, D_MODEL // 2), jnp.uint32),
                   jax.ShapeDtypeStruct((t, LANES), I32), jax.ShapeDtypeStruct((t, LANES), F32),
                   jax.ShapeDtypeStruct((t, LANES), I32), jax.ShapeDtypeStruct((1, LANES), F32)],
        scratch_shapes=[pltpu.VMEM((1, LANES), F32)],
        compiler_params=_cparams("arbitrary"),
        name="post",
    )(a, r, sga, sgb, x, mod3, wpa, wpb, wo, g, wr, br, ltri, cnt0)


def _dispatch_kernel(dest_ref, h2w_ref, xs_in, xs_out, sem):
    del xs_in
    i = pl.program_id(0)
    tm = h2w_ref.shape[0]

    def row_copy(r, d):
        return pltpu.make_async_copy(h2w_ref.at[pl.ds(r, 1)], xs_out.at[pl.ds(d, 1)], sem)

    def start(r, carry):
        base = (i * tm + r) * TOP_K
        for j in range(TOP_K):
            row_copy(r, dest_ref[base + j]).start()
        return carry

    lax.fori_loop(0, tm, start, 0)
    for j in range(TOP_K):
        pltpu.make_async_copy(h2w_ref, xs_out.at[pl.ds(0, tm)], sem).wait()


def _dispatch(dest_flat, h2w, x_sorted):
    t = h2w.shape[0]
    tm = TOK_BLOCK
    grid_spec = pltpu.PrefetchScalarGridSpec(
        num_scalar_prefetch=1,
        grid=(t // tm,),
        in_specs=[pl.BlockSpec((tm, D_MODEL // 2), lambda i, d: (i, 0)), pl.BlockSpec(memory_space=pl.ANY)],
        out_specs=pl.BlockSpec(memory_space=pl.ANY),
        scratch_shapes=[pltpu.SemaphoreType.DMA(())])
    return pl.pallas_call(
        _dispatch_kernel,
        grid_spec=grid_spec,
        out_shape=jax.ShapeDtypeStruct(x_sorted.shape, x_sorted.dtype),
        input_output_aliases={2: 0},
        compiler_params=_cparams("arbitrary"),
        name="dispatch",
    )(dest_flat, h2w, x_sorted)


def _moe_kernel(be_ref, first_ref, nused_ref, x_ref, wgu_ref, bgu_ref, wd_ref, bd_ref, o_ref, wgu_b, wd_b):
    i = pl.program_id(0)

    @pl.when(first_ref[i] == 1)
    def _():
        wgu_b[...] = wgu_ref[...].astype(BF16)
        wd_b[...] = wd_ref[...].astype(BF16)

    @pl.when(i < nused_ref[0])
    def _():
        x = jnp.concatenate(_unpack_bf16_pairs(x_ref[...]), axis=1)
        gu = jnp.dot(x, wgu_b[...], preferred_element_type=F32) + bgu_ref[...]
        g = jnp.minimum(gu[:, :D_FF], SWIGLU_LIMIT)
        u = jnp.clip(gu[:, D_FF:], -SWIGLU_LIMIT, SWIGLU_LIMIT)
        act = (u + 1.0) * (g * jax.nn.sigmoid(SWIGLU_ALPHA * g))
        o_ref[...] = jnp.dot(act.astype(BF16), wd_b[...], preferred_element_type=F32) + bd_ref[...]

    @pl.when(i >= nused_ref[0])
    def _():
        o_ref[...] = jnp.zeros_like(o_ref)


def _moe(blk_expert, blk_first, n_used, x_sorted, w_gate_up, b_gate_up, w_down, b_down):
    n_rows = x_sorted.shape[0]
    grid_spec = pltpu.PrefetchScalarGridSpec(
        num_scalar_prefetch=3,
        grid=(n_rows // MOE_ROWS,),
        in_specs=[pl.BlockSpec((MOE_ROWS, D_MODEL // 2), lambda i, be, bf, nu: (i, 0)),
                  pl.BlockSpec((None, D_MODEL, 2 * D_FF), lambda i, be, bf, nu: (be[i], 0, 0)),
                  pl.BlockSpec((None, 1, 2 * D_FF), lambda i, be, bf, nu: (be[i], 0, 0)),
                  pl.BlockSpec((None, D_FF, D_MODEL), lambda i, be, bf, nu: (be[i], 0, 0)),
                  pl.BlockSpec((None, 1, D_MODEL), lambda i, be, bf, nu: (be[i], 0, 0))],
        out_specs=pl.BlockSpec((MOE_ROWS, D_MODEL), lambda i, be, bf, nu: (i, 0)),
        scratch_shapes=[pltpu.VMEM((D_MODEL, 2 * D_FF), BF16), pltpu.VMEM((D_FF, D_MODEL), BF16)])
    return pl.pallas_call(
        _moe_kernel,
        grid_spec=grid_spec,
        out_shape=jax.ShapeDtypeStruct((n_rows, D_MODEL), F32),
        compiler_params=_cparams("arbitrary"),
        name="moe",
    )(blk_expert, blk_first, n_used, x_sorted, w_gate_up, b_gate_up.reshape(N_EXPERTS, 1, -1),
      w_down, b_down.reshape(N_EXPERTS, 1, -1))


def _final_kernel(dest_ref, x1_ref, mod_ref, gate_ref, g_ref, rows_hbm, y_ref, buf, sems):
    i = pl.program_id(0)
    n = pl.num_programs(0)
    tm = x1_ref.shape[0]

    def issue(blk, slot):
        def body(r, carry):
            base = (blk * tm + r) * TOP_K
            for j in range(TOP_K):
                pltpu.make_async_copy(rows_hbm.at[pl.ds(dest_ref[base + j], 1)], buf.at[slot, j, pl.ds(r, 1)],
                                      sems.at[slot]).start()
            return carry
        lax.fori_loop(0, tm, body, 0)

    @pl.when(i == 0)
    def _():
        issue(0, 0)

    @pl.when(i + 1 < n)
    def _():
        issue(i + 1, (i + 1) % 2)

    slot = i % 2
    for j in range(TOP_K):
        pltpu.make_async_copy(rows_hbm.at[pl.ds(0, tm)], buf.at[slot, j], sems.at[slot]).wait()
    gates = gate_ref[...]
    moe = buf[slot, 0] * gates[:, 0:1]
    for j in range(1, TOP_K):
        moe = moe + buf[slot, j] * gates[:, j:j + 1]
    x2 = x1_ref[...] + mod_ref[:, 5 * D_MODEL:6 * D_MODEL] * moe
    ms = jnp.mean(x2 * x2, axis=-1, keepdims=True)
    y_ref[...] = x2 * lax.rsqrt(ms + NORM_EPS) * g_ref[...]


def _final(dest_flat, x1, mod3, gates, rows_out, g, blocks_per_mod):
    t = x1.shape[0]
    tm = TOK_BLOCK
    mod_rows = mod3.shape[1]
    row = lambda n: pl.BlockSpec((tm, n), lambda i, d: (i, 0))
    grid_spec = pltpu.PrefetchScalarGridSpec(
        num_scalar_prefetch=1,
        grid=(t // tm,),
        in_specs=[row(D_MODEL),
                  pl.BlockSpec((None, mod_rows, 6 * D_MODEL), lambda i, d: (i // blocks_per_mod, 0, 0)),
                  row(LANES),
                  pl.BlockSpec((1, D_MODEL), lambda i, d: (0, 0)),
                  pl.BlockSpec(memory_space=pl.ANY)],
        out_specs=row(D_MODEL),
        scratch_shapes=[pltpu.VMEM((2, TOP_K, tm, D_MODEL), F32), pltpu.SemaphoreType.DMA((2,))])
    return pl.pallas_call(
        _final_kernel,
        grid_spec=grid_spec,
        out_shape=jax.ShapeDtypeStruct((t, D_MODEL), F32),
        compiler_params=_cparams("arbitrary"),
        name="final",
    )(dest_flat, x1, mod3, gates, g, rows_out)


def _rope_tables(pos, rot_dim, theta, head_dim):
    half = rot_dim // 2
    inv = theta ** (-jnp.arange(half, dtype=F32) * (2.0 / rot_dim))
    ang = pos.astype(F32)[:, None] * inv[None, :]
    cos, sin = jnp.cos(ang), jnp.sin(ang)
    n = pos.shape[0]
    rest = head_dim - rot_dim
    zh = jnp.zeros((n, half), F32)
    c = jnp.concatenate([cos, cos, jnp.ones((n, rest), F32)], axis=1)
    sa = jnp.concatenate([-sin, zh, jnp.zeros((n, rest), F32)], axis=1)
    sb = jnp.concatenate([zh, sin, jnp.zeros((n, rest), F32)], axis=1)
    rep = LANES // head_dim
    return tuple(jnp.tile(a, (1, rep)) for a in (c, sa, sb))


def _pack_w_in(w_in):
    offs = np.cumsum((0,) + IN_SPLITS)
    part = lambda j: w_in[:, offs[j]:offs[j + 1]]
    zero = lambda n: jnp.zeros((D_MODEL, n), w_in.dtype)
    group = ATT_HEADS // ATT_KV_HEADS
    cols = []
    wq = part(0)
    for h in range(ATT_HEADS):
        wh = wq[:, h * HEAD_DIM:(h + 1) * HEAD_DIM]
        cols += [wh, zero(HEAD_DIM)] if h // group == 0 else [zero(HEAD_DIM), wh]
    wiq = part(3)
    for h in range(IDX_HEADS):
        cols += [wiq[:, h * IDX_DIM:(h + 1) * IDX_DIM], zero(LANES - IDX_DIM)]
    cols += [part(1), part(2), part(4), part(5), zero(LANES - IDX_DIM - IDX_HEADS)]
    cols += [part(j) for j in range(6, 12)]
    return jnp.concatenate(cols, axis=1).astype(BF16)


def _heads_major(a, db, t, width):
    heads = a.shape[1] // width
    return a.reshape(db, t, heads, width).transpose(0, 2, 1, 3).reshape(db, heads * t, width)


def _pad_rows(a, db, t, rows):
    return jnp.pad(a.reshape(db, t, -1), ((0, 0), (0, rows - t), (0, 0)))


def kernel(x_prompt, x_sample, cache_k, cache_v, cache_ik, state_ret, page_table, c_prompt, c_sample, norm_mix_g, norm_ffn_g, norm_final_g, w_ada, b_ada, w_in, w_branch_a, w_branch_b, w_out, w_router, b_router, w_gate_up, b_gate_up, w_down, b_down):
    batch, seq, _ = x_prompt.shape
    db, dt, _ = x_sample.shape
    assert w_in.shape[0] == 1, "one layer"
    tp, ts = batch * seq, db * dt
    xp = x_prompt.reshape(tp, D_MODEL)
    xs = x_sample.reshape(ts, D_MODEL)

    mod = _adaln(jnp.concatenate([c_prompt, c_sample], axis=0), w_ada[0], b_ada[0])
    mod_p = mod[:batch].reshape(batch, 1, 6 * D_MODEL)
    mod_s = jnp.repeat(mod[batch:], dt, axis=0).reshape(ts // TOK_BLOCK, TOK_BLOCK, 6 * D_MODEL)
    bpm_p = seq // TOK_BLOCK

    w_packed = _pack_w_in(w_in[0])
    pos_p = jnp.arange(seq)
    pos_s = PAST_LEN + (jnp.arange(TOK_BLOCK) % dt)
    g_mix = norm_mix_g[0].reshape(1, D_MODEL)
    outs_p = _inproj(xp, mod_p, g_mix, w_packed, _rope_tables(pos_p, ROPE_DIM, ROPE_THETA, HEAD_DIM),
                     _rope_tables(pos_p, RET_DK, RET_THETA, RET_DK), bpm_p, seq // TOK_BLOCK)
    outs_s = _inproj(xs, mod_s, g_mix, w_packed, _rope_tables(pos_s, ROPE_DIM, ROPE_THETA, HEAD_DIM),
                     _rope_tables(pos_s, RET_DK, RET_THETA, RET_DK), 1, 1)
    (q_p, k_p, v_p, kb_p, vb_p, iq_p, ikw_p, ikb_p, rq_p, rk_p, rv_p, sg_p, sga_p, sgb_p) = outs_p
    (q_s, k_s, v_s, kb_s, vb_s, iq_s, ikw_s, ikb_s, rq_s, rk_s, rv_s, sg_s, sga_s, sgb_s) = outs_s

    tri = (jnp.arange(LANES)[:, None] < jnp.arange(LANES)[None, :]).astype(BF16)

    a_p = _dsa_prompt(q_p, iq_p, ikw_p, kb_p, vb_p, ikb_p, tri, batch, seq)
    qs = _heads_major(q_s, db, dt, LANES)
    iqs = _heads_major(iq_s, db, dt, LANES)[:, :, :IDX_DIM]
    ws = _heads_major(ikw_s[:, IDX_DIM:IDX_DIM + IDX_HEADS] * (IDX_HEADS ** -0.5), db, dt, 1)
    o_s = _dsa_sample(page_table, qs, iqs, ws, _pad_rows(kb_s, db, dt, LANES), _pad_rows(vb_s, db, dt, LANES),
                      _pad_rows(ikb_s[:, :IDX_DIM], db, dt, LANES), cache_ik[0].astype(BF16),
                      cache_k[0].reshape(-1, PAGE_SIZE, LANES).astype(BF16),
                      cache_v[0].reshape(-1, PAGE_SIZE, LANES).astype(BF16), tri, dt)
    group = ATT_HEADS // ATT_KV_HEADS
    o_s = o_s.reshape(db, ATT_HEADS, dt, ATT_KV_HEADS, HEAD_DIM)
    a_s = jnp.stack([o_s[:, h, :, h // group, :] for h in range(ATT_HEADS)], axis=2)
    a_s = a_s.reshape(ts, ATT_OUT).astype(BF16)

    r_p, st_p = _retention(rq_p, rk_p, rv_p, sg_p, _retention_tables(float(RET_CHUNK)), None, batch,
                           seq // RET_CHUNK)
    pad = lambda a: _pad_rows(a, db, dt, RET_CHUNK).reshape(db * RET_CHUNK, -1)
    r_s, st_s = _retention(pad(rq_s), pad(rk_s), pad(rv_s), pad(sg_s), _retention_tables(float(dt)),
                           state_ret[0].reshape(db, RET_HEADS // 2, LANES, LANES), db, 1)
    r_s = r_s.reshape(db, RET_CHUNK, RET_OUT)[:, :dt].reshape(ts, RET_OUT)

    wr = jnp.pad(w_router[0], ((0, 0), (0, LANES - N_EXPERTS)))
    br = jnp.concatenate([b_router[0], jnp.full((LANES - N_EXPERTS,), -jnp.inf, F32)]).reshape(1, LANES)
    post_w = (w_branch_a[0].astype(BF16), w_branch_b[0].astype(BF16), w_out[0].astype(BF16),
              norm_ffn_g[0].reshape(1, D_MODEL), wr, br)
    ar = jnp.arange(TOK_BLOCK)
    ltri = (ar[None, :] < ar[:, None]).astype(BF16)
    x1_p, h2w_p, idx_p, gate_p, rank_p, cnt_p = _post(a_p, r_p, sga_p, sgb_p, xp, mod_p, *post_w, ltri,
                                                      jnp.zeros((1, LANES), F32), bpm_p)
    x1_s, h2w_s, idx_s, gate_s, rank_s, cnt_all = _post(a_s, r_s, sga_s, sgb_s, xs, mod_s, *post_w, ltri, cnt_p, 1)

    n_asg = (tp + ts) * TOP_K
    counts = cnt_all[0, :N_EXPERTS].astype(I32)
    padded = (counts + MOE_ROWS - 1) // MOE_ROWS * MOE_ROWS
    pend = jnp.cumsum(padded)
    pstart = pend - padded
    dest_p = (pstart[idx_p[:, :TOP_K]] + rank_p[:, :TOP_K]).reshape(-1)
    dest_s = (pstart[idx_s[:, :TOP_K]] + rank_s[:, :TOP_K]).reshape(-1)
    n_blocks = -(-n_asg // MOE_ROWS) + N_EXPERTS
    blk_start = jnp.arange(n_blocks, dtype=I32) * MOE_ROWS
    blk_expert = jnp.minimum(jnp.sum((blk_start[:, None] >= pend[None, :]).astype(I32), axis=1), N_EXPERTS - 1)
    blk_first = jnp.concatenate([jnp.ones((1,), I32), (blk_expert[1:] != blk_expert[:-1]).astype(I32)])
    n_used = (pend[-1] // MOE_ROWS).astype(I32).reshape(1)
    x_sorted = jnp.zeros((n_blocks * MOE_ROWS, D_MODEL // 2), jnp.uint32)
    x_sorted = _dispatch(dest_p, h2w_p, x_sorted)
    x_sorted = _dispatch(dest_s, h2w_s, x_sorted)
    rows_out = _moe(blk_expert, blk_first, n_used, x_sorted, w_gate_up[0], b_gate_up[0], w_down[0], b_down[0])

    g_final = norm_final_g.reshape(1, D_MODEL)
    y_p = _final(dest_p, x1_p, mod_p, gate_p, rows_out, g_final, bpm_p)
    y_s = _final(dest_s, x1_s, mod_s, gate_s, rows_out, g_final, 1)

    kv_shape = lambda b, s: (1, b, s, ATT_KV_HEADS, HEAD_DIM)
    st_shape = lambda b: (1, b, RET_HEADS, RET_DK, RET_DV)
    return (y_p.reshape(batch, seq, D_MODEL), y_s.reshape(db, dt, D_MODEL),
            k_p.reshape(kv_shape(batch, seq)), v_p.reshape(kv_shape(batch, seq)),
            ikw_p[:, :IDX_DIM].reshape(1, batch, seq, IDX_DIM), st_p.reshape(st_shape(batch)),
            k_s.reshape(kv_shape(db, dt)), v_s.reshape(kv_shape(db, dt)),
            ikw_s[:, :IDX_DIM].reshape(1, db, dt, IDX_DIM), st_s.reshape(st_shape(db)))
```

```python
import functools

import jax
import jax.numpy as jnp
import numpy as np
from jax import lax
from jax.experimental import pallas as pl
from jax.experimental.pallas import tpu as pltpu

F32 = jnp.float32
BF16 = jnp.bfloat16
I32 = jnp.int32

D_MODEL = 1024
PAST_LEN = 8192
PAGE_SIZE = 128
ATT_HEADS = 8
ATT_KV_HEADS = 2
HEAD_DIM = 64
ROPE_DIM = HEAD_DIM // 4
ROPE_THETA = 500000.0
IDX_HEADS = 8
IDX_DIM = 64
IDX_ROPE_DIM = IDX_DIM // 4
TOPK_MAX = 256
RET_HEADS = 8
RET_DK = 64
RET_DV = 128
RET_THETA = 10000.0
RET_CHUNK = 128
N_EXPERTS = 32
TOP_K = 4
D_FF = D_MODEL
SWIGLU_LIMIT = 7.0
SWIGLU_ALPHA = 1.702
NORM_EPS = 1e-6
GN_EPS = 1e-5
ATT_OUT = ATT_HEADS * HEAD_DIM
RET_OUT = RET_HEADS * RET_DV
IN_SPLITS = (ATT_HEADS * HEAD_DIM, ATT_KV_HEADS * HEAD_DIM, ATT_KV_HEADS * HEAD_DIM,
             IDX_HEADS * IDX_DIM, IDX_DIM, IDX_HEADS,
             RET_HEADS * RET_DK, RET_HEADS * RET_DK, RET_OUT, RET_OUT, D_MODEL, D_MODEL)

LANES = 128
MASK_NEG = -1e30
FLT_MAX = 3.4028234663852886e38
SELECT_UNROLL = 4
CAUSAL_VARIANTS = 4
IDX_KEY_CHUNK = 256
VMEM_LIMIT = 56 * 1024 * 1024

TOK_BLOCK = 256
Q_BLOCK = 128
MOE_ROWS = 512

_W_GROUPS = (("q", 1024), ("iq", 1024), ("kvi", 384), ("rq", 512), ("rk", 512),
             ("rv", 1024), ("rg", 1024), ("ga", 1024), ("gb", 1024))
PROJ_COLS = 512
_W_OFF = {}
_off = 0
for _n, _w in _W_GROUPS:
    _W_OFF[_n] = (_off, _w)
    _off += _w
W_COLS = _off


def _cparams(*sem):
    return pltpu.CompilerParams(dimension_semantics=sem, vmem_limit_bytes=VMEM_LIMIT)


def _adaln_kernel(c_ref, w_ref, b_ref, o_ref):
    c = c_ref[...]
    s = c * jax.nn.sigmoid(c)
    o_ref[...] = jnp.dot(s, w_ref[...], preferred_element_type=F32, precision=lax.Precision.HIGHEST) + b_ref[...]


def _adaln(c_all, w_ada, b_ada):
    n = c_all.shape[0]
    nb = 1536
    return pl.pallas_call(
        _adaln_kernel,
        grid=(6 * D_MODEL // nb,),
        in_specs=[pl.BlockSpec((n, D_MODEL), lambda j: (0, 0)),
                  pl.BlockSpec((D_MODEL, nb), lambda j: (0, j)),
                  pl.BlockSpec((1, nb), lambda j: (0, j))],
        out_specs=pl.BlockSpec((n, nb), lambda j: (0, j)),
        out_shape=jax.ShapeDtypeStruct((n, 6 * D_MODEL), F32),
        compiler_params=_cparams("arbitrary"),
        name="adaln",
    )(c_all, w_ada, b_ada.reshape(1, -1))


def _rope_slab(z, c, sa, sb, half):
    return z * c + pltpu.roll(z, LANES - half, 1) * sa + pltpu.roll(z, half, 1) * sb


def _inproj_kernel(x_ref, mod_ref, g_ref, w_ref, ca_ref, saa_ref, sba_ref, cr_ref, sar_ref, sbr_ref,
                   q_ref, k_ref, v_ref, kb_ref, vb_ref, iq_ref, ikw_ref, ikb_ref,
                   rq_ref, rk_ref, rv_ref, sg_ref, sga_ref, sgb_ref):
    x = x_ref[...]
    ms = jnp.mean(x * x, axis=-1, keepdims=True)
    y = x * lax.rsqrt(ms + NORM_EPS) * g_ref[...]
    h = (y * (1.0 + mod_ref[:, D_MODEL:2 * D_MODEL]) + mod_ref[:, 0:D_MODEL]).astype(BF16)

    def slabs(name):
        c0, width = _W_OFF[name]
        step = min(width, PROJ_COLS)
        for j in range(width // step):
            z = jnp.dot(h, w_ref[:, c0 + j * step:c0 + (j + 1) * step], preferred_element_type=F32)
            for s in range(step // LANES):
                yield j * (step // LANES) + s, z[:, s * LANES:(s + 1) * LANES]

    ca, saa, sba = ca_ref[...], saa_ref[...], sba_ref[...]
    cr, sar, sbr = cr_ref[...], sar_ref[...], sbr_ref[...]
    att_half, ret_half = ROPE_DIM // 2, RET_DK // 2
    lane = lax.broadcasted_iota(I32, (x.shape[0], LANES), 1)
    sl = lambda s: slice(s * LANES, (s + 1) * LANES)

    for s, z in slabs("q"):
        q_ref[:, sl(s)] = (_rope_slab(z, ca, saa, sba, att_half) * 0.125).astype(BF16)
    for s, z in slabs("iq"):
        iq_ref[:, sl(s)] = (_rope_slab(z, ca, saa, sba, att_half) * 0.125).astype(BF16)
    (_, zk), (_, zv), (_, zi) = slabs("kvi")
    kk = _rope_slab(zk, ca, saa, sba, att_half)
    k_ref[...] = kk
    kb_ref[...] = kk.astype(BF16)
    v_ref[...] = zv
    vb_ref[...] = zv.astype(BF16)
    zr = _rope_slab(zi, ca, saa, sba, att_half)
    ikw_ref[...] = jnp.where(lane < IDX_DIM, zr, zi)
    ikb_ref[...] = jnp.where(lane < IDX_DIM, zr, 0.0).astype(BF16)
    for s, z in slabs("rq"):
        rq_ref[:, sl(s)] = _rope_slab(z, cr, sar, sbr, ret_half).astype(BF16)
    for s, z in slabs("rk"):
        rk_ref[:, sl(s)] = (_rope_slab(z, cr, sar, sbr, ret_half) * 0.125).astype(BF16)
    for s, z in slabs("rv"):
        rv_ref[:, sl(s)] = z.astype(BF16)
    for s, z in slabs("rg"):
        sg_ref[:, sl(s)] = (z * jax.nn.sigmoid(z)).astype(BF16)
    for s, z in slabs("ga"):
        sga_ref[:, sl(s)] = jax.nn.sigmoid(z).astype(BF16)
    for s, z in slabs("gb"):
        sgb_ref[:, sl(s)] = jax.nn.sigmoid(z).astype(BF16)


def _inproj(x, mod3, g, w_packed, tabs_att, tabs_ret, blocks_per_mod, tab_blocks):
    t = x.shape[0]
    tm = TOK_BLOCK
    nblk = t // tm
    mod_rows = mod3.shape[1]
    tab_spec = pl.BlockSpec((tm, LANES), lambda i: (i % tab_blocks, 0))
    row = lambda n: pl.BlockSpec((tm, n), lambda i: (i, 0))
    out_defs = [(1024, BF16), (128, F32), (128, F32), (128, BF16), (128, BF16), (1024, BF16), (128, F32),
                (128, BF16), (512, BF16), (512, BF16), (1024, BF16), (1024, BF16), (1024, BF16), (1024, BF16)]
    return pl.pallas_call(
        _inproj_kernel,
        grid=(nblk,),
        in_specs=[row(D_MODEL),
                  pl.BlockSpec((None, mod_rows, 6 * D_MODEL), lambda i: (i // blocks_per_mod, 0, 0)),
                  pl.BlockSpec((1, D_MODEL), lambda i: (0, 0)),
                  pl.BlockSpec((D_MODEL, W_COLS), lambda i: (0, 0))] + [tab_spec] * 6,
        out_specs=[row(n) for n, _ in out_defs],
        out_shape=[jax.ShapeDtypeStruct((t, n), d) for n, d in out_defs],
        compiler_params=_cparams("parallel"),
        name="inproj",
    )(x, mod3, g, w_packed, *tabs_att, *tabs_ret)


def _count(score_ref, n, pred):
    acc = jnp.zeros((score_ref.shape[0], LANES), F32)
    for c in range(n // LANES):
        acc = acc + jnp.where(pred(score_ref[:, c * LANES:(c + 1) * LANES]), 1.0, 0.0)
    return jnp.sum(acc, axis=1, keepdims=True)


def _kth_largest(score_ref, n, k):
    sc = score_ref[:, :n]
    finite = sc > -jnp.inf
    n_fin = jnp.sum(jnp.where(finite, 1.0, 0.0), axis=1, keepdims=True)
    n_pos = jnp.sum(jnp.where(sc > 0.0, 1.0, 0.0), axis=1, keepdims=True)
    n_nonneg = jnp.sum(jnp.where(sc >= 0.0, 1.0, 0.0), axis=1, keepdims=True)
    mx = jnp.max(sc, axis=1, keepdims=True)
    mn = jnp.min(jnp.where(finite, sc, jnp.inf), axis=1, keepdims=True)
    small = n_fin <= k
    positive = n_pos >= k
    at_zero = jnp.logical_and(jnp.logical_not(positive), n_nonneg >= k)
    lo = jnp.where(positive, 0.0, mn)
    hi = jnp.where(positive, mx + (jnp.abs(mx) * 2.0 ** -20 + 2.0 ** -100), 0.0)
    lo = jnp.where(at_zero, 0.0, lo)
    done = jnp.where(jnp.logical_or(small, at_zero), 1.0, 0.0)

    def cond(state):
        return jnp.min(state[2]) < 0.5

    def body(state):
        lo, hi, done = state
        for _ in range(SELECT_UNROLL):
            mid = 0.5 * lo + 0.5 * hi
            cnt = _count(score_ref, n, lambda s: s >= mid)
            stuck = jnp.logical_or(mid <= lo, mid >= hi)
            live = jnp.logical_and(done < 0.5, jnp.logical_not(stuck))
            ge = cnt >= k
            lo = jnp.where(jnp.logical_and(live, ge), mid, lo)
            hi = jnp.where(jnp.logical_and(live, jnp.logical_not(ge)), mid, hi)
            done = jnp.where(jnp.logical_or(stuck, cnt == k), 1.0, done)
        return lo, hi, done

    lo, _, _ = lax.while_loop(cond, body, (lo, hi, done))
    return jnp.where(small, -FLT_MAX, lo)


def _topk_bias(score_ref, bias_ref, tri_ref, n, k):
    rows = score_ref.shape[0]
    thr = _kth_largest(score_ref, n, k)
    need = k - _count(score_ref, n, lambda s: s > thr)
    n_eq = _count(score_ref, n, lambda s: s == thr)
    has_ties = jnp.max(jnp.where(need < n_eq, 1.0, 0.0)) > 0.5

    @pl.when(jnp.logical_not(has_ties))
    def _():
        for c in range(n // LANES):
            sl = slice(c * LANES, (c + 1) * LANES)
            bias_ref[:, sl] = jnp.where(score_ref[:, sl] >= thr, 0.0, MASK_NEG)

    @pl.when(has_ties)
    def _():
        run = jnp.zeros((rows, 1), F32)
        for c in range(n // LANES):
            sl = slice(c * LANES, (c + 1) * LANES)
            sc = score_ref[:, sl]
            eq = sc == thr
            eqf = jnp.where(eq, 1.0, 0.0)
            before = jnp.dot(eqf.astype(BF16), tri_ref[...], preferred_element_type=F32) + run
            take = jnp.logical_or(sc > thr, jnp.logical_and(eq, before < need))
            bias_ref[:, sl] = jnp.where(take, 0.0, MASK_NEG)
            run = run + jnp.sum(eqf, axis=1, keepdims=True)


def _dsa_prompt_kernel(q_ref, iq_ref, ikw_ref, kb_ref, vb_ref, ikb_ref, tri_ref, o_ref, score_ref, bias_ref):
    qb, s_len = score_ref.shape
    i = pl.program_id(1)
    nqb = s_len // qb
    per_variant = nqb // CAUSAL_VARIANTS
    for v in range(CAUSAL_VARIANTS):
        pl.when(i // per_variant == v)(
            functools.partial(_dsa_prompt_body, q_ref, iq_ref, ikw_ref, kb_ref, vb_ref, ikb_ref, tri_ref, o_ref,
                              score_ref, bias_ref, (v + 1) * per_variant * qb, min(TOPK_MAX, s_len // 4)))


def _dsa_prompt_body(q_ref, iq_ref, ikw_ref, kb_ref, vb_ref, ikb_ref, tri_ref, o_ref, score_ref, bias_ref,
                     n_keys, topk):
    qb = score_ref.shape[0]
    i = pl.program_id(1)
    w = ikw_ref[:, IDX_DIM:IDX_DIM + IDX_HEADS] * (IDX_HEADS ** -0.5)
    nt = (((1,), (1,)), ((), ()))
    kc = IDX_KEY_CHUNK
    qpos = i * qb + lax.broadcasted_iota(I32, (qb, kc), 0)
    for c in range(n_keys // kc):
        ikc = ikb_ref[c * kc:(c + 1) * kc, :]
        acc = jnp.zeros((qb, kc), F32)
        for h in range(IDX_HEADS):
            d = lax.dot_general(iq_ref[:, h * LANES:(h + 1) * LANES], ikc, nt, preferred_element_type=F32)
            acc = acc + jnp.maximum(d, 0.0) * w[:, h:h + 1]
        kpos = c * kc + lax.broadcasted_iota(I32, (qb, kc), 1)
        score_ref[:, c * kc:(c + 1) * kc] = jnp.where(kpos <= qpos, acc, -jnp.inf)

    _topk_bias(score_ref, bias_ref, tri_ref, n_keys, topk)

    kb = kb_ref[0:n_keys, :]
    vb = vb_ref[0:n_keys, :]
    bias = bias_ref[:, 0:n_keys]
    lane = lax.broadcasted_iota(I32, (qb, LANES), 1)
    heads = []
    for h in range(ATT_HEADS):
        s = lax.dot_general(q_ref[:, h * LANES:(h + 1) * LANES], kb, nt, preferred_element_type=F32) + bias
        m = jnp.max(s, axis=1, keepdims=True)
        p = jnp.exp(s - m)
        l = jnp.sum(p, axis=1, keepdims=True)
        heads.append(jnp.dot(p.astype(BF16), vb, preferred_element_type=F32) / l)
    group = ATT_HEADS // ATT_KV_HEADS
    for pp in range(ATT_HEADS // 2):
        a, b = heads[2 * pp], heads[2 * pp + 1]
        if (2 * pp) // group == 0:
            slab = jnp.where(lane < HEAD_DIM, a, pltpu.roll(b, HEAD_DIM, 1))
        else:
            slab = jnp.where(lane < HEAD_DIM, pltpu.roll(a, HEAD_DIM, 1), b)
        o_ref[:, pp * LANES:(pp + 1) * LANES] = slab.astype(BF16)


def _dsa_prompt(q, iq, ikw, kb, vb, ikb, tri, batch, seq):
    nqb = seq // Q_BLOCK
    qrow = lambda n: pl.BlockSpec((Q_BLOCK, n), lambda b, i: (b * nqb + i, 0))
    keys = pl.BlockSpec((seq, LANES), lambda b, i: (b, 0))
    return pl.pallas_call(
        _dsa_prompt_kernel,
        grid=(batch, nqb),
        in_specs=[qrow(1024), qrow(1024), qrow(LANES), keys, keys, keys,
                  pl.BlockSpec((LANES, LANES), lambda b, i: (0, 0))],
        out_specs=qrow(ATT_OUT),
        out_shape=jax.ShapeDtypeStruct((batch * seq, ATT_OUT), BF16),
        scratch_shapes=[pltpu.VMEM((Q_BLOCK, seq), F32), pltpu.VMEM((Q_BLOCK, seq), F32)],
        compiler_params=_cparams("parallel", "arbitrary"),
        name="dsa_prompt",
    )(q, iq, ikw, kb, vb, ikb, tri)


def _dsa_sample_kernel(pt_ref, qs_ref, iqs_ref, ws_ref, knew_ref, vnew_ref, iknew_ref, cik_hbm, ck_hbm, cv_hbm,
                       tri_ref, o_ref, ikbuf, kbuf, vbuf, ikt, kt, vt, sems, key_ref, bias_ref):
    db = pl.program_id(0)
    n_pages = ikbuf.shape[1]
    t = key_ref.shape[0]
    n_past = n_pages * PAGE_SIZE
    last = pl.num_programs(0) - 1

    def fetch(src, dst_of_page, sem, req):
        def body(p, carry):
            pltpu.make_async_copy(src.at[pt_ref[req, p]], dst_of_page(p), sem).start()
            return carry
        lax.fori_loop(0, n_pages, body, 0)

    def wait_all(src, dst, sem):
        pltpu.make_async_copy(src.at[pl.ds(0, n_pages)], dst, sem).wait()

    fetch_ik = lambda req, slot: fetch(cik_hbm, lambda p: ikbuf.at[slot, p], sems.at[slot], req)
    fetch_k = lambda req: fetch(ck_hbm, lambda p: kbuf.at[p], sems.at[2], req)
    fetch_v = lambda req: fetch(cv_hbm, lambda p: vbuf.at[p], sems.at[3], req)

    @pl.when(db == 0)
    def _():
        fetch_ik(0, 0)
        fetch_k(0)
        fetch_v(0)

    @pl.when(db < last)
    def _():
        fetch_ik(db + 1, (db + 1) % 2)

    slot = db % 2
    wait_all(cik_hbm, ikbuf.at[slot], sems.at[slot])

    nt = (((1,), (1,)), ((), ()))
    page = lambda p: slice(p * PAGE_SIZE, (p + 1) * PAGE_SIZE)
    for p in range(n_pages):
        ikt[:, page(p)] = ikbuf[slot, p].astype(BF16)
    iqs = iqs_ref[...]
    wcol = ws_ref[...]
    d_past = jnp.maximum(jnp.dot(iqs, ikt[...], preferred_element_type=F32), 0.0) * wcol
    d_new = jnp.maximum(jnp.dot(iqs, iknew_ref[...], preferred_element_type=F32), 0.0) * wcol
    s_past = d_past[0:t]
    s_new = d_new[0:t]
    for h in range(1, IDX_HEADS):
        s_past = s_past + d_past[h * t:(h + 1) * t]
        s_new = s_new + d_new[h * t:(h + 1) * t]
    row = lax.broadcasted_iota(I32, (t, LANES), 0)
    lane = lax.broadcasted_iota(I32, (t, LANES), 1)
    new_ok = lane <= row
    key_ref[:, 0:n_past] = s_past
    key_ref[:, n_past:n_past + LANES] = jnp.where(new_ok, s_new, -jnp.inf)
    _topk_bias(key_ref, bias_ref, tri_ref, n_past + LANES, min(TOPK_MAX, (n_past + t) // 4))

    def stage(buf, dst):
        for p in range(n_pages):
            for j in range(ATT_KV_HEADS):
                dst[j, :, page(p)] = buf[p, j].astype(BF16)

    wait_all(ck_hbm, kbuf, sems.at[2])
    stage(kbuf, kt)

    @pl.when(db < last)
    def _():
        fetch_k(db + 1)

    wait_all(cv_hbm, vbuf, sems.at[3])
    stage(vbuf, vt)

    @pl.when(db < last)
    def _():
        fetch_v(db + 1)

    rows_per_kv = qs_ref.shape[0] // ATT_KV_HEADS
    bias = jnp.concatenate([bias_ref[...]] * (rows_per_kv // t), axis=0)
    for j in range(ATT_KV_HEADS):
        qj = qs_ref[j * rows_per_kv:(j + 1) * rows_per_kv, :]
        sp = jnp.dot(qj, kt[j], preferred_element_type=F32) + bias[:, 0:n_past]
        sn = jnp.dot(qj, knew_ref[j], preferred_element_type=F32) + bias[:, n_past:n_past + LANES]
        m = jnp.maximum(jnp.max(sp, axis=1, keepdims=True), jnp.max(sn, axis=1, keepdims=True))
        pp = jnp.exp(sp - m)
        pn = jnp.exp(sn - m)
        l = jnp.sum(pp, axis=1, keepdims=True) + jnp.sum(pn, axis=1, keepdims=True)
        o = (lax.dot_general(pp.astype(BF16), vt[j], nt, preferred_element_type=F32)
             + lax.dot_general(pn.astype(BF16), vnew_ref[j], nt, preferred_element_type=F32))
        o_ref[j * rows_per_kv:(j + 1) * rows_per_kv, :] = o / l


def _dsa_sample(page_table, qs, iqs, ws, knew, vnew, iknew, cache_ik, cache_k, cache_v, tri, t):
    db, n_pages = page_table.shape
    rows = qs.shape[1]
    per_db = lambda r, n: pl.BlockSpec((None, r, n), lambda b, pt: (b, 0, 0))
    any_spec = pl.BlockSpec(memory_space=pl.ANY)
    n_keys = n_pages * PAGE_SIZE + LANES
    grid_spec = pltpu.PrefetchScalarGridSpec(
        num_scalar_prefetch=1,
        grid=(db,),
        in_specs=[per_db(rows, HEAD_DIM), per_db(rows, IDX_DIM), per_db(rows, 1),
                  pl.BlockSpec((None, ATT_KV_HEADS, HEAD_DIM, LANES), lambda b, pt: (b, 0, 0, 0)),
                  pl.BlockSpec((None, ATT_KV_HEADS, HEAD_DIM, LANES), lambda b, pt: (b, 0, 0, 0)),
                  per_db(IDX_DIM, LANES), any_spec, any_spec, any_spec,
                  pl.BlockSpec((LANES, LANES), lambda b, pt: (0, 0))],
        out_specs=per_db(rows, HEAD_DIM),
        scratch_shapes=[pltpu.VMEM((2, n_pages, IDX_DIM, PAGE_SIZE), F32),
                        pltpu.VMEM((n_pages, ATT_KV_HEADS, HEAD_DIM, PAGE_SIZE), F32),
                        pltpu.VMEM((n_pages, ATT_KV_HEADS, HEAD_DIM, PAGE_SIZE), F32),
                        pltpu.VMEM((IDX_DIM, n_pages * PAGE_SIZE), BF16),
                        pltpu.VMEM((ATT_KV_HEADS, HEAD_DIM, n_pages * PAGE_SIZE), BF16),
                        pltpu.VMEM((ATT_KV_HEADS, HEAD_DIM, n_pages * PAGE_SIZE), BF16),
                        pltpu.SemaphoreType.DMA((4,)),
                        pltpu.VMEM((t, n_keys), F32),
                        pltpu.VMEM((t, n_keys), F32)])
    return pl.pallas_call(
        _dsa_sample_kernel,
        grid_spec=grid_spec,
        out_shape=jax.ShapeDtypeStruct((db, rows, HEAD_DIM), F32),
        compiler_params=_cparams("arbitrary"),
        name="dsa_sample",
    )(page_table, qs, iqs, ws, knew, vnew, iknew, cache_ik, cache_k, cache_v, tri)


def _retention_kernel(has_init, rq_ref, rk_ref, rv_ref, sg_ref, decay_ref, qdec_ref, kdec_ref, gst_ref, *rest):
    if has_init:
        init_ref, o_ref, st_ref, state = rest
    else:
        o_ref, st_ref, state = rest
    c = pl.program_id(1)

    @pl.when(c == 0)
    def _():
        if has_init:
            state[...] = init_ref[...]
        else:
            state[...] = jnp.zeros_like(state)

    nt = (((1,), (1,)), ((), ()))
    tn = (((0,), (0,)), ((), ()))
    rows = rq_ref.shape[0]
    lane = lax.broadcasted_iota(I32, (rows, LANES), 1)
    for p in range(RET_HEADS // 2):
        sl = slice(p * LANES, (p + 1) * LANES)
        qp = rq_ref[:, sl]
        kp = rk_ref[:, sl]
        qd = (qp.astype(F32) * qdec_ref[:, sl]).astype(BF16)
        kd = (kp.astype(F32) * kdec_ref[:, sl]).astype(BF16)
        s_old = state[p]
        s_old_b = s_old.astype(BF16)
        s_new = s_old * gst_ref[p]
        for e in range(2):
            h = 2 * p + e
            hs = slice(h * LANES, (h + 1) * LANES)
            mine = jnp.where((lane >= e * RET_DK) & (lane < (e + 1) * RET_DK), 1.0, 0.0).astype(BF16)
            sc = lax.dot_general(qp * mine, kp, nt, preferred_element_type=F32) * decay_ref[h]
            vh = rv_ref[:, hs]
            o = (jnp.dot(sc.astype(BF16), vh, preferred_element_type=F32)
                 + jnp.dot(qd * mine, s_old_b, preferred_element_type=F32))
            s_new = s_new + lax.dot_general(kd * mine, vh, tn, preferred_element_type=F32)
            mu = jnp.mean(o, axis=-1, keepdims=True)
            var = jnp.mean(jnp.square(o - mu), axis=-1, keepdims=True)
            on = (o - mu) * lax.rsqrt(var + GN_EPS)
            o_ref[:, hs] = (on * sg_ref[:, hs].astype(F32)).astype(BF16)
        state[p] = s_new

    @pl.when(c == pl.num_programs(1) - 1)
    def _():
        st_ref[...] = state[...]


def _retention(rq, rk, rv, sg, tables, init, batch, n_chunks):
    decay, qdec, kdec, gst = tables
    cr = RET_CHUNK
    rowspec = lambda n: pl.BlockSpec((cr, n), lambda b, c: (b * n_chunks + c, 0))
    const = lambda shape: pl.BlockSpec(shape, lambda b, c: (0,) * len(shape))
    st_spec = pl.BlockSpec((None, RET_HEADS // 2, LANES, LANES), lambda b, c: (b, 0, 0, 0))
    in_specs = [rowspec(512), rowspec(512), rowspec(RET_OUT), rowspec(RET_OUT),
                const(decay.shape), const(qdec.shape), const(kdec.shape), const(gst.shape)]
    args = [rq, rk, rv, sg, decay, qdec, kdec, gst]
    if init is not None:
        in_specs.append(st_spec)
        args.append(init)
    return pl.pallas_call(
        functools.partial(_retention_kernel, init is not None),
        grid=(batch, n_chunks),
        in_specs=in_specs,
        out_specs=[rowspec(RET_OUT), st_spec],
        out_shape=[jax.ShapeDtypeStruct((batch * n_chunks * cr, RET_OUT), BF16),
                   jax.ShapeDtypeStruct((batch, RET_HEADS // 2, LANES, LANES), F32)],
        scratch_shapes=[pltpu.VMEM((RET_HEADS // 2, LANES, LANES), F32)],
        compiler_params=_cparams("parallel", "arbitrary"),
        name="retention",
    )(*args)


def _retention_tables(c_eff):
    lg = jnp.log(1.0 - 2.0 ** (-5.0 - jnp.arange(RET_HEADS, dtype=F32)))
    i = jnp.arange(RET_CHUNK, dtype=F32)
    diff = i[:, None] - i[None, :]
    decay = jnp.where(diff >= 0, jnp.exp(jnp.maximum(diff, 0.0)[None] * lg[:, None, None]), 0.0)
    q_decay = jnp.exp((i + 1.0)[:, None] * lg[None, :])
    k_decay = jnp.exp((c_eff - 1.0 - i)[:, None] * lg[None, :])
    qdec = jnp.repeat(q_decay, RET_DK, axis=1)
    kdec = jnp.repeat(k_decay, RET_DK, axis=1)
    g_state = jnp.exp(c_eff * lg)
    gst = jnp.broadcast_to(jnp.repeat(g_state, RET_DK).reshape(RET_HEADS // 2, LANES, 1),
                           (RET_HEADS // 2, LANES, LANES))
    return decay, qdec, kdec, gst


def _pack_bf16_pairs(lo, hi):
    return pltpu.pack_elementwise([lo, hi], packed_dtype=BF16)


def _unpack_bf16_pairs(words):
    return tuple(pltpu.unpack_elementwise(words, index=j, packed_dtype=BF16, unpacked_dtype=F32).astype(BF16)
                 for j in range(2))


def _post_kernel(a_ref, r_ref, sga_ref, sgb_ref, x_ref, mod_ref, wpa_ref, wpb_ref, wo_ref, g_ref, wr_ref, br_ref,
                 ltri_ref, cnt0_ref, x1_ref, h2w_ref, idx_ref, gate_ref, rank_ref, cnt_ref, cnt):
    @pl.when(pl.program_id(0) == 0)
    def _():
        cnt[...] = cnt0_ref[...]

    pa = jnp.dot(a_ref[...], wpa_ref[...], preferred_element_type=F32)
    pb = jnp.dot(r_ref[...], wpb_ref[...], preferred_element_type=F32)
    merged = sga_ref[...].astype(F32) * pa + sgb_ref[...].astype(F32) * pb
    gt1 = mod_ref[:, 2 * D_MODEL:3 * D_MODEL]
    x1 = x_ref[...] + gt1 * jnp.dot(merged.astype(BF16), wo_ref[...], preferred_element_type=F32)
    x1_ref[...] = x1
    ms = jnp.mean(x1 * x1, axis=-1, keepdims=True)
    y = x1 * lax.rsqrt(ms + NORM_EPS) * g_ref[...]
    h2 = y * (1.0 + mod_ref[:, 4 * D_MODEL:5 * D_MODEL]) + mod_ref[:, 3 * D_MODEL:4 * D_MODEL]
    half = D_MODEL // 2
    h2w_ref[...] = _pack_bf16_pairs(h2[:, :half], h2[:, half:])
    logits = jnp.dot(h2, wr_ref[...], preferred_element_type=F32, precision=lax.Precision.HIGHEST) + br_ref[...]
    lane = lax.broadcasted_iota(I32, logits.shape, 1).astype(F32)
    idx_out = jnp.zeros(logits.shape, F32)
    val_out = jnp.zeros(logits.shape, F32)
    chosen = []
    top = None
    for j in range(TOP_K):
        m = jnp.max(logits, axis=1, keepdims=True)
        am = jnp.min(jnp.where(logits == m, lane, float(LANES)), axis=1, keepdims=True)
        if j == 0:
            top = m
        idx_out = jnp.where(lane == j, am, idx_out)
        val_out = jnp.where(lane == j, jnp.exp(m - top), val_out)
        chosen.append(lane == am)
        logits = jnp.where(chosen[-1], -jnp.inf, logits)
    idx_ref[...] = idx_out.astype(I32)
    gate_ref[...] = val_out / jnp.sum(val_out, axis=1, keepdims=True)
    onehot = jnp.where(chosen[0] | chosen[1] | chosen[2] | chosen[3], 1.0, 0.0)
    before = jnp.dot(ltri_ref[...], onehot.astype(BF16), preferred_element_type=F32) + cnt[...]
    rank_out = jnp.zeros(logits.shape, F32)
    for j in range(TOP_K):
        rj = jnp.sum(jnp.where(chosen[j], before, 0.0), axis=1, keepdims=True)
        rank_out = jnp.where(lane == j, rj, rank_out)
    rank_ref[...] = rank_out.astype(I32)
    cnt[...] = cnt[...] + jnp.sum(onehot, axis=0, keepdims=True)
    cnt_ref[...] = cnt[...]


def _post(a, r, sga, sgb, x, mod3, wpa, wpb, wo, g, wr, br, ltri, cnt0, blocks_per_mod):
    t = x.shape[0]
    tm = TOK_BLOCK
    mod_rows = mod3.shape[1]
    row = lambda n: pl.BlockSpec((tm, n), lambda i: (i, 0))
    const = lambda a_: pl.BlockSpec(a_.shape, lambda i: (0,) * a_.ndim)
    return pl.pallas_call(
        _post_kernel,
        grid=(t // tm,),
        in_specs=[row(ATT_OUT), row(RET_OUT), row(D_MODEL), row(D_MODEL), row(D_MODEL),
                  pl.BlockSpec((None, mod_rows, 6 * D_MODEL), lambda i: (i // blocks_per_mod, 0, 0)),
                  const(wpa), const(wpb), const(wo), const(g), const(wr), const(br), const(ltri), const(cnt0)],
        out_specs=[row(D_MODEL), row(D_MODEL // 2), row(LANES), row(LANES), row(LANES),
                   pl.BlockSpec((1, LANES), lambda i: (0, 0))],
        out_shape=[jax.ShapeDtypeStruct((t, D_MODEL), F32), jax.ShapeDtypeStruct((t, D_MODEL // 2), jnp.uint32),
                   jax.ShapeDtypeStruct((t, LANES), I32), jax.ShapeDtypeStruct((t, LANES), F32),
                   jax.ShapeDtypeStruct((t, LANES), I32), jax.ShapeDtypeStruct((1, LANES), F32)],
        scratch_shapes=[pltpu.VMEM((1, LANES), F32)],
        compiler_params=_cparams("arbitrary"),
        name="post",
    )(a, r, sga, sgb, x, mod3, wpa, wpb, wo, g, wr, br, ltri, cnt0)


def _dispatch_kernel(dest_ref, h2w_ref, xs_in, xs_out, sem):
    del xs_in
    i = pl.program_id(0)
    tm = h2w_ref.shape[0]

    def row_copy(r, d):
        return pltpu.make_async_copy(h2w_ref.at[pl.ds(r, 1)], xs_out.at[pl.ds(d, 1)], sem)

    def start(r, carry):
        base = (i * tm + r) * TOP_K
        for j in range(TOP_K):
            row_copy(r, dest_ref[base + j]).start()
        return carry

    lax.fori_loop(0, tm, start, 0)
    for j in range(TOP_K):
        pltpu.make_async_copy(h2w_ref, xs_out.at[pl.ds(0, tm)], sem).wait()


def _dispatch(dest_flat, h2w, x_sorted):
    t = h2w.shape[0]
    tm = TOK_BLOCK
    grid_spec = pltpu.PrefetchScalarGridSpec(
        num_scalar_prefetch=1,
        grid=(t // tm,),
        in_specs=[pl.BlockSpec((tm, D_MODEL // 2), lambda i, d: (i, 0)), pl.BlockSpec(memory_space=pl.ANY)],
        out_specs=pl.BlockSpec(memory_space=pl.ANY),
        scratch_shapes=[pltpu.SemaphoreType.DMA(())])
    return pl.pallas_call(
        _dispatch_kernel,
        grid_spec=grid_spec,
        out_shape=jax.ShapeDtypeStruct(x_sorted.shape, x_sorted.dtype),
        input_output_aliases={2: 0},
        compiler_params=_cparams("arbitrary"),
        name="dispatch",
    )(dest_flat, h2w, x_sorted)


def _moe_kernel(be_ref, first_ref, nused_ref, x_ref, wgu_ref, bgu_ref, wd_ref, bd_ref, o_ref, wgu_b, wd_b):
    i = pl.program_id(0)

    @pl.when(first_ref[i] == 1)
    def _():
        wgu_b[...] = wgu_ref[...].astype(BF16)
        wd_b[...] = wd_ref[...].astype(BF16)

    @pl.when(i < nused_ref[0])
    def _():
        x = jnp.concatenate(_unpack_bf16_pairs(x_ref[...]), axis=1)
        gu = jnp.dot(x, wgu_b[...], preferred_element_type=F32) + bgu_ref[...]
        g = jnp.minimum(gu[:, :D_FF], SWIGLU_LIMIT)
        u = jnp.clip(gu[:, D_FF:], -SWIGLU_LIMIT, SWIGLU_LIMIT)
        act = (u + 1.0) * (g * jax.nn.sigmoid(SWIGLU_ALPHA * g))
        o_ref[...] = jnp.dot(act.astype(BF16), wd_b[...], preferred_element_type=F32) + bd_ref[...]

    @pl.when(i >= nused_ref[0])
    def _():
        o_ref[...] = jnp.zeros_like(o_ref)


def _moe(blk_expert, blk_first, n_used, x_sorted, w_gate_up, b_gate_up, w_down, b_down):
    n_rows = x_sorted.shape[0]
    grid_spec = pltpu.PrefetchScalarGridSpec(
        num_scalar_prefetch=3,
        grid=(n_rows // MOE_ROWS,),
        in_specs=[pl.BlockSpec((MOE_ROWS, D_MODEL // 2), lambda i, be, bf, nu: (i, 0)),
                  pl.BlockSpec((None, D_MODEL, 2 * D_FF), lambda i, be, bf, nu: (be[i], 0, 0)),
                  pl.BlockSpec((None, 1, 2 * D_FF), lambda i, be, bf, nu: (be[i], 0, 0)),
                  pl.BlockSpec((None, D_FF, D_MODEL), lambda i, be, bf, nu: (be[i], 0, 0)),
                  pl.BlockSpec((None, 1, D_MODEL), lambda i, be, bf, nu: (be[i], 0, 0))],
        out_specs=pl.BlockSpec((MOE_ROWS, D_MODEL), lambda i, be, bf, nu: (i, 0)),
        scratch_shapes=[pltpu.VMEM((D_MODEL, 2 * D_FF), BF16), pltpu.VMEM((D_FF, D_MODEL), BF16)])
    return pl.pallas_call(
        _moe_kernel,
        grid_spec=grid_spec,
        out_shape=jax.ShapeDtypeStruct((n_rows, D_MODEL), F32),
        compiler_params=_cparams("arbitrary"),
        name="moe",
    )(blk_expert, blk_first, n_used, x_sorted, w_gate_up, b_gate_up.reshape(N_EXPERTS, 1, -1),
      w_down, b_down.reshape(N_EXPERTS, 1, -1))


def _final_kernel(dest_ref, x1_ref, mod_ref, gate_ref, g_ref, rows_hbm, y_ref, buf, sems):
    i = pl.program_id(0)
    n = pl.num_programs(0)
    tm = x1_ref.shape[0]

    def issue(blk, slot):
        def body(r, carry):
            base = (blk * tm + r) * TOP_K
            for j in range(TOP_K):
                pltpu.make_async_copy(rows_hbm.at[pl.ds(dest_ref[base + j], 1)], buf.at[slot, j, pl.ds(r, 1)],
                                      sems.at[slot]).start()
            return carry
        lax.fori_loop(0, tm, body, 0)

    @pl.when(i == 0)
    def _():
        issue(0, 0)

    @pl.when(i + 1 < n)
    def _():
        issue(i + 1, (i + 1) % 2)

    slot = i % 2
    for j in range(TOP_K):
        pltpu.make_async_copy(rows_hbm.at[pl.ds(0, tm)], buf.at[slot, j], sems.at[slot]).wait()
    gates = gate_ref[...]
    moe = buf[slot, 0] * gates[:, 0:1]
    for j in range(1, TOP_K):
        moe = moe + buf[slot, j] * gates[:, j:j + 1]
    x2 = x1_ref[...] + mod_ref[:, 5 * D_MODEL:6 * D_MODEL] * moe
    ms = jnp.mean(x2 * x2, axis=-1, keepdims=True)
    y_ref[...] = x2 * lax.rsqrt(ms + NORM_EPS) * g_ref[...]


def _final(dest_flat, x1, mod3, gates, rows_out, g, blocks_per_mod):
    t = x1.shape[0]
    tm = TOK_BLOCK
    mod_rows = mod3.shape[1]
    row = lambda n: pl.BlockSpec((tm, n), lambda i, d: (i, 0))
    grid_spec = pltpu.PrefetchScalarGridSpec(
        num_scalar_prefetch=1,
        grid=(t // tm,),
        in_specs=[row(D_MODEL),
                  pl.BlockSpec((None, mod_rows, 6 * D_MODEL), lambda i, d: (i // blocks_per_mod, 0, 0)),
                  row(LANES),
                  pl.BlockSpec((1, D_MODEL), lambda i, d: (0, 0)),
                  pl.BlockSpec(memory_space=pl.ANY)],
        out_specs=row(D_MODEL),
        scratch_shapes=[pltpu.VMEM((2, TOP_K, tm, D_MODEL), F32), pltpu.SemaphoreType.DMA((2,))])
    return pl.pallas_call(
        _final_kernel,
        grid_spec=grid_spec,
        out_shape=jax.ShapeDtypeStruct((t, D_MODEL), F32),
        compiler_params=_cparams("arbitrary"),
        name="final",
    )(dest_flat, x1, mod3, gates, g, rows_out)


def _rope_tables(pos, rot_dim, theta, head_dim):
    half = rot_dim // 2
    inv = theta ** (-jnp.arange(half, dtype=F32) * (2.0 / rot_dim))
    ang = pos.astype(F32)[:, None] * inv[None, :]
    cos, sin = jnp.cos(ang), jnp.sin(ang)
    n = pos.shape[0]
    rest = head_dim - rot_dim
    zh = jnp.zeros((n, half), F32)
    c = jnp.concatenate([cos, cos, jnp.ones((n, rest), F32)], axis=1)
    sa = jnp.concatenate([-sin, zh, jnp.zeros((n, rest), F32)], axis=1)
    sb = jnp.concatenate([zh, sin, jnp.zeros((n, rest), F32)], axis=1)
    rep = LANES // head_dim
    return tuple(jnp.tile(a, (1, rep)) for a in (c, sa, sb))


def _pack_w_in(w_in):
    offs = np.cumsum((0,) + IN_SPLITS)
    part = lambda j: w_in[:, offs[j]:offs[j + 1]]
    zero = lambda n: jnp.zeros((D_MODEL, n), w_in.dtype)
    group = ATT_HEADS // ATT_KV_HEADS
    cols = []
    wq = part(0)
    for h in range(ATT_HEADS):
        wh = wq[:, h * HEAD_DIM:(h + 1) * HEAD_DIM]
        cols += [wh, zero(HEAD_DIM)] if h // group == 0 else [zero(HEAD_DIM), wh]
    wiq = part(3)
    for h in range(IDX_HEADS):
        cols += [wiq[:, h * IDX_DIM:(h + 1) * IDX_DIM], zero(LANES - IDX_DIM)]
    cols += [part(1), part(2), part(4), part(5), zero(LANES - IDX_DIM - IDX_HEADS)]
    cols += [part(j) for j in range(6, 12)]
    return jnp.concatenate(cols, axis=1).astype(BF16)


def _heads_major(a, db, t, width):
    heads = a.shape[1] // width
    return a.reshape(db, t, heads, width).transpose(0, 2, 1, 3).reshape(db, heads * t, width)


def _pad_rows(a, db, t, rows):
    return jnp.pad(a.reshape(db, t, -1), ((0, 0), (0, rows - t), (0, 0)))


def kernel(x_prompt, x_sample, cache_k, cache_v, cache_ik, state_ret, page_table, c_prompt, c_sample, norm_mix_g, norm_ffn_g, norm_final_g, w_ada, b_ada, w_in, w_branch_a, w_branch_b, w_out, w_router, b_router, w_gate_up, b_gate_up, w_down, b_down):
    batch, seq, _ = x_prompt.shape
    db, dt, _ = x_sample.shape
    assert w_in.shape[0] == 1, "one layer"
    tp, ts = batch * seq, db * dt
    xp = x_prompt.reshape(tp, D_MODEL)
    xs = x_sample.reshape(ts, D_MODEL)

    mod = _adaln(jnp.concatenate([c_prompt, c_sample], axis=0), w_ada[0], b_ada[0])
    mod_p = mod[:batch].reshape(batch, 1, 6 * D_MODEL)
    mod_s = jnp.repeat(mod[batch:], dt, axis=0).reshape(ts // TOK_BLOCK, TOK_BLOCK, 6 * D_MODEL)
    bpm_p = seq // TOK_BLOCK

    w_packed = _pack_w_in(w_in[0])
    pos_p = jnp.arange(seq)
    pos_s = PAST_LEN + (jnp.arange(TOK_BLOCK) % dt)
    g_mix = norm_mix_g[0].reshape(1, D_MODEL)
    outs_p = _inproj(xp, mod_p, g_mix, w_packed, _rope_tables(pos_p, ROPE_DIM, ROPE_THETA, HEAD_DIM),
                     _rope_tables(pos_p, RET_DK, RET_THETA, RET_DK), bpm_p, seq // TOK_BLOCK)
    outs_s = _inproj(xs, mod_s, g_mix, w_packed, _rope_tables(pos_s, ROPE_DIM, ROPE_THETA, HEAD_DIM),
                     _rope_tables(pos_s, RET_DK, RET_THETA, RET_DK), 1, 1)
    (q_p, k_p, v_p, kb_p, vb_p, iq_p, ikw_p, ikb_p, rq_p, rk_p, rv_p, sg_p, sga_p, sgb_p) = outs_p
    (q_s, k_s, v_s, kb_s, vb_s, iq_s, ikw_s, ikb_s, rq_s, rk_s, rv_s, sg_s, sga_s, sgb_s) = outs_s

    tri = (jnp.arange(LANES)[:, None] < jnp.arange(LANES)[None, :]).astype(BF16)

    a_p = _dsa_prompt(q_p, iq_p, ikw_p, kb_p, vb_p, ikb_p, tri, batch, seq)
    group = ATT_HEADS // ATT_KV_HEADS
    q4 = q_s.reshape(db, dt, ATT_HEADS, LANES)
    qs = jnp.stack([q4[:, :, h, (h // group) * HEAD_DIM:(h // group + 1) * HEAD_DIM] for h in range(ATT_HEADS)],
                   axis=1).reshape(db, ATT_HEADS * dt, HEAD_DIM)
    iqs = _heads_major(iq_s, db, dt, LANES)[:, :, :IDX_DIM]
    ws = _heads_major(ikw_s[:, IDX_DIM:IDX_DIM + IDX_HEADS] * (IDX_HEADS ** -0.5), db, dt, 1)
    new_t = lambda a: jnp.pad(a.reshape(db, dt, ATT_KV_HEADS, HEAD_DIM).transpose(0, 2, 3, 1),
                              ((0, 0), (0, 0), (0, 0), (0, LANES - dt)))
    iknew_t = jnp.pad(ikb_s[:, :IDX_DIM].reshape(db, dt, IDX_DIM).transpose(0, 2, 1),
                      ((0, 0), (0, 0), (0, LANES - dt)))
    o_s = _dsa_sample(page_table, qs, iqs, ws, new_t(kb_s), new_t(vb_s), iknew_t,
                      cache_ik[0].transpose(0, 2, 1), cache_k[0].transpose(0, 2, 3, 1),
                      cache_v[0].transpose(0, 2, 3, 1), tri, dt)
    a_s = o_s.reshape(db, ATT_HEADS, dt, HEAD_DIM).transpose(0, 2, 1, 3).reshape(ts, ATT_OUT).astype(BF16)

    r_p, st_p = _retention(rq_p, rk_p, rv_p, sg_p, _retention_tables(float(RET_CHUNK)), None, batch,
                           seq // RET_CHUNK)
    pad = lambda a: _pad_rows(a, db, dt, RET_CHUNK).reshape(db * RET_CHUNK, -1)
    r_s, st_s = _retention(pad(rq_s), pad(rk_s), pad(rv_s), pad(sg_s), _retention_tables(float(dt)),
                           state_ret[0].reshape(db, RET_HEADS // 2, LANES, LANES), db, 1)
    r_s = r_s.reshape(db, RET_CHUNK, RET_OUT)[:, :dt].reshape(ts, RET_OUT)

    wr = jnp.pad(w_router[0], ((0, 0), (0, LANES - N_EXPERTS)))
    br = jnp.concatenate([b_router[0], jnp.full((LANES - N_EXPERTS,), -jnp.inf, F32)]).reshape(1, LANES)
    post_w = (w_branch_a[0].astype(BF16), w_branch_b[0].astype(BF16), w_out[0].astype(BF16),
              norm_ffn_g[0].reshape(1, D_MODEL), wr, br)
    ar = jnp.arange(TOK_BLOCK)
    ltri = (ar[None, :] < ar[:, None]).astype(BF16)
    x1_p, h2w_p, idx_p, gate_p, rank_p, cnt_p = _post(a_p, r_p, sga_p, sgb_p, xp, mod_p, *post_w, ltri,
                                                      jnp.zeros((1, LANES), F32), bpm_p)
    x1_s, h2w_s, idx_s, gate_s, rank_s, cnt_all = _post(a_s, r_s, sga_s, sgb_s, xs, mod_s, *post_w, ltri, cnt_p, 1)

    n_asg = (tp + ts) * TOP_K
    counts = cnt_all[0, :N_EXPERTS].astype(I32)
    padded = (counts + MOE_ROWS - 1) // MOE_ROWS * MOE_ROWS
    pend = jnp.cumsum(padded)
    pstart = pend - padded
    dest_p = (pstart[idx_p[:, :TOP_K]] + rank_p[:, :TOP_K]).reshape(-1)
    dest_s = (pstart[idx_s[:, :TOP_K]] + rank_s[:, :TOP_K]).reshape(-1)
    n_blocks = -(-n_asg // MOE_ROWS) + N_EXPERTS
    blk_start = jnp.arange(n_blocks, dtype=I32) * MOE_ROWS
    blk_expert = jnp.minimum(jnp.sum((blk_start[:, None] >= pend[None, :]).astype(I32), axis=1), N_EXPERTS - 1)
    blk_first = jnp.concatenate([jnp.ones((1,), I32), (blk_expert[1:] != blk_expert[:-1]).astype(I32)])
    n_used = (pend[-1] // MOE_ROWS).astype(I32).reshape(1)
    x_sorted = jnp.zeros((n_blocks * MOE_ROWS, D_MODEL // 2), jnp.uint32)
    x_sorted = _dispatch(dest_p, h2w_p, x_sorted)
    x_sorted = _dispatch(dest_s, h2w_s, x_sorted)
    rows_out = _moe(blk_expert, blk_first, n_used, x_sorted, w_gate_up[0], b_gate_up[0], w_down[0], b_down[0])

    g_final = norm_final_g.reshape(1, D_MODEL)
    y_p = _final(dest_p, x1_p, mod_p, gate_p, rows_out, g_final, bpm_p)
    y_s = _final(dest_s, x1_s, mod_s, gate_s, rows_out, g_final, 1)

    kv_shape = lambda b, s: (1, b, s, ATT_KV_HEADS, HEAD_DIM)
    st_shape = lambda b: (1, b, RET_HEADS, RET_DK, RET_DV)
    return (y_p.reshape(batch, seq, D_MODEL), y_s.reshape(db, dt, D_MODEL),
            k_p.reshape(kv_shape(batch, seq)), v_p.reshape(kv_shape(batch, seq)),
            ikw_p[:, :IDX_DIM].reshape(1, batch, seq, IDX_DIM), st_p.reshape(st_shape(batch)),
            k_s.reshape(kv_shape(db, dt)), v_s.reshape(kv_shape(db, dt)),
            ikw_s[:, :IDX_DIM].reshape(1, db, dt, IDX_DIM), st_s.reshape(st_shape(db)))
```

```python
import functools

import jax
import jax.numpy as jnp
import numpy as np
from jax import lax
from jax.experimental import pallas as pl
from jax.experimental.pallas import tpu as pltpu

F32 = jnp.float32
BF16 = jnp.bfloat16
I32 = jnp.int32

D_MODEL = 1024
PAST_LEN = 8192
PAGE_SIZE = 128
ATT_HEADS = 8
ATT_KV_HEADS = 2
HEAD_DIM = 64
ROPE_DIM = HEAD_DIM // 4
ROPE_THETA = 500000.0
IDX_HEADS = 8
IDX_DIM = 64
IDX_ROPE_DIM = IDX_DIM // 4
TOPK_MAX = 256
RET_HEADS = 8
RET_DK = 64
RET_DV = 128
RET_THETA = 10000.0
RET_CHUNK = 128
N_EXPERTS = 32
TOP_K = 4
D_FF = D_MODEL
SWIGLU_LIMIT = 7.0
SWIGLU_ALPHA = 1.702
NORM_EPS = 1e-6
GN_EPS = 1e-5
ATT_OUT = ATT_HEADS * HEAD_DIM
RET_OUT = RET_HEADS * RET_DV
IN_SPLITS = (ATT_HEADS * HEAD_DIM, ATT_KV_HEADS * HEAD_DIM, ATT_KV_HEADS * HEAD_DIM,
             IDX_HEADS * IDX_DIM, IDX_DIM, IDX_HEADS,
             RET_HEADS * RET_DK, RET_HEADS * RET_DK, RET_OUT, RET_OUT, D_MODEL, D_MODEL)

LANES = 128
MASK_NEG = -1e30
FLT_MAX = 3.4028234663852886e38
SELECT_UNROLL = 4
CAUSAL_VARIANTS = 8
IDX_KEY_CHUNK = 256
VMEM_LIMIT = 56 * 1024 * 1024

TOK_BLOCK = 256
PROMPT_TOK_BLOCK = 512
Q_BLOCK = 128
MOE_ROWS = 512

_W_GROUPS = (("q", 1024), ("iq", 1024), ("kvi", 384), ("rq", 512), ("rk", 512),
             ("rv", 1024), ("rg", 1024), ("ga", 1024), ("gb", 1024))
PROJ_COLS = 512
_W_OFF = {}
_off = 0
for _n, _w in _W_GROUPS:
    _W_OFF[_n] = (_off, _w)
    _off += _w
W_COLS = _off


def _cparams(*sem):
    return pltpu.CompilerParams(dimension_semantics=sem, vmem_limit_bytes=VMEM_LIMIT)


def _adaln_kernel(c_ref, w_ref, b_ref, o_ref):
    c = c_ref[...]
    s = c * jax.nn.sigmoid(c)
    o_ref[...] = jnp.dot(s, w_ref[...], preferred_element_type=F32, precision=lax.Precision.HIGHEST) + b_ref[...]


def _adaln(c_all, w_ada, b_ada):
    n = c_all.shape[0]
    nb = 1536
    return pl.pallas_call(
        _adaln_kernel,
        grid=(6 * D_MODEL // nb,),
        in_specs=[pl.BlockSpec((n, D_MODEL), lambda j: (0, 0)),
                  pl.BlockSpec((D_MODEL, nb), lambda j: (0, j)),
                  pl.BlockSpec((1, nb), lambda j: (0, j))],
        out_specs=pl.BlockSpec((n, nb), lambda j: (0, j)),
        out_shape=jax.ShapeDtypeStruct((n, 6 * D_MODEL), F32),
        compiler_params=_cparams("arbitrary"),
        name="adaln",
    )(c_all, w_ada, b_ada.reshape(1, -1))


def _rope_slab(z, c, sa, sb, half):
    return z * c + pltpu.roll(z, LANES - half, 1) * sa + pltpu.roll(z, half, 1) * sb


def _inproj_kernel(x_ref, mod_ref, g_ref, w_ref, ca_ref, saa_ref, sba_ref, cr_ref, sar_ref, sbr_ref,
                   q_ref, k_ref, v_ref, kb_ref, vb_ref, iq_ref, ikw_ref, ikb_ref,
                   rq_ref, rk_ref, rv_ref, sg_ref, sga_ref, sgb_ref):
    x = x_ref[...]
    ms = jnp.mean(x * x, axis=-1, keepdims=True)
    y = x * lax.rsqrt(ms + NORM_EPS) * g_ref[...]
    h = (y * (1.0 + mod_ref[:, D_MODEL:2 * D_MODEL]) + mod_ref[:, 0:D_MODEL]).astype(BF16)

    def slabs(name):
        c0, width = _W_OFF[name]
        step = min(width, PROJ_COLS)
        for j in range(width // step):
            z = jnp.dot(h, w_ref[:, c0 + j * step:c0 + (j + 1) * step], preferred_element_type=F32)
            for s in range(step // LANES):
                yield j * (step // LANES) + s, z[:, s * LANES:(s + 1) * LANES]

    ca, saa, sba = ca_ref[...], saa_ref[...], sba_ref[...]
    cr, sar, sbr = cr_ref[...], sar_ref[...], sbr_ref[...]
    att_half, ret_half = ROPE_DIM // 2, RET_DK // 2
    lane = lax.broadcasted_iota(I32, (x.shape[0], LANES), 1)
    sl = lambda s: slice(s * LANES, (s + 1) * LANES)

    for s, z in slabs("q"):
        q_ref[:, sl(s)] = (_rope_slab(z, ca, saa, sba, att_half) * 0.125).astype(BF16)
    for s, z in slabs("iq"):
        iq_ref[:, sl(s)] = (_rope_slab(z, ca, saa, sba, att_half) * 0.125).astype(BF16)
    (_, zk), (_, zv), (_, zi) = slabs("kvi")
    kk = _rope_slab(zk, ca, saa, sba, att_half)
    k_ref[...] = kk
    kb_ref[...] = kk.astype(BF16)
    v_ref[...] = zv
    vb_ref[...] = zv.astype(BF16)
    zr = _rope_slab(zi, ca, saa, sba, att_half)
    ikw_ref[...] = jnp.where(lane < IDX_DIM, zr, zi)
    ikb_ref[...] = jnp.where(lane < IDX_DIM, zr, 0.0).astype(BF16)
    for s, z in slabs("rq"):
        rq_ref[:, sl(s)] = _rope_slab(z, cr, sar, sbr, ret_half).astype(BF16)
    for s, z in slabs("rk"):
        rk_ref[:, sl(s)] = (_rope_slab(z, cr, sar, sbr, ret_half) * 0.125).astype(BF16)
    for s, z in slabs("rv"):
        rv_ref[:, sl(s)] = z.astype(BF16)
    for s, z in slabs("rg"):
        sg_ref[:, sl(s)] = (z * jax.nn.sigmoid(z)).astype(BF16)
    for s, z in slabs("ga"):
        sga_ref[:, sl(s)] = jax.nn.sigmoid(z).astype(BF16)
    for s, z in slabs("gb"):
        sgb_ref[:, sl(s)] = jax.nn.sigmoid(z).astype(BF16)


def _inproj(x, mod3, g, w_packed, tabs_att, tabs_ret, tm, blocks_per_mod, tab_blocks):
    t = x.shape[0]
    nblk = t // tm
    mod_rows = mod3.shape[1]
    tab_spec = pl.BlockSpec((tm, LANES), lambda i: (i % tab_blocks, 0))
    row = lambda n: pl.BlockSpec((tm, n), lambda i: (i, 0))
    out_defs = [(1024, BF16), (128, F32), (128, F32), (128, BF16), (128, BF16), (1024, BF16), (128, F32),
                (128, BF16), (512, BF16), (512, BF16), (1024, BF16), (1024, BF16), (1024, BF16), (1024, BF16)]
    return pl.pallas_call(
        _inproj_kernel,
        grid=(nblk,),
        in_specs=[row(D_MODEL),
                  pl.BlockSpec((None, mod_rows, 6 * D_MODEL), lambda i: (i // blocks_per_mod, 0, 0)),
                  pl.BlockSpec((1, D_MODEL), lambda i: (0, 0)),
                  pl.BlockSpec((D_MODEL, W_COLS), lambda i: (0, 0), pipeline_mode=pl.Buffered(1))]
                 + [tab_spec] * 6,
        out_specs=[row(n) for n, _ in out_defs],
        out_shape=[jax.ShapeDtypeStruct((t, n), d) for n, d in out_defs],
        compiler_params=_cparams("parallel"),
        name="inproj",
    )(x, mod3, g, w_packed, *tabs_att, *tabs_ret)


def _count(score_ref, n, pred):
    acc = jnp.zeros((score_ref.shape[0], LANES), F32)
    for c in range(n // LANES):
        acc = acc + jnp.where(pred(score_ref[:, c * LANES:(c + 1) * LANES]), 1.0, 0.0)
    return jnp.sum(acc, axis=1, keepdims=True)


def _kth_largest(score_ref, n, k):
    sc = score_ref[:, :n]
    finite = sc > -jnp.inf
    n_fin = jnp.sum(jnp.where(finite, 1.0, 0.0), axis=1, keepdims=True)
    n_pos = jnp.sum(jnp.where(sc > 0.0, 1.0, 0.0), axis=1, keepdims=True)
    n_nonneg = jnp.sum(jnp.where(sc >= 0.0, 1.0, 0.0), axis=1, keepdims=True)
    mx = jnp.max(sc, axis=1, keepdims=True)
    mn = jnp.min(jnp.where(finite, sc, jnp.inf), axis=1, keepdims=True)
    small = n_fin <= k
    positive = n_pos >= k
    at_zero = jnp.logical_and(jnp.logical_not(positive), n_nonneg >= k)
    lo = jnp.where(positive, 0.0, mn)
    hi = jnp.where(positive, mx + (jnp.abs(mx) * 2.0 ** -20 + 2.0 ** -100), 0.0)
    lo = jnp.where(at_zero, 0.0, lo)
    done = jnp.where(jnp.logical_or(small, at_zero), 1.0, 0.0)

    def cond(state):
        return jnp.min(state[2]) < 0.5

    def body(state):
        lo, hi, done = state
        for _ in range(SELECT_UNROLL):
            mid = 0.5 * lo + 0.5 * hi
            cnt = _count(score_ref, n, lambda s: s >= mid)
            stuck = jnp.logical_or(mid <= lo, mid >= hi)
            live = jnp.logical_and(done < 0.5, jnp.logical_not(stuck))
            ge = cnt >= k
            lo = jnp.where(jnp.logical_and(live, ge), mid, lo)
            hi = jnp.where(jnp.logical_and(live, jnp.logical_not(ge)), mid, hi)
            done = jnp.where(jnp.logical_or(stuck, cnt == k), 1.0, done)
        return lo, hi, done

    lo, _, _ = lax.while_loop(cond, body, (lo, hi, done))
    return jnp.where(small, -FLT_MAX, lo)


def _topk_bias(score_ref, bias_ref, tri_ref, n, k):
    rows = score_ref.shape[0]
    thr = _kth_largest(score_ref, n, k)
    need = k - _count(score_ref, n, lambda s: s > thr)
    n_eq = _count(score_ref, n, lambda s: s == thr)
    has_ties = jnp.max(jnp.where(need < n_eq, 1.0, 0.0)) > 0.5

    @pl.when(jnp.logical_not(has_ties))
    def _():
        for c in range(n // LANES):
            sl = slice(c * LANES, (c + 1) * LANES)
            bias_ref[:, sl] = jnp.where(score_ref[:, sl] >= thr, 0.0, MASK_NEG)

    @pl.when(has_ties)
    def _():
        run = jnp.zeros((rows, 1), F32)
        for c in range(n // LANES):
            sl = slice(c * LANES, (c + 1) * LANES)
            sc = score_ref[:, sl]
            eq = sc == thr
            eqf = jnp.where(eq, 1.0, 0.0)
            before = jnp.dot(eqf.astype(BF16), tri_ref[...], preferred_element_type=F32) + run
            take = jnp.logical_or(sc > thr, jnp.logical_and(eq, before < need))
            bias_ref[:, sl] = jnp.where(take, 0.0, MASK_NEG)
            run = run + jnp.sum(eqf, axis=1, keepdims=True)


def _dsa_prompt_kernel(q_ref, iq_ref, ikw_ref, kb_ref, vb_ref, ikb_ref, tri_ref, o_ref, score_ref, bias_ref):
    qb, s_len = score_ref.shape
    i = pl.program_id(1)
    nqb = s_len // qb
    per_variant = nqb // CAUSAL_VARIANTS
    for v in range(CAUSAL_VARIANTS):
        pl.when(i // per_variant == v)(
            functools.partial(_dsa_prompt_body, q_ref, iq_ref, ikw_ref, kb_ref, vb_ref, ikb_ref, tri_ref, o_ref,
                              score_ref, bias_ref, (v + 1) * per_variant * qb, min(TOPK_MAX, s_len // 4)))


def _dsa_prompt_body(q_ref, iq_ref, ikw_ref, kb_ref, vb_ref, ikb_ref, tri_ref, o_ref, score_ref, bias_ref,
                     n_keys, topk):
    qb = score_ref.shape[0]
    i = pl.program_id(1)
    w = ikw_ref[:, IDX_DIM:IDX_DIM + IDX_HEADS] * (IDX_HEADS ** -0.5)
    nt = (((1,), (1,)), ((), ()))
    kc = IDX_KEY_CHUNK
    qpos = i * qb + lax.broadcasted_iota(I32, (qb, kc), 0)
    for c in range(n_keys // kc):
        ikc = ikb_ref[c * kc:(c + 1) * kc, :]
        acc = jnp.zeros((qb, kc), F32)
        for h in range(IDX_HEADS):
            d = lax.dot_general(iq_ref[:, h * LANES:(h + 1) * LANES], ikc, nt, preferred_element_type=F32)
            acc = acc + jnp.maximum(d, 0.0) * w[:, h:h + 1]
        kpos = c * kc + lax.broadcasted_iota(I32, (qb, kc), 1)
        score_ref[:, c * kc:(c + 1) * kc] = jnp.where(kpos <= qpos, acc, -jnp.inf)

    _topk_bias(score_ref, bias_ref, tri_ref, n_keys, topk)

    kb = kb_ref[0:n_keys, :]
    vb = vb_ref[0:n_keys, :]
    bias = bias_ref[:, 0:n_keys]
    lane = lax.broadcasted_iota(I32, (qb, LANES), 1)
    heads = []
    for h in range(ATT_HEADS):
        s = lax.dot_general(q_ref[:, h * LANES:(h + 1) * LANES], kb, nt, preferred_element_type=F32) + bias
        m = jnp.max(s, axis=1, keepdims=True)
        p = jnp.exp(s - m)
        l = jnp.sum(p, axis=1, keepdims=True)
        heads.append(jnp.dot(p.astype(BF16), vb, preferred_element_type=F32) / l)
    group = ATT_HEADS // ATT_KV_HEADS
    for pp in range(ATT_HEADS // 2):
        a, b = heads[2 * pp], heads[2 * pp + 1]
        if (2 * pp) // group == 0:
            slab = jnp.where(lane < HEAD_DIM, a, pltpu.roll(b, HEAD_DIM, 1))
        else:
            slab = jnp.where(lane < HEAD_DIM, pltpu.roll(a, HEAD_DIM, 1), b)
        o_ref[:, pp * LANES:(pp + 1) * LANES] = slab.astype(BF16)


def _dsa_prompt(q, iq, ikw, kb, vb, ikb, tri, batch, seq):
    nqb = seq // Q_BLOCK
    qrow = lambda n: pl.BlockSpec((Q_BLOCK, n), lambda b, i: (b * nqb + i, 0))
    keys = pl.BlockSpec((seq, LANES), lambda b, i: (b, 0))
    return pl.pallas_call(
        _dsa_prompt_kernel,
        grid=(batch, nqb),
        in_specs=[qrow(1024), qrow(1024), qrow(LANES), keys, keys, keys,
                  pl.BlockSpec((LANES, LANES), lambda b, i: (0, 0))],
        out_specs=qrow(ATT_OUT),
        out_shape=jax.ShapeDtypeStruct((batch * seq, ATT_OUT), BF16),
        scratch_shapes=[pltpu.VMEM((Q_BLOCK, seq), F32), pltpu.VMEM((Q_BLOCK, seq), F32)],
        compiler_params=_cparams("parallel", "arbitrary"),
        name="dsa_prompt",
    )(q, iq, ikw, kb, vb, ikb, tri)


def _dsa_sample_kernel(pt_ref, qs_ref, iqs_ref, ws_ref, knew_ref, vnew_ref, iknew_ref, cik_hbm, ck_hbm, cv_hbm,
                       tri_ref, o_ref, ikbuf, kbuf, vbuf, ikt, kt, vt, sems, key_ref, bias_ref):
    db = pl.program_id(0)
    n_pages = ikbuf.shape[1]
    t = key_ref.shape[0]
    n_past = n_pages * PAGE_SIZE
    last = pl.num_programs(0) - 1

    def fetch(src, dst_of_page, sem, req):
        def body(p, carry):
            pltpu.make_async_copy(src.at[pt_ref[req, p]], dst_of_page(p), sem).start()
            return carry
        lax.fori_loop(0, n_pages, body, 0)

    def wait_all(src, dst, sem):
        pltpu.make_async_copy(src.at[pl.ds(0, n_pages)], dst, sem).wait()

    fetch_ik = lambda req, slot: fetch(cik_hbm, lambda p: ikbuf.at[slot, p], sems.at[slot], req)
    fetch_k = lambda req: fetch(ck_hbm, lambda p: kbuf.at[p], sems.at[2], req)
    fetch_v = lambda req: fetch(cv_hbm, lambda p: vbuf.at[p], sems.at[3], req)

    @pl.when(db == 0)
    def _():
        fetch_ik(0, 0)
        fetch_k(0)
        fetch_v(0)

    @pl.when(db < last)
    def _():
        fetch_ik(db + 1, (db + 1) % 2)

    slot = db % 2
    wait_all(cik_hbm, ikbuf.at[slot], sems.at[slot])

    nt = (((1,), (1,)), ((), ()))
    page = lambda p: slice(p * PAGE_SIZE, (p + 1) * PAGE_SIZE)
    for p in range(n_pages):
        ikt[:, page(p)] = ikbuf[slot, p].astype(BF16)
    iqs = iqs_ref[...]
    wcol = ws_ref[...]
    d_past = jnp.maximum(jnp.dot(iqs, ikt[...], preferred_element_type=F32), 0.0) * wcol
    d_new = jnp.maximum(jnp.dot(iqs, iknew_ref[...], preferred_element_type=F32), 0.0) * wcol
    s_past = d_past[0:t]
    s_new = d_new[0:t]
    for h in range(1, IDX_HEADS):
        s_past = s_past + d_past[h * t:(h + 1) * t]
        s_new = s_new + d_new[h * t:(h + 1) * t]
    row = lax.broadcasted_iota(I32, (t, LANES), 0)
    lane = lax.broadcasted_iota(I32, (t, LANES), 1)
    new_ok = lane <= row
    key_ref[:, 0:n_past] = s_past
    key_ref[:, n_past:n_past + LANES] = jnp.where(new_ok, s_new, -jnp.inf)
    _topk_bias(key_ref, bias_ref, tri_ref, n_past + LANES, min(TOPK_MAX, (n_past + t) // 4))

    def stage(buf, dst):
        for p in range(n_pages):
            for j in range(ATT_KV_HEADS):
                dst[j, :, page(p)] = buf[p, j].astype(BF16)

    wait_all(ck_hbm, kbuf, sems.at[2])
    stage(kbuf, kt)

    @pl.when(db < last)
    def _():
        fetch_k(db + 1)

    wait_all(cv_hbm, vbuf, sems.at[3])
    stage(vbuf, vt)

    @pl.when(db < last)
    def _():
        fetch_v(db + 1)

    rows_per_kv = qs_ref.shape[0] // ATT_KV_HEADS
    bias = jnp.concatenate([bias_ref[...]] * (rows_per_kv // t), axis=0)
    for j in range(ATT_KV_HEADS):
        qj = qs_ref[j * rows_per_kv:(j + 1) * rows_per_kv, :]
        sp = jnp.dot(qj, kt[j], preferred_element_type=F32) + bias[:, 0:n_past]
        sn = jnp.dot(qj, knew_ref[j], preferred_element_type=F32) + bias[:, n_past:n_past + LANES]
        m = jnp.maximum(jnp.max(sp, axis=1, keepdims=True), jnp.max(sn, axis=1, keepdims=True))
        pp = jnp.exp(sp - m)
        pn = jnp.exp(sn - m)
        l = jnp.sum(pp, axis=1, keepdims=True) + jnp.sum(pn, axis=1, keepdims=True)
        o = (lax.dot_general(pp.astype(BF16), vt[j], nt, preferred_element_type=F32)
             + lax.dot_general(pn.astype(BF16), vnew_ref[j], nt, preferred_element_type=F32))
        o_ref[j * rows_per_kv:(j + 1) * rows_per_kv, :] = o / l


def _dsa_sample(page_table, qs, iqs, ws, knew, vnew, iknew, cache_ik, cache_k, cache_v, tri, t):
    db, n_pages = page_table.shape
    rows = qs.shape[1]
    per_db = lambda r, n: pl.BlockSpec((None, r, n), lambda b, pt: (b, 0, 0))
    any_spec = pl.BlockSpec(memory_space=pl.ANY)
    n_keys = n_pages * PAGE_SIZE + LANES
    grid_spec = pltpu.PrefetchScalarGridSpec(
        num_scalar_prefetch=1,
        grid=(db,),
        in_specs=[per_db(rows, HEAD_DIM), per_db(rows, IDX_DIM), per_db(rows, 1),
                  pl.BlockSpec((None, ATT_KV_HEADS, HEAD_DIM, LANES), lambda b, pt: (b, 0, 0, 0)),
                  pl.BlockSpec((None, ATT_KV_HEADS, HEAD_DIM, LANES), lambda b, pt: (b, 0, 0, 0)),
                  per_db(IDX_DIM, LANES), any_spec, any_spec, any_spec,
                  pl.BlockSpec((LANES, LANES), lambda b, pt: (0, 0))],
        out_specs=per_db(rows, HEAD_DIM),
        scratch_shapes=[pltpu.VMEM((2, n_pages, IDX_DIM, PAGE_SIZE), F32),
                        pltpu.VMEM((n_pages, ATT_KV_HEADS, HEAD_DIM, PAGE_SIZE), F32),
                        pltpu.VMEM((n_pages, ATT_KV_HEADS, HEAD_DIM, PAGE_SIZE), F32),
                        pltpu.VMEM((IDX_DIM, n_pages * PAGE_SIZE), BF16),
                        pltpu.VMEM((ATT_KV_HEADS, HEAD_DIM, n_pages * PAGE_SIZE), BF16),
                        pltpu.VMEM((ATT_KV_HEADS, HEAD_DIM, n_pages * PAGE_SIZE), BF16),
                        pltpu.SemaphoreType.DMA((4,)),
                        pltpu.VMEM((t, n_keys), F32),
                        pltpu.VMEM((t, n_keys), F32)])
    return pl.pallas_call(
        _dsa_sample_kernel,
        grid_spec=grid_spec,
        out_shape=jax.ShapeDtypeStruct((db, rows, HEAD_DIM), F32),
        compiler_params=_cparams("arbitrary"),
        name="dsa_sample",
    )(page_table, qs, iqs, ws, knew, vnew, iknew, cache_ik, cache_k, cache_v, tri)


def _retention_kernel(has_init, rq_ref, rk_ref, rv_ref, sg_ref, decay_ref, qdec_ref, kdec_ref, gst_ref, *rest):
    if has_init:
        init_ref, o_ref, st_ref, state = rest
    else:
        o_ref, st_ref, state = rest
    c = pl.program_id(1)

    @pl.when(c == 0)
    def _():
        if has_init:
            state[...] = init_ref[...]
        else:
            state[...] = jnp.zeros_like(state)

    nt = (((1,), (1,)), ((), ()))
    tn = (((0,), (0,)), ((), ()))
    rows = rq_ref.shape[0]
    lane = lax.broadcasted_iota(I32, (rows, LANES), 1)
    for p in range(RET_HEADS // 2):
        sl = slice(p * LANES, (p + 1) * LANES)
        qp = rq_ref[:, sl]
        kp = rk_ref[:, sl]
        qd = (qp.astype(F32) * qdec_ref[:, sl]).astype(BF16)
        kd = (kp.astype(F32) * kdec_ref[:, sl]).astype(BF16)
        s_old = state[p]
        s_old_b = s_old.astype(BF16)
        s_new = s_old * gst_ref[p]
        for e in range(2):
            h = 2 * p + e
            hs = slice(h * LANES, (h + 1) * LANES)
            mine = jnp.where((lane >= e * RET_DK) & (lane < (e + 1) * RET_DK), 1.0, 0.0).astype(BF16)
            sc = lax.dot_general(qp * mine, kp, nt, preferred_element_type=F32) * decay_ref[h]
            vh = rv_ref[:, hs]
            o = (jnp.dot(sc.astype(BF16), vh, preferred_element_type=F32)
                 + jnp.dot(qd * mine, s_old_b, preferred_element_type=F32))
            s_new = s_new + lax.dot_general(kd * mine, vh, tn, preferred_element_type=F32)
            mu = jnp.mean(o, axis=-1, keepdims=True)
            var = jnp.mean(jnp.square(o - mu), axis=-1, keepdims=True)
            on = (o - mu) * lax.rsqrt(var + GN_EPS)
            o_ref[:, hs] = (on * sg_ref[:, hs].astype(F32)).astype(BF16)
        state[p] = s_new

    @pl.when(c == pl.num_programs(1) - 1)
    def _():
        st_ref[...] = state[...]


def _retention(rq, rk, rv, sg, tables, init, batch, n_chunks):
    decay, qdec, kdec, gst = tables
    cr = RET_CHUNK
    rowspec = lambda n: pl.BlockSpec((cr, n), lambda b, c: (b * n_chunks + c, 0))
    const = lambda shape: pl.BlockSpec(shape, lambda b, c: (0,) * len(shape))
    st_spec = pl.BlockSpec((None, RET_HEADS // 2, LANES, LANES), lambda b, c: (b, 0, 0, 0))
    in_specs = [rowspec(512), rowspec(512), rowspec(RET_OUT), rowspec(RET_OUT),
                const(decay.shape), const(qdec.shape), const(kdec.shape), const(gst.shape)]
    args = [rq, rk, rv, sg, decay, qdec, kdec, gst]
    if init is not None:
        in_specs.append(st_spec)
        args.append(init)
    return pl.pallas_call(
        functools.partial(_retention_kernel, init is not None),
        grid=(batch, n_chunks),
        in_specs=in_specs,
        out_specs=[rowspec(RET_OUT), st_spec],
        out_shape=[jax.ShapeDtypeStruct((batch * n_chunks * cr, RET_OUT), BF16),
                   jax.ShapeDtypeStruct((batch, RET_HEADS // 2, LANES, LANES), F32)],
        scratch_shapes=[pltpu.VMEM((RET_HEADS // 2, LANES, LANES), F32)],
        compiler_params=_cparams("parallel", "arbitrary"),
        name="retention",
    )(*args)


def _retention_tables(c_eff):
    lg = jnp.log(1.0 - 2.0 ** (-5.0 - jnp.arange(RET_HEADS, dtype=F32)))
    i = jnp.arange(RET_CHUNK, dtype=F32)
    diff = i[:, None] - i[None, :]
    decay = jnp.where(diff >= 0, jnp.exp(jnp.maximum(diff, 0.0)[None] * lg[:, None, None]), 0.0)
    q_decay = jnp.exp((i + 1.0)[:, None] * lg[None, :])
    k_decay = jnp.exp((c_eff - 1.0 - i)[:, None] * lg[None, :])
    qdec = jnp.repeat(q_decay, RET_DK, axis=1)
    kdec = jnp.repeat(k_decay, RET_DK, axis=1)
    g_state = jnp.exp(c_eff * lg)
    gst = jnp.broadcast_to(jnp.repeat(g_state, RET_DK).reshape(RET_HEADS // 2, LANES, 1),
                           (RET_HEADS // 2, LANES, LANES))
    return decay, qdec, kdec, gst


def _pack_bf16_pairs(lo, hi):
    return pltpu.pack_elementwise([lo, hi], packed_dtype=BF16)


def _unpack_bf16_pairs(words):
    return tuple(pltpu.unpack_elementwise(words, index=j, packed_dtype=BF16, unpacked_dtype=F32).astype(BF16)
                 for j in range(2))


def _post_kernel(a_ref, r_ref, sga_ref, sgb_ref, x_ref, mod_ref, wpa_ref, wpb_ref, wo_ref, g_ref, wr_ref, br_ref,
                 ltri_ref, cnt0_ref, x1_ref, h2w_ref, idx_ref, gate_ref, rank_ref, cnt_ref, cnt):
    @pl.when(pl.program_id(0) == 0)
    def _():
        cnt[...] = cnt0_ref[...]

    pa = jnp.dot(a_ref[...], wpa_ref[...], preferred_element_type=F32)
    pb = jnp.dot(r_ref[...], wpb_ref[...], preferred_element_type=F32)
    merged = sga_ref[...].astype(F32) * pa + sgb_ref[...].astype(F32) * pb
    gt1 = mod_ref[:, 2 * D_MODEL:3 * D_MODEL]
    x1 = x_ref[...] + gt1 * jnp.dot(merged.astype(BF16), wo_ref[...], preferred_element_type=F32)
    x1_ref[...] = x1
    ms = jnp.mean(x1 * x1, axis=-1, keepdims=True)
    y = x1 * lax.rsqrt(ms + NORM_EPS) * g_ref[...]
    h2 = y * (1.0 + mod_ref[:, 4 * D_MODEL:5 * D_MODEL]) + mod_ref[:, 3 * D_MODEL:4 * D_MODEL]
    half = D_MODEL // 2
    h2w_ref[...] = _pack_bf16_pairs(h2[:, :half], h2[:, half:])
    logits = jnp.dot(h2, wr_ref[...], preferred_element_type=F32, precision=lax.Precision.HIGHEST) + br_ref[...]
    lane = lax.broadcasted_iota(I32, logits.shape, 1).astype(F32)
    idx_out = jnp.zeros(logits.shape, F32)
    val_out = jnp.zeros(logits.shape, F32)
    chosen = []
    top = None
    for j in range(TOP_K):
        m = jnp.max(logits, axis=1, keepdims=True)
        am = jnp.min(jnp.where(logits == m, lane, float(LANES)), axis=1, keepdims=True)
        if j == 0:
            top = m
        idx_out = jnp.where(lane == j, am, idx_out)
        val_out = jnp.where(lane == j, jnp.exp(m - top), val_out)
        chosen.append(lane == am)
        logits = jnp.where(chosen[-1], -jnp.inf, logits)
    idx_ref[...] = jnp.transpose(idx_out)[0:8, :].astype(I32)
    gate_ref[...] = val_out / jnp.sum(val_out, axis=1, keepdims=True)
    onehot = jnp.where(chosen[0] | chosen[1] | chosen[2] | chosen[3], 1.0, 0.0)
    before = jnp.dot(ltri_ref[...], onehot.astype(BF16), preferred_element_type=F32) + cnt[...]
    rank_out = jnp.zeros(logits.shape, F32)
    for j in range(TOP_K):
        rj = jnp.sum(jnp.where(chosen[j], before, 0.0), axis=1, keepdims=True)
        rank_out = jnp.where(lane == j, rj, rank_out)
    rank_ref[...] = jnp.transpose(rank_out)[0:8, :].astype(I32)
    cnt[...] = cnt[...] + jnp.sum(onehot, axis=0, keepdims=True)
    cnt_ref[...] = cnt[...]


def _post(a, r, sga, sgb, x, mod3, wpa, wpb, wo, g, wr, br, cnt0, tm, blocks_per_mod):
    t = x.shape[0]
    mod_rows = mod3.shape[1]
    ar = jnp.arange(tm)
    ltri = (ar[None, :] < ar[:, None]).astype(BF16)
    row = lambda n: pl.BlockSpec((tm, n), lambda i: (i, 0))
    col8 = pl.BlockSpec((8, tm), lambda i: (0, i))
    const = lambda a_: pl.BlockSpec(a_.shape, lambda i: (0,) * a_.ndim)
    return pl.pallas_call(
        _post_kernel,
        grid=(t // tm,),
        in_specs=[row(ATT_OUT), row(RET_OUT), row(D_MODEL), row(D_MODEL), row(D_MODEL),
                  pl.BlockSpec((None, mod_rows, 6 * D_MODEL), lambda i: (i // blocks_per_mod, 0, 0)),
                  const(wpa), const(wpb), const(wo), const(g), const(wr), const(br), const(ltri), const(cnt0)],
        out_specs=[row(D_MODEL), row(D_MODEL // 2), col8, row(LANES), col8,
                   pl.BlockSpec((1, LANES), lambda i: (0, 0))],
        out_shape=[jax.ShapeDtypeStruct((t, D_MODEL), F32), jax.ShapeDtypeStruct((t, D_MODEL // 2), jnp.uint32),
                   jax.ShapeDtypeStruct((8, t), I32), jax.ShapeDtypeStruct((t, LANES), F32),
                   jax.ShapeDtypeStruct((8, t), I32), jax.ShapeDtypeStruct((1, LANES), F32)],
        scratch_shapes=[pltpu.VMEM((1, LANES), F32)],
        compiler_params=_cparams("arbitrary"),
        name="post",
    )(a, r, sga, sgb, x, mod3, wpa, wpb, wo, g, wr, br, ltri, cnt0)


def _dispatch_kernel(dest_ref, h2w_ref, xs_in, xs_out, sem):
    del xs_in
    i = pl.program_id(0)
    tm = h2w_ref.shape[0]

    def row_copy(r, d):
        return pltpu.make_async_copy(h2w_ref.at[pl.ds(r, 1)], xs_out.at[pl.ds(d, 1)], sem)

    n_tok = dest_ref.shape[0] // TOP_K

    def start(r, carry):
        for j in range(TOP_K):
            row_copy(r, dest_ref[j * n_tok + i * tm + r]).start()
        return carry

    lax.fori_loop(0, tm, start, 0)
    for j in range(TOP_K):
        pltpu.make_async_copy(h2w_ref, xs_out.at[pl.ds(0, tm)], sem).wait()


def _dispatch(dest_flat, h2w, x_sorted):
    t = h2w.shape[0]
    tm = TOK_BLOCK
    grid_spec = pltpu.PrefetchScalarGridSpec(
        num_scalar_prefetch=1,
        grid=(t // tm,),
        in_specs=[pl.BlockSpec((tm, D_MODEL // 2), lambda i, d: (i, 0)), pl.BlockSpec(memory_space=pl.ANY)],
        out_specs=pl.BlockSpec(memory_space=pl.ANY),
        scratch_shapes=[pltpu.SemaphoreType.DMA(())])
    return pl.pallas_call(
        _dispatch_kernel,
        grid_spec=grid_spec,
        out_shape=jax.ShapeDtypeStruct(x_sorted.shape, x_sorted.dtype),
        input_output_aliases={2: 0},
        compiler_params=_cparams("arbitrary"),
        name="dispatch",
    )(dest_flat, h2w, x_sorted)


def _moe_kernel(be_ref, first_ref, nused_ref, x_ref, wgu_ref, bgu_ref, wd_ref, bd_ref, o_ref, wgu_b, wd_b):
    i = pl.program_id(0)

    @pl.when(first_ref[i] == 1)
    def _():
        wgu_b[...] = wgu_ref[...].astype(BF16)
        wd_b[...] = wd_ref[...].astype(BF16)

    @pl.when(i < nused_ref[0])
    def _():
        x = jnp.concatenate(_unpack_bf16_pairs(x_ref[...]), axis=1)
        gu = jnp.dot(x, wgu_b[...], preferred_element_type=F32) + bgu_ref[...]
        g = jnp.minimum(gu[:, :D_FF], SWIGLU_LIMIT)
        u = jnp.clip(gu[:, D_FF:], -SWIGLU_LIMIT, SWIGLU_LIMIT)
        act = (u + 1.0) * (g * jax.nn.sigmoid(SWIGLU_ALPHA * g))
        o_ref[...] = jnp.dot(act.astype(BF16), wd_b[...], preferred_element_type=F32) + bd_ref[...]

    @pl.when(i >= nused_ref[0])
    def _():
        o_ref[...] = jnp.zeros_like(o_ref)


def _moe(blk_expert, blk_first, n_used, x_sorted, w_gate_up, b_gate_up, w_down, b_down):
    n_rows = x_sorted.shape[0]
    grid_spec = pltpu.PrefetchScalarGridSpec(
        num_scalar_prefetch=3,
        grid=(n_rows // MOE_ROWS,),
        in_specs=[pl.BlockSpec((MOE_ROWS, D_MODEL // 2), lambda i, be, bf, nu: (i, 0)),
                  pl.BlockSpec((None, D_MODEL, 2 * D_FF), lambda i, be, bf, nu: (be[i], 0, 0)),
                  pl.BlockSpec((None, 1, 2 * D_FF), lambda i, be, bf, nu: (be[i], 0, 0)),
                  pl.BlockSpec((None, D_FF, D_MODEL), lambda i, be, bf, nu: (be[i], 0, 0)),
                  pl.BlockSpec((None, 1, D_MODEL), lambda i, be, bf, nu: (be[i], 0, 0))],
        out_specs=pl.BlockSpec((MOE_ROWS, D_MODEL), lambda i, be, bf, nu: (i, 0)),
        scratch_shapes=[pltpu.VMEM((D_MODEL, 2 * D_FF), BF16), pltpu.VMEM((D_FF, D_MODEL), BF16)])
    return pl.pallas_call(
        _moe_kernel,
        grid_spec=grid_spec,
        out_shape=jax.ShapeDtypeStruct((n_rows, D_MODEL), F32),
        compiler_params=_cparams("arbitrary"),
        name="moe",
    )(blk_expert, blk_first, n_used, x_sorted, w_gate_up, b_gate_up.reshape(N_EXPERTS, 1, -1),
      w_down, b_down.reshape(N_EXPERTS, 1, -1))


def _final_kernel(dest_ref, x1_ref, mod_ref, gate_ref, g_ref, rows_hbm, y_ref, buf, sems):
    i = pl.program_id(0)
    n = pl.num_programs(0)
    tm = x1_ref.shape[0]

    n_tok = dest_ref.shape[0] // TOP_K

    def issue(blk, slot):
        def body(r, carry):
            for j in range(TOP_K):
                d = dest_ref[j * n_tok + blk * tm + r]
                pltpu.make_async_copy(rows_hbm.at[pl.ds(d, 1)], buf.at[slot, j, pl.ds(r, 1)], sems.at[slot]).start()
            return carry
        lax.fori_loop(0, tm, body, 0)

    @pl.when(i == 0)
    def _():
        issue(0, 0)

    @pl.when(i + 1 < n)
    def _():
        issue(i + 1, (i + 1) % 2)

    slot = i % 2
    for j in range(TOP_K):
        pltpu.make_async_copy(rows_hbm.at[pl.ds(0, tm)], buf.at[slot, j], sems.at[slot]).wait()
    gates = gate_ref[...]
    moe = buf[slot, 0] * gates[:, 0:1]
    for j in range(1, TOP_K):
        moe = moe + buf[slot, j] * gates[:, j:j + 1]
    x2 = x1_ref[...] + mod_ref[:, 5 * D_MODEL:6 * D_MODEL] * moe
    ms = jnp.mean(x2 * x2, axis=-1, keepdims=True)
    y_ref[...] = x2 * lax.rsqrt(ms + NORM_EPS) * g_ref[...]


def _final(dest_flat, x1, mod3, gates, rows_out, g, blocks_per_mod):
    t = x1.shape[0]
    tm = TOK_BLOCK
    mod_rows = mod3.shape[1]
    row = lambda n: pl.BlockSpec((tm, n), lambda i, d: (i, 0))
    grid_spec = pltpu.PrefetchScalarGridSpec(
        num_scalar_prefetch=1,
        grid=(t // tm,),
        in_specs=[row(D_MODEL),
                  pl.BlockSpec((None, mod_rows, 6 * D_MODEL), lambda i, d: (i // blocks_per_mod, 0, 0)),
                  row(LANES),
                  pl.BlockSpec((1, D_MODEL), lambda i, d: (0, 0)),
                  pl.BlockSpec(memory_space=pl.ANY)],
        out_specs=row(D_MODEL),
        scratch_shapes=[pltpu.VMEM((2, TOP_K, tm, D_MODEL), F32), pltpu.SemaphoreType.DMA((2,))])
    return pl.pallas_call(
        _final_kernel,
        grid_spec=grid_spec,
        out_shape=jax.ShapeDtypeStruct((t, D_MODEL), F32),
        compiler_params=_cparams("arbitrary"),
        name="final",
    )(dest_flat, x1, mod3, gates, g, rows_out)


def _rope_tables(pos, rot_dim, theta, head_dim):
    half = rot_dim // 2
    inv = theta ** (-jnp.arange(half, dtype=F32) * (2.0 / rot_dim))
    ang = pos.astype(F32)[:, None] * inv[None, :]
    cos, sin = jnp.cos(ang), jnp.sin(ang)
    n = pos.shape[0]
    rest = head_dim - rot_dim
    zh = jnp.zeros((n, half), F32)
    c = jnp.concatenate([cos, cos, jnp.ones((n, rest), F32)], axis=1)
    sa = jnp.concatenate([-sin, zh, jnp.zeros((n, rest), F32)], axis=1)
    sb = jnp.concatenate([zh, sin, jnp.zeros((n, rest), F32)], axis=1)
    rep = LANES // head_dim
    return tuple(jnp.tile(a, (1, rep)) for a in (c, sa, sb))


def _pack_w_in(w_in):
    offs = np.cumsum((0,) + IN_SPLITS)
    part = lambda j: w_in[:, offs[j]:offs[j + 1]]
    zero = lambda n: jnp.zeros((D_MODEL, n), w_in.dtype)
    group = ATT_HEADS // ATT_KV_HEADS
    cols = []
    wq = part(0)
    for h in range(ATT_HEADS):
        wh = wq[:, h * HEAD_DIM:(h + 1) * HEAD_DIM]
        cols += [wh, zero(HEAD_DIM)] if h // group == 0 else [zero(HEAD_DIM), wh]
    wiq = part(3)
    for h in range(IDX_HEADS):
        cols += [wiq[:, h * IDX_DIM:(h + 1) * IDX_DIM], zero(LANES - IDX_DIM)]
    cols += [part(1), part(2), part(4), part(5), zero(LANES - IDX_DIM - IDX_HEADS)]
    cols += [part(j) for j in range(6, 12)]
    return jnp.concatenate(cols, axis=1).astype(BF16)


def _heads_major(a, db, t, width):
    heads = a.shape[1] // width
    return a.reshape(db, t, heads, width).transpose(0, 2, 1, 3).reshape(db, heads * t, width)


def _pad_rows(a, db, t, rows):
    return jnp.pad(a.reshape(db, t, -1), ((0, 0), (0, rows - t), (0, 0)))


def kernel(x_prompt, x_sample, cache_k, cache_v, cache_ik, state_ret, page_table, c_prompt, c_sample, norm_mix_g, norm_ffn_g, norm_final_g, w_ada, b_ada, w_in, w_branch_a, w_branch_b, w_out, w_router, b_router, w_gate_up, b_gate_up, w_down, b_down):
    batch, seq, _ = x_prompt.shape
    db, dt, _ = x_sample.shape
    assert w_in.shape[0] == 1, "one layer"
    tp, ts = batch * seq, db * dt
    xp = x_prompt.reshape(tp, D_MODEL)
    xs = x_sample.reshape(ts, D_MODEL)

    mod = _adaln(jnp.concatenate([c_prompt, c_sample], axis=0), w_ada[0], b_ada[0])
    mod_p = mod[:batch].reshape(batch, 1, 6 * D_MODEL)
    mod_s = jnp.repeat(mod[batch:], dt, axis=0).reshape(ts // TOK_BLOCK, TOK_BLOCK, 6 * D_MODEL)
    bpm_p = seq // TOK_BLOCK
    tmp = PROMPT_TOK_BLOCK
    bpm_big = seq // tmp

    w_packed = _pack_w_in(w_in[0])
    pos_p = jnp.arange(seq)
    pos_s = PAST_LEN + (jnp.arange(TOK_BLOCK) % dt)
    g_mix = norm_mix_g[0].reshape(1, D_MODEL)
    outs_p = _inproj(xp, mod_p, g_mix, w_packed, _rope_tables(pos_p, ROPE_DIM, ROPE_THETA, HEAD_DIM),
                     _rope_tables(pos_p, RET_DK, RET_THETA, RET_DK), tmp, bpm_big, bpm_big)
    outs_s = _inproj(xs, mod_s, g_mix, w_packed, _rope_tables(pos_s, ROPE_DIM, ROPE_THETA, HEAD_DIM),
                     _rope_tables(pos_s, RET_DK, RET_THETA, RET_DK), TOK_BLOCK, 1, 1)
    (q_p, k_p, v_p, kb_p, vb_p, iq_p, ikw_p, ikb_p, rq_p, rk_p, rv_p, sg_p, sga_p, sgb_p) = outs_p
    (q_s, k_s, v_s, kb_s, vb_s, iq_s, ikw_s, ikb_s, rq_s, rk_s, rv_s, sg_s, sga_s, sgb_s) = outs_s

    tri = (jnp.arange(LANES)[:, None] < jnp.arange(LANES)[None, :]).astype(BF16)

    a_p = _dsa_prompt(q_p, iq_p, ikw_p, kb_p, vb_p, ikb_p, tri, batch, seq)
    group = ATT_HEADS // ATT_KV_HEADS
    q4 = q_s.reshape(db, dt, ATT_HEADS, LANES)
    qs = jnp.stack([q4[:, :, h, (h // group) * HEAD_DIM:(h // group + 1) * HEAD_DIM] for h in range(ATT_HEADS)],
                   axis=1).reshape(db, ATT_HEADS * dt, HEAD_DIM)
    iqs = _heads_major(iq_s, db, dt, LANES)[:, :, :IDX_DIM]
    ws = _heads_major(ikw_s[:, IDX_DIM:IDX_DIM + IDX_HEADS] * (IDX_HEADS ** -0.5), db, dt, 1)
    new_t = lambda a: jnp.pad(a.reshape(db, dt, ATT_KV_HEADS, HEAD_DIM).transpose(0, 2, 3, 1),
                              ((0, 0), (0, 0), (0, 0), (0, LANES - dt)))
    iknew_t = jnp.pad(ikb_s[:, :IDX_DIM].reshape(db, dt, IDX_DIM).transpose(0, 2, 1),
                      ((0, 0), (0, 0), (0, LANES - dt)))
    o_s = _dsa_sample(page_table, qs, iqs, ws, new_t(kb_s), new_t(vb_s), iknew_t,
                      cache_ik[0].transpose(0, 2, 1), cache_k[0].transpose(0, 2, 3, 1),
                      cache_v[0].transpose(0, 2, 3, 1), tri, dt)
    a_s = o_s.reshape(db, ATT_HEADS, dt, HEAD_DIM).transpose(0, 2, 1, 3).reshape(ts, ATT_OUT).astype(BF16)

    r_p, st_p = _retention(rq_p, rk_p, rv_p, sg_p, _retention_tables(float(RET_CHUNK)), None, batch,
                           seq // RET_CHUNK)
    pad = lambda a: _pad_rows(a, db, dt, RET_CHUNK).reshape(db * RET_CHUNK, -1)
    r_s, st_s = _retention(pad(rq_s), pad(rk_s), pad(rv_s), pad(sg_s), _retention_tables(float(dt)),
                           state_ret[0].reshape(db, RET_HEADS // 2, LANES, LANES), db, 1)
    r_s = r_s.reshape(db, RET_CHUNK, RET_OUT)[:, :dt].reshape(ts, RET_OUT)

    wr = jnp.pad(w_router[0], ((0, 0), (0, LANES - N_EXPERTS)))
    br = jnp.concatenate([b_router[0], jnp.full((LANES - N_EXPERTS,), -jnp.inf, F32)]).reshape(1, LANES)
    post_w = (w_branch_a[0].astype(BF16), w_branch_b[0].astype(BF16), w_out[0].astype(BF16),
              norm_ffn_g[0].reshape(1, D_MODEL), wr, br)
    x1_p, h2w_p, idx_p, gate_p, rank_p, cnt_p = _post(a_p, r_p, sga_p, sgb_p, xp, mod_p, *post_w,
                                                      jnp.zeros((1, LANES), F32), tmp, bpm_big)
    x1_s, h2w_s, idx_s, gate_s, rank_s, cnt_all = _post(a_s, r_s, sga_s, sgb_s, xs, mod_s, *post_w, cnt_p,
                                                        TOK_BLOCK, 1)

    n_asg = (tp + ts) * TOP_K
    counts = cnt_all[0, :N_EXPERTS].astype(I32)
    padded = (counts + MOE_ROWS - 1) // MOE_ROWS * MOE_ROWS
    pend = jnp.cumsum(padded)
    pstart = pend - padded
    dest_p = (pstart[idx_p] + rank_p)[:TOP_K].reshape(-1)
    dest_s = (pstart[idx_s] + rank_s)[:TOP_K].reshape(-1)
    n_blocks = -(-n_asg // MOE_ROWS) + N_EXPERTS
    blk_start = jnp.arange(n_blocks, dtype=I32) * MOE_ROWS
    blk_expert = jnp.minimum(jnp.sum((blk_start[:, None] >= pend[None, :]).astype(I32), axis=1), N_EXPERTS - 1)
    blk_first = jnp.concatenate([jnp.ones((1,), I32), (blk_expert[1:] != blk_expert[:-1]).astype(I32)])
    n_used = (pend[-1] // MOE_ROWS).astype(I32).reshape(1)
    x_sorted = jnp.zeros((n_blocks * MOE_ROWS, D_MODEL // 2), jnp.uint32)
    x_sorted = _dispatch(dest_p, h2w_p, x_sorted)
    x_sorted = _dispatch(dest_s, h2w_s, x_sorted)
    rows_out = _moe(blk_expert, blk_first, n_used, x_sorted, w_gate_up[0], b_gate_up[0], w_down[0], b_down[0])

    g_final = norm_final_g.reshape(1, D_MODEL)
    y_p = _final(dest_p, x1_p, mod_p, gate_p, rows_out, g_final, bpm_p)
    y_s = _final(dest_s, x1_s, mod_s, gate_s, rows_out, g_final, 1)

    kv_shape = lambda b, s: (1, b, s, ATT_KV_HEADS, HEAD_DIM)
    st_shape = lambda b: (1, b, RET_HEADS, RET_DK, RET_DV)
    return (y_p.reshape(batch, seq, D_MODEL), y_s.reshape(db, dt, D_MODEL),
            k_p.reshape(kv_shape(batch, seq)), v_p.reshape(kv_shape(batch, seq)),
            ikw_p[:, :IDX_DIM].reshape(1, batch, seq, IDX_DIM), st_p.reshape(st_shape(batch)),
            k_s.reshape(kv_shape(db, dt)), v_s.reshape(kv_shape(db, dt)),
            ikw_s[:, :IDX_DIM].reshape(1, db, dt, IDX_DIM), st_s.reshape(st_shape(db)))
```

```python
import functools

import jax
import jax.numpy as jnp
import numpy as np
from jax import lax
from jax.experimental import pallas as pl
from jax.experimental.pallas import tpu as pltpu

F32 = jnp.float32
BF16 = jnp.bfloat16
I32 = jnp.int32

D_MODEL = 1024
PAST_LEN = 8192
PAGE_SIZE = 128
ATT_HEADS = 8
ATT_KV_HEADS = 2
HEAD_DIM = 64
ROPE_DIM = HEAD_DIM // 4
ROPE_THETA = 500000.0
IDX_HEADS = 8
IDX_DIM = 64
IDX_ROPE_DIM = IDX_DIM // 4
TOPK_MAX = 256
RET_HEADS = 8
RET_DK = 64
RET_DV = 128
RET_THETA = 10000.0
RET_CHUNK = 128
N_EXPERTS = 32
TOP_K = 4
D_FF = D_MODEL
SWIGLU_LIMIT = 7.0
SWIGLU_ALPHA = 1.702
NORM_EPS = 1e-6
GN_EPS = 1e-5
ATT_OUT = ATT_HEADS * HEAD_DIM
RET_OUT = RET_HEADS * RET_DV
IN_SPLITS = (ATT_HEADS * HEAD_DIM, ATT_KV_HEADS * HEAD_DIM, ATT_KV_HEADS * HEAD_DIM,
             IDX_HEADS * IDX_DIM, IDX_DIM, IDX_HEADS,
             RET_HEADS * RET_DK, RET_HEADS * RET_DK, RET_OUT, RET_OUT, D_MODEL, D_MODEL)

LANES = 128
MASK_NEG = -1e30
FLT_MAX = 3.4028234663852886e38
SELECT_UNROLL = 4
CAUSAL_VARIANTS = 8
IDX_KEY_CHUNK = 256
VMEM_LIMIT = 56 * 1024 * 1024

TOK_BLOCK = 256
PROMPT_TOK_BLOCK = 512
Q_BLOCK = 256
IDX_Q_ROWS = 128
MOE_ROWS = 512

_W_GROUPS = (("q", 1024), ("iq", 1024), ("kvi", 384), ("rq", 512), ("rk", 512),
             ("rv", 1024), ("rg", 1024), ("ga", 1024), ("gb", 1024))
PROJ_COLS = 512
_W_OFF = {}
_off = 0
for _n, _w in _W_GROUPS:
    _W_OFF[_n] = (_off, _w)
    _off += _w
W_COLS = _off


def _cparams(*sem):
    return pltpu.CompilerParams(dimension_semantics=sem, vmem_limit_bytes=VMEM_LIMIT)


def _adaln_kernel(c_ref, w_ref, b_ref, o_ref):
    c = c_ref[...]
    s = c * jax.nn.sigmoid(c)
    o_ref[...] = jnp.dot(s, w_ref[...], preferred_element_type=F32, precision=lax.Precision.HIGHEST) + b_ref[...]


def _adaln(c_all, w_ada, b_ada):
    n = c_all.shape[0]
    nb = 1536
    return pl.pallas_call(
        _adaln_kernel,
        grid=(6 * D_MODEL // nb,),
        in_specs=[pl.BlockSpec((n, D_MODEL), lambda j: (0, 0)),
                  pl.BlockSpec((D_MODEL, nb), lambda j: (0, j)),
                  pl.BlockSpec((1, nb), lambda j: (0, j))],
        out_specs=pl.BlockSpec((n, nb), lambda j: (0, j)),
        out_shape=jax.ShapeDtypeStruct((n, 6 * D_MODEL), F32),
        compiler_params=_cparams("arbitrary"),
        name="adaln",
    )(c_all, w_ada, b_ada.reshape(1, -1))


def _rope_slab(z, c, sa, sb, half):
    return z * c + pltpu.roll(z, LANES - half, 1) * sa + pltpu.roll(z, half, 1) * sb


def _inproj_kernel(x_ref, mod_ref, g_ref, w_ref, ca_ref, saa_ref, sba_ref, cr_ref, sar_ref, sbr_ref,
                   q_ref, k_ref, v_ref, kb_ref, vb_ref, iq_ref, ikw_ref, ikb_ref,
                   rq_ref, rk_ref, rv_ref, sg_ref, sga_ref, sgb_ref):
    x = x_ref[...]
    ms = jnp.mean(x * x, axis=-1, keepdims=True)
    y = x * lax.rsqrt(ms + NORM_EPS) * g_ref[...]
    h = (y * (1.0 + mod_ref[:, D_MODEL:2 * D_MODEL]) + mod_ref[:, 0:D_MODEL]).astype(BF16)

    def slabs(name):
        c0, width = _W_OFF[name]
        step = min(width, PROJ_COLS)
        for j in range(width // step):
            z = jnp.dot(h, w_ref[:, c0 + j * step:c0 + (j + 1) * step], preferred_element_type=F32)
            for s in range(step // LANES):
                yield j * (step // LANES) + s, z[:, s * LANES:(s + 1) * LANES]

    ca, saa, sba = ca_ref[...], saa_ref[...], sba_ref[...]
    cr, sar, sbr = cr_ref[...], sar_ref[...], sbr_ref[...]
    att_half, ret_half = ROPE_DIM // 2, RET_DK // 2
    lane = lax.broadcasted_iota(I32, (x.shape[0], LANES), 1)
    sl = lambda s: slice(s * LANES, (s + 1) * LANES)

    for s, z in slabs("q"):
        q_ref[:, sl(s)] = (_rope_slab(z, ca, saa, sba, att_half) * 0.125).astype(BF16)
    for s, z in slabs("iq"):
        iq_ref[:, sl(s)] = (_rope_slab(z, ca, saa, sba, att_half) * 0.125).astype(BF16)
    (_, zk), (_, zv), (_, zi) = slabs("kvi")
    kk = _rope_slab(zk, ca, saa, sba, att_half)
    k_ref[...] = kk
    kb_ref[...] = kk.astype(BF16)
    v_ref[...] = zv
    vb_ref[...] = zv.astype(BF16)
    zr = _rope_slab(zi, ca, saa, sba, att_half)
    ikw_ref[...] = jnp.where(lane < IDX_DIM, zr, zi)
    ikb_ref[...] = jnp.where(lane < IDX_DIM, zr, 0.0).astype(BF16)
    for s, z in slabs("rq"):
        rq_ref[:, sl(s)] = _rope_slab(z, cr, sar, sbr, ret_half).astype(BF16)
    for s, z in slabs("rk"):
        rk_ref[:, sl(s)] = (_rope_slab(z, cr, sar, sbr, ret_half) * 0.125).astype(BF16)
    for s, z in slabs("rv"):
        rv_ref[:, sl(s)] = z.astype(BF16)
    for s, z in slabs("rg"):
        sg_ref[:, sl(s)] = (z * jax.nn.sigmoid(z)).astype(BF16)
    for s, z in slabs("ga"):
        sga_ref[:, sl(s)] = jax.nn.sigmoid(z).astype(BF16)
    for s, z in slabs("gb"):
        sgb_ref[:, sl(s)] = jax.nn.sigmoid(z).astype(BF16)


def _inproj(x, mod3, g, w_packed, tabs_att, tabs_ret, tm, blocks_per_mod, tab_blocks):
    t = x.shape[0]
    nblk = t // tm
    mod_rows = mod3.shape[1]
    tab_spec = pl.BlockSpec((tm, LANES), lambda i: (i % tab_blocks, 0))
    row = lambda n: pl.BlockSpec((tm, n), lambda i: (i, 0))
    out_defs = [(1024, BF16), (128, F32), (128, F32), (128, BF16), (128, BF16), (1024, BF16), (128, F32),
                (128, BF16), (512, BF16), (512, BF16), (1024, BF16), (1024, BF16), (1024, BF16), (1024, BF16)]
    return pl.pallas_call(
        _inproj_kernel,
        grid=(nblk,),
        in_specs=[row(D_MODEL),
                  pl.BlockSpec((None, mod_rows, 6 * D_MODEL), lambda i: (i // blocks_per_mod, 0, 0)),
                  pl.BlockSpec((1, D_MODEL), lambda i: (0, 0)),
                  pl.BlockSpec((D_MODEL, W_COLS), lambda i: (0, 0), pipeline_mode=pl.Buffered(1))]
                 + [tab_spec] * 6,
        out_specs=[row(n) for n, _ in out_defs],
        out_shape=[jax.ShapeDtypeStruct((t, n), d) for n, d in out_defs],
        compiler_params=_cparams("parallel"),
        name="inproj",
    )(x, mod3, g, w_packed, *tabs_att, *tabs_ret)


def _count(score_ref, n, pred):
    acc = jnp.zeros((score_ref.shape[0], LANES), F32)
    for c in range(n // LANES):
        acc = acc + jnp.where(pred(score_ref[:, c * LANES:(c + 1) * LANES]), 1.0, 0.0)
    return jnp.sum(acc, axis=1, keepdims=True)


def _kth_largest(score_ref, n, k):
    sc = score_ref[:, :n]
    finite = sc > -jnp.inf
    n_fin = jnp.sum(jnp.where(finite, 1.0, 0.0), axis=1, keepdims=True)
    n_pos = jnp.sum(jnp.where(sc > 0.0, 1.0, 0.0), axis=1, keepdims=True)
    n_nonneg = jnp.sum(jnp.where(sc >= 0.0, 1.0, 0.0), axis=1, keepdims=True)
    mx = jnp.max(sc, axis=1, keepdims=True)
    mn = jnp.min(jnp.where(finite, sc, jnp.inf), axis=1, keepdims=True)
    small = n_fin <= k
    positive = n_pos >= k
    at_zero = jnp.logical_and(jnp.logical_not(positive), n_nonneg >= k)
    lo = jnp.where(positive, 0.0, mn)
    hi = jnp.where(positive, mx + (jnp.abs(mx) * 2.0 ** -20 + 2.0 ** -100), 0.0)
    lo = jnp.where(at_zero, 0.0, lo)
    done = jnp.where(jnp.logical_or(small, at_zero), 1.0, 0.0)

    def cond(state):
        return jnp.min(state[2]) < 0.5

    def body(state):
        lo, hi, done = state
        for _ in range(SELECT_UNROLL):
            mid = 0.5 * lo + 0.5 * hi
            cnt = _count(score_ref, n, lambda s: s >= mid)
            stuck = jnp.logical_or(mid <= lo, mid >= hi)
            live = jnp.logical_and(done < 0.5, jnp.logical_not(stuck))
            ge = cnt >= k
            lo = jnp.where(jnp.logical_and(live, ge), mid, lo)
            hi = jnp.where(jnp.logical_and(live, jnp.logical_not(ge)), mid, hi)
            done = jnp.where(jnp.logical_or(stuck, cnt == k), 1.0, done)
        return lo, hi, done

    lo, _, _ = lax.while_loop(cond, body, (lo, hi, done))
    return jnp.where(small, -FLT_MAX, lo)


def _topk_bias(score_ref, bias_ref, tri_ref, n, k):
    rows = score_ref.shape[0]
    thr = _kth_largest(score_ref, n, k)
    need = k - _count(score_ref, n, lambda s: s > thr)
    n_eq = _count(score_ref, n, lambda s: s == thr)
    has_ties = jnp.max(jnp.where(need < n_eq, 1.0, 0.0)) > 0.5

    @pl.when(jnp.logical_not(has_ties))
    def _():
        for c in range(n // LANES):
            sl = slice(c * LANES, (c + 1) * LANES)
            bias_ref[:, sl] = jnp.where(score_ref[:, sl] >= thr, 0.0, MASK_NEG)

    @pl.when(has_ties)
    def _():
        run = jnp.zeros((rows, 1), F32)
        for c in range(n // LANES):
            sl = slice(c * LANES, (c + 1) * LANES)
            sc = score_ref[:, sl]
            eq = sc == thr
            eqf = jnp.where(eq, 1.0, 0.0)
            before = jnp.dot(eqf.astype(BF16), tri_ref[...], preferred_element_type=F32) + run
            take = jnp.logical_or(sc > thr, jnp.logical_and(eq, before < need))
            bias_ref[:, sl] = jnp.where(take, 0.0, MASK_NEG)
            run = run + jnp.sum(eqf, axis=1, keepdims=True)


def _dsa_prompt_kernel(q_ref, iq_ref, ikw_ref, kb_ref, vb_ref, ikb_ref, tri_ref, o_ref, score_ref, bias_ref):
    qb, s_len = score_ref.shape
    i = pl.program_id(1)
    nqb = s_len // qb
    per_variant = nqb // CAUSAL_VARIANTS
    for v in range(CAUSAL_VARIANTS):
        pl.when(i // per_variant == v)(
            functools.partial(_dsa_prompt_body, q_ref, iq_ref, ikw_ref, kb_ref, vb_ref, ikb_ref, tri_ref, o_ref,
                              score_ref, bias_ref, (v + 1) * per_variant * qb, min(TOPK_MAX, s_len // 4)))


def _dsa_prompt_body(q_ref, iq_ref, ikw_ref, kb_ref, vb_ref, ikb_ref, tri_ref, o_ref, score_ref, bias_ref,
                     n_keys, topk):
    qb = score_ref.shape[0]
    i = pl.program_id(1)
    w = ikw_ref[:, IDX_DIM:IDX_DIM + IDX_HEADS] * (IDX_HEADS ** -0.5)
    nt = (((1,), (1,)), ((), ()))
    kc = IDX_KEY_CHUNK
    qr = IDX_Q_ROWS
    for c in range(n_keys // kc):
        ikc = ikb_ref[c * kc:(c + 1) * kc, :]
        kpos = c * kc + lax.broadcasted_iota(I32, (qr, kc), 1)
        for r0 in range(0, qb, qr):
            acc = jnp.zeros((qr, kc), F32)
            for h in range(IDX_HEADS):
                d = lax.dot_general(iq_ref[r0:r0 + qr, h * LANES:(h + 1) * LANES], ikc, nt,
                                    preferred_element_type=F32)
                acc = acc + jnp.maximum(d, 0.0) * w[r0:r0 + qr, h:h + 1]
            qpos = i * qb + r0 + lax.broadcasted_iota(I32, (qr, kc), 0)
            score_ref[r0:r0 + qr, c * kc:(c + 1) * kc] = jnp.where(kpos <= qpos, acc, -jnp.inf)

    _topk_bias(score_ref, bias_ref, tri_ref, n_keys, topk)

    kb = kb_ref[0:n_keys, :]
    vb = vb_ref[0:n_keys, :]
    bias = bias_ref[:, 0:n_keys]
    lane = lax.broadcasted_iota(I32, (qb, LANES), 1)
    heads = []
    for h in range(ATT_HEADS):
        s = lax.dot_general(q_ref[:, h * LANES:(h + 1) * LANES], kb, nt, preferred_element_type=F32) + bias
        m = jnp.max(s, axis=1, keepdims=True)
        p = jnp.exp(s - m)
        l = jnp.sum(p, axis=1, keepdims=True)
        heads.append(jnp.dot(p.astype(BF16), vb, preferred_element_type=F32) / l)
    group = ATT_HEADS // ATT_KV_HEADS
    for pp in range(ATT_HEADS // 2):
        a, b = heads[2 * pp], heads[2 * pp + 1]
        if (2 * pp) // group == 0:
            slab = jnp.where(lane < HEAD_DIM, a, pltpu.roll(b, HEAD_DIM, 1))
        else:
            slab = jnp.where(lane < HEAD_DIM, pltpu.roll(a, HEAD_DIM, 1), b)
        o_ref[:, pp * LANES:(pp + 1) * LANES] = slab.astype(BF16)


def _dsa_prompt(q, iq, ikw, kb, vb, ikb, tri, batch, seq):
    nqb = seq // Q_BLOCK
    qrow = lambda n: pl.BlockSpec((Q_BLOCK, n), lambda b, i: (b * nqb + i, 0))
    keys = pl.BlockSpec((seq, LANES), lambda b, i: (b, 0))
    return pl.pallas_call(
        _dsa_prompt_kernel,
        grid=(batch, nqb),
        in_specs=[qrow(1024), qrow(1024), qrow(LANES), keys, keys, keys,
                  pl.BlockSpec((LANES, LANES), lambda b, i: (0, 0))],
        out_specs=qrow(ATT_OUT),
        out_shape=jax.ShapeDtypeStruct((batch * seq, ATT_OUT), BF16),
        scratch_shapes=[pltpu.VMEM((Q_BLOCK, seq), F32), pltpu.VMEM((Q_BLOCK, seq), F32)],
        compiler_params=_cparams("parallel", "arbitrary"),
        name="dsa_prompt",
    )(q, iq, ikw, kb, vb, ikb, tri)


def _dsa_sample_kernel(pt_ref, qs_ref, iqs_ref, ws_ref, knew_ref, vnew_ref, iknew_ref, cik_hbm, ck_hbm, cv_hbm,
                       tri_ref, o_ref, ikbuf, kbuf, vbuf, ikt, kt, vt, sems, key_ref, bias_ref):
    db = pl.program_id(0)
    n_pages = ikbuf.shape[1]
    t = key_ref.shape[0]
    n_past = n_pages * PAGE_SIZE
    last = pl.num_programs(0) - 1

    def fetch(src, dst_of_page, sem, req):
        def body(p, carry):
            pltpu.make_async_copy(src.at[pt_ref[req, p]], dst_of_page(p), sem).start()
            return carry
        lax.fori_loop(0, n_pages, body, 0)

    def wait_all(src, dst, sem):
        pltpu.make_async_copy(src.at[pl.ds(0, n_pages)], dst, sem).wait()

    fetch_ik = lambda req, slot: fetch(cik_hbm, lambda p: ikbuf.at[slot, p], sems.at[slot], req)
    fetch_k = lambda req: fetch(ck_hbm, lambda p: kbuf.at[p], sems.at[2], req)
    fetch_v = lambda req: fetch(cv_hbm, lambda p: vbuf.at[p], sems.at[3], req)

    @pl.when(db == 0)
    def _():
        fetch_ik(0, 0)
        fetch_k(0)
        fetch_v(0)

    @pl.when(db < last)
    def _():
        fetch_ik(db + 1, (db + 1) % 2)

    slot = db % 2
    wait_all(cik_hbm, ikbuf.at[slot], sems.at[slot])

    nt = (((1,), (1,)), ((), ()))
    page = lambda p: slice(p * PAGE_SIZE, (p + 1) * PAGE_SIZE)
    for p in range(n_pages):
        ikt[:, page(p)] = ikbuf[slot, p].astype(BF16)
    iqs = iqs_ref[...]
    wcol = ws_ref[...]
    d_past = jnp.maximum(jnp.dot(iqs, ikt[...], preferred_element_type=F32), 0.0) * wcol
    d_new = jnp.maximum(jnp.dot(iqs, iknew_ref[...], preferred_element_type=F32), 0.0) * wcol
    s_past = d_past[0:t]
    s_new = d_new[0:t]
    for h in range(1, IDX_HEADS):
        s_past = s_past + d_past[h * t:(h + 1) * t]
        s_new = s_new + d_new[h * t:(h + 1) * t]
    row = lax.broadcasted_iota(I32, (t, LANES), 0)
    lane = lax.broadcasted_iota(I32, (t, LANES), 1)
    new_ok = lane <= row
    key_ref[:, 0:n_past] = s_past
    key_ref[:, n_past:n_past + LANES] = jnp.where(new_ok, s_new, -jnp.inf)
    _topk_bias(key_ref, bias_ref, tri_ref, n_past + LANES, min(TOPK_MAX, (n_past + t) // 4))

    def stage(buf, dst):
        for p in range(n_pages):
            for j in range(ATT_KV_HEADS):
                dst[j, :, page(p)] = buf[p, j].astype(BF16)

    wait_all(ck_hbm, kbuf, sems.at[2])
    stage(kbuf, kt)

    @pl.when(db < last)
    def _():
        fetch_k(db + 1)

    wait_all(cv_hbm, vbuf, sems.at[3])
    stage(vbuf, vt)

    @pl.when(db < last)
    def _():
        fetch_v(db + 1)

    rows_per_kv = qs_ref.shape[0] // ATT_KV_HEADS
    bias = jnp.concatenate([bias_ref[...]] * (rows_per_kv // t), axis=0)
    for j in range(ATT_KV_HEADS):
        qj = qs_ref[j * rows_per_kv:(j + 1) * rows_per_kv, :]
        sp = jnp.dot(qj, kt[j], preferred_element_type=F32) + bias[:, 0:n_past]
        sn = jnp.dot(qj, knew_ref[j], preferred_element_type=F32) + bias[:, n_past:n_past + LANES]
        m = jnp.maximum(jnp.max(sp, axis=1, keepdims=True), jnp.max(sn, axis=1, keepdims=True))
        pp = jnp.exp(sp - m)
        pn = jnp.exp(sn - m)
        l = jnp.sum(pp, axis=1, keepdims=True) + jnp.sum(pn, axis=1, keepdims=True)
        o = (lax.dot_general(pp.astype(BF16), vt[j], nt, preferred_element_type=F32)
             + lax.dot_general(pn.astype(BF16), vnew_ref[j], nt, preferred_element_type=F32))
        o_ref[j * rows_per_kv:(j + 1) * rows_per_kv, :] = o / l


def _dsa_sample(page_table, qs, iqs, ws, knew, vnew, iknew, cache_ik, cache_k, cache_v, tri, t):
    db, n_pages = page_table.shape
    rows = qs.shape[1]
    per_db = lambda r, n: pl.BlockSpec((None, r, n), lambda b, pt: (b, 0, 0))
    any_spec = pl.BlockSpec(memory_space=pl.ANY)
    n_keys = n_pages * PAGE_SIZE + LANES
    grid_spec = pltpu.PrefetchScalarGridSpec(
        num_scalar_prefetch=1,
        grid=(db,),
        in_specs=[per_db(rows, HEAD_DIM), per_db(rows, IDX_DIM), per_db(rows, 1),
                  pl.BlockSpec((None, ATT_KV_HEADS, HEAD_DIM, LANES), lambda b, pt: (b, 0, 0, 0)),
                  pl.BlockSpec((None, ATT_KV_HEADS, HEAD_DIM, LANES), lambda b, pt: (b, 0, 0, 0)),
                  per_db(IDX_DIM, LANES), any_spec, any_spec, any_spec,
                  pl.BlockSpec((LANES, LANES), lambda b, pt: (0, 0))],
        out_specs=per_db(rows, HEAD_DIM),
        scratch_shapes=[pltpu.VMEM((2, n_pages, IDX_DIM, PAGE_SIZE), F32),
                        pltpu.VMEM((n_pages, ATT_KV_HEADS, HEAD_DIM, PAGE_SIZE), F32),
                        pltpu.VMEM((n_pages, ATT_KV_HEADS, HEAD_DIM, PAGE_SIZE), F32),
                        pltpu.VMEM((IDX_DIM, n_pages * PAGE_SIZE), BF16),
                        pltpu.VMEM((ATT_KV_HEADS, HEAD_DIM, n_pages * PAGE_SIZE), BF16),
                        pltpu.VMEM((ATT_KV_HEADS, HEAD_DIM, n_pages * PAGE_SIZE), BF16),
                        pltpu.SemaphoreType.DMA((4,)),
                        pltpu.VMEM((t, n_keys), F32),
                        pltpu.VMEM((t, n_keys), F32)])
    return pl.pallas_call(
        _dsa_sample_kernel,
        grid_spec=grid_spec,
        out_shape=jax.ShapeDtypeStruct((db, rows, HEAD_DIM), F32),
        compiler_params=_cparams("arbitrary"),
        name="dsa_sample",
    )(page_table, qs, iqs, ws, knew, vnew, iknew, cache_ik, cache_k, cache_v, tri)


def _retention_kernel(has_init, rq_ref, rk_ref, rv_ref, sg_ref, decay_ref, qdec_ref, kdec_ref, gst_ref, *rest):
    if has_init:
        init_ref, o_ref, st_ref, state = rest
    else:
        o_ref, st_ref, state = rest
    c = pl.program_id(1)

    @pl.when(c == 0)
    def _():
        if has_init:
            state[...] = init_ref[...]
        else:
            state[...] = jnp.zeros_like(state)

    nt = (((1,), (1,)), ((), ()))
    tn = (((0,), (0,)), ((), ()))
    rows = rq_ref.shape[0]
    lane = lax.broadcasted_iota(I32, (rows, LANES), 1)
    for p in range(RET_HEADS // 2):
        sl = slice(p * LANES, (p + 1) * LANES)
        qp = rq_ref[:, sl]
        kp = rk_ref[:, sl]
        qd = (qp.astype(F32) * qdec_ref[:, sl]).astype(BF16)
        kd = (kp.astype(F32) * kdec_ref[:, sl]).astype(BF16)
        s_old = state[p]
        s_old_b = s_old.astype(BF16)
        s_new = s_old * gst_ref[p]
        for e in range(2):
            h = 2 * p + e
            hs = slice(h * LANES, (h + 1) * LANES)
            mine = jnp.where((lane >= e * RET_DK) & (lane < (e + 1) * RET_DK), 1.0, 0.0).astype(BF16)
            sc = lax.dot_general(qp * mine, kp, nt, preferred_element_type=F32) * decay_ref[h]
            vh = rv_ref[:, hs]
            o = (jnp.dot(sc.astype(BF16), vh, preferred_element_type=F32)
                 + jnp.dot(qd * mine, s_old_b, preferred_element_type=F32))
            s_new = s_new + lax.dot_general(kd * mine, vh, tn, preferred_element_type=F32)
            mu = jnp.mean(o, axis=-1, keepdims=True)
            var = jnp.mean(jnp.square(o - mu), axis=-1, keepdims=True)
            on = (o - mu) * lax.rsqrt(var + GN_EPS)
            o_ref[:, hs] = (on * sg_ref[:, hs].astype(F32)).astype(BF16)
        state[p] = s_new

    @pl.when(c == pl.num_programs(1) - 1)
    def _():
        st_ref[...] = state[...]


def _retention(rq, rk, rv, sg, tables, init, batch, n_chunks):
    decay, qdec, kdec, gst = tables
    cr = RET_CHUNK
    rowspec = lambda n: pl.BlockSpec((cr, n), lambda b, c: (b * n_chunks + c, 0))
    const = lambda shape: pl.BlockSpec(shape, lambda b, c: (0,) * len(shape))
    st_spec = pl.BlockSpec((None, RET_HEADS // 2, LANES, LANES), lambda b, c: (b, 0, 0, 0))
    in_specs = [rowspec(512), rowspec(512), rowspec(RET_OUT), rowspec(RET_OUT),
                const(decay.shape), const(qdec.shape), const(kdec.shape), const(gst.shape)]
    args = [rq, rk, rv, sg, decay, qdec, kdec, gst]
    if init is not None:
        in_specs.append(st_spec)
        args.append(init)
    return pl.pallas_call(
        functools.partial(_retention_kernel, init is not None),
        grid=(batch, n_chunks),
        in_specs=in_specs,
        out_specs=[rowspec(RET_OUT), st_spec],
        out_shape=[jax.ShapeDtypeStruct((batch * n_chunks * cr, RET_OUT), BF16),
                   jax.ShapeDtypeStruct((batch, RET_HEADS // 2, LANES, LANES), F32)],
        scratch_shapes=[pltpu.VMEM((RET_HEADS // 2, LANES, LANES), F32)],
        compiler_params=_cparams("parallel", "arbitrary"),
        name="retention",
    )(*args)


def _retention_tables(c_eff):
    lg = jnp.log(1.0 - 2.0 ** (-5.0 - jnp.arange(RET_HEADS, dtype=F32)))
    i = jnp.arange(RET_CHUNK, dtype=F32)
    diff = i[:, None] - i[None, :]
    decay = jnp.where(diff >= 0, jnp.exp(jnp.maximum(diff, 0.0)[None] * lg[:, None, None]), 0.0)
    q_decay = jnp.exp((i + 1.0)[:, None] * lg[None, :])
    k_decay = jnp.exp((c_eff - 1.0 - i)[:, None] * lg[None, :])
    qdec = jnp.repeat(q_decay, RET_DK, axis=1)
    kdec = jnp.repeat(k_decay, RET_DK, axis=1)
    g_state = jnp.exp(c_eff * lg)
    gst = jnp.broadcast_to(jnp.repeat(g_state, RET_DK).reshape(RET_HEADS // 2, LANES, 1),
                           (RET_HEADS // 2, LANES, LANES))
    return decay, qdec, kdec, gst


def _pack_bf16_pairs(lo, hi):
    return pltpu.pack_elementwise([lo, hi], packed_dtype=BF16)


def _unpack_bf16_pairs(words):
    return tuple(pltpu.unpack_elementwise(words, index=j, packed_dtype=BF16, unpacked_dtype=F32).astype(BF16)
                 for j in range(2))


def _post_kernel(a_ref, r_ref, sga_ref, sgb_ref, x_ref, mod_ref, wpa_ref, wpb_ref, wo_ref, g_ref, wr_ref, br_ref,
                 ltri_ref, cnt0_ref, x1_ref, h2w_ref, idx_ref, gate_ref, rank_ref, cnt_ref, cnt):
    @pl.when(pl.program_id(0) == 0)
    def _():
        cnt[...] = cnt0_ref[...]

    pa = jnp.dot(a_ref[...], wpa_ref[...], preferred_element_type=F32)
    pb = jnp.dot(r_ref[...], wpb_ref[...], preferred_element_type=F32)
    merged = sga_ref[...].astype(F32) * pa + sgb_ref[...].astype(F32) * pb
    gt1 = mod_ref[:, 2 * D_MODEL:3 * D_MODEL]
    x1 = x_ref[...] + gt1 * jnp.dot(merged.astype(BF16), wo_ref[...], preferred_element_type=F32)
    x1_ref[...] = x1
    ms = jnp.mean(x1 * x1, axis=-1, keepdims=True)
    y = x1 * lax.rsqrt(ms + NORM_EPS) * g_ref[...]
    h2 = y * (1.0 + mod_ref[:, 4 * D_MODEL:5 * D_MODEL]) + mod_ref[:, 3 * D_MODEL:4 * D_MODEL]
    half = D_MODEL // 2
    h2w_ref[...] = _pack_bf16_pairs(h2[:, :half], h2[:, half:])
    logits = jnp.dot(h2, wr_ref[...], preferred_element_type=F32, precision=lax.Precision.HIGHEST) + br_ref[...]
    lane = lax.broadcasted_iota(I32, logits.shape, 1).astype(F32)
    idx_out = jnp.zeros(logits.shape, F32)
    val_out = jnp.zeros(logits.shape, F32)
    chosen = []
    top = None
    for j in range(TOP_K):
        m = jnp.max(logits, axis=1, keepdims=True)
        am = jnp.min(jnp.where(logits == m, lane, float(LANES)), axis=1, keepdims=True)
        if j == 0:
            top = m
        idx_out = jnp.where(lane == j, am, idx_out)
        val_out = jnp.where(lane == j, jnp.exp(m - top), val_out)
        chosen.append(lane == am)
        logits = jnp.where(chosen[-1], -jnp.inf, logits)
    idx_ref[...] = jnp.transpose(idx_out)[0:8, :].astype(I32)
    gate_ref[...] = val_out / jnp.sum(val_out, axis=1, keepdims=True)
    onehot = jnp.where(chosen[0] | chosen[1] | chosen[2] | chosen[3], 1.0, 0.0)
    before = jnp.dot(ltri_ref[...], onehot.astype(BF16), preferred_element_type=F32) + cnt[...]
    rank_out = jnp.zeros(logits.shape, F32)
    for j in range(TOP_K):
        rj = jnp.sum(jnp.where(chosen[j], before, 0.0), axis=1, keepdims=True)
        rank_out = jnp.where(lane == j, rj, rank_out)
    rank_ref[...] = jnp.transpose(rank_out)[0:8, :].astype(I32)
    cnt[...] = cnt[...] + jnp.sum(onehot, axis=0, keepdims=True)
    cnt_ref[...] = cnt[...]


def _post(a, r, sga, sgb, x, mod3, wpa, wpb, wo, g, wr, br, cnt0, tm, blocks_per_mod):
    t = x.shape[0]
    mod_rows = mod3.shape[1]
    ar = jnp.arange(tm)
    ltri = (ar[None, :] < ar[:, None]).astype(BF16)
    row = lambda n: pl.BlockSpec((tm, n), lambda i: (i, 0))
    col8 = pl.BlockSpec((8, tm), lambda i: (0, i))
    const = lambda a_: pl.BlockSpec(a_.shape, lambda i: (0,) * a_.ndim)
    return pl.pallas_call(
        _post_kernel,
        grid=(t // tm,),
        in_specs=[row(ATT_OUT), row(RET_OUT), row(D_MODEL), row(D_MODEL), row(D_MODEL),
                  pl.BlockSpec((None, mod_rows, 6 * D_MODEL), lambda i: (i // blocks_per_mod, 0, 0)),
                  const(wpa), const(wpb), const(wo), const(g), const(wr), const(br), const(ltri), const(cnt0)],
        out_specs=[row(D_MODEL), row(D_MODEL // 2), col8, row(LANES), col8,
                   pl.BlockSpec((1, LANES), lambda i: (0, 0))],
        out_shape=[jax.ShapeDtypeStruct((t, D_MODEL), F32), jax.ShapeDtypeStruct((t, D_MODEL // 2), jnp.uint32),
                   jax.ShapeDtypeStruct((8, t), I32), jax.ShapeDtypeStruct((t, LANES), F32),
                   jax.ShapeDtypeStruct((8, t), I32), jax.ShapeDtypeStruct((1, LANES), F32)],
        scratch_shapes=[pltpu.VMEM((1, LANES), F32)],
        compiler_params=_cparams("arbitrary"),
        name="post",
    )(a, r, sga, sgb, x, mod3, wpa, wpb, wo, g, wr, br, ltri, cnt0)


def _dispatch_kernel(dest_ref, h2w_ref, xs_in, xs_out, sem):
    del xs_in
    i = pl.program_id(0)
    tm = h2w_ref.shape[0]

    def row_copy(r, d):
        return pltpu.make_async_copy(h2w_ref.at[pl.ds(r, 1)], xs_out.at[pl.ds(d, 1)], sem)

    n_tok = dest_ref.shape[0] // TOP_K

    def start(r, carry):
        for j in range(TOP_K):
            row_copy(r, dest_ref[j * n_tok + i * tm + r]).start()
        return carry

    lax.fori_loop(0, tm, start, 0)
    for j in range(TOP_K):
        pltpu.make_async_copy(h2w_ref, xs_out.at[pl.ds(0, tm)], sem).wait()


def _dispatch(dest_flat, h2w, x_sorted):
    t = h2w.shape[0]
    tm = TOK_BLOCK
    grid_spec = pltpu.PrefetchScalarGridSpec(
        num_scalar_prefetch=1,
        grid=(t // tm,),
        in_specs=[pl.BlockSpec((tm, D_MODEL // 2), lambda i, d: (i, 0)), pl.BlockSpec(memory_space=pl.ANY)],
        out_specs=pl.BlockSpec(memory_space=pl.ANY),
        scratch_shapes=[pltpu.SemaphoreType.DMA(())])
    return pl.pallas_call(
        _dispatch_kernel,
        grid_spec=grid_spec,
        out_shape=jax.ShapeDtypeStruct(x_sorted.shape, x_sorted.dtype),
        input_output_aliases={2: 0},
        compiler_params=_cparams("arbitrary"),
        name="dispatch",
    )(dest_flat, h2w, x_sorted)


def _moe_kernel(be_ref, first_ref, nused_ref, x_ref, wgu_ref, bgu_ref, wd_ref, bd_ref, o_ref, wgu_b, wd_b):
    i = pl.program_id(0)

    @pl.when(first_ref[i] == 1)
    def _():
        wgu_b[...] = wgu_ref[...].astype(BF16)
        wd_b[...] = wd_ref[...].astype(BF16)

    @pl.when(i < nused_ref[0])
    def _():
        x = jnp.concatenate(_unpack_bf16_pairs(x_ref[...]), axis=1)
        gu = jnp.dot(x, wgu_b[...], preferred_element_type=F32) + bgu_ref[...]
        g = jnp.minimum(gu[:, :D_FF], SWIGLU_LIMIT)
        u = jnp.clip(gu[:, D_FF:], -SWIGLU_LIMIT, SWIGLU_LIMIT)
        act = (u + 1.0) * (g * jax.nn.sigmoid(SWIGLU_ALPHA * g))
        o_ref[...] = jnp.dot(act.astype(BF16), wd_b[...], preferred_element_type=F32) + bd_ref[...]

    @pl.when(i >= nused_ref[0])
    def _():
        o_ref[...] = jnp.zeros_like(o_ref)


def _moe(blk_expert, blk_first, n_used, x_sorted, w_gate_up, b_gate_up, w_down, b_down):
    n_rows = x_sorted.shape[0]
    grid_spec = pltpu.PrefetchScalarGridSpec(
        num_scalar_prefetch=3,
        grid=(n_rows // MOE_ROWS,),
        in_specs=[pl.BlockSpec((MOE_ROWS, D_MODEL // 2), lambda i, be, bf, nu: (i, 0)),
                  pl.BlockSpec((None, D_MODEL, 2 * D_FF), lambda i, be, bf, nu: (be[i], 0, 0)),
                  pl.BlockSpec((None, 1, 2 * D_FF), lambda i, be, bf, nu: (be[i], 0, 0)),
                  pl.BlockSpec((None, D_FF, D_MODEL), lambda i, be, bf, nu: (be[i], 0, 0)),
                  pl.BlockSpec((None, 1, D_MODEL), lambda i, be, bf, nu: (be[i], 0, 0))],
        out_specs=pl.BlockSpec((MOE_ROWS, D_MODEL), lambda i, be, bf, nu: (i, 0)),
        scratch_shapes=[pltpu.VMEM((D_MODEL, 2 * D_FF), BF16), pltpu.VMEM((D_FF, D_MODEL), BF16)])
    return pl.pallas_call(
        _moe_kernel,
        grid_spec=grid_spec,
        out_shape=jax.ShapeDtypeStruct((n_rows, D_MODEL), F32),
        compiler_params=_cparams("arbitrary"),
        name="moe",
    )(blk_expert, blk_first, n_used, x_sorted, w_gate_up, b_gate_up.reshape(N_EXPERTS, 1, -1),
      w_down, b_down.reshape(N_EXPERTS, 1, -1))


def _final_kernel(dest_ref, x1_ref, mod_ref, gate_ref, g_ref, rows_hbm, y_ref, buf, sems):
    i = pl.program_id(0)
    n = pl.num_programs(0)
    tm = x1_ref.shape[0]

    n_tok = dest_ref.shape[0] // TOP_K

    def issue(blk, slot):
        def body(r, carry):
            for j in range(TOP_K):
                d = dest_ref[j * n_tok + blk * tm + r]
                pltpu.make_async_copy(rows_hbm.at[pl.ds(d, 1)], buf.at[slot, j, pl.ds(r, 1)], sems.at[slot]).start()
            return carry
        lax.fori_loop(0, tm, body, 0)

    @pl.when(i == 0)
    def _():
        issue(0, 0)

    @pl.when(i + 1 < n)
    def _():
        issue(i + 1, (i + 1) % 2)

    slot = i % 2
    for j in range(TOP_K):
        pltpu.make_async_copy(rows_hbm.at[pl.ds(0, tm)], buf.at[slot, j], sems.at[slot]).wait()
    gates = gate_ref[...]
    moe = buf[slot, 0] * gates[:, 0:1]
    for j in range(1, TOP_K):
        moe = moe + buf[slot, j] * gates[:, j:j + 1]
    x2 = x1_ref[...] + mod_ref[:, 5 * D_MODEL:6 * D_MODEL] * moe
    ms = jnp.mean(x2 * x2, axis=-1, keepdims=True)
    y_ref[...] = x2 * lax.rsqrt(ms + NORM_EPS) * g_ref[...]


def _final(dest_flat, x1, mod3, gates, rows_out, g, blocks_per_mod):
    t = x1.shape[0]
    tm = TOK_BLOCK
    mod_rows = mod3.shape[1]
    row = lambda n: pl.BlockSpec((tm, n), lambda i, d: (i, 0))
    grid_spec = pltpu.PrefetchScalarGridSpec(
        num_scalar_prefetch=1,
        grid=(t // tm,),
        in_specs=[row(D_MODEL),
                  pl.BlockSpec((None, mod_rows, 6 * D_MODEL), lambda i, d: (i // blocks_per_mod, 0, 0)),
                  row(LANES),
                  pl.BlockSpec((1, D_MODEL), lambda i, d: (0, 0)),
                  pl.BlockSpec(memory_space=pl.ANY)],
        out_specs=row(D_MODEL),
        scratch_shapes=[pltpu.VMEM((2, TOP_K, tm, D_MODEL), F32), pltpu.SemaphoreType.DMA((2,))])
    return pl.pallas_call(
        _final_kernel,
        grid_spec=grid_spec,
        out_shape=jax.ShapeDtypeStruct((t, D_MODEL), F32),
        compiler_params=_cparams("arbitrary"),
        name="final",
    )(dest_flat, x1, mod3, gates, g, rows_out)


def _rope_tables(pos, rot_dim, theta, head_dim):
    half = rot_dim // 2
    inv = theta ** (-jnp.arange(half, dtype=F32) * (2.0 / rot_dim))
    ang = pos.astype(F32)[:, None] * inv[None, :]
    cos, sin = jnp.cos(ang), jnp.sin(ang)
    n = pos.shape[0]
    rest = head_dim - rot_dim
    zh = jnp.zeros((n, half), F32)
    c = jnp.concatenate([cos, cos, jnp.ones((n, rest), F32)], axis=1)
    sa = jnp.concatenate([-sin, zh, jnp.zeros((n, rest), F32)], axis=1)
    sb = jnp.concatenate([zh, sin, jnp.zeros((n, rest), F32)], axis=1)
    rep = LANES // head_dim
    return tuple(jnp.tile(a, (1, rep)) for a in (c, sa, sb))


def _pack_w_in(w_in):
    offs = np.cumsum((0,) + IN_SPLITS)
    part = lambda j: w_in[:, offs[j]:offs[j + 1]]
    zero = lambda n: jnp.zeros((D_MODEL, n), w_in.dtype)
    group = ATT_HEADS // ATT_KV_HEADS
    cols = []
    wq = part(0)
    for h in range(ATT_HEADS):
        wh = wq[:, h * HEAD_DIM:(h + 1) * HEAD_DIM]
        cols += [wh, zero(HEAD_DIM)] if h // group == 0 else [zero(HEAD_DIM), wh]
    wiq = part(3)
    for h in range(IDX_HEADS):
        cols += [wiq[:, h * IDX_DIM:(h + 1) * IDX_DIM], zero(LANES - IDX_DIM)]
    cols += [part(1), part(2), part(4), part(5), zero(LANES - IDX_DIM - IDX_HEADS)]
    cols += [part(j) for j in range(6, 12)]
    return jnp.concatenate(cols, axis=1).astype(BF16)


def _heads_major(a, db, t, width):
    heads = a.shape[1] // width
    return a.reshape(db, t, heads, width).transpose(0, 2, 1, 3).reshape(db, heads * t, width)


def _pad_rows(a, db, t, rows):
    return jnp.pad(a.reshape(db, t, -1), ((0, 0), (0, rows - t), (0, 0)))


def kernel(x_prompt, x_sample, cache_k, cache_v, cache_ik, state_ret, page_table, c_prompt, c_sample, norm_mix_g, norm_ffn_g, norm_final_g, w_ada, b_ada, w_in, w_branch_a, w_branch_b, w_out, w_router, b_router, w_gate_up, b_gate_up, w_down, b_down):
    batch, seq, _ = x_prompt.shape
    db, dt, _ = x_sample.shape
    assert w_in.shape[0] == 1, "one layer"
    tp, ts = batch * seq, db * dt
    xp = x_prompt.reshape(tp, D_MODEL)
    xs = x_sample.reshape(ts, D_MODEL)

    mod = _adaln(jnp.concatenate([c_prompt, c_sample], axis=0), w_ada[0], b_ada[0])
    mod_p = mod[:batch].reshape(batch, 1, 6 * D_MODEL)
    mod_s = jnp.repeat(mod[batch:], dt, axis=0).reshape(ts // TOK_BLOCK, TOK_BLOCK, 6 * D_MODEL)
    bpm_p = seq // TOK_BLOCK
    tmp = PROMPT_TOK_BLOCK
    bpm_big = seq // tmp

    w_packed = _pack_w_in(w_in[0])
    pos_p = jnp.arange(seq)
    pos_s = PAST_LEN + (jnp.arange(TOK_BLOCK) % dt)
    g_mix = norm_mix_g[0].reshape(1, D_MODEL)
    outs_p = _inproj(xp, mod_p, g_mix, w_packed, _rope_tables(pos_p, ROPE_DIM, ROPE_THETA, HEAD_DIM),
                     _rope_tables(pos_p, RET_DK, RET_THETA, RET_DK), TOK_BLOCK, bpm_p, bpm_p)
    outs_s = _inproj(xs, mod_s, g_mix, w_packed, _rope_tables(pos_s, ROPE_DIM, ROPE_THETA, HEAD_DIM),
                     _rope_tables(pos_s, RET_DK, RET_THETA, RET_DK), TOK_BLOCK, 1, 1)
    (q_p, k_p, v_p, kb_p, vb_p, iq_p, ikw_p, ikb_p, rq_p, rk_p, rv_p, sg_p, sga_p, sgb_p) = outs_p
    (q_s, k_s, v_s, kb_s, vb_s, iq_s, ikw_s, ikb_s, rq_s, rk_s, rv_s, sg_s, sga_s, sgb_s) = outs_s

    tri = (jnp.arange(LANES)[:, None] < jnp.arange(LANES)[None, :]).astype(BF16)

    a_p = _dsa_prompt(q_p, iq_p, ikw_p, kb_p, vb_p, ikb_p, tri, batch, seq)
    group = ATT_HEADS // ATT_KV_HEADS
    q4 = q_s.reshape(db, dt, ATT_HEADS, LANES)
    qs = jnp.stack([q4[:, :, h, (h // group) * HEAD_DIM:(h // group + 1) * HEAD_DIM] for h in range(ATT_HEADS)],
                   axis=1).reshape(db, ATT_HEADS * dt, HEAD_DIM)
    iqs = _heads_major(iq_s, db, dt, LANES)[:, :, :IDX_DIM]
    ws = _heads_major(ikw_s[:, IDX_DIM:IDX_DIM + IDX_HEADS] * (IDX_HEADS ** -0.5), db, dt, 1)
    new_t = lambda a: jnp.pad(a.reshape(db, dt, ATT_KV_HEADS, HEAD_DIM).transpose(0, 2, 3, 1),
                              ((0, 0), (0, 0), (0, 0), (0, LANES - dt)))
    iknew_t = jnp.pad(ikb_s[:, :IDX_DIM].reshape(db, dt, IDX_DIM).transpose(0, 2, 1),
                      ((0, 0), (0, 0), (0, LANES - dt)))
    o_s = _dsa_sample(page_table, qs, iqs, ws, new_t(kb_s), new_t(vb_s), iknew_t,
                      cache_ik[0].transpose(0, 2, 1), cache_k[0].transpose(0, 2, 3, 1),
                      cache_v[0].transpose(0, 2, 3, 1), tri, dt)
    a_s = o_s.reshape(db, ATT_HEADS, dt, HEAD_DIM).transpose(0, 2, 1, 3).reshape(ts, ATT_OUT).astype(BF16)

    r_p, st_p = _retention(rq_p, rk_p, rv_p, sg_p, _retention_tables(float(RET_CHUNK)), None, batch,
                           seq // RET_CHUNK)
    pad = lambda a: _pad_rows(a, db, dt, RET_CHUNK).reshape(db * RET_CHUNK, -1)
    r_s, st_s = _retention(pad(rq_s), pad(rk_s), pad(rv_s), pad(sg_s), _retention_tables(float(dt)),
                           state_ret[0].reshape(db, RET_HEADS // 2, LANES, LANES), db, 1)
    r_s = r_s.reshape(db, RET_CHUNK, RET_OUT)[:, :dt].reshape(ts, RET_OUT)

    wr = jnp.pad(w_router[0], ((0, 0), (0, LANES - N_EXPERTS)))
    br = jnp.concatenate([b_router[0], jnp.full((LANES - N_EXPERTS,), -jnp.inf, F32)]).reshape(1, LANES)
    post_w = (w_branch_a[0].astype(BF16), w_branch_b[0].astype(BF16), w_out[0].astype(BF16),
              norm_ffn_g[0].reshape(1, D_MODEL), wr, br)
    x1_p, h2w_p, idx_p, gate_p, rank_p, cnt_p = _post(a_p, r_p, sga_p, sgb_p, xp, mod_p, *post_w,
                                                      jnp.zeros((1, LANES), F32), tmp, bpm_big)
    x1_s, h2w_s, idx_s, gate_s, rank_s, cnt_all = _post(a_s, r_s, sga_s, sgb_s, xs, mod_s, *post_w, cnt_p,
                                                        TOK_BLOCK, 1)

    n_asg = (tp + ts) * TOP_K
    counts = cnt_all[0, :N_EXPERTS].astype(I32)
    padded = (counts + MOE_ROWS - 1) // MOE_ROWS * MOE_ROWS
    pend = jnp.cumsum(padded)
    pstart = pend - padded
    def dest_rows(idx8, rank8):
        idx, dest = idx8[:TOP_K], rank8[:TOP_K]
        for e in range(N_EXPERTS):
            dest = dest + jnp.where(idx == e, pstart[e], 0)
        return dest.reshape(-1)

    dest_p = dest_rows(idx_p, rank_p)
    dest_s = dest_rows(idx_s, rank_s)
    n_blocks = -(-n_asg // MOE_ROWS) + N_EXPERTS
    blk_start = jnp.arange(n_blocks, dtype=I32) * MOE_ROWS
    blk_expert = jnp.minimum(jnp.sum((blk_start[:, None] >= pend[None, :]).astype(I32), axis=1), N_EXPERTS - 1)
    blk_first = jnp.concatenate([jnp.ones((1,), I32), (blk_expert[1:] != blk_expert[:-1]).astype(I32)])
    n_used = (pend[-1] // MOE_ROWS).astype(I32).reshape(1)
    x_sorted = jnp.zeros((n_blocks * MOE_ROWS, D_MODEL // 2), jnp.uint32)
    x_sorted = _dispatch(dest_p, h2w_p, x_sorted)
    x_sorted = _dispatch(dest_s, h2w_s, x_sorted)
    rows_out = _moe(blk_expert, blk_first, n_used, x_sorted, w_gate_up[0], b_gate_up[0], w_down[0], b_down[0])

    g_final = norm_final_g.reshape(1, D_MODEL)
    y_p = _final(dest_p, x1_p, mod_p, gate_p, rows_out, g_final, bpm_p)
    y_s = _final(dest_s, x1_s, mod_s, gate_s, rows_out, g_final, 1)

    kv_shape = lambda b, s: (1, b, s, ATT_KV_HEADS, HEAD_DIM)
    st_shape = lambda b: (1, b, RET_HEADS, RET_DK, RET_DV)
    return (y_p.reshape(batch, seq, D_MODEL), y_s.reshape(db, dt, D_MODEL),
            k_p.reshape(kv_shape(batch, seq)), v_p.reshape(kv_shape(batch, seq)),
            ikw_p[:, :IDX_DIM].reshape(1, batch, seq, IDX_DIM), st_p.reshape(st_shape(batch)),
            k_s.reshape(kv_shape(db, dt)), v_s.reshape(kv_shape(db, dt)),
            ikw_s[:, :IDX_DIM].reshape(1, db, dt, IDX_DIM), st_s.reshape(st_shape(db)))
```

```python
import functools

import jax
import jax.numpy as jnp
import numpy as np
from jax import lax
from jax.experimental import pallas as pl
from jax.experimental.pallas import tpu as pltpu

F32 = jnp.float32
BF16 = jnp.bfloat16
I32 = jnp.int32

D_MODEL = 1024
PAST_LEN = 8192
PAGE_SIZE = 128
ATT_HEADS = 8
ATT_KV_HEADS = 2
HEAD_DIM = 64
ROPE_DIM = HEAD_DIM // 4
ROPE_THETA = 500000.0
IDX_HEADS = 8
IDX_DIM = 64
IDX_ROPE_DIM = IDX_DIM // 4
TOPK_MAX = 256
RET_HEADS = 8
RET_DK = 64
RET_DV = 128
RET_THETA = 10000.0
RET_CHUNK = 128
N_EXPERTS = 32
TOP_K = 4
D_FF = D_MODEL
SWIGLU_LIMIT = 7.0
SWIGLU_ALPHA = 1.702
NORM_EPS = 1e-6
GN_EPS = 1e-5
ATT_OUT = ATT_HEADS * HEAD_DIM
RET_OUT = RET_HEADS * RET_DV
IN_SPLITS = (ATT_HEADS * HEAD_DIM, ATT_KV_HEADS * HEAD_DIM, ATT_KV_HEADS * HEAD_DIM,
             IDX_HEADS * IDX_DIM, IDX_DIM, IDX_HEADS,
             RET_HEADS * RET_DK, RET_HEADS * RET_DK, RET_OUT, RET_OUT, D_MODEL, D_MODEL)

LANES = 128
MASK_NEG = -1e30
FLT_MAX = 3.4028234663852886e38
SELECT_UNROLL = 4
CAUSAL_VARIANTS = 8
IDX_KEY_CHUNK = 256
VMEM_LIMIT = 56 * 1024 * 1024

TOK_BLOCK = 256
PROMPT_TOK_BLOCK = 512
Q_BLOCK = 128
IDX_Q_ROWS = 128
MOE_ROWS = 512

_W_GROUPS = (("q", 1024), ("iq", 1024), ("kvi", 384), ("rq", 512), ("rk", 512),
             ("rv", 1024), ("rg", 1024), ("ga", 1024), ("gb", 1024))
PROJ_COLS = 512
_W_OFF = {}
_off = 0
for _n, _w in _W_GROUPS:
    _W_OFF[_n] = (_off, _w)
    _off += _w
W_COLS = _off


def _cparams(*sem):
    return pltpu.CompilerParams(dimension_semantics=sem, vmem_limit_bytes=VMEM_LIMIT)


def _adaln_kernel(c_ref, w_ref, b_ref, o_ref):
    c = c_ref[...]
    s = c * jax.nn.sigmoid(c)
    o_ref[...] = jnp.dot(s, w_ref[...], preferred_element_type=F32, precision=lax.Precision.HIGHEST) + b_ref[...]


def _adaln(c_all, w_ada, b_ada):
    n = c_all.shape[0]
    nb = 1536
    return pl.pallas_call(
        _adaln_kernel,
        grid=(6 * D_MODEL // nb,),
        in_specs=[pl.BlockSpec((n, D_MODEL), lambda j: (0, 0)),
                  pl.BlockSpec((D_MODEL, nb), lambda j: (0, j)),
                  pl.BlockSpec((1, nb), lambda j: (0, j))],
        out_specs=pl.BlockSpec((n, nb), lambda j: (0, j)),
        out_shape=jax.ShapeDtypeStruct((n, 6 * D_MODEL), F32),
        compiler_params=_cparams("arbitrary"),
        name="adaln",
    )(c_all, w_ada, b_ada.reshape(1, -1))


def _rope_slab(z, c, sa, sb, half):
    return z * c + pltpu.roll(z, LANES - half, 1) * sa + pltpu.roll(z, half, 1) * sb


def _inproj_kernel(x_ref, mod_ref, g_ref, w_ref, ca_ref, saa_ref, sba_ref, cr_ref, sar_ref, sbr_ref,
                   q_ref, k_ref, v_ref, kb_ref, vb_ref, iq_ref, ikw_ref, ikb_ref,
                   rq_ref, rk_ref, rv_ref, sg_ref, sga_ref, sgb_ref):
    x = x_ref[...]
    ms = jnp.mean(x * x, axis=-1, keepdims=True)
    y = x * lax.rsqrt(ms + NORM_EPS) * g_ref[...]
    h = (y * (1.0 + mod_ref[:, D_MODEL:2 * D_MODEL]) + mod_ref[:, 0:D_MODEL]).astype(BF16)

    def slabs(name):
        c0, width = _W_OFF[name]
        step = min(width, PROJ_COLS)
        for j in range(width // step):
            z = jnp.dot(h, w_ref[:, c0 + j * step:c0 + (j + 1) * step], preferred_element_type=F32)
            for s in range(step // LANES):
                yield j * (step // LANES) + s, z[:, s * LANES:(s + 1) * LANES]

    ca, saa, sba = ca_ref[...], saa_ref[...], sba_ref[...]
    cr, sar, sbr = cr_ref[...], sar_ref[...], sbr_ref[...]
    att_half, ret_half = ROPE_DIM // 2, RET_DK // 2
    lane = lax.broadcasted_iota(I32, (x.shape[0], LANES), 1)
    sl = lambda s: slice(s * LANES, (s + 1) * LANES)

    for s, z in slabs("q"):
        q_ref[:, sl(s)] = (_rope_slab(z, ca, saa, sba, att_half) * 0.125).astype(BF16)
    for s, z in slabs("iq"):
        iq_ref[:, sl(s)] = (_rope_slab(z, ca, saa, sba, att_half) * 0.125).astype(BF16)
    (_, zk), (_, zv), (_, zi) = slabs("kvi")
    kk = _rope_slab(zk, ca, saa, sba, att_half)
    k_ref[...] = kk
    kb_ref[...] = kk.astype(BF16)
    v_ref[...] = zv
    vb_ref[...] = zv.astype(BF16)
    zr = _rope_slab(zi, ca, saa, sba, att_half)
    ikw_ref[...] = jnp.where(lane < IDX_DIM, zr, zi)
    ikb_ref[...] = jnp.where(lane < IDX_DIM, zr, 0.0).astype(BF16)
    for s, z in slabs("rq"):
        rq_ref[:, sl(s)] = _rope_slab(z, cr, sar, sbr, ret_half).astype(BF16)
    for s, z in slabs("rk"):
        rk_ref[:, sl(s)] = (_rope_slab(z, cr, sar, sbr, ret_half) * 0.125).astype(BF16)
    for s, z in slabs("rv"):
        rv_ref[:, sl(s)] = z.astype(BF16)
    for s, z in slabs("rg"):
        sg_ref[:, sl(s)] = (z * jax.nn.sigmoid(z)).astype(BF16)
    for s, z in slabs("ga"):
        sga_ref[:, sl(s)] = jax.nn.sigmoid(z).astype(BF16)
    for s, z in slabs("gb"):
        sgb_ref[:, sl(s)] = jax.nn.sigmoid(z).astype(BF16)


def _inproj(x, mod3, g, w_packed, tabs_att, tabs_ret, tm, blocks_per_mod, tab_blocks):
    t = x.shape[0]
    nblk = t // tm
    mod_rows = mod3.shape[1]
    tab_spec = pl.BlockSpec((tm, LANES), lambda i: (i % tab_blocks, 0))
    row = lambda n: pl.BlockSpec((tm, n), lambda i: (i, 0))
    out_defs = [(1024, BF16), (128, F32), (128, F32), (128, BF16), (128, BF16), (1024, BF16), (128, F32),
                (128, BF16), (512, BF16), (512, BF16), (1024, BF16), (1024, BF16), (1024, BF16), (1024, BF16)]
    return pl.pallas_call(
        _inproj_kernel,
        grid=(nblk,),
        in_specs=[row(D_MODEL),
                  pl.BlockSpec((None, mod_rows, 6 * D_MODEL), lambda i: (i // blocks_per_mod, 0, 0)),
                  pl.BlockSpec((1, D_MODEL), lambda i: (0, 0)),
                  pl.BlockSpec((D_MODEL, W_COLS), lambda i: (0, 0), pipeline_mode=pl.Buffered(1))]
                 + [tab_spec] * 6,
        out_specs=[row(n) for n, _ in out_defs],
        out_shape=[jax.ShapeDtypeStruct((t, n), d) for n, d in out_defs],
        compiler_params=_cparams("parallel"),
        name="inproj",
    )(x, mod3, g, w_packed, *tabs_att, *tabs_ret)


def _count(score_ref, n, pred):
    acc = jnp.zeros((score_ref.shape[0], LANES), F32)
    for c in range(n // LANES):
        acc = acc + jnp.where(pred(score_ref[:, c * LANES:(c + 1) * LANES]), 1.0, 0.0)
    return jnp.sum(acc, axis=1, keepdims=True)


def _kth_largest(score_ref, n, k):
    sc = score_ref[:, :n]
    finite = sc > -jnp.inf
    n_fin = jnp.sum(jnp.where(finite, 1.0, 0.0), axis=1, keepdims=True)
    n_pos = jnp.sum(jnp.where(sc > 0.0, 1.0, 0.0), axis=1, keepdims=True)
    n_nonneg = jnp.sum(jnp.where(sc >= 0.0, 1.0, 0.0), axis=1, keepdims=True)
    mx = jnp.max(sc, axis=1, keepdims=True)
    mn = jnp.min(jnp.where(finite, sc, jnp.inf), axis=1, keepdims=True)
    small = n_fin <= k
    positive = n_pos >= k
    at_zero = jnp.logical_and(jnp.logical_not(positive), n_nonneg >= k)
    lo = jnp.where(positive, 0.0, mn)
    hi = jnp.where(positive, mx + (jnp.abs(mx) * 2.0 ** -20 + 2.0 ** -100), 0.0)
    lo = jnp.where(at_zero, 0.0, lo)
    done = jnp.where(jnp.logical_or(small, at_zero), 1.0, 0.0)

    def cond(state):
        return jnp.min(state[2]) < 0.5

    def body(state):
        lo, hi, done = state
        for _ in range(SELECT_UNROLL):
            mid = 0.5 * lo + 0.5 * hi
            cnt = _count(score_ref, n, lambda s: s >= mid)
            stuck = jnp.logical_or(mid <= lo, mid >= hi)
            live = jnp.logical_and(done < 0.5, jnp.logical_not(stuck))
            ge = cnt >= k
            lo = jnp.where(jnp.logical_and(live, ge), mid, lo)
            hi = jnp.where(jnp.logical_and(live, jnp.logical_not(ge)), mid, hi)
            done = jnp.where(jnp.logical_or(stuck, cnt == k), 1.0, done)
        return lo, hi, done

    lo, _, _ = lax.while_loop(cond, body, (lo, hi, done))
    return jnp.where(small, -FLT_MAX, lo)


def _topk_bias(score_ref, bias_ref, tri_ref, n, k):
    rows = score_ref.shape[0]
    thr = _kth_largest(score_ref, n, k)
    need = k - _count(score_ref, n, lambda s: s > thr)
    n_eq = _count(score_ref, n, lambda s: s == thr)
    has_ties = jnp.max(jnp.where(need < n_eq, 1.0, 0.0)) > 0.5

    @pl.when(jnp.logical_not(has_ties))
    def _():
        for c in range(n // LANES):
            sl = slice(c * LANES, (c + 1) * LANES)
            bias_ref[:, sl] = jnp.where(score_ref[:, sl] >= thr, 0.0, MASK_NEG)

    @pl.when(has_ties)
    def _():
        run = jnp.zeros((rows, 1), F32)
        for c in range(n // LANES):
            sl = slice(c * LANES, (c + 1) * LANES)
            sc = score_ref[:, sl]
            eq = sc == thr
            eqf = jnp.where(eq, 1.0, 0.0)
            before = jnp.dot(eqf.astype(BF16), tri_ref[...], preferred_element_type=F32) + run
            take = jnp.logical_or(sc > thr, jnp.logical_and(eq, before < need))
            bias_ref[:, sl] = jnp.where(take, 0.0, MASK_NEG)
            run = run + jnp.sum(eqf, axis=1, keepdims=True)


def _dsa_prompt_kernel(q_ref, iq_ref, ikw_ref, kb_ref, vb_ref, ikb_ref, tri_ref, o_ref, score_ref, bias_ref):
    qb, s_len = score_ref.shape
    i = pl.program_id(1)
    nqb = s_len // qb
    per_variant = nqb // CAUSAL_VARIANTS
    for v in range(CAUSAL_VARIANTS):
        pl.when(i // per_variant == v)(
            functools.partial(_dsa_prompt_body, q_ref, iq_ref, ikw_ref, kb_ref, vb_ref, ikb_ref, tri_ref, o_ref,
                              score_ref, bias_ref, (v + 1) * per_variant * qb, min(TOPK_MAX, s_len // 4)))


def _dsa_prompt_body(q_ref, iq_ref, ikw_ref, kb_ref, vb_ref, ikb_ref, tri_ref, o_ref, score_ref, bias_ref,
                     n_keys, topk):
    qb = score_ref.shape[0]
    i = pl.program_id(1)
    w = ikw_ref[:, IDX_DIM:IDX_DIM + IDX_HEADS] * (IDX_HEADS ** -0.5)
    nt = (((1,), (1,)), ((), ()))
    kc = IDX_KEY_CHUNK
    qr = IDX_Q_ROWS
    for c in range(n_keys // kc):
        ikc = ikb_ref[c * kc:(c + 1) * kc, :]
        kpos = c * kc + lax.broadcasted_iota(I32, (qr, kc), 1)
        for r0 in range(0, qb, qr):
            acc = jnp.zeros((qr, kc), F32)
            for h in range(IDX_HEADS):
                d = lax.dot_general(iq_ref[r0:r0 + qr, h * LANES:(h + 1) * LANES], ikc, nt,
                                    preferred_element_type=F32)
                acc = acc + jnp.maximum(d, 0.0) * w[r0:r0 + qr, h:h + 1]
            qpos = i * qb + r0 + lax.broadcasted_iota(I32, (qr, kc), 0)
            score_ref[r0:r0 + qr, c * kc:(c + 1) * kc] = jnp.where(kpos <= qpos, acc, -jnp.inf)

    _topk_bias(score_ref, bias_ref, tri_ref, n_keys, topk)

    kb = kb_ref[0:n_keys, :]
    vb = vb_ref[0:n_keys, :]
    bias = bias_ref[:, 0:n_keys]
    lane = lax.broadcasted_iota(I32, (qb, LANES), 1)
    heads = []
    for h in range(ATT_HEADS):
        s = lax.dot_general(q_ref[:, h * LANES:(h + 1) * LANES], kb, nt, preferred_element_type=F32) + bias
        m = jnp.max(s, axis=1, keepdims=True)
        p = jnp.exp(s - m)
        l = jnp.sum(p, axis=1, keepdims=True)
        heads.append(jnp.dot(p.astype(BF16), vb, preferred_element_type=F32) / l)
    group = ATT_HEADS // ATT_KV_HEADS
    for pp in range(ATT_HEADS // 2):
        a, b = heads[2 * pp], heads[2 * pp + 1]
        if (2 * pp) // group == 0:
            slab = jnp.where(lane < HEAD_DIM, a, pltpu.roll(b, HEAD_DIM, 1))
        else:
            slab = jnp.where(lane < HEAD_DIM, pltpu.roll(a, HEAD_DIM, 1), b)
        o_ref[:, pp * LANES:(pp + 1) * LANES] = slab.astype(BF16)


def _dsa_prompt(q, iq, ikw, kb, vb, ikb, tri, batch, seq):
    nqb = seq // Q_BLOCK
    qrow = lambda n: pl.BlockSpec((Q_BLOCK, n), lambda b, i: (b * nqb + i, 0))
    keys = pl.BlockSpec((seq, LANES), lambda b, i: (b, 0))
    return pl.pallas_call(
        _dsa_prompt_kernel,
        grid=(batch, nqb),
        in_specs=[qrow(1024), qrow(1024), qrow(LANES), keys, keys, keys,
                  pl.BlockSpec((LANES, LANES), lambda b, i: (0, 0))],
        out_specs=qrow(ATT_OUT),
        out_shape=jax.ShapeDtypeStruct((batch * seq, ATT_OUT), BF16),
        scratch_shapes=[pltpu.VMEM((Q_BLOCK, seq), F32), pltpu.VMEM((Q_BLOCK, seq), F32)],
        compiler_params=_cparams("parallel", "arbitrary"),
        name="dsa_prompt",
    )(q, iq, ikw, kb, vb, ikb, tri)


def _dsa_sample_kernel(pt_ref, qs_ref, iqs_ref, ws_ref, knew_ref, vnew_ref, iknew_ref, cik_hbm, ck_hbm, cv_hbm,
                       tri_ref, o_ref, ikbuf, kbuf, vbuf, ikt, kt, vt, sems, key_ref, bias_ref):
    db = pl.program_id(0)
    n_pages = ikbuf.shape[1]
    t = key_ref.shape[0]
    n_past = n_pages * PAGE_SIZE
    last = pl.num_programs(0) - 1

    def fetch(src, dst_of_page, sem, req):
        def body(p, carry):
            pltpu.make_async_copy(src.at[pt_ref[req, p]], dst_of_page(p), sem).start()
            return carry
        lax.fori_loop(0, n_pages, body, 0)

    def wait_all(src, dst, sem):
        pltpu.make_async_copy(src.at[pl.ds(0, n_pages)], dst, sem).wait()

    fetch_ik = lambda req, slot: fetch(cik_hbm, lambda p: ikbuf.at[slot, p], sems.at[slot], req)
    fetch_k = lambda req: fetch(ck_hbm, lambda p: kbuf.at[p], sems.at[2], req)
    fetch_v = lambda req: fetch(cv_hbm, lambda p: vbuf.at[p], sems.at[3], req)

    @pl.when(db == 0)
    def _():
        fetch_ik(0, 0)
        fetch_k(0)
        fetch_v(0)

    @pl.when(db < last)
    def _():
        fetch_ik(db + 1, (db + 1) % 2)

    slot = db % 2
    wait_all(cik_hbm, ikbuf.at[slot], sems.at[slot])

    nt = (((1,), (1,)), ((), ()))
    page = lambda p: slice(p * PAGE_SIZE, (p + 1) * PAGE_SIZE)
    for p in range(n_pages):
        ikt[:, page(p)] = ikbuf[slot, p].astype(BF16)
    iqs = iqs_ref[...]
    wcol = ws_ref[...]
    d_past = jnp.maximum(jnp.dot(iqs, ikt[...], preferred_element_type=F32), 0.0) * wcol
    d_new = jnp.maximum(jnp.dot(iqs, iknew_ref[...], preferred_element_type=F32), 0.0) * wcol
    s_past = d_past[0:t]
    s_new = d_new[0:t]
    for h in range(1, IDX_HEADS):
        s_past = s_past + d_past[h * t:(h + 1) * t]
        s_new = s_new + d_new[h * t:(h + 1) * t]
    row = lax.broadcasted_iota(I32, (t, LANES), 0)
    lane = lax.broadcasted_iota(I32, (t, LANES), 1)
    new_ok = lane <= row
    key_ref[:, 0:n_past] = s_past
    key_ref[:, n_past:n_past + LANES] = jnp.where(new_ok, s_new, -jnp.inf)
    _topk_bias(key_ref, bias_ref, tri_ref, n_past + LANES, min(TOPK_MAX, (n_past + t) // 4))

    def stage(buf, dst):
        for p in range(n_pages):
            for j in range(ATT_KV_HEADS):
                dst[j, :, page(p)] = buf[p, j].astype(BF16)

    wait_all(ck_hbm, kbuf, sems.at[2])
    stage(kbuf, kt)

    @pl.when(db < last)
    def _():
        fetch_k(db + 1)

    wait_all(cv_hbm, vbuf, sems.at[3])
    stage(vbuf, vt)

    @pl.when(db < last)
    def _():
        fetch_v(db + 1)

    rows_per_kv = qs_ref.shape[0] // ATT_KV_HEADS
    bias = jnp.concatenate([bias_ref[...]] * (rows_per_kv // t), axis=0)
    for j in range(ATT_KV_HEADS):
        qj = qs_ref[j * rows_per_kv:(j + 1) * rows_per_kv, :]
        sp = jnp.dot(qj, kt[j], preferred_element_type=F32) + bias[:, 0:n_past]
        sn = jnp.dot(qj, knew_ref[j], preferred_element_type=F32) + bias[:, n_past:n_past + LANES]
        m = jnp.maximum(jnp.max(sp, axis=1, keepdims=True), jnp.max(sn, axis=1, keepdims=True))
        pp = jnp.exp(sp - m)
        pn = jnp.exp(sn - m)
        l = jnp.sum(pp, axis=1, keepdims=True) + jnp.sum(pn, axis=1, keepdims=True)
        o = (lax.dot_general(pp.astype(BF16), vt[j], nt, preferred_element_type=F32)
             + lax.dot_general(pn.astype(BF16), vnew_ref[j], nt, preferred_element_type=F32))
        o_ref[j * rows_per_kv:(j + 1) * rows_per_kv, :] = o / l


def _dsa_sample(page_table, qs, iqs, ws, knew, vnew, iknew, cache_ik, cache_k, cache_v, tri, t):
    db, n_pages = page_table.shape
    rows = qs.shape[1]
    per_db = lambda r, n: pl.BlockSpec((None, r, n), lambda b, pt: (b, 0, 0))
    any_spec = pl.BlockSpec(memory_space=pl.ANY)
    n_keys = n_pages * PAGE_SIZE + LANES
    grid_spec = pltpu.PrefetchScalarGridSpec(
        num_scalar_prefetch=1,
        grid=(db,),
        in_specs=[per_db(rows, HEAD_DIM), per_db(rows, IDX_DIM), per_db(rows, 1),
                  pl.BlockSpec((None, ATT_KV_HEADS, HEAD_DIM, LANES), lambda b, pt: (b, 0, 0, 0)),
                  pl.BlockSpec((None, ATT_KV_HEADS, HEAD_DIM, LANES), lambda b, pt: (b, 0, 0, 0)),
                  per_db(IDX_DIM, LANES), any_spec, any_spec, any_spec,
                  pl.BlockSpec((LANES, LANES), lambda b, pt: (0, 0))],
        out_specs=per_db(rows, HEAD_DIM),
        scratch_shapes=[pltpu.VMEM((2, n_pages, IDX_DIM, PAGE_SIZE), F32),
                        pltpu.VMEM((n_pages, ATT_KV_HEADS, HEAD_DIM, PAGE_SIZE), F32),
                        pltpu.VMEM((n_pages, ATT_KV_HEADS, HEAD_DIM, PAGE_SIZE), F32),
                        pltpu.VMEM((IDX_DIM, n_pages * PAGE_SIZE), BF16),
                        pltpu.VMEM((ATT_KV_HEADS, HEAD_DIM, n_pages * PAGE_SIZE), BF16),
                        pltpu.VMEM((ATT_KV_HEADS, HEAD_DIM, n_pages * PAGE_SIZE), BF16),
                        pltpu.SemaphoreType.DMA((4,)),
                        pltpu.VMEM((t, n_keys), F32),
                        pltpu.VMEM((t, n_keys), F32)])
    return pl.pallas_call(
        _dsa_sample_kernel,
        grid_spec=grid_spec,
        out_shape=jax.ShapeDtypeStruct((db, rows, HEAD_DIM), F32),
        compiler_params=_cparams("arbitrary"),
        name="dsa_sample",
    )(page_table, qs, iqs, ws, knew, vnew, iknew, cache_ik, cache_k, cache_v, tri)


def _retention_kernel(has_init, rq_ref, rk_ref, rv_ref, sg_ref, decay_ref, qdec_ref, kdec_ref, gst_ref, *rest):
    if has_init:
        init_ref, o_ref, st_ref, state = rest
    else:
        o_ref, st_ref, state = rest
    c = pl.program_id(1)

    @pl.when(c == 0)
    def _():
        if has_init:
            state[...] = init_ref[...]
        else:
            state[...] = jnp.zeros_like(state)

    nt = (((1,), (1,)), ((), ()))
    tn = (((0,), (0,)), ((), ()))
    rows = rq_ref.shape[0]
    lane = lax.broadcasted_iota(I32, (rows, LANES), 1)
    for p in range(RET_HEADS // 2):
        sl = slice(p * LANES, (p + 1) * LANES)
        qp = rq_ref[:, sl]
        kp = rk_ref[:, sl]
        qd = (qp.astype(F32) * qdec_ref[:, sl]).astype(BF16)
        kd = (kp.astype(F32) * kdec_ref[:, sl]).astype(BF16)
        s_old = state[p]
        s_old_b = s_old.astype(BF16)
        s_new = s_old * gst_ref[p]
        for e in range(2):
            h = 2 * p + e
            hs = slice(h * LANES, (h + 1) * LANES)
            mine = jnp.where((lane >= e * RET_DK) & (lane < (e + 1) * RET_DK), 1.0, 0.0).astype(BF16)
            sc = lax.dot_general(qp * mine, kp, nt, preferred_element_type=F32) * decay_ref[h]
            vh = rv_ref[:, hs]
            o = (jnp.dot(sc.astype(BF16), vh, preferred_element_type=F32)
                 + jnp.dot(qd * mine, s_old_b, preferred_element_type=F32))
            s_new = s_new + lax.dot_general(kd * mine, vh, tn, preferred_element_type=F32)
            mu = jnp.mean(o, axis=-1, keepdims=True)
            var = jnp.mean(jnp.square(o - mu), axis=-1, keepdims=True)
            on = (o - mu) * lax.rsqrt(var + GN_EPS)
            o_ref[:, hs] = (on * sg_ref[:, hs].astype(F32)).astype(BF16)
        state[p] = s_new

    @pl.when(c == pl.num_programs(1) - 1)
    def _():
        st_ref[...] = state[...]


def _retention(rq, rk, rv, sg, tables, init, batch, n_chunks):
    decay, qdec, kdec, gst = tables
    cr = RET_CHUNK
    rowspec = lambda n: pl.BlockSpec((cr, n), lambda b, c: (b * n_chunks + c, 0))
    const = lambda shape: pl.BlockSpec(shape, lambda b, c: (0,) * len(shape))
    st_spec = pl.BlockSpec((None, RET_HEADS // 2, LANES, LANES), lambda b, c: (b, 0, 0, 0))
    in_specs = [rowspec(512), rowspec(512), rowspec(RET_OUT), rowspec(RET_OUT),
                const(decay.shape), const(qdec.shape), const(kdec.shape), const(gst.shape)]
    args = [rq, rk, rv, sg, decay, qdec, kdec, gst]
    if init is not None:
        in_specs.append(st_spec)
        args.append(init)
    return pl.pallas_call(
        functools.partial(_retention_kernel, init is not None),
        grid=(batch, n_chunks),
        in_specs=in_specs,
        out_specs=[rowspec(RET_OUT), st_spec],
        out_shape=[jax.ShapeDtypeStruct((batch * n_chunks * cr, RET_OUT), BF16),
                   jax.ShapeDtypeStruct((batch, RET_HEADS // 2, LANES, LANES), F32)],
        scratch_shapes=[pltpu.VMEM((RET_HEADS // 2, LANES, LANES), F32)],
        compiler_params=_cparams("parallel", "arbitrary"),
        name="retention",
    )(*args)


def _retention_tables(c_eff):
    lg = jnp.log(1.0 - 2.0 ** (-5.0 - jnp.arange(RET_HEADS, dtype=F32)))
    i = jnp.arange(RET_CHUNK, dtype=F32)
    diff = i[:, None] - i[None, :]
    decay = jnp.where(diff >= 0, jnp.exp(jnp.maximum(diff, 0.0)[None] * lg[:, None, None]), 0.0)
    q_decay = jnp.exp((i + 1.0)[:, None] * lg[None, :])
    k_decay = jnp.exp((c_eff - 1.0 - i)[:, None] * lg[None, :])
    qdec = jnp.repeat(q_decay, RET_DK, axis=1)
    kdec = jnp.repeat(k_decay, RET_DK, axis=1)
    g_state = jnp.exp(c_eff * lg)
    gst = jnp.broadcast_to(jnp.repeat(g_state, RET_DK).reshape(RET_HEADS // 2, LANES, 1),
                           (RET_HEADS // 2, LANES, LANES))
    return decay, qdec, kdec, gst


def _pack_bf16_pairs(lo, hi):
    return pltpu.pack_elementwise([lo, hi], packed_dtype=BF16)


def _unpack_bf16_pairs(words):
    return tuple(pltpu.unpack_elementwise(words, index=j, packed_dtype=BF16, unpacked_dtype=F32).astype(BF16)
                 for j in range(2))


def _post_kernel(a_ref, r_ref, sga_ref, sgb_ref, x_ref, mod_ref, wpa_ref, wpb_ref, wo_ref, g_ref, wr_ref, br_ref,
                 ltri_ref, cnt0_ref, x1_ref, h2w_ref, idx_ref, gate_ref, rank_ref, cnt_ref, cnt):
    @pl.when(pl.program_id(0) == 0)
    def _():
        cnt[...] = cnt0_ref[...]

    pa = jnp.dot(a_ref[...], wpa_ref[...], preferred_element_type=F32)
    pb = jnp.dot(r_ref[...], wpb_ref[...], preferred_element_type=F32)
    merged = sga_ref[...].astype(F32) * pa + sgb_ref[...].astype(F32) * pb
    gt1 = mod_ref[:, 2 * D_MODEL:3 * D_MODEL]
    x1 = x_ref[...] + gt1 * jnp.dot(merged.astype(BF16), wo_ref[...], preferred_element_type=F32)
    x1_ref[...] = x1
    ms = jnp.mean(x1 * x1, axis=-1, keepdims=True)
    y = x1 * lax.rsqrt(ms + NORM_EPS) * g_ref[...]
    h2 = y * (1.0 + mod_ref[:, 4 * D_MODEL:5 * D_MODEL]) + mod_ref[:, 3 * D_MODEL:4 * D_MODEL]
    half = D_MODEL // 2
    h2w_ref[...] = _pack_bf16_pairs(h2[:, :half], h2[:, half:])
    logits = jnp.dot(h2, wr_ref[...], preferred_element_type=F32, precision=lax.Precision.HIGHEST) + br_ref[...]
    lane = lax.broadcasted_iota(I32, logits.shape, 1).astype(F32)
    idx_out = jnp.zeros(logits.shape, F32)
    val_out = jnp.zeros(logits.shape, F32)
    chosen = []
    top = None
    for j in range(TOP_K):
        m = jnp.max(logits, axis=1, keepdims=True)
        am = jnp.min(jnp.where(logits == m, lane, float(LANES)), axis=1, keepdims=True)
        if j == 0:
            top = m
        idx_out = jnp.where(lane == j, am, idx_out)
        val_out = jnp.where(lane == j, jnp.exp(m - top), val_out)
        chosen.append(lane == am)
        logits = jnp.where(chosen[-1], -jnp.inf, logits)
    idx_ref[...] = jnp.transpose(idx_out)[0:8, :].astype(I32)
    gate_ref[...] = val_out / jnp.sum(val_out, axis=1, keepdims=True)
    onehot = jnp.where(chosen[0] | chosen[1] | chosen[2] | chosen[3], 1.0, 0.0)
    before = jnp.dot(ltri_ref[...], onehot.astype(BF16), preferred_element_type=F32) + cnt[...]
    rank_out = jnp.zeros(logits.shape, F32)
    for j in range(TOP_K):
        rj = jnp.sum(jnp.where(chosen[j], before, 0.0), axis=1, keepdims=True)
        rank_out = jnp.where(lane == j, rj, rank_out)
    rank_ref[...] = jnp.transpose(rank_out)[0:8, :].astype(I32)
    cnt[...] = cnt[...] + jnp.sum(onehot, axis=0, keepdims=True)
    cnt_ref[...] = cnt[...]


def _post(a, r, sga, sgb, x, mod3, wpa, wpb, wo, g, wr, br, cnt0, tm, blocks_per_mod):
    t = x.shape[0]
    mod_rows = mod3.shape[1]
    ar = jnp.arange(tm)
    ltri = (ar[None, :] < ar[:, None]).astype(BF16)
    row = lambda n: pl.BlockSpec((tm, n), lambda i: (i, 0))
    col8 = pl.BlockSpec((8, tm), lambda i: (0, i))
    const = lambda a_: pl.BlockSpec(a_.shape, lambda i: (0,) * a_.ndim)
    return pl.pallas_call(
        _post_kernel,
        grid=(t // tm,),
        in_specs=[row(ATT_OUT), row(RET_OUT), row(D_MODEL), row(D_MODEL), row(D_MODEL),
                  pl.BlockSpec((None, mod_rows, 6 * D_MODEL), lambda i: (i // blocks_per_mod, 0, 0)),
                  const(wpa), const(wpb), const(wo), const(g), const(wr), const(br), const(ltri), const(cnt0)],
        out_specs=[row(D_MODEL), row(D_MODEL // 2), col8, row(LANES), col8,
                   pl.BlockSpec((1, LANES), lambda i: (0, 0))],
        out_shape=[jax.ShapeDtypeStruct((t, D_MODEL), F32), jax.ShapeDtypeStruct((t, D_MODEL // 2), jnp.uint32),
                   jax.ShapeDtypeStruct((8, t), I32), jax.ShapeDtypeStruct((t, LANES), F32),
                   jax.ShapeDtypeStruct((8, t), I32), jax.ShapeDtypeStruct((1, LANES), F32)],
        scratch_shapes=[pltpu.VMEM((1, LANES), F32)],
        compiler_params=_cparams("arbitrary"),
        name="post",
    )(a, r, sga, sgb, x, mod3, wpa, wpb, wo, g, wr, br, ltri, cnt0)


def _dispatch_kernel(dest_ref, h2w_ref, xs_in, xs_out, sem):
    del xs_in
    i = pl.program_id(0)
    tm = h2w_ref.shape[0]

    def row_copy(r, d):
        return pltpu.make_async_copy(h2w_ref.at[pl.ds(r, 1)], xs_out.at[pl.ds(d, 1)], sem)

    n_tok = dest_ref.shape[0] // TOP_K

    def start(r, carry):
        for j in range(TOP_K):
            row_copy(r, dest_ref[j * n_tok + i * tm + r]).start()
        return carry

    lax.fori_loop(0, tm, start, 0)
    for j in range(TOP_K):
        pltpu.make_async_copy(h2w_ref, xs_out.at[pl.ds(0, tm)], sem).wait()


def _dispatch(dest_flat, h2w, x_sorted):
    t = h2w.shape[0]
    tm = TOK_BLOCK
    grid_spec = pltpu.PrefetchScalarGridSpec(
        num_scalar_prefetch=1,
        grid=(t // tm,),
        in_specs=[pl.BlockSpec((tm, D_MODEL // 2), lambda i, d: (i, 0)), pl.BlockSpec(memory_space=pl.ANY)],
        out_specs=pl.BlockSpec(memory_space=pl.ANY),
        scratch_shapes=[pltpu.SemaphoreType.DMA(())])
    return pl.pallas_call(
        _dispatch_kernel,
        grid_spec=grid_spec,
        out_shape=jax.ShapeDtypeStruct(x_sorted.shape, x_sorted.dtype),
        input_output_aliases={2: 0},
        compiler_params=_cparams("arbitrary"),
        name="dispatch",
    )(dest_flat, h2w, x_sorted)


def _moe_kernel(be_ref, first_ref, nused_ref, x_ref, wgu_ref, bgu_ref, wd_ref, bd_ref, o_ref, wgu_b, wd_b):
    i = pl.program_id(0)

    @pl.when(first_ref[i] == 1)
    def _():
        wgu_b[...] = wgu_ref[...].astype(BF16)
        wd_b[...] = wd_ref[...].astype(BF16)

    @pl.when(i < nused_ref[0])
    def _():
        x = jnp.concatenate(_unpack_bf16_pairs(x_ref[...]), axis=1)
        gu = jnp.dot(x, wgu_b[...], preferred_element_type=F32) + bgu_ref[...]
        g = jnp.minimum(gu[:, :D_FF], SWIGLU_LIMIT)
        u = jnp.clip(gu[:, D_FF:], -SWIGLU_LIMIT, SWIGLU_LIMIT)
        act = (u + 1.0) * (g * jax.nn.sigmoid(SWIGLU_ALPHA * g))
        o_ref[...] = jnp.dot(act.astype(BF16), wd_b[...], preferred_element_type=F32) + bd_ref[...]

    @pl.when(i >= nused_ref[0])
    def _():
        o_ref[...] = jnp.zeros_like(o_ref)


def _moe(blk_expert, blk_first, n_used, x_sorted, w_gate_up, b_gate_up, w_down, b_down):
    n_rows = x_sorted.shape[0]
    grid_spec = pltpu.PrefetchScalarGridSpec(
        num_scalar_prefetch=3,
        grid=(n_rows // MOE_ROWS,),
        in_specs=[pl.BlockSpec((MOE_ROWS, D_MODEL // 2), lambda i, be, bf, nu: (i, 0)),
                  pl.BlockSpec((None, D_MODEL, 2 * D_FF), lambda i, be, bf, nu: (be[i], 0, 0)),
                  pl.BlockSpec((None, 1, 2 * D_FF), lambda i, be, bf, nu: (be[i], 0, 0)),
                  pl.BlockSpec((None, D_FF, D_MODEL), lambda i, be, bf, nu: (be[i], 0, 0)),
                  pl.BlockSpec((None, 1, D_MODEL), lambda i, be, bf, nu: (be[i], 0, 0))],
        out_specs=pl.BlockSpec((MOE_ROWS, D_MODEL), lambda i, be, bf, nu: (i, 0)),
        scratch_shapes=[pltpu.VMEM((D_MODEL, 2 * D_FF), BF16), pltpu.VMEM((D_FF, D_MODEL), BF16)])
    return pl.pallas_call(
        _moe_kernel,
        grid_spec=grid_spec,
        out_shape=jax.ShapeDtypeStruct((n_rows, D_MODEL), F32),
        compiler_params=_cparams("arbitrary"),
        name="moe",
    )(blk_expert, blk_first, n_used, x_sorted, w_gate_up, b_gate_up.reshape(N_EXPERTS, 1, -1),
      w_down, b_down.reshape(N_EXPERTS, 1, -1))


def _final_kernel(dest_ref, x1_ref, mod_ref, gate_ref, g_ref, rows_hbm, y_ref, buf, sems):
    i = pl.program_id(0)
    n = pl.num_programs(0)
    tm = x1_ref.shape[0]

    n_tok = dest_ref.shape[0] // TOP_K

    def issue(blk, slot):
        def body(r, carry):
            for j in range(TOP_K):
                d = dest_ref[j * n_tok + blk * tm + r]
                pltpu.make_async_copy(rows_hbm.at[pl.ds(d, 1)], buf.at[slot, j, pl.ds(r, 1)], sems.at[slot]).start()
            return carry
        lax.fori_loop(0, tm, body, 0)

    @pl.when(i == 0)
    def _():
        issue(0, 0)

    @pl.when(i + 1 < n)
    def _():
        issue(i + 1, (i + 1) % 2)

    slot = i % 2
    for j in range(TOP_K):
        pltpu.make_async_copy(rows_hbm.at[pl.ds(0, tm)], buf.at[slot, j], sems.at[slot]).wait()
    gates = gate_ref[...]
    moe = buf[slot, 0] * gates[:, 0:1]
    for j in range(1, TOP_K):
        moe = moe + buf[slot, j] * gates[:, j:j + 1]
    x2 = x1_ref[...] + mod_ref[:, 5 * D_MODEL:6 * D_MODEL] * moe
    ms = jnp.mean(x2 * x2, axis=-1, keepdims=True)
    y_ref[...] = x2 * lax.rsqrt(ms + NORM_EPS) * g_ref[...]


def _final(dest_flat, x1, mod3, gates, rows_out, g, blocks_per_mod):
    t = x1.shape[0]
    tm = TOK_BLOCK
    mod_rows = mod3.shape[1]
    row = lambda n: pl.BlockSpec((tm, n), lambda i, d: (i, 0))
    grid_spec = pltpu.PrefetchScalarGridSpec(
        num_scalar_prefetch=1,
        grid=(t // tm,),
        in_specs=[row(D_MODEL),
                  pl.BlockSpec((None, mod_rows, 6 * D_MODEL), lambda i, d: (i // blocks_per_mod, 0, 0)),
                  row(LANES),
                  pl.BlockSpec((1, D_MODEL), lambda i, d: (0, 0)),
                  pl.BlockSpec(memory_space=pl.ANY)],
        out_specs=row(D_MODEL),
        scratch_shapes=[pltpu.VMEM((2, TOP_K, tm, D_MODEL), F32), pltpu.SemaphoreType.DMA((2,))])
    return pl.pallas_call(
        _final_kernel,
        grid_spec=grid_spec,
        out_shape=jax.ShapeDtypeStruct((t, D_MODEL), F32),
        compiler_params=_cparams("arbitrary"),
        name="final",
    )(dest_flat, x1, mod3, gates, g, rows_out)


def _rope_tables(pos, rot_dim, theta, head_dim):
    half = rot_dim // 2
    inv = theta ** (-jnp.arange(half, dtype=F32) * (2.0 / rot_dim))
    ang = pos.astype(F32)[:, None] * inv[None, :]
    cos, sin = jnp.cos(ang), jnp.sin(ang)
    n = pos.shape[0]
    rest = head_dim - rot_dim
    zh = jnp.zeros((n, half), F32)
    c = jnp.concatenate([cos, cos, jnp.ones((n, rest), F32)], axis=1)
    sa = jnp.concatenate([-sin, zh, jnp.zeros((n, rest), F32)], axis=1)
    sb = jnp.concatenate([zh, sin, jnp.zeros((n, rest), F32)], axis=1)
    rep = LANES // head_dim
    return tuple(jnp.tile(a, (1, rep)) for a in (c, sa, sb))


def _pack_w_in(w_in):
    offs = np.cumsum((0,) + IN_SPLITS)
    part = lambda j: w_in[:, offs[j]:offs[j + 1]]
    zero = lambda n: jnp.zeros((D_MODEL, n), w_in.dtype)
    group = ATT_HEADS // ATT_KV_HEADS
    cols = []
    wq = part(0)
    for h in range(ATT_HEADS):
        wh = wq[:, h * HEAD_DIM:(h + 1) * HEAD_DIM]
        cols += [wh, zero(HEAD_DIM)] if h // group == 0 else [zero(HEAD_DIM), wh]
    wiq = part(3)
    for h in range(IDX_HEADS):
        cols += [wiq[:, h * IDX_DIM:(h + 1) * IDX_DIM], zero(LANES - IDX_DIM)]
    cols += [part(1), part(2), part(4), part(5), zero(LANES - IDX_DIM - IDX_HEADS)]
    cols += [part(j) for j in range(6, 12)]
    return jnp.concatenate(cols, axis=1).astype(BF16)


def _heads_major(a, db, t, width):
    heads = a.shape[1] // width
    return a.reshape(db, t, heads, width).transpose(0, 2, 1, 3).reshape(db, heads * t, width)


def _pad_rows(a, db, t, rows):
    return jnp.pad(a.reshape(db, t, -1), ((0, 0), (0, rows - t), (0, 0)))


def kernel(x_prompt, x_sample, cache_k, cache_v, cache_ik, state_ret, page_table, c_prompt, c_sample, norm_mix_g, norm_ffn_g, norm_final_g, w_ada, b_ada, w_in, w_branch_a, w_branch_b, w_out, w_router, b_router, w_gate_up, b_gate_up, w_down, b_down):
    batch, seq, _ = x_prompt.shape
    db, dt, _ = x_sample.shape
    assert w_in.shape[0] == 1, "one layer"
    tp, ts = batch * seq, db * dt
    xp = x_prompt.reshape(tp, D_MODEL)
    xs = x_sample.reshape(ts, D_MODEL)

    mod = _adaln(jnp.concatenate([c_prompt, c_sample], axis=0), w_ada[0], b_ada[0])
    mod_p = mod[:batch].reshape(batch, 1, 6 * D_MODEL)
    mod_s = jnp.repeat(mod[batch:], dt, axis=0).reshape(ts // TOK_BLOCK, TOK_BLOCK, 6 * D_MODEL)
    bpm_p = seq // TOK_BLOCK
    tmp = PROMPT_TOK_BLOCK
    bpm_big = seq // tmp

    w_packed = _pack_w_in(w_in[0])
    pos_p = jnp.arange(seq)
    pos_s = PAST_LEN + (jnp.arange(TOK_BLOCK) % dt)
    g_mix = norm_mix_g[0].reshape(1, D_MODEL)
    outs_p = _inproj(xp, mod_p, g_mix, w_packed, _rope_tables(pos_p, ROPE_DIM, ROPE_THETA, HEAD_DIM),
                     _rope_tables(pos_p, RET_DK, RET_THETA, RET_DK), TOK_BLOCK, bpm_p, bpm_p)
    outs_s = _inproj(xs, mod_s, g_mix, w_packed, _rope_tables(pos_s, ROPE_DIM, ROPE_THETA, HEAD_DIM),
                     _rope_tables(pos_s, RET_DK, RET_THETA, RET_DK), TOK_BLOCK, 1, 1)
    (q_p, k_p, v_p, kb_p, vb_p, iq_p, ikw_p, ikb_p, rq_p, rk_p, rv_p, sg_p, sga_p, sgb_p) = outs_p
    (q_s, k_s, v_s, kb_s, vb_s, iq_s, ikw_s, ikb_s, rq_s, rk_s, rv_s, sg_s, sga_s, sgb_s) = outs_s

    tri = (jnp.arange(LANES)[:, None] < jnp.arange(LANES)[None, :]).astype(BF16)

    a_p = _dsa_prompt(q_p, iq_p, ikw_p, kb_p, vb_p, ikb_p, tri, batch, seq)
    group = ATT_HEADS // ATT_KV_HEADS
    q4 = q_s.reshape(db, dt, ATT_HEADS, LANES)
    qs = jnp.stack([q4[:, :, h, (h // group) * HEAD_DIM:(h // group + 1) * HEAD_DIM] for h in range(ATT_HEADS)],
                   axis=1).reshape(db, ATT_HEADS * dt, HEAD_DIM)
    iqs = _heads_major(iq_s, db, dt, LANES)[:, :, :IDX_DIM]
    ws = _heads_major(ikw_s[:, IDX_DIM:IDX_DIM + IDX_HEADS] * (IDX_HEADS ** -0.5), db, dt, 1)
    new_t = lambda a: jnp.pad(a.reshape(db, dt, ATT_KV_HEADS, HEAD_DIM).transpose(0, 2, 3, 1),
                              ((0, 0), (0, 0), (0, 0), (0, LANES - dt)))
    iknew_t = jnp.pad(ikb_s[:, :IDX_DIM].reshape(db, dt, IDX_DIM).transpose(0, 2, 1),
                      ((0, 0), (0, 0), (0, LANES - dt)))
    o_s = _dsa_sample(page_table, qs, iqs, ws, new_t(kb_s), new_t(vb_s), iknew_t,
                      cache_ik[0].transpose(0, 2, 1), cache_k[0].transpose(0, 2, 3, 1),
                      cache_v[0].transpose(0, 2, 3, 1), tri, dt)
    a_s = o_s.reshape(db, ATT_HEADS, dt, HEAD_DIM).transpose(0, 2, 1, 3).reshape(ts, ATT_OUT).astype(BF16)

    r_p, st_p = _retention(rq_p, rk_p, rv_p, sg_p, _retention_tables(float(RET_CHUNK)), None, batch,
                           seq // RET_CHUNK)
    pad = lambda a: _pad_rows(a, db, dt, RET_CHUNK).reshape(db * RET_CHUNK, -1)
    r_s, st_s = _retention(pad(rq_s), pad(rk_s), pad(rv_s), pad(sg_s), _retention_tables(float(dt)),
                           state_ret[0].reshape(db, RET_HEADS // 2, LANES, LANES), db, 1)
    r_s = r_s.reshape(db, RET_CHUNK, RET_OUT)[:, :dt].reshape(ts, RET_OUT)

    wr = jnp.pad(w_router[0], ((0, 0), (0, LANES - N_EXPERTS)))
    br = jnp.concatenate([b_router[0], jnp.full((LANES - N_EXPERTS,), -jnp.inf, F32)]).reshape(1, LANES)
    post_w = (w_branch_a[0].astype(BF16), w_branch_b[0].astype(BF16), w_out[0].astype(BF16),
              norm_ffn_g[0].reshape(1, D_MODEL), wr, br)
    x1_p, h2w_p, idx_p, gate_p, rank_p, cnt_p = _post(a_p, r_p, sga_p, sgb_p, xp, mod_p, *post_w,
                                                      jnp.zeros((1, LANES), F32), tmp, bpm_big)
    x1_s, h2w_s, idx_s, gate_s, rank_s, cnt_all = _post(a_s, r_s, sga_s, sgb_s, xs, mod_s, *post_w, cnt_p,
                                                        TOK_BLOCK, 1)

    n_asg = (tp + ts) * TOP_K
    counts = cnt_all[0, :N_EXPERTS].astype(I32)
    padded = (counts + MOE_ROWS - 1) // MOE_ROWS * MOE_ROWS
    pend = jnp.cumsum(padded)
    pstart = pend - padded
    def dest_rows(idx8, rank8):
        idx, dest = idx8[:TOP_K], rank8[:TOP_K]
        for e in range(N_EXPERTS):
            dest = dest + jnp.where(idx == e, pstart[e], 0)
        return dest.reshape(-1)

    dest_p = dest_rows(idx_p, rank_p)
    dest_s = dest_rows(idx_s, rank_s)
    n_blocks = -(-n_asg // MOE_ROWS) + N_EXPERTS
    blk_start = jnp.arange(n_blocks, dtype=I32) * MOE_ROWS
    blk_expert = jnp.minimum(jnp.sum((blk_start[:, None] >= pend[None, :]).astype(I32), axis=1), N_EXPERTS - 1)
    blk_first = jnp.concatenate([jnp.ones((1,), I32), (blk_expert[1:] != blk_expert[:-1]).astype(I32)])
    n_used = (pend[-1] // MOE_ROWS).astype(I32).reshape(1)
    x_sorted = jnp.zeros((n_blocks * MOE_ROWS, D_MODEL // 2), jnp.uint32)
    x_sorted = _dispatch(dest_p, h2w_p, x_sorted)
    x_sorted = _dispatch(dest_s, h2w_s, x_sorted)
    rows_out = _moe(blk_expert, blk_first, n_used, x_sorted, w_gate_up[0], b_gate_up[0], w_down[0], b_down[0])

    g_final = norm_final_g.reshape(1, D_MODEL)
    y_p = _final(dest_p, x1_p, mod_p, gate_p, rows_out, g_final, bpm_p)
    y_s = _final(dest_s, x1_s, mod_s, gate_s, rows_out, g_final, 1)

    kv_shape = lambda b, s: (1, b, s, ATT_KV_HEADS, HEAD_DIM)
    st_shape = lambda b: (1, b, RET_HEADS, RET_DK, RET_DV)
    return (y_p.reshape(batch, seq, D_MODEL), y_s.reshape(db, dt, D_MODEL),
            k_p.reshape(kv_shape(batch, seq)), v_p.reshape(kv_shape(batch, seq)),
            ikw_p[:, :IDX_DIM].reshape(1, batch, seq, IDX_DIM), st_p.reshape(st_shape(batch)),
            k_s.reshape(kv_shape(db, dt)), v_s.reshape(kv_shape(db, dt)),
            ikw_s[:, :IDX_DIM].reshape(1, db, dt, IDX_DIM), st_s.reshape(st_shape(db)))
```

```python
import functools

import jax
import jax.numpy as jnp
import numpy as np
from jax import lax
from jax.experimental import pallas as pl
from jax.experimental.pallas import tpu as pltpu

F32 = jnp.float32
BF16 = jnp.bfloat16
I32 = jnp.int32

D_MODEL = 1024
PAST_LEN = 8192
PAGE_SIZE = 128
ATT_HEADS = 8
ATT_KV_HEADS = 2
HEAD_DIM = 64
ROPE_DIM = HEAD_DIM // 4
ROPE_THETA = 500000.0
IDX_HEADS = 8
IDX_DIM = 64
IDX_ROPE_DIM = IDX_DIM // 4
TOPK_MAX = 256
RET_HEADS = 8
RET_DK = 64
RET_DV = 128
RET_THETA = 10000.0
RET_CHUNK = 128
N_EXPERTS = 32
TOP_K = 4
D_FF = D_MODEL
SWIGLU_LIMIT = 7.0
SWIGLU_ALPHA = 1.702
NORM_EPS = 1e-6
GN_EPS = 1e-5
ATT_OUT = ATT_HEADS * HEAD_DIM
RET_OUT = RET_HEADS * RET_DV
IN_SPLITS = (ATT_HEADS * HEAD_DIM, ATT_KV_HEADS * HEAD_DIM, ATT_KV_HEADS * HEAD_DIM,
             IDX_HEADS * IDX_DIM, IDX_DIM, IDX_HEADS,
             RET_HEADS * RET_DK, RET_HEADS * RET_DK, RET_OUT, RET_OUT, D_MODEL, D_MODEL)

LANES = 128
MASK_NEG = -1e30
FLT_MAX = 3.4028234663852886e38
SELECT_UNROLL = 4
CAUSAL_VARIANTS = 8
IDX_KEY_CHUNK = 256
VMEM_LIMIT = 56 * 1024 * 1024

TOK_BLOCK = 256
PROMPT_TOK_BLOCK = 512
Q_BLOCK = 128
IDX_Q_ROWS = 128
MOE_ROWS = 512
SEG_ALIGN = 8
SEG_BITS = 6
SEG_ROWS = TOP_K * TOK_BLOCK + N_EXPERTS * SEG_ALIGN

_W_GROUPS = (("q", 1024), ("iq", 1024), ("kvi", 384), ("rq", 512), ("rk", 512),
             ("rv", 1024), ("rg", 1024), ("ga", 1024), ("gb", 1024))
PROJ_COLS = 512
_W_OFF = {}
_off = 0
for _n, _w in _W_GROUPS:
    _W_OFF[_n] = (_off, _w)
    _off += _w
W_COLS = _off


def _cparams(*sem):
    return pltpu.CompilerParams(dimension_semantics=sem, vmem_limit_bytes=VMEM_LIMIT)


def _adaln_kernel(c_ref, w_ref, b_ref, o_ref):
    c = c_ref[...]
    s = c * jax.nn.sigmoid(c)
    o_ref[...] = jnp.dot(s, w_ref[...], preferred_element_type=F32, precision=lax.Precision.HIGHEST) + b_ref[...]


def _adaln(c_all, w_ada, b_ada):
    n = c_all.shape[0]
    nb = 1536
    return pl.pallas_call(
        _adaln_kernel,
        grid=(6 * D_MODEL // nb,),
        in_specs=[pl.BlockSpec((n, D_MODEL), lambda j: (0, 0)),
                  pl.BlockSpec((D_MODEL, nb), lambda j: (0, j)),
                  pl.BlockSpec((1, nb), lambda j: (0, j))],
        out_specs=pl.BlockSpec((n, nb), lambda j: (0, j)),
        out_shape=jax.ShapeDtypeStruct((n, 6 * D_MODEL), F32),
        compiler_params=_cparams("arbitrary"),
        name="adaln",
    )(c_all, w_ada, b_ada.reshape(1, -1))


def _rope_slab(z, c, sa, sb, half):
    return z * c + pltpu.roll(z, LANES - half, 1) * sa + pltpu.roll(z, half, 1) * sb


def _inproj_kernel(x_ref, mod_ref, g_ref, w_ref, ca_ref, saa_ref, sba_ref, cr_ref, sar_ref, sbr_ref,
                   q_ref, k_ref, v_ref, kb_ref, vb_ref, iq_ref, ikw_ref, ikb_ref,
                   rq_ref, rk_ref, rv_ref, sg_ref, sga_ref, sgb_ref):
    x = x_ref[...]
    ms = jnp.mean(x * x, axis=-1, keepdims=True)
    y = x * lax.rsqrt(ms + NORM_EPS) * g_ref[...]
    h = (y * (1.0 + mod_ref[:, D_MODEL:2 * D_MODEL]) + mod_ref[:, 0:D_MODEL]).astype(BF16)

    def slabs(name):
        c0, width = _W_OFF[name]
        step = min(width, PROJ_COLS)
        for j in range(width // step):
            z = jnp.dot(h, w_ref[:, c0 + j * step:c0 + (j + 1) * step], preferred_element_type=F32)
            for s in range(step // LANES):
                yield j * (step // LANES) + s, z[:, s * LANES:(s + 1) * LANES]

    ca, saa, sba = ca_ref[...], saa_ref[...], sba_ref[...]
    cr, sar, sbr = cr_ref[...], sar_ref[...], sbr_ref[...]
    att_half, ret_half = ROPE_DIM // 2, RET_DK // 2
    lane = lax.broadcasted_iota(I32, (x.shape[0], LANES), 1)
    sl = lambda s: slice(s * LANES, (s + 1) * LANES)

    for s, z in slabs("q"):
        q_ref[:, sl(s)] = (_rope_slab(z, ca, saa, sba, att_half) * 0.125).astype(BF16)
    for s, z in slabs("iq"):
        iq_ref[:, sl(s)] = (_rope_slab(z, ca, saa, sba, att_half) * 0.125).astype(BF16)
    (_, zk), (_, zv), (_, zi) = slabs("kvi")
    kk = _rope_slab(zk, ca, saa, sba, att_half)
    k_ref[...] = kk
    kb_ref[...] = kk.astype(BF16)
    v_ref[...] = zv
    vb_ref[...] = zv.astype(BF16)
    zr = _rope_slab(zi, ca, saa, sba, att_half)
    ikw_ref[...] = jnp.where(lane < IDX_DIM, zr, zi)
    ikb_ref[...] = jnp.where(lane < IDX_DIM, zr, 0.0).astype(BF16)
    for s, z in slabs("rq"):
        rq_ref[:, sl(s)] = _rope_slab(z, cr, sar, sbr, ret_half).astype(BF16)
    for s, z in slabs("rk"):
        rk_ref[:, sl(s)] = (_rope_slab(z, cr, sar, sbr, ret_half) * 0.125).astype(BF16)
    for s, z in slabs("rv"):
        rv_ref[:, sl(s)] = z.astype(BF16)
    for s, z in slabs("rg"):
        sg_ref[:, sl(s)] = (z * jax.nn.sigmoid(z)).astype(BF16)
    for s, z in slabs("ga"):
        sga_ref[:, sl(s)] = jax.nn.sigmoid(z).astype(BF16)
    for s, z in slabs("gb"):
        sgb_ref[:, sl(s)] = jax.nn.sigmoid(z).astype(BF16)


def _inproj(x, mod3, g, w_packed, tabs_att, tabs_ret, tm, blocks_per_mod, tab_blocks):
    t = x.shape[0]
    nblk = t // tm
    mod_rows = mod3.shape[1]
    tab_spec = pl.BlockSpec((tm, LANES), lambda i: (i % tab_blocks, 0))
    row = lambda n: pl.BlockSpec((tm, n), lambda i: (i, 0))
    out_defs = [(1024, BF16), (128, F32), (128, F32), (128, BF16), (128, BF16), (1024, BF16), (128, F32),
                (128, BF16), (512, BF16), (512, BF16), (1024, BF16), (1024, BF16), (1024, BF16), (1024, BF16)]
    return pl.pallas_call(
        _inproj_kernel,
        grid=(nblk,),
        in_specs=[row(D_MODEL),
                  pl.BlockSpec((None, mod_rows, 6 * D_MODEL), lambda i: (i // blocks_per_mod, 0, 0)),
                  pl.BlockSpec((1, D_MODEL), lambda i: (0, 0)),
                  pl.BlockSpec((D_MODEL, W_COLS), lambda i: (0, 0), pipeline_mode=pl.Buffered(1))]
                 + [tab_spec] * 6,
        out_specs=[row(n) for n, _ in out_defs],
        out_shape=[jax.ShapeDtypeStruct((t, n), d) for n, d in out_defs],
        compiler_params=_cparams("parallel"),
        name="inproj",
    )(x, mod3, g, w_packed, *tabs_att, *tabs_ret)


def _count(score_ref, n, pred):
    acc = jnp.zeros((score_ref.shape[0], LANES), F32)
    for c in range(n // LANES):
        acc = acc + jnp.where(pred(score_ref[:, c * LANES:(c + 1) * LANES]), 1.0, 0.0)
    return jnp.sum(acc, axis=1, keepdims=True)


def _kth_largest(score_ref, n, k):
    sc = score_ref[:, :n]
    finite = sc > -jnp.inf
    n_fin = jnp.sum(jnp.where(finite, 1.0, 0.0), axis=1, keepdims=True)
    n_pos = jnp.sum(jnp.where(sc > 0.0, 1.0, 0.0), axis=1, keepdims=True)
    n_nonneg = jnp.sum(jnp.where(sc >= 0.0, 1.0, 0.0), axis=1, keepdims=True)
    mx = jnp.max(sc, axis=1, keepdims=True)
    mn = jnp.min(jnp.where(finite, sc, jnp.inf), axis=1, keepdims=True)
    small = n_fin <= k
    positive = n_pos >= k
    at_zero = jnp.logical_and(jnp.logical_not(positive), n_nonneg >= k)
    lo = jnp.where(positive, 0.0, mn)
    hi = jnp.where(positive, mx + (jnp.abs(mx) * 2.0 ** -20 + 2.0 ** -100), 0.0)
    lo = jnp.where(at_zero, 0.0, lo)
    done = jnp.where(jnp.logical_or(small, at_zero), 1.0, 0.0)

    def cond(state):
        return jnp.min(state[2]) < 0.5

    def body(state):
        lo, hi, done = state
        for _ in range(SELECT_UNROLL):
            mid = 0.5 * lo + 0.5 * hi
            cnt = _count(score_ref, n, lambda s: s >= mid)
            stuck = jnp.logical_or(mid <= lo, mid >= hi)
            live = jnp.logical_and(done < 0.5, jnp.logical_not(stuck))
            ge = cnt >= k
            lo = jnp.where(jnp.logical_and(live, ge), mid, lo)
            hi = jnp.where(jnp.logical_and(live, jnp.logical_not(ge)), mid, hi)
            done = jnp.where(jnp.logical_or(stuck, cnt == k), 1.0, done)
        return lo, hi, done

    lo, _, _ = lax.while_loop(cond, body, (lo, hi, done))
    return jnp.where(small, -FLT_MAX, lo)


def _topk_bias(score_ref, bias_ref, tri_ref, n, k):
    rows = score_ref.shape[0]
    thr = _kth_largest(score_ref, n, k)
    need = k - _count(score_ref, n, lambda s: s > thr)
    n_eq = _count(score_ref, n, lambda s: s == thr)
    has_ties = jnp.max(jnp.where(need < n_eq, 1.0, 0.0)) > 0.5

    @pl.when(jnp.logical_not(has_ties))
    def _():
        for c in range(n // LANES):
            sl = slice(c * LANES, (c + 1) * LANES)
            bias_ref[:, sl] = jnp.where(score_ref[:, sl] >= thr, 0.0, MASK_NEG)

    @pl.when(has_ties)
    def _():
        run = jnp.zeros((rows, 1), F32)
        for c in range(n // LANES):
            sl = slice(c * LANES, (c + 1) * LANES)
            sc = score_ref[:, sl]
            eq = sc == thr
            eqf = jnp.where(eq, 1.0, 0.0)
            before = jnp.dot(eqf.astype(BF16), tri_ref[...], preferred_element_type=F32) + run
            take = jnp.logical_or(sc > thr, jnp.logical_and(eq, before < need))
            bias_ref[:, sl] = jnp.where(take, 0.0, MASK_NEG)
            run = run + jnp.sum(eqf, axis=1, keepdims=True)


def _dsa_prompt_kernel(q_ref, iq_ref, ikw_ref, kb_ref, vb_ref, ikb_ref, tri_ref, o_ref, score_ref, bias_ref):
    qb, s_len = score_ref.shape
    i = pl.program_id(1)
    nqb = s_len // qb
    per_variant = nqb // CAUSAL_VARIANTS
    for v in range(CAUSAL_VARIANTS):
        pl.when(i // per_variant == v)(
            functools.partial(_dsa_prompt_body, q_ref, iq_ref, ikw_ref, kb_ref, vb_ref, ikb_ref, tri_ref, o_ref,
                              score_ref, bias_ref, (v + 1) * per_variant * qb, min(TOPK_MAX, s_len // 4)))


def _dsa_prompt_body(q_ref, iq_ref, ikw_ref, kb_ref, vb_ref, ikb_ref, tri_ref, o_ref, score_ref, bias_ref,
                     n_keys, topk):
    qb = score_ref.shape[0]
    i = pl.program_id(1)
    w = ikw_ref[:, IDX_DIM:IDX_DIM + IDX_HEADS] * (IDX_HEADS ** -0.5)
    nt = (((1,), (1,)), ((), ()))
    kc = IDX_KEY_CHUNK
    qr = IDX_Q_ROWS
    for c in range(n_keys // kc):
        ikc = ikb_ref[c * kc:(c + 1) * kc, :]
        kpos = c * kc + lax.broadcasted_iota(I32, (qr, kc), 1)
        for r0 in range(0, qb, qr):
            acc = jnp.zeros((qr, kc), F32)
            for h in range(IDX_HEADS):
                d = lax.dot_general(iq_ref[r0:r0 + qr, h * LANES:(h + 1) * LANES], ikc, nt,
                                    preferred_element_type=F32)
                acc = acc + jnp.maximum(d, 0.0) * w[r0:r0 + qr, h:h + 1]
            qpos = i * qb + r0 + lax.broadcasted_iota(I32, (qr, kc), 0)
            score_ref[r0:r0 + qr, c * kc:(c + 1) * kc] = jnp.where(kpos <= qpos, acc, -jnp.inf)

    _topk_bias(score_ref, bias_ref, tri_ref, n_keys, topk)

    kb = kb_ref[0:n_keys, :]
    vb = vb_ref[0:n_keys, :]
    bias = bias_ref[:, 0:n_keys]
    lane = lax.broadcasted_iota(I32, (qb, LANES), 1)
    heads = []
    for h in range(ATT_HEADS):
        s = lax.dot_general(q_ref[:, h * LANES:(h + 1) * LANES], kb, nt, preferred_element_type=F32) + bias
        m = jnp.max(s, axis=1, keepdims=True)
        p = jnp.exp(s - m)
        l = jnp.sum(p, axis=1, keepdims=True)
        heads.append(jnp.dot(p.astype(BF16), vb, preferred_element_type=F32) / l)
    group = ATT_HEADS // ATT_KV_HEADS
    for pp in range(ATT_HEADS // 2):
        a, b = heads[2 * pp], heads[2 * pp + 1]
        if (2 * pp) // group == 0:
            slab = jnp.where(lane < HEAD_DIM, a, pltpu.roll(b, HEAD_DIM, 1))
        else:
            slab = jnp.where(lane < HEAD_DIM, pltpu.roll(a, HEAD_DIM, 1), b)
        o_ref[:, pp * LANES:(pp + 1) * LANES] = slab.astype(BF16)


def _dsa_prompt(q, iq, ikw, kb, vb, ikb, tri, batch, seq):
    nqb = seq // Q_BLOCK
    qrow = lambda n: pl.BlockSpec((Q_BLOCK, n), lambda b, i: (b * nqb + i, 0))
    keys = pl.BlockSpec((seq, LANES), lambda b, i: (b, 0))
    return pl.pallas_call(
        _dsa_prompt_kernel,
        grid=(batch, nqb),
        in_specs=[qrow(1024), qrow(1024), qrow(LANES), keys, keys, keys,
                  pl.BlockSpec((LANES, LANES), lambda b, i: (0, 0))],
        out_specs=qrow(ATT_OUT),
        out_shape=jax.ShapeDtypeStruct((batch * seq, ATT_OUT), BF16),
        scratch_shapes=[pltpu.VMEM((Q_BLOCK, seq), F32), pltpu.VMEM((Q_BLOCK, seq), F32)],
        compiler_params=_cparams("parallel", "arbitrary"),
        name="dsa_prompt",
    )(q, iq, ikw, kb, vb, ikb, tri)


def _dsa_sample_kernel(pt_ref, qs_ref, iqs_ref, ws_ref, knew_ref, vnew_ref, iknew_ref, cik_hbm, ck_hbm, cv_hbm,
                       tri_ref, o_ref, ikbuf, kbuf, vbuf, ikt, kt, vt, sems, key_ref, bias_ref):
    db = pl.program_id(0)
    n_pages = ikbuf.shape[1]
    t = key_ref.shape[0]
    n_past = n_pages * PAGE_SIZE
    last = pl.num_programs(0) - 1

    def fetch(src, dst_of_page, sem, req):
        def body(p, carry):
            pltpu.make_async_copy(src.at[pt_ref[req, p]], dst_of_page(p), sem).start()
            return carry
        lax.fori_loop(0, n_pages, body, 0)

    def wait_all(src, dst, sem):
        pltpu.make_async_copy(src.at[pl.ds(0, n_pages)], dst, sem).wait()

    fetch_ik = lambda req, slot: fetch(cik_hbm, lambda p: ikbuf.at[slot, p], sems.at[slot], req)
    fetch_k = lambda req: fetch(ck_hbm, lambda p: kbuf.at[p], sems.at[2], req)
    fetch_v = lambda req: fetch(cv_hbm, lambda p: vbuf.at[p], sems.at[3], req)

    @pl.when(db == 0)
    def _():
        fetch_ik(0, 0)
        fetch_k(0)
        fetch_v(0)

    @pl.when(db < last)
    def _():
        fetch_ik(db + 1, (db + 1) % 2)

    slot = db % 2
    wait_all(cik_hbm, ikbuf.at[slot], sems.at[slot])

    nt = (((1,), (1,)), ((), ()))
    page = lambda p: slice(p * PAGE_SIZE, (p + 1) * PAGE_SIZE)
    for p in range(n_pages):
        ikt[:, page(p)] = ikbuf[slot, p].astype(BF16)
    iqs = iqs_ref[...]
    wcol = ws_ref[...]
    d_past = jnp.maximum(jnp.dot(iqs, ikt[...], preferred_element_type=F32), 0.0) * wcol
    d_new = jnp.maximum(jnp.dot(iqs, iknew_ref[...], preferred_element_type=F32), 0.0) * wcol
    s_past = d_past[0:t]
    s_new = d_new[0:t]
    for h in range(1, IDX_HEADS):
        s_past = s_past + d_past[h * t:(h + 1) * t]
        s_new = s_new + d_new[h * t:(h + 1) * t]
    row = lax.broadcasted_iota(I32, (t, LANES), 0)
    lane = lax.broadcasted_iota(I32, (t, LANES), 1)
    new_ok = lane <= row
    key_ref[:, 0:n_past] = s_past
    key_ref[:, n_past:n_past + LANES] = jnp.where(new_ok, s_new, -jnp.inf)
    _topk_bias(key_ref, bias_ref, tri_ref, n_past + LANES, min(TOPK_MAX, (n_past + t) // 4))

    def stage(buf, dst):
        for p in range(n_pages):
            for j in range(ATT_KV_HEADS):
                dst[j, :, page(p)] = buf[p, j].astype(BF16)

    wait_all(ck_hbm, kbuf, sems.at[2])
    stage(kbuf, kt)

    @pl.when(db < last)
    def _():
        fetch_k(db + 1)

    wait_all(cv_hbm, vbuf, sems.at[3])
    stage(vbuf, vt)

    @pl.when(db < last)
    def _():
        fetch_v(db + 1)

    rows_per_kv = qs_ref.shape[0] // ATT_KV_HEADS
    bias = jnp.concatenate([bias_ref[...]] * (rows_per_kv // t), axis=0)
    for j in range(ATT_KV_HEADS):
        qj = qs_ref[j * rows_per_kv:(j + 1) * rows_per_kv, :]
        sp = jnp.dot(qj, kt[j], preferred_element_type=F32) + bias[:, 0:n_past]
        sn = jnp.dot(qj, knew_ref[j], preferred_element_type=F32) + bias[:, n_past:n_past + LANES]
        m = jnp.maximum(jnp.max(sp, axis=1, keepdims=True), jnp.max(sn, axis=1, keepdims=True))
        pp = jnp.exp(sp - m)
        pn = jnp.exp(sn - m)
        l = jnp.sum(pp, axis=1, keepdims=True) + jnp.sum(pn, axis=1, keepdims=True)
        o = (lax.dot_general(pp.astype(BF16), vt[j], nt, preferred_element_type=F32)
             + lax.dot_general(pn.astype(BF16), vnew_ref[j], nt, preferred_element_type=F32))
        o_ref[j * rows_per_kv:(j + 1) * rows_per_kv, :] = o / l


def _dsa_sample(page_table, qs, iqs, ws, knew, vnew, iknew, cache_ik, cache_k, cache_v, tri, t):
    db, n_pages = page_table.shape
    rows = qs.shape[1]
    per_db = lambda r, n: pl.BlockSpec((None, r, n), lambda b, pt: (b, 0, 0))
    any_spec = pl.BlockSpec(memory_space=pl.ANY)
    n_keys = n_pages * PAGE_SIZE + LANES
    grid_spec = pltpu.PrefetchScalarGridSpec(
        num_scalar_prefetch=1,
        grid=(db,),
        in_specs=[per_db(rows, HEAD_DIM), per_db(rows, IDX_DIM), per_db(rows, 1),
                  pl.BlockSpec((None, ATT_KV_HEADS, HEAD_DIM, LANES), lambda b, pt: (b, 0, 0, 0)),
                  pl.BlockSpec((None, ATT_KV_HEADS, HEAD_DIM, LANES), lambda b, pt: (b, 0, 0, 0)),
                  per_db(IDX_DIM, LANES), any_spec, any_spec, any_spec,
                  pl.BlockSpec((LANES, LANES), lambda b, pt: (0, 0))],
        out_specs=per_db(rows, HEAD_DIM),
        scratch_shapes=[pltpu.VMEM((2, n_pages, IDX_DIM, PAGE_SIZE), F32),
                        pltpu.VMEM((n_pages, ATT_KV_HEADS, HEAD_DIM, PAGE_SIZE), F32),
                        pltpu.VMEM((n_pages, ATT_KV_HEADS, HEAD_DIM, PAGE_SIZE), F32),
                        pltpu.VMEM((IDX_DIM, n_pages * PAGE_SIZE), BF16),
                        pltpu.VMEM((ATT_KV_HEADS, HEAD_DIM, n_pages * PAGE_SIZE), BF16),
                        pltpu.VMEM((ATT_KV_HEADS, HEAD_DIM, n_pages * PAGE_SIZE), BF16),
                        pltpu.SemaphoreType.DMA((4,)),
                        pltpu.VMEM((t, n_keys), F32),
                        pltpu.VMEM((t, n_keys), F32)])
    return pl.pallas_call(
        _dsa_sample_kernel,
        grid_spec=grid_spec,
        out_shape=jax.ShapeDtypeStruct((db, rows, HEAD_DIM), F32),
        compiler_params=_cparams("arbitrary"),
        name="dsa_sample",
    )(page_table, qs, iqs, ws, knew, vnew, iknew, cache_ik, cache_k, cache_v, tri)


def _retention_kernel(has_init, rq_ref, rk_ref, rv_ref, sg_ref, decay_ref, qdec_ref, kdec_ref, gst_ref, *rest):
    if has_init:
        init_ref, o_ref, st_ref, state = rest
    else:
        o_ref, st_ref, state = rest
    c = pl.program_id(1)

    @pl.when(c == 0)
    def _():
        if has_init:
            state[...] = init_ref[...]
        else:
            state[...] = jnp.zeros_like(state)

    nt = (((1,), (1,)), ((), ()))
    tn = (((0,), (0,)), ((), ()))
    rows = rq_ref.shape[0]
    lane = lax.broadcasted_iota(I32, (rows, LANES), 1)
    for p in range(RET_HEADS // 2):
        sl = slice(p * LANES, (p + 1) * LANES)
        qp = rq_ref[:, sl]
        kp = rk_ref[:, sl]
        qd = (qp.astype(F32) * qdec_ref[:, sl]).astype(BF16)
        kd = (kp.astype(F32) * kdec_ref[:, sl]).astype(BF16)
        s_old = state[p]
        s_old_b = s_old.astype(BF16)
        s_new = s_old * gst_ref[p]
        for e in range(2):
            h = 2 * p + e
            hs = slice(h * LANES, (h + 1) * LANES)
            mine = jnp.where((lane >= e * RET_DK) & (lane < (e + 1) * RET_DK), 1.0, 0.0).astype(BF16)
            sc = lax.dot_general(qp * mine, kp, nt, preferred_element_type=F32) * decay_ref[h]
            vh = rv_ref[:, hs]
            o = (jnp.dot(sc.astype(BF16), vh, preferred_element_type=F32)
                 + jnp.dot(qd * mine, s_old_b, preferred_element_type=F32))
            s_new = s_new + lax.dot_general(kd * mine, vh, tn, preferred_element_type=F32)
            mu = jnp.mean(o, axis=-1, keepdims=True)
            var = jnp.mean(jnp.square(o - mu), axis=-1, keepdims=True)
            on = (o - mu) * lax.rsqrt(var + GN_EPS)
            o_ref[:, hs] = (on * sg_ref[:, hs].astype(F32)).astype(BF16)
        state[p] = s_new

    @pl.when(c == pl.num_programs(1) - 1)
    def _():
        st_ref[...] = state[...]


def _retention(rq, rk, rv, sg, tables, init, batch, n_chunks):
    decay, qdec, kdec, gst = tables
    cr = RET_CHUNK
    rowspec = lambda n: pl.BlockSpec((cr, n), lambda b, c: (b * n_chunks + c, 0))
    const = lambda shape: pl.BlockSpec(shape, lambda b, c: (0,) * len(shape))
    st_spec = pl.BlockSpec((None, RET_HEADS // 2, LANES, LANES), lambda b, c: (b, 0, 0, 0))
    in_specs = [rowspec(512), rowspec(512), rowspec(RET_OUT), rowspec(RET_OUT),
                const(decay.shape), const(qdec.shape), const(kdec.shape), const(gst.shape)]
    args = [rq, rk, rv, sg, decay, qdec, kdec, gst]
    if init is not None:
        in_specs.append(st_spec)
        args.append(init)
    return pl.pallas_call(
        functools.partial(_retention_kernel, init is not None),
        grid=(batch, n_chunks),
        in_specs=in_specs,
        out_specs=[rowspec(RET_OUT), st_spec],
        out_shape=[jax.ShapeDtypeStruct((batch * n_chunks * cr, RET_OUT), BF16),
                   jax.ShapeDtypeStruct((batch, RET_HEADS // 2, LANES, LANES), F32)],
        scratch_shapes=[pltpu.VMEM((RET_HEADS // 2, LANES, LANES), F32)],
        compiler_params=_cparams("parallel", "arbitrary"),
        name="retention",
    )(*args)


def _retention_tables(c_eff):
    lg = jnp.log(1.0 - 2.0 ** (-5.0 - jnp.arange(RET_HEADS, dtype=F32)))
    i = jnp.arange(RET_CHUNK, dtype=F32)
    diff = i[:, None] - i[None, :]
    decay = jnp.where(diff >= 0, jnp.exp(jnp.maximum(diff, 0.0)[None] * lg[:, None, None]), 0.0)
    q_decay = jnp.exp((i + 1.0)[:, None] * lg[None, :])
    k_decay = jnp.exp((c_eff - 1.0 - i)[:, None] * lg[None, :])
    qdec = jnp.repeat(q_decay, RET_DK, axis=1)
    kdec = jnp.repeat(k_decay, RET_DK, axis=1)
    g_state = jnp.exp(c_eff * lg)
    gst = jnp.broadcast_to(jnp.repeat(g_state, RET_DK).reshape(RET_HEADS // 2, LANES, 1),
                           (RET_HEADS // 2, LANES, LANES))
    return decay, qdec, kdec, gst


def _pack_bf16_pairs(lo, hi):
    return pltpu.pack_elementwise([lo, hi], packed_dtype=BF16)


def _unpack_bf16_pairs(words):
    return tuple(pltpu.unpack_elementwise(words, index=j, packed_dtype=BF16, unpacked_dtype=F32).astype(BF16)
                 for j in range(2))


def _post_kernel(a_ref, r_ref, sga_ref, sgb_ref, x_ref, mod_ref, wpa_ref, wpb_ref, wo_ref, g_ref, wr_ref, br_ref,
                 ltri_ref, cnt0_ref, x1_ref, h2w_ref, idx_ref, gate_ref, rank_ref, cnt_ref, cnt):
    @pl.when(pl.program_id(0) == 0)
    def _():
        cnt[...] = cnt0_ref[...]

    pa = jnp.dot(a_ref[...], wpa_ref[...], preferred_element_type=F32)
    pb = jnp.dot(r_ref[...], wpb_ref[...], preferred_element_type=F32)
    merged = sga_ref[...].astype(F32) * pa + sgb_ref[...].astype(F32) * pb
    gt1 = mod_ref[:, 2 * D_MODEL:3 * D_MODEL]
    x1 = x_ref[...] + gt1 * jnp.dot(merged.astype(BF16), wo_ref[...], preferred_element_type=F32)
    x1_ref[...] = x1
    ms = jnp.mean(x1 * x1, axis=-1, keepdims=True)
    y = x1 * lax.rsqrt(ms + NORM_EPS) * g_ref[...]
    h2 = y * (1.0 + mod_ref[:, 4 * D_MODEL:5 * D_MODEL]) + mod_ref[:, 3 * D_MODEL:4 * D_MODEL]
    half = D_MODEL // 2
    h2w_ref[...] = _pack_bf16_pairs(h2[:, :half], h2[:, half:])
    logits = jnp.dot(h2, wr_ref[...], preferred_element_type=F32, precision=lax.Precision.HIGHEST) + br_ref[...]
    lane = lax.broadcasted_iota(I32, logits.shape, 1).astype(F32)
    idx_out = jnp.zeros(logits.shape, F32)
    val_out = jnp.zeros(logits.shape, F32)
    chosen = []
    top = None
    for j in range(TOP_K):
        m = jnp.max(logits, axis=1, keepdims=True)
        am = jnp.min(jnp.where(logits == m, lane, float(LANES)), axis=1, keepdims=True)
        if j == 0:
            top = m
        idx_out = jnp.where(lane == j, am, idx_out)
        val_out = jnp.where(lane == j, jnp.exp(m - top), val_out)
        chosen.append(lane == am)
        logits = jnp.where(chosen[-1], -jnp.inf, logits)
    idx_ref[...] = jnp.transpose(idx_out)[0:8, :].astype(I32)
    gate_ref[...] = jnp.transpose(val_out / jnp.sum(val_out, axis=1, keepdims=True))[0:8, :]
    onehot = jnp.where(chosen[0] | chosen[1] | chosen[2] | chosen[3], 1.0, 0.0)
    before = jnp.dot(ltri_ref[...], onehot.astype(BF16), preferred_element_type=F32) + cnt[...]
    rank_out = jnp.zeros(logits.shape, F32)
    for j in range(TOP_K):
        rj = jnp.sum(jnp.where(chosen[j], before, 0.0), axis=1, keepdims=True)
        rank_out = jnp.where(lane == j, rj, rank_out)
    rank_ref[...] = jnp.transpose(rank_out)[0:8, :].astype(I32)
    cnt[...] = cnt[...] + jnp.sum(onehot, axis=0, keepdims=True)
    cnt_ref[...] = cnt[...]


def _post(a, r, sga, sgb, x, mod3, wpa, wpb, wo, g, wr, br, cnt0, tm, blocks_per_mod):
    t = x.shape[0]
    mod_rows = mod3.shape[1]
    ar = jnp.arange(tm)
    ltri = (ar[None, :] < ar[:, None]).astype(BF16)
    row = lambda n: pl.BlockSpec((tm, n), lambda i: (i, 0))
    col8 = pl.BlockSpec((8, tm), lambda i: (0, i))
    const = lambda a_: pl.BlockSpec(a_.shape, lambda i: (0,) * a_.ndim)
    return pl.pallas_call(
        _post_kernel,
        grid=(t // tm,),
        in_specs=[row(ATT_OUT), row(RET_OUT), row(D_MODEL), row(D_MODEL), row(D_MODEL),
                  pl.BlockSpec((None, mod_rows, 6 * D_MODEL), lambda i: (i // blocks_per_mod, 0, 0)),
                  const(wpa), const(wpb), const(wo), const(g), const(wr), const(br), const(ltri), const(cnt0)],
        out_specs=[row(D_MODEL), row(D_MODEL // 2), col8, col8, col8,
                   pl.BlockSpec((1, LANES), lambda i: (0, 0))],
        out_shape=[jax.ShapeDtypeStruct((t, D_MODEL), F32), jax.ShapeDtypeStruct((t, D_MODEL // 2), jnp.uint32),
                   jax.ShapeDtypeStruct((8, t), I32), jax.ShapeDtypeStruct((8, t), F32),
                   jax.ShapeDtypeStruct((8, t), I32), jax.ShapeDtypeStruct((1, LANES), F32)],
        scratch_shapes=[pltpu.VMEM((1, LANES), F32)],
        compiler_params=_cparams("arbitrary"),
        name="post",
    )(a, r, sga, sgb, x, mod3, wpa, wpb, wo, g, wr, br, ltri, cnt0)


def _segment_copies(seg_ref, blk, make_copy):
    for e in range(N_EXPERTS + 1):
        base = (blk * (N_EXPERTS + 1) + e) * 3
        loc, glob, n = seg_ref[base], seg_ref[base + 1], seg_ref[base + 2]
        for k in range(SEG_BITS):
            size = SEG_ALIGN << k

            @pl.when((n & size) != 0)
            def _():
                done = n & (size - 1)
                make_copy(pl.multiple_of(loc + done, SEG_ALIGN), pl.multiple_of(glob + done, SEG_ALIGN),
                          size).start()


def _local_order_matrix(lpos_ref, values, rows):
    lp = lpos_ref[...]
    r = lax.broadcasted_iota(I32, (rows, lp.shape[1]), 0)
    m = jnp.zeros((rows, lp.shape[1]), F32)
    for j in range(TOP_K):
        m = jnp.where(r == lp[j:j + 1, :], values(j), m)
    return m


def _dispatch_kernel(seg_ref, lpos_ref, h2w_ref, xs_in, xs_out, buf, sems):
    del xs_in
    i = pl.program_id(0)
    slot = i % 2
    rows = buf.shape[1]

    def wait_slot(s):
        pltpu.make_async_copy(buf.at[s], xs_out.at[pl.ds(0, rows)], sems.at[s]).wait()

    @pl.when(i >= 2)
    def _():
        wait_slot(slot)

    x = jnp.concatenate(_unpack_bf16_pairs(h2w_ref[...]), axis=1)
    perm = _local_order_matrix(lpos_ref, lambda j: 1.0, rows).astype(BF16)
    xs = jnp.dot(perm, x, preferred_element_type=F32)
    half = D_MODEL // 2
    buf[slot] = _pack_bf16_pairs(xs[:, :half], xs[:, half:])
    _segment_copies(seg_ref, i, lambda loc, glob, size: pltpu.make_async_copy(
        buf.at[slot, pl.ds(loc, size)], xs_out.at[pl.ds(glob, size)], sems.at[slot]))

    @pl.when(i == pl.num_programs(0) - 1)
    def _():
        wait_slot(slot)

        @pl.when(i >= 1)
        def _():
            wait_slot(1 - slot)


def _dispatch(seg, lpos8, h2w, x_sorted):
    t = h2w.shape[0]
    tm = TOK_BLOCK
    grid_spec = pltpu.PrefetchScalarGridSpec(
        num_scalar_prefetch=1,
        grid=(t // tm,),
        in_specs=[pl.BlockSpec((8, tm), lambda i, s: (0, i)),
                  pl.BlockSpec((tm, D_MODEL // 2), lambda i, s: (i, 0)), pl.BlockSpec(memory_space=pl.ANY)],
        out_specs=pl.BlockSpec(memory_space=pl.ANY),
        scratch_shapes=[pltpu.VMEM((2, SEG_ROWS, D_MODEL // 2), jnp.uint32), pltpu.SemaphoreType.DMA((2,))])
    return pl.pallas_call(
        _dispatch_kernel,
        grid_spec=grid_spec,
        out_shape=jax.ShapeDtypeStruct(x_sorted.shape, x_sorted.dtype),
        input_output_aliases={3: 0},
        compiler_params=_cparams("arbitrary"),
        name="dispatch",
    )(seg, lpos8, h2w, x_sorted)


def _moe_kernel(be_ref, first_ref, nused_ref, x_ref, wgu_ref, bgu_ref, wd_ref, bd_ref, o_ref, wgu_b, wd_b):
    i = pl.program_id(0)

    @pl.when(first_ref[i] == 1)
    def _():
        wgu_b[...] = wgu_ref[...].astype(BF16)
        wd_b[...] = wd_ref[...].astype(BF16)

    @pl.when(i < nused_ref[0])
    def _():
        x = jnp.concatenate(_unpack_bf16_pairs(x_ref[...]), axis=1)
        gu = jnp.dot(x, wgu_b[...], preferred_element_type=F32) + bgu_ref[...]
        g = jnp.minimum(gu[:, :D_FF], SWIGLU_LIMIT)
        u = jnp.clip(gu[:, D_FF:], -SWIGLU_LIMIT, SWIGLU_LIMIT)
        act = (u + 1.0) * (g * jax.nn.sigmoid(SWIGLU_ALPHA * g))
        o_ref[...] = jnp.dot(act.astype(BF16), wd_b[...], preferred_element_type=F32) + bd_ref[...]

    @pl.when(i >= nused_ref[0])
    def _():
        o_ref[...] = jnp.zeros_like(o_ref)


def _moe(blk_expert, blk_first, n_used, x_sorted, w_gate_up, b_gate_up, w_down, b_down):
    n_rows = blk_expert.shape[0] * MOE_ROWS
    grid_spec = pltpu.PrefetchScalarGridSpec(
        num_scalar_prefetch=3,
        grid=(n_rows // MOE_ROWS,),
        in_specs=[pl.BlockSpec((MOE_ROWS, D_MODEL // 2), lambda i, be, bf, nu: (i, 0)),
                  pl.BlockSpec((None, D_MODEL, 2 * D_FF), lambda i, be, bf, nu: (be[i], 0, 0)),
                  pl.BlockSpec((None, 1, 2 * D_FF), lambda i, be, bf, nu: (be[i], 0, 0)),
                  pl.BlockSpec((None, D_FF, D_MODEL), lambda i, be, bf, nu: (be[i], 0, 0)),
                  pl.BlockSpec((None, 1, D_MODEL), lambda i, be, bf, nu: (be[i], 0, 0))],
        out_specs=pl.BlockSpec((MOE_ROWS, D_MODEL), lambda i, be, bf, nu: (i, 0)),
        scratch_shapes=[pltpu.VMEM((D_MODEL, 2 * D_FF), BF16), pltpu.VMEM((D_FF, D_MODEL), BF16)])
    return pl.pallas_call(
        _moe_kernel,
        grid_spec=grid_spec,
        out_shape=jax.ShapeDtypeStruct((n_rows, D_MODEL), F32),
        compiler_params=_cparams("arbitrary"),
        name="moe",
    )(blk_expert, blk_first, n_used, x_sorted, w_gate_up, b_gate_up.reshape(N_EXPERTS, 1, -1),
      w_down, b_down.reshape(N_EXPERTS, 1, -1))


def _final_kernel(block_offset, seg_ref, x1_ref, mod_ref, lpos_ref, gate_ref, g_ref, rows_hbm, y_ref, buf, sems):
    i = pl.program_id(0)
    n = pl.num_programs(0)
    rows = buf.shape[1]

    def issue(blk, slot):
        _segment_copies(seg_ref, blk + block_offset, lambda loc, glob, size: pltpu.make_async_copy(
            rows_hbm.at[pl.ds(glob, size)], buf.at[slot, pl.ds(loc, size)], sems.at[slot]))

    @pl.when(i == 0)
    def _():
        issue(0, 0)

    @pl.when(i + 1 < n)
    def _():
        issue(i + 1, (i + 1) % 2)

    slot = i % 2
    pltpu.make_async_copy(rows_hbm.at[pl.ds(0, rows)], buf.at[slot], sems.at[slot]).wait()
    gate = gate_ref[...]
    gmat = _local_order_matrix(lpos_ref, lambda j: gate[j:j + 1, :], rows).astype(BF16)
    moe = lax.dot_general(gmat, buf[slot].astype(BF16), (((0,), (0,)), ((), ())), preferred_element_type=F32)
    x2 = x1_ref[...] + mod_ref[:, 5 * D_MODEL:6 * D_MODEL] * moe
    ms = jnp.mean(x2 * x2, axis=-1, keepdims=True)
    y_ref[...] = x2 * lax.rsqrt(ms + NORM_EPS) * g_ref[...]


def _final(seg, x1, mod3, lpos8, gate8, rows_out, g, blocks_per_mod, block_offset):
    t = x1.shape[0]
    tm = TOK_BLOCK
    mod_rows = mod3.shape[1]
    row = lambda n: pl.BlockSpec((tm, n), lambda i, s: (i, 0))
    col8 = pl.BlockSpec((8, tm), lambda i, s: (0, i + block_offset))
    grid_spec = pltpu.PrefetchScalarGridSpec(
        num_scalar_prefetch=1,
        grid=(t // tm,),
        in_specs=[row(D_MODEL),
                  pl.BlockSpec((None, mod_rows, 6 * D_MODEL), lambda i, s: (i // blocks_per_mod, 0, 0)),
                  col8, col8,
                  pl.BlockSpec((1, D_MODEL), lambda i, s: (0, 0)),
                  pl.BlockSpec(memory_space=pl.ANY)],
        out_specs=row(D_MODEL),
        scratch_shapes=[pltpu.VMEM((2, SEG_ROWS, D_MODEL), F32), pltpu.SemaphoreType.DMA((2,))])
    return pl.pallas_call(
        functools.partial(_final_kernel, block_offset),
        grid_spec=grid_spec,
        out_shape=jax.ShapeDtypeStruct((t, D_MODEL), F32),
        compiler_params=_cparams("arbitrary"),
        name="final",
    )(seg, x1, mod3, lpos8, gate8, g, rows_out)


def _rope_tables(pos, rot_dim, theta, head_dim):
    half = rot_dim // 2
    inv = theta ** (-jnp.arange(half, dtype=F32) * (2.0 / rot_dim))
    ang = pos.astype(F32)[:, None] * inv[None, :]
    cos, sin = jnp.cos(ang), jnp.sin(ang)
    n = pos.shape[0]
    rest = head_dim - rot_dim
    zh = jnp.zeros((n, half), F32)
    c = jnp.concatenate([cos, cos, jnp.ones((n, rest), F32)], axis=1)
    sa = jnp.concatenate([-sin, zh, jnp.zeros((n, rest), F32)], axis=1)
    sb = jnp.concatenate([zh, sin, jnp.zeros((n, rest), F32)], axis=1)
    rep = LANES // head_dim
    return tuple(jnp.tile(a, (1, rep)) for a in (c, sa, sb))


def _pack_w_in(w_in):
    offs = np.cumsum((0,) + IN_SPLITS)
    part = lambda j: w_in[:, offs[j]:offs[j + 1]]
    zero = lambda n: jnp.zeros((D_MODEL, n), w_in.dtype)
    group = ATT_HEADS // ATT_KV_HEADS
    cols = []
    wq = part(0)
    for h in range(ATT_HEADS):
        wh = wq[:, h * HEAD_DIM:(h + 1) * HEAD_DIM]
        cols += [wh, zero(HEAD_DIM)] if h // group == 0 else [zero(HEAD_DIM), wh]
    wiq = part(3)
    for h in range(IDX_HEADS):
        cols += [wiq[:, h * IDX_DIM:(h + 1) * IDX_DIM], zero(LANES - IDX_DIM)]
    cols += [part(1), part(2), part(4), part(5), zero(LANES - IDX_DIM - IDX_HEADS)]
    cols += [part(j) for j in range(6, 12)]
    return jnp.concatenate(cols, axis=1).astype(BF16)


def _heads_major(a, db, t, width):
    heads = a.shape[1] // width
    return a.reshape(db, t, heads, width).transpose(0, 2, 1, 3).reshape(db, heads * t, width)


def _pad_rows(a, db, t, rows):
    return jnp.pad(a.reshape(db, t, -1), ((0, 0), (0, rows - t), (0, 0)))


def kernel(x_prompt, x_sample, cache_k, cache_v, cache_ik, state_ret, page_table, c_prompt, c_sample, norm_mix_g, norm_ffn_g, norm_final_g, w_ada, b_ada, w_in, w_branch_a, w_branch_b, w_out, w_router, b_router, w_gate_up, b_gate_up, w_down, b_down):
    batch, seq, _ = x_prompt.shape
    db, dt, _ = x_sample.shape
    assert w_in.shape[0] == 1, "one layer"
    tp, ts = batch * seq, db * dt
    xp = x_prompt.reshape(tp, D_MODEL)
    xs = x_sample.reshape(ts, D_MODEL)

    mod = _adaln(jnp.concatenate([c_prompt, c_sample], axis=0), w_ada[0], b_ada[0])
    mod_p = mod[:batch].reshape(batch, 1, 6 * D_MODEL)
    mod_s = jnp.repeat(mod[batch:], dt, axis=0).reshape(ts // TOK_BLOCK, TOK_BLOCK, 6 * D_MODEL)
    bpm_p = seq // TOK_BLOCK
    tmp = PROMPT_TOK_BLOCK
    bpm_big = seq // tmp

    w_packed = _pack_w_in(w_in[0])
    pos_p = jnp.arange(seq)
    pos_s = PAST_LEN + (jnp.arange(TOK_BLOCK) % dt)
    g_mix = norm_mix_g[0].reshape(1, D_MODEL)
    outs_p = _inproj(xp, mod_p, g_mix, w_packed, _rope_tables(pos_p, ROPE_DIM, ROPE_THETA, HEAD_DIM),
                     _rope_tables(pos_p, RET_DK, RET_THETA, RET_DK), TOK_BLOCK, bpm_p, bpm_p)
    outs_s = _inproj(xs, mod_s, g_mix, w_packed, _rope_tables(pos_s, ROPE_DIM, ROPE_THETA, HEAD_DIM),
                     _rope_tables(pos_s, RET_DK, RET_THETA, RET_DK), TOK_BLOCK, 1, 1)
    (q_p, k_p, v_p, kb_p, vb_p, iq_p, ikw_p, ikb_p, rq_p, rk_p, rv_p, sg_p, sga_p, sgb_p) = outs_p
    (q_s, k_s, v_s, kb_s, vb_s, iq_s, ikw_s, ikb_s, rq_s, rk_s, rv_s, sg_s, sga_s, sgb_s) = outs_s

    tri = (jnp.arange(LANES)[:, None] < jnp.arange(LANES)[None, :]).astype(BF16)

    a_p = _dsa_prompt(q_p, iq_p, ikw_p, kb_p, vb_p, ikb_p, tri, batch, seq)
    group = ATT_HEADS // ATT_KV_HEADS
    q4 = q_s.reshape(db, dt, ATT_HEADS, LANES)
    qs = jnp.stack([q4[:, :, h, (h // group) * HEAD_DIM:(h // group + 1) * HEAD_DIM] for h in range(ATT_HEADS)],
                   axis=1).reshape(db, ATT_HEADS * dt, HEAD_DIM)
    iqs = _heads_major(iq_s, db, dt, LANES)[:, :, :IDX_DIM]
    ws = _heads_major(ikw_s[:, IDX_DIM:IDX_DIM + IDX_HEADS] * (IDX_HEADS ** -0.5), db, dt, 1)
    new_t = lambda a: jnp.pad(a.reshape(db, dt, ATT_KV_HEADS, HEAD_DIM).transpose(0, 2, 3, 1),
                              ((0, 0), (0, 0), (0, 0), (0, LANES - dt)))
    iknew_t = jnp.pad(ikb_s[:, :IDX_DIM].reshape(db, dt, IDX_DIM).transpose(0, 2, 1),
                      ((0, 0), (0, 0), (0, LANES - dt)))
    o_s = _dsa_sample(page_table, qs, iqs, ws, new_t(kb_s), new_t(vb_s), iknew_t,
                      cache_ik[0].transpose(0, 2, 1), cache_k[0].transpose(0, 2, 3, 1),
                      cache_v[0].transpose(0, 2, 3, 1), tri, dt)
    a_s = o_s.reshape(db, ATT_HEADS, dt, HEAD_DIM).transpose(0, 2, 1, 3).reshape(ts, ATT_OUT).astype(BF16)

    r_p, st_p = _retention(rq_p, rk_p, rv_p, sg_p, _retention_tables(float(RET_CHUNK)), None, batch,
                           seq // RET_CHUNK)
    pad = lambda a: _pad_rows(a, db, dt, RET_CHUNK).reshape(db * RET_CHUNK, -1)
    r_s, st_s = _retention(pad(rq_s), pad(rk_s), pad(rv_s), pad(sg_s), _retention_tables(float(dt)),
                           state_ret[0].reshape(db, RET_HEADS // 2, LANES, LANES), db, 1)
    r_s = r_s.reshape(db, RET_CHUNK, RET_OUT)[:, :dt].reshape(ts, RET_OUT)

    wr = jnp.pad(w_router[0], ((0, 0), (0, LANES - N_EXPERTS)))
    br = jnp.concatenate([b_router[0], jnp.full((LANES - N_EXPERTS,), -jnp.inf, F32)]).reshape(1, LANES)
    post_w = (w_branch_a[0].astype(BF16), w_branch_b[0].astype(BF16), w_out[0].astype(BF16),
              norm_ffn_g[0].reshape(1, D_MODEL), wr, br)
    x1_p, h2w_p, idx_p, gate_p, rank_p, cnt_p = _post(a_p, r_p, sga_p, sgb_p, xp, mod_p, *post_w,
                                                      jnp.zeros((1, LANES), F32), tmp, bpm_big)
    x1_s, h2w_s, idx_s, gate_s, rank_s, cnt_all = _post(a_s, r_s, sga_s, sgb_s, xs, mod_s, *post_w, cnt_p,
                                                        TOK_BLOCK, 1)

    del cnt_all
    n_tok, lb = tp + ts, TOK_BLOCK
    nb = n_tok // lb
    idx4 = jnp.concatenate([idx_p[:TOP_K], idx_s[:TOP_K]], axis=1)
    rank4 = jnp.concatenate([rank_p[:TOP_K], rank_s[:TOP_K]], axis=1)
    gate8 = jnp.concatenate([gate_p, gate_s], axis=1)
    hit = idx4[None] == jnp.arange(N_EXPERTS, dtype=I32)[:, None, None]
    bc = jnp.sum(hit.reshape(N_EXPERTS, TOP_K, nb, lb).astype(I32), axis=(1, 3)).T
    run = (bc + SEG_ALIGN - 1) // SEG_ALIGN * SEG_ALIGN
    padded = (jnp.sum(run, axis=0) + MOE_ROWS - 1) // MOE_ROWS * MOE_ROWS
    pend = jnp.cumsum(padded)
    pstart = pend - padded
    loc = jnp.cumsum(run, axis=1) - run
    glob = pstart[None, :] + jnp.cumsum(run, axis=0) - run
    carry = jnp.cumsum(bc, axis=0) - bc
    n_blocks = -(-(n_tok * TOP_K + nb * N_EXPERTS * (SEG_ALIGN - 1)) // MOE_ROWS) + N_EXPERTS
    n_rows = n_blocks * MOE_ROWS
    used = jnp.sum(run, axis=1, keepdims=True)
    runs = jnp.stack([loc, glob, run], axis=-1)
    seg_table = lambda spare: jnp.concatenate(
        [runs, jnp.concatenate([used, spare, SEG_ROWS - used], axis=1)[:, None, :]], axis=1).reshape(-1).astype(I32)
    spare_out = n_rows + (jnp.arange(nb, dtype=I32)[:, None] % 2) * (SEG_ROWS // 2)
    seg_out, seg_in = seg_table(spare_out), seg_table(jnp.zeros((nb, 1), I32))
    shift = jnp.repeat(loc - carry, lb, axis=0).T
    lpos4 = rank4 + jnp.sum(jnp.where(hit, shift[:, None, :], 0), axis=0)
    lpos8 = jnp.pad(lpos4, ((0, 8 - TOP_K), (0, 0))).astype(I32)
    blk_start = jnp.arange(n_blocks, dtype=I32) * MOE_ROWS
    blk_expert = jnp.minimum(jnp.sum((blk_start[:, None] >= pend[None, :]).astype(I32), axis=1), N_EXPERTS - 1)
    blk_first = jnp.concatenate([jnp.ones((1,), I32), (blk_expert[1:] != blk_expert[:-1]).astype(I32)])
    n_used = (pend[-1] // MOE_ROWS).astype(I32).reshape(1)
    x_sorted = jnp.zeros((n_rows + SEG_ROWS, D_MODEL // 2), jnp.uint32)
    x_sorted = _dispatch(seg_out, lpos8, jnp.concatenate([h2w_p, h2w_s], axis=0), x_sorted)
    rows_out = _moe(blk_expert, blk_first, n_used, x_sorted, w_gate_up[0], b_gate_up[0], w_down[0], b_down[0])

    g_final = norm_final_g.reshape(1, D_MODEL)
    y_p = _final(seg_in, x1_p, mod_p, lpos8, gate8, rows_out, g_final, bpm_p, 0)
    y_s = _final(seg_in, x1_s, mod_s, lpos8, gate8, rows_out, g_final, 1, tp // lb)

    kv_shape = lambda b, s: (1, b, s, ATT_KV_HEADS, HEAD_DIM)
    st_shape = lambda b: (1, b, RET_HEADS, RET_DK, RET_DV)
    return (y_p.reshape(batch, seq, D_MODEL), y_s.reshape(db, dt, D_MODEL),
            k_p.reshape(kv_shape(batch, seq)), v_p.reshape(kv_shape(batch, seq)),
            ikw_p[:, :IDX_DIM].reshape(1, batch, seq, IDX_DIM), st_p.reshape(st_shape(batch)),
            k_s.reshape(kv_shape(db, dt)), v_s.reshape(kv_shape(db, dt)),
            ikw_s[:, :IDX_DIM].reshape(1, db, dt, IDX_DIM), st_s.reshape(st_shape(db)))
```

```python
import functools

import jax
import jax.numpy as jnp
import numpy as np
from jax import lax
from jax.experimental import pallas as pl
from jax.experimental.pallas import tpu as pltpu

F32 = jnp.float32
BF16 = jnp.bfloat16
I32 = jnp.int32

D_MODEL = 1024
PAST_LEN = 8192
PAGE_SIZE = 128
ATT_HEADS = 8
ATT_KV_HEADS = 2
HEAD_DIM = 64
ROPE_DIM = HEAD_DIM // 4
ROPE_THETA = 500000.0
IDX_HEADS = 8
IDX_DIM = 64
IDX_ROPE_DIM = IDX_DIM // 4
TOPK_MAX = 256
RET_HEADS = 8
RET_DK = 64
RET_DV = 128
RET_THETA = 10000.0
RET_CHUNK = 128
N_EXPERTS = 32
TOP_K = 4
D_FF = D_MODEL
SWIGLU_LIMIT = 7.0
SWIGLU_ALPHA = 1.702
NORM_EPS = 1e-6
GN_EPS = 1e-5
ATT_OUT = ATT_HEADS * HEAD_DIM
RET_OUT = RET_HEADS * RET_DV
IN_SPLITS = (ATT_HEADS * HEAD_DIM, ATT_KV_HEADS * HEAD_DIM, ATT_KV_HEADS * HEAD_DIM,
             IDX_HEADS * IDX_DIM, IDX_DIM, IDX_HEADS,
             RET_HEADS * RET_DK, RET_HEADS * RET_DK, RET_OUT, RET_OUT, D_MODEL, D_MODEL)

LANES = 128
MASK_NEG = -1e30
FLT_MAX = 3.4028234663852886e38
SELECT_UNROLL = 4
CAUSAL_VARIANTS = 8
IDX_KEY_CHUNK = 256
VMEM_LIMIT = 56 * 1024 * 1024

TOK_BLOCK = 256
PROMPT_TOK_BLOCK = 512
Q_BLOCK = 128
IDX_Q_ROWS = 128
MOE_ROWS = 512
SEG_ALIGN = 8
SEG_BITS = 6
SEG_ROWS = TOP_K * TOK_BLOCK + N_EXPERTS * SEG_ALIGN

_W_GROUPS = (("q", 1024), ("iq", 1024), ("kvi", 384), ("rq", 512), ("rk", 512),
             ("rv", 1024), ("rg", 1024), ("ga", 1024), ("gb", 1024))
PROJ_COLS = 512
_W_OFF = {}
_off = 0
for _n, _w in _W_GROUPS:
    _W_OFF[_n] = (_off, _w)
    _off += _w
W_COLS = _off


def _cparams(*sem):
    return pltpu.CompilerParams(dimension_semantics=sem, vmem_limit_bytes=VMEM_LIMIT)


def _adaln_kernel(c_ref, w_ref, b_ref, o_ref):
    c = c_ref[...]
    s = c * jax.nn.sigmoid(c)
    o_ref[...] = jnp.dot(s, w_ref[...], preferred_element_type=F32, precision=lax.Precision.HIGHEST) + b_ref[...]


def _adaln(c_all, w_ada, b_ada):
    n = c_all.shape[0]
    nb = 1536
    return pl.pallas_call(
        _adaln_kernel,
        grid=(6 * D_MODEL // nb,),
        in_specs=[pl.BlockSpec((n, D_MODEL), lambda j: (0, 0)),
                  pl.BlockSpec((D_MODEL, nb), lambda j: (0, j)),
                  pl.BlockSpec((1, nb), lambda j: (0, j))],
        out_specs=pl.BlockSpec((n, nb), lambda j: (0, j)),
        out_shape=jax.ShapeDtypeStruct((n, 6 * D_MODEL), F32),
        compiler_params=_cparams("arbitrary"),
        name="adaln",
    )(c_all, w_ada, b_ada.reshape(1, -1))


def _rope_slab(z, c, sa, sb, half):
    return z * c + pltpu.roll(z, LANES - half, 1) * sa + pltpu.roll(z, half, 1) * sb


def _inproj_kernel(x_ref, mod_ref, g_ref, w_ref, ca_ref, saa_ref, sba_ref, cr_ref, sar_ref, sbr_ref,
                   q_ref, k_ref, v_ref, kb_ref, vb_ref, iq_ref, ikw_ref, ikb_ref,
                   rq_ref, rk_ref, rv_ref, sg_ref, sga_ref, sgb_ref):
    x = x_ref[...]
    ms = jnp.mean(x * x, axis=-1, keepdims=True)
    y = x * lax.rsqrt(ms + NORM_EPS) * g_ref[...]
    h = (y * (1.0 + mod_ref[:, D_MODEL:2 * D_MODEL]) + mod_ref[:, 0:D_MODEL]).astype(BF16)

    def slabs(name):
        c0, width = _W_OFF[name]
        step = min(width, PROJ_COLS)
        for j in range(width // step):
            z = jnp.dot(h, w_ref[:, c0 + j * step:c0 + (j + 1) * step], preferred_element_type=F32)
            for s in range(step // LANES):
                yield j * (step // LANES) + s, z[:, s * LANES:(s + 1) * LANES]

    ca, saa, sba = ca_ref[...], saa_ref[...], sba_ref[...]
    cr, sar, sbr = cr_ref[...], sar_ref[...], sbr_ref[...]
    att_half, ret_half = ROPE_DIM // 2, RET_DK // 2
    lane = lax.broadcasted_iota(I32, (x.shape[0], LANES), 1)
    sl = lambda s: slice(s * LANES, (s + 1) * LANES)

    for s, z in slabs("q"):
        q_ref[:, sl(s)] = (_rope_slab(z, ca, saa, sba, att_half) * 0.125).astype(BF16)
    for s, z in slabs("iq"):
        iq_ref[:, sl(s)] = (_rope_slab(z, ca, saa, sba, att_half) * 0.125).astype(BF16)
    (_, zk), (_, zv), (_, zi) = slabs("kvi")
    kk = _rope_slab(zk, ca, saa, sba, att_half)
    for j in range(ATT_KV_HEADS):
        k_ref[:, j, :] = kk[:, j * HEAD_DIM:(j + 1) * HEAD_DIM]
        v_ref[:, j, :] = zv[:, j * HEAD_DIM:(j + 1) * HEAD_DIM]
    kb_ref[...] = kk.astype(BF16)
    vb_ref[...] = zv.astype(BF16)
    zr = _rope_slab(zi, ca, saa, sba, att_half)
    ikw_ref[...] = jnp.where(lane < IDX_DIM, zr, zi)
    ikb_ref[...] = jnp.where(lane < IDX_DIM, zr, 0.0).astype(BF16)
    for s, z in slabs("rq"):
        rq_ref[:, sl(s)] = _rope_slab(z, cr, sar, sbr, ret_half).astype(BF16)
    for s, z in slabs("rk"):
        rk_ref[:, sl(s)] = (_rope_slab(z, cr, sar, sbr, ret_half) * 0.125).astype(BF16)
    for s, z in slabs("rv"):
        rv_ref[:, sl(s)] = z.astype(BF16)
    for s, z in slabs("rg"):
        sg_ref[:, sl(s)] = (z * jax.nn.sigmoid(z)).astype(BF16)
    for s, z in slabs("ga"):
        sga_ref[:, sl(s)] = jax.nn.sigmoid(z).astype(BF16)
    for s, z in slabs("gb"):
        sgb_ref[:, sl(s)] = jax.nn.sigmoid(z).astype(BF16)


def _inproj(x, mod3, g, w_packed, tabs_att, tabs_ret, tm, blocks_per_mod, tab_blocks):
    t = x.shape[0]
    nblk = t // tm
    mod_rows = mod3.shape[1]
    tab_spec = pl.BlockSpec((tm, LANES), lambda i: (i % tab_blocks, 0))
    row = lambda n: pl.BlockSpec((tm, n), lambda i: (i, 0))
    kv_spec = pl.BlockSpec((tm, ATT_KV_HEADS, HEAD_DIM), lambda i: (i, 0, 0))
    out_defs = [(1024, BF16), (128, F32), (128, F32), (128, BF16), (128, BF16), (1024, BF16), (128, F32),
                (128, BF16), (512, BF16), (512, BF16), (1024, BF16), (1024, BF16), (1024, BF16), (1024, BF16)]
    return pl.pallas_call(
        _inproj_kernel,
        grid=(nblk,),
        in_specs=[row(D_MODEL),
                  pl.BlockSpec((None, mod_rows, 6 * D_MODEL), lambda i: (i // blocks_per_mod, 0, 0)),
                  pl.BlockSpec((1, D_MODEL), lambda i: (0, 0)),
                  pl.BlockSpec((D_MODEL, W_COLS), lambda i: (0, 0), pipeline_mode=pl.Buffered(1))]
                 + [tab_spec] * 6,
        out_specs=[kv_spec if j in (1, 2) else row(n) for j, (n, _) in enumerate(out_defs)],
        out_shape=[jax.ShapeDtypeStruct((t, ATT_KV_HEADS, HEAD_DIM) if j in (1, 2) else (t, n), d)
                   for j, (n, d) in enumerate(out_defs)],
        compiler_params=_cparams("parallel"),
        name="inproj",
    )(x, mod3, g, w_packed, *tabs_att, *tabs_ret)


def _count(score_ref, n, pred):
    acc = jnp.zeros((score_ref.shape[0], LANES), F32)
    for c in range(n // LANES):
        acc = acc + jnp.where(pred(score_ref[:, c * LANES:(c + 1) * LANES]), 1.0, 0.0)
    return jnp.sum(acc, axis=1, keepdims=True)


def _kth_largest(score_ref, n, k):
    sc = score_ref[:, :n]
    finite = sc > -jnp.inf
    n_fin = jnp.sum(jnp.where(finite, 1.0, 0.0), axis=1, keepdims=True)
    n_pos = jnp.sum(jnp.where(sc > 0.0, 1.0, 0.0), axis=1, keepdims=True)
    n_nonneg = jnp.sum(jnp.where(sc >= 0.0, 1.0, 0.0), axis=1, keepdims=True)
    mx = jnp.max(sc, axis=1, keepdims=True)
    mn = jnp.min(jnp.where(finite, sc, jnp.inf), axis=1, keepdims=True)
    small = n_fin <= k
    positive = n_pos >= k
    at_zero = jnp.logical_and(jnp.logical_not(positive), n_nonneg >= k)
    lo = jnp.where(positive, 0.0, mn)
    hi = jnp.where(positive, mx + (jnp.abs(mx) * 2.0 ** -20 + 2.0 ** -100), 0.0)
    lo = jnp.where(at_zero, 0.0, lo)
    done = jnp.where(jnp.logical_or(small, at_zero), 1.0, 0.0)

    def cond(state):
        return jnp.min(state[2]) < 0.5

    def body(state):
        lo, hi, done = state
        for _ in range(SELECT_UNROLL):
            mid = 0.5 * lo + 0.5 * hi
            cnt = _count(score_ref, n, lambda s: s >= mid)
            stuck = jnp.logical_or(mid <= lo, mid >= hi)
            live = jnp.logical_and(done < 0.5, jnp.logical_not(stuck))
            ge = cnt >= k
            lo = jnp.where(jnp.logical_and(live, ge), mid, lo)
            hi = jnp.where(jnp.logical_and(live, jnp.logical_not(ge)), mid, hi)
            done = jnp.where(jnp.logical_or(stuck, cnt == k), 1.0, done)
        return lo, hi, done

    lo, _, _ = lax.while_loop(cond, body, (lo, hi, done))
    return jnp.where(small, -FLT_MAX, lo)


def _topk_bias(score_ref, bias_ref, tri_ref, n, k):
    rows = score_ref.shape[0]
    thr = _kth_largest(score_ref, n, k)
    n_ge = _count(score_ref, n, lambda s: s >= thr)
    has_ties = jnp.max(jnp.where(n_ge > k, 1.0, 0.0)) > 0.5

    @pl.when(jnp.logical_not(has_ties))
    def _():
        for c in range(n // LANES):
            sl = slice(c * LANES, (c + 1) * LANES)
            bias_ref[:, sl] = jnp.where(score_ref[:, sl] >= thr, 0.0, MASK_NEG)

    @pl.when(has_ties)
    def _():
        need = k - _count(score_ref, n, lambda s: s > thr)
        run = jnp.zeros((rows, 1), F32)
        for c in range(n // LANES):
            sl = slice(c * LANES, (c + 1) * LANES)
            sc = score_ref[:, sl]
            eq = sc == thr
            eqf = jnp.where(eq, 1.0, 0.0)
            before = jnp.dot(eqf.astype(BF16), tri_ref[...], preferred_element_type=F32) + run
            take = jnp.logical_or(sc > thr, jnp.logical_and(eq, before < need))
            bias_ref[:, sl] = jnp.where(take, 0.0, MASK_NEG)
            run = run + jnp.sum(eqf, axis=1, keepdims=True)


def _dsa_prompt_kernel(q_ref, iq_ref, ikw_ref, kb_ref, vb_ref, ikb_ref, tri_ref, o_ref, score_ref, bias_ref):
    qb, s_len = score_ref.shape
    i = pl.program_id(1)
    nqb = s_len // qb
    per_variant = nqb // CAUSAL_VARIANTS
    for v in range(CAUSAL_VARIANTS):
        pl.when(i // per_variant == v)(
            functools.partial(_dsa_prompt_body, q_ref, iq_ref, ikw_ref, kb_ref, vb_ref, ikb_ref, tri_ref, o_ref,
                              score_ref, bias_ref, (v + 1) * per_variant * qb, min(TOPK_MAX, s_len // 4)))


def _dsa_prompt_body(q_ref, iq_ref, ikw_ref, kb_ref, vb_ref, ikb_ref, tri_ref, o_ref, score_ref, bias_ref,
                     n_keys, topk):
    qb = score_ref.shape[0]
    i = pl.program_id(1)
    w = ikw_ref[:, IDX_DIM:IDX_DIM + IDX_HEADS] * (IDX_HEADS ** -0.5)
    nt = (((1,), (1,)), ((), ()))
    kc = IDX_KEY_CHUNK
    qr = IDX_Q_ROWS
    for c in range(n_keys // kc):
        ikc = ikb_ref[c * kc:(c + 1) * kc, :]
        kpos = c * kc + lax.broadcasted_iota(I32, (qr, kc), 1)
        for r0 in range(0, qb, qr):
            acc = jnp.zeros((qr, kc), F32)
            for h in range(IDX_HEADS):
                d = lax.dot_general(iq_ref[r0:r0 + qr, h * LANES:(h + 1) * LANES], ikc, nt,
                                    preferred_element_type=F32)
                acc = acc + jnp.maximum(d, 0.0) * w[r0:r0 + qr, h:h + 1]
            qpos = i * qb + r0 + lax.broadcasted_iota(I32, (qr, kc), 0)
            score_ref[r0:r0 + qr, c * kc:(c + 1) * kc] = jnp.where(kpos <= qpos, acc, -jnp.inf)

    _topk_bias(score_ref, bias_ref, tri_ref, n_keys, topk)

    kb = kb_ref[0:n_keys, :]
    vb = vb_ref[0:n_keys, :]
    bias = bias_ref[:, 0:n_keys]
    lane = lax.broadcasted_iota(I32, (qb, LANES), 1)
    heads = []
    for h in range(ATT_HEADS):
        s = lax.dot_general(q_ref[:, h * LANES:(h + 1) * LANES], kb, nt, preferred_element_type=F32) + bias
        m = jnp.max(s, axis=1, keepdims=True)
        p = jnp.exp(s - m)
        l = jnp.sum(p, axis=1, keepdims=True)
        heads.append(jnp.dot(p.astype(BF16), vb, preferred_element_type=F32) / l)
    group = ATT_HEADS // ATT_KV_HEADS
    for pp in range(ATT_HEADS // 2):
        a, b = heads[2 * pp], heads[2 * pp + 1]
        if (2 * pp) // group == 0:
            slab = jnp.where(lane < HEAD_DIM, a, pltpu.roll(b, HEAD_DIM, 1))
        else:
            slab = jnp.where(lane < HEAD_DIM, pltpu.roll(a, HEAD_DIM, 1), b)
        o_ref[:, pp * LANES:(pp + 1) * LANES] = slab.astype(BF16)


def _dsa_prompt(q, iq, ikw, kb, vb, ikb, tri, batch, seq):
    nqb = seq // Q_BLOCK
    qrow = lambda n: pl.BlockSpec((Q_BLOCK, n), lambda b, i: (b * nqb + i, 0))
    keys = pl.BlockSpec((seq, LANES), lambda b, i: (b, 0))
    return pl.pallas_call(
        _dsa_prompt_kernel,
        grid=(batch, nqb),
        in_specs=[qrow(1024), qrow(1024), qrow(LANES), keys, keys, keys,
                  pl.BlockSpec((LANES, LANES), lambda b, i: (0, 0))],
        out_specs=qrow(ATT_OUT),
        out_shape=jax.ShapeDtypeStruct((batch * seq, ATT_OUT), BF16),
        scratch_shapes=[pltpu.VMEM((Q_BLOCK, seq), F32), pltpu.VMEM((Q_BLOCK, seq), F32)],
        compiler_params=_cparams("parallel", "arbitrary"),
        name="dsa_prompt",
    )(q, iq, ikw, kb, vb, ikb, tri)


def _dsa_sample_kernel(pt_ref, qs_ref, iqs_ref, ws_ref, knew_ref, vnew_ref, iknew_ref, cik_hbm, ck_hbm, cv_hbm,
                       tri_ref, o_ref, ikbuf, kbuf, vbuf, ikt, kt, vt, sems, key_ref, bias_ref):
    db = pl.program_id(0)
    n_pages = ikbuf.shape[1]
    t = key_ref.shape[0]
    n_past = n_pages * PAGE_SIZE
    last = pl.num_programs(0) - 1

    def fetch(src, dst_of_page, sem, req):
        def body(p, carry):
            pltpu.make_async_copy(src.at[pt_ref[req, p]], dst_of_page(p), sem).start()
            return carry
        lax.fori_loop(0, n_pages, body, 0)

    def wait_all(src, dst, sem):
        pltpu.make_async_copy(src.at[pl.ds(0, n_pages)], dst, sem).wait()

    fetch_ik = lambda req, slot: fetch(cik_hbm, lambda p: ikbuf.at[slot, p], sems.at[slot], req)
    fetch_k = lambda req: fetch(ck_hbm, lambda p: kbuf.at[p], sems.at[2], req)
    fetch_v = lambda req: fetch(cv_hbm, lambda p: vbuf.at[p], sems.at[3], req)

    @pl.when(db == 0)
    def _():
        fetch_ik(0, 0)
        fetch_k(0)
        fetch_v(0)

    @pl.when(db < last)
    def _():
        fetch_ik(db + 1, (db + 1) % 2)

    slot = db % 2
    wait_all(cik_hbm, ikbuf.at[slot], sems.at[slot])

    nt = (((1,), (1,)), ((), ()))
    page = lambda p: slice(p * PAGE_SIZE, (p + 1) * PAGE_SIZE)
    for p in range(n_pages):
        ikt[:, page(p)] = ikbuf[slot, p].astype(BF16)
    iqs = iqs_ref[...]
    wcol = ws_ref[...]
    d_past = jnp.maximum(jnp.dot(iqs, ikt[...], preferred_element_type=F32), 0.0) * wcol
    d_new = jnp.maximum(jnp.dot(iqs, iknew_ref[...], preferred_element_type=F32), 0.0) * wcol
    s_past = d_past[0:t]
    s_new = d_new[0:t]
    for h in range(1, IDX_HEADS):
        s_past = s_past + d_past[h * t:(h + 1) * t]
        s_new = s_new + d_new[h * t:(h + 1) * t]
    row = lax.broadcasted_iota(I32, (t, LANES), 0)
    lane = lax.broadcasted_iota(I32, (t, LANES), 1)
    new_ok = lane <= row
    key_ref[:, 0:n_past] = s_past
    key_ref[:, n_past:n_past + LANES] = jnp.where(new_ok, s_new, -jnp.inf)
    _topk_bias(key_ref, bias_ref, tri_ref, n_past + LANES, min(TOPK_MAX, (n_past + t) // 4))

    def stage(buf, dst):
        for p in range(n_pages):
            for j in range(ATT_KV_HEADS):
                dst[j, :, page(p)] = buf[p, j].astype(BF16)

    wait_all(ck_hbm, kbuf, sems.at[2])
    stage(kbuf, kt)

    @pl.when(db < last)
    def _():
        fetch_k(db + 1)

    wait_all(cv_hbm, vbuf, sems.at[3])
    stage(vbuf, vt)

    @pl.when(db < last)
    def _():
        fetch_v(db + 1)

    rows_per_kv = qs_ref.shape[0] // ATT_KV_HEADS
    bias = jnp.concatenate([bias_ref[...]] * (rows_per_kv // t), axis=0)
    for j in range(ATT_KV_HEADS):
        qj = qs_ref[j * rows_per_kv:(j + 1) * rows_per_kv, :]
        sp = jnp.dot(qj, kt[j], preferred_element_type=F32) + bias[:, 0:n_past]
        sn = jnp.dot(qj, knew_ref[j], preferred_element_type=F32) + bias[:, n_past:n_past + LANES]
        m = jnp.maximum(jnp.max(sp, axis=1, keepdims=True), jnp.max(sn, axis=1, keepdims=True))
        pp = jnp.exp(sp - m)
        pn = jnp.exp(sn - m)
        l = jnp.sum(pp, axis=1, keepdims=True) + jnp.sum(pn, axis=1, keepdims=True)
        o = (lax.dot_general(pp.astype(BF16), vt[j], nt, preferred_element_type=F32)
             + lax.dot_general(pn.astype(BF16), vnew_ref[j], nt, preferred_element_type=F32))
        o_ref[j * rows_per_kv:(j + 1) * rows_per_kv, :] = o / l


def _dsa_sample(page_table, qs, iqs, ws, knew, vnew, iknew, cache_ik, cache_k, cache_v, tri, t):
    db, n_pages = page_table.shape
    rows = qs.shape[1]
    per_db = lambda r, n: pl.BlockSpec((None, r, n), lambda b, pt: (b, 0, 0))
    any_spec = pl.BlockSpec(memory_space=pl.ANY)
    n_keys = n_pages * PAGE_SIZE + LANES
    grid_spec = pltpu.PrefetchScalarGridSpec(
        num_scalar_prefetch=1,
        grid=(db,),
        in_specs=[per_db(rows, HEAD_DIM), per_db(rows, IDX_DIM), per_db(rows, 1),
                  pl.BlockSpec((None, ATT_KV_HEADS, HEAD_DIM, LANES), lambda b, pt: (b, 0, 0, 0)),
                  pl.BlockSpec((None, ATT_KV_HEADS, HEAD_DIM, LANES), lambda b, pt: (b, 0, 0, 0)),
                  per_db(IDX_DIM, LANES), any_spec, any_spec, any_spec,
                  pl.BlockSpec((LANES, LANES), lambda b, pt: (0, 0))],
        out_specs=per_db(rows, HEAD_DIM),
        scratch_shapes=[pltpu.VMEM((2, n_pages, IDX_DIM, PAGE_SIZE), F32),
                        pltpu.VMEM((n_pages, ATT_KV_HEADS, HEAD_DIM, PAGE_SIZE), F32),
                        pltpu.VMEM((n_pages, ATT_KV_HEADS, HEAD_DIM, PAGE_SIZE), F32),
                        pltpu.VMEM((IDX_DIM, n_pages * PAGE_SIZE), BF16),
                        pltpu.VMEM((ATT_KV_HEADS, HEAD_DIM, n_pages * PAGE_SIZE), BF16),
                        pltpu.VMEM((ATT_KV_HEADS, HEAD_DIM, n_pages * PAGE_SIZE), BF16),
                        pltpu.SemaphoreType.DMA((4,)),
                        pltpu.VMEM((t, n_keys), F32),
                        pltpu.VMEM((t, n_keys), F32)])
    return pl.pallas_call(
        _dsa_sample_kernel,
        grid_spec=grid_spec,
        out_shape=jax.ShapeDtypeStruct((db, rows, HEAD_DIM), F32),
        compiler_params=_cparams("arbitrary"),
        name="dsa_sample",
    )(page_table, qs, iqs, ws, knew, vnew, iknew, cache_ik, cache_k, cache_v, tri)


def _retention_kernel(has_init, rq_ref, rk_ref, rv_ref, sg_ref, decay_ref, qdec_ref, kdec_ref, gst_ref, *rest):
    if has_init:
        init_ref, o_ref, st_ref, state = rest
    else:
        o_ref, st_ref, state = rest
    c = pl.program_id(1)

    @pl.when(c == 0)
    def _():
        if has_init:
            state[...] = init_ref[...]
        else:
            state[...] = jnp.zeros_like(state)

    nt = (((1,), (1,)), ((), ()))
    tn = (((0,), (0,)), ((), ()))
    rows = rq_ref.shape[0]
    lane = lax.broadcasted_iota(I32, (rows, LANES), 1)
    for p in range(RET_HEADS // 2):
        sl = slice(p * LANES, (p + 1) * LANES)
        qp = rq_ref[:, sl]
        kp = rk_ref[:, sl]
        qd = (qp.astype(F32) * qdec_ref[:, sl]).astype(BF16)
        kd = (kp.astype(F32) * kdec_ref[:, sl]).astype(BF16)
        s_old = state[p]
        s_old_b = s_old.astype(BF16)
        s_new = s_old * gst_ref[p]
        for e in range(2):
            h = 2 * p + e
            hs = slice(h * LANES, (h + 1) * LANES)
            mine = jnp.where((lane >= e * RET_DK) & (lane < (e + 1) * RET_DK), 1.0, 0.0).astype(BF16)
            sc = lax.dot_general(qp * mine, kp, nt, preferred_element_type=F32) * decay_ref[h]
            vh = rv_ref[:, hs]
            o = (jnp.dot(sc.astype(BF16), vh, preferred_element_type=F32)
                 + jnp.dot(qd * mine, s_old_b, preferred_element_type=F32))
            s_new = s_new + lax.dot_general(kd * mine, vh, tn, preferred_element_type=F32)
            mu = jnp.mean(o, axis=-1, keepdims=True)
            var = jnp.mean(jnp.square(o - mu), axis=-1, keepdims=True)
            on = (o - mu) * lax.rsqrt(var + GN_EPS)
            o_ref[:, hs] = (on * sg_ref[:, hs].astype(F32)).astype(BF16)
        state[p] = s_new

    @pl.when(c == pl.num_programs(1) - 1)
    def _():
        st_ref[...] = state[...]


def _retention(rq, rk, rv, sg, tables, init, batch, n_chunks):
    decay, qdec, kdec, gst = tables
    cr = RET_CHUNK
    rowspec = lambda n: pl.BlockSpec((cr, n), lambda b, c: (b * n_chunks + c, 0))
    const = lambda shape: pl.BlockSpec(shape, lambda b, c: (0,) * len(shape))
    st_spec = pl.BlockSpec((None, RET_HEADS // 2, LANES, LANES), lambda b, c: (b, 0, 0, 0))
    in_specs = [rowspec(512), rowspec(512), rowspec(RET_OUT), rowspec(RET_OUT),
                const(decay.shape), const(qdec.shape), const(kdec.shape), const(gst.shape)]
    args = [rq, rk, rv, sg, decay, qdec, kdec, gst]
    if init is not None:
        in_specs.append(st_spec)
        args.append(init)
    return pl.pallas_call(
        functools.partial(_retention_kernel, init is not None),
        grid=(batch, n_chunks),
        in_specs=in_specs,
        out_specs=[rowspec(RET_OUT), st_spec],
        out_shape=[jax.ShapeDtypeStruct((batch * n_chunks * cr, RET_OUT), BF16),
                   jax.ShapeDtypeStruct((batch, RET_HEADS // 2, LANES, LANES), F32)],
        scratch_shapes=[pltpu.VMEM((RET_HEADS // 2, LANES, LANES), F32)],
        compiler_params=_cparams("parallel", "arbitrary"),
        name="retention",
    )(*args)


def _retention_tables(c_eff):
    lg = jnp.log(1.0 - 2.0 ** (-5.0 - jnp.arange(RET_HEADS, dtype=F32)))
    i = jnp.arange(RET_CHUNK, dtype=F32)
    diff = i[:, None] - i[None, :]
    decay = jnp.where(diff >= 0, jnp.exp(jnp.maximum(diff, 0.0)[None] * lg[:, None, None]), 0.0)
    q_decay = jnp.exp((i + 1.0)[:, None] * lg[None, :])
    k_decay = jnp.exp((c_eff - 1.0 - i)[:, None] * lg[None, :])
    qdec = jnp.repeat(q_decay, RET_DK, axis=1)
    kdec = jnp.repeat(k_decay, RET_DK, axis=1)
    g_state = jnp.exp(c_eff * lg)
    gst = jnp.broadcast_to(jnp.repeat(g_state, RET_DK).reshape(RET_HEADS // 2, LANES, 1),
                           (RET_HEADS // 2, LANES, LANES))
    return decay, qdec, kdec, gst


def _pack_bf16_pairs(lo, hi):
    return pltpu.pack_elementwise([lo, hi], packed_dtype=BF16)


def _unpack_bf16_pairs(words):
    return tuple(pltpu.unpack_elementwise(words, index=j, packed_dtype=BF16, unpacked_dtype=F32).astype(BF16)
                 for j in range(2))


def _post_kernel(a_ref, r_ref, sga_ref, sgb_ref, x_ref, mod_ref, wpa_ref, wpb_ref, wo_ref, g_ref, wr_ref, br_ref,
                 ltri_ref, cnt0_ref, x1_ref, h2w_ref, idx_ref, gate_ref, rank_ref, cnt_ref, cnt):
    @pl.when(pl.program_id(0) == 0)
    def _():
        cnt[...] = cnt0_ref[...]

    pa = jnp.dot(a_ref[...], wpa_ref[...], preferred_element_type=F32)
    pb = jnp.dot(r_ref[...], wpb_ref[...], preferred_element_type=F32)
    merged = sga_ref[...].astype(F32) * pa + sgb_ref[...].astype(F32) * pb
    gt1 = mod_ref[:, 2 * D_MODEL:3 * D_MODEL]
    x1 = x_ref[...] + gt1 * jnp.dot(merged.astype(BF16), wo_ref[...], preferred_element_type=F32)
    x1_ref[...] = x1
    ms = jnp.mean(x1 * x1, axis=-1, keepdims=True)
    y = x1 * lax.rsqrt(ms + NORM_EPS) * g_ref[...]
    h2 = y * (1.0 + mod_ref[:, 4 * D_MODEL:5 * D_MODEL]) + mod_ref[:, 3 * D_MODEL:4 * D_MODEL]
    half = D_MODEL // 2
    h2w_ref[...] = _pack_bf16_pairs(h2[:, :half], h2[:, half:])
    logits = jnp.dot(h2, wr_ref[...], preferred_element_type=F32, precision=lax.Precision.HIGHEST) + br_ref[...]
    lane = lax.broadcasted_iota(I32, logits.shape, 1).astype(F32)
    idx_out = jnp.zeros(logits.shape, F32)
    val_out = jnp.zeros(logits.shape, F32)
    chosen = []
    top = None
    for j in range(TOP_K):
        m = jnp.max(logits, axis=1, keepdims=True)
        am = jnp.min(jnp.where(logits == m, lane, float(LANES)), axis=1, keepdims=True)
        if j == 0:
            top = m
        idx_out = jnp.where(lane == j, am, idx_out)
        val_out = jnp.where(lane == j, jnp.exp(m - top), val_out)
        chosen.append(lane == am)
        logits = jnp.where(chosen[-1], -jnp.inf, logits)
    idx_ref[...] = jnp.transpose(idx_out)[0:8, :].astype(I32)
    gate_ref[...] = jnp.transpose(val_out / jnp.sum(val_out, axis=1, keepdims=True))[0:8, :]
    onehot = jnp.where(chosen[0] | chosen[1] | chosen[2] | chosen[3], 1.0, 0.0)
    before = jnp.dot(ltri_ref[...], onehot.astype(BF16), preferred_element_type=F32) + cnt[...]
    rank_out = jnp.zeros(logits.shape, F32)
    for j in range(TOP_K):
        rj = jnp.sum(jnp.where(chosen[j], before, 0.0), axis=1, keepdims=True)
        rank_out = jnp.where(lane == j, rj, rank_out)
    rank_ref[...] = jnp.transpose(rank_out)[0:8, :].astype(I32)
    cnt[...] = cnt[...] + jnp.sum(onehot, axis=0, keepdims=True)
    cnt_ref[...] = cnt[...]


def _post(a, r, sga, sgb, x, mod3, wpa, wpb, wo, g, wr, br, cnt0, tm, blocks_per_mod):
    t = x.shape[0]
    mod_rows = mod3.shape[1]
    ar = jnp.arange(tm)
    ltri = (ar[None, :] < ar[:, None]).astype(BF16)
    row = lambda n: pl.BlockSpec((tm, n), lambda i: (i, 0))
    col8 = pl.BlockSpec((8, tm), lambda i: (0, i))
    const = lambda a_: pl.BlockSpec(a_.shape, lambda i: (0,) * a_.ndim)
    return pl.pallas_call(
        _post_kernel,
        grid=(t // tm,),
        in_specs=[row(ATT_OUT), row(RET_OUT), row(D_MODEL), row(D_MODEL), row(D_MODEL),
                  pl.BlockSpec((None, mod_rows, 6 * D_MODEL), lambda i: (i // blocks_per_mod, 0, 0)),
                  const(wpa), const(wpb), const(wo), const(g), const(wr), const(br), const(ltri), const(cnt0)],
        out_specs=[row(D_MODEL), row(D_MODEL // 2), col8, col8, col8,
                   pl.BlockSpec((1, LANES), lambda i: (0, 0))],
        out_shape=[jax.ShapeDtypeStruct((t, D_MODEL), F32), jax.ShapeDtypeStruct((t, D_MODEL // 2), jnp.uint32),
                   jax.ShapeDtypeStruct((8, t), I32), jax.ShapeDtypeStruct((8, t), F32),
                   jax.ShapeDtypeStruct((8, t), I32), jax.ShapeDtypeStruct((1, LANES), F32)],
        scratch_shapes=[pltpu.VMEM((1, LANES), F32)],
        compiler_params=_cparams("arbitrary"),
        name="post",
    )(a, r, sga, sgb, x, mod3, wpa, wpb, wo, g, wr, br, ltri, cnt0)


def _segment_copies(seg_ref, blk, make_copy):
    for e in range(N_EXPERTS + 1):
        base = (blk * (N_EXPERTS + 1) + e) * 3
        loc, glob, n = seg_ref[base], seg_ref[base + 1], seg_ref[base + 2]
        for k in range(SEG_BITS):
            size = SEG_ALIGN << k

            @pl.when((n & size) != 0)
            def _():
                done = n & (size - 1)
                make_copy(pl.multiple_of(loc + done, SEG_ALIGN), pl.multiple_of(glob + done, SEG_ALIGN),
                          size).start()


def _local_order_matrix(lpos_ref, values, rows):
    lp = lpos_ref[...]
    r = lax.broadcasted_iota(I32, (rows, lp.shape[1]), 0)
    m = jnp.zeros((rows, lp.shape[1]), F32)
    for j in range(TOP_K):
        m = jnp.where(r == lp[j:j + 1, :], values(j), m)
    return m


def _dispatch_kernel(seg_ref, lpos_ref, h2w_ref, xs_in, xs_out, buf, sems):
    del xs_in
    i = pl.program_id(0)
    slot = i % 2
    rows = buf.shape[1]

    def wait_slot(s):
        pltpu.make_async_copy(buf.at[s], xs_out.at[pl.ds(0, rows)], sems.at[s]).wait()

    @pl.when(i >= 2)
    def _():
        wait_slot(slot)

    x = jnp.concatenate(_unpack_bf16_pairs(h2w_ref[...]), axis=1)
    perm = _local_order_matrix(lpos_ref, lambda j: 1.0, rows).astype(BF16)
    xs = jnp.dot(perm, x, preferred_element_type=F32)
    half = D_MODEL // 2
    buf[slot] = _pack_bf16_pairs(xs[:, :half], xs[:, half:])
    _segment_copies(seg_ref, i, lambda loc, glob, size: pltpu.make_async_copy(
        buf.at[slot, pl.ds(loc, size)], xs_out.at[pl.ds(glob, size)], sems.at[slot]))

    @pl.when(i == pl.num_programs(0) - 1)
    def _():
        wait_slot(slot)

        @pl.when(i >= 1)
        def _():
            wait_slot(1 - slot)


def _dispatch(seg, lpos8, h2w, x_sorted):
    t = h2w.shape[0]
    tm = TOK_BLOCK
    grid_spec = pltpu.PrefetchScalarGridSpec(
        num_scalar_prefetch=1,
        grid=(t // tm,),
        in_specs=[pl.BlockSpec((8, tm), lambda i, s: (0, i)),
                  pl.BlockSpec((tm, D_MODEL // 2), lambda i, s: (i, 0)), pl.BlockSpec(memory_space=pl.ANY)],
        out_specs=pl.BlockSpec(memory_space=pl.ANY),
        scratch_shapes=[pltpu.VMEM((2, SEG_ROWS, D_MODEL // 2), jnp.uint32), pltpu.SemaphoreType.DMA((2,))])
    return pl.pallas_call(
        _dispatch_kernel,
        grid_spec=grid_spec,
        out_shape=jax.ShapeDtypeStruct(x_sorted.shape, x_sorted.dtype),
        input_output_aliases={3: 0},
        compiler_params=_cparams("arbitrary"),
        name="dispatch",
    )(seg, lpos8, h2w, x_sorted)


def _moe_kernel(be_ref, first_ref, nused_ref, x_ref, wgu_ref, bgu_ref, wd_ref, bd_ref, o_ref, wgu_b, wd_b):
    i = pl.program_id(0)

    @pl.when(first_ref[i] == 1)
    def _():
        wgu_b[...] = wgu_ref[...].astype(BF16)
        wd_b[...] = wd_ref[...].astype(BF16)

    @pl.when(i < nused_ref[0])
    def _():
        x = jnp.concatenate(_unpack_bf16_pairs(x_ref[...]), axis=1)
        gu = jnp.dot(x, wgu_b[...], preferred_element_type=F32) + bgu_ref[...]
        g = jnp.minimum(gu[:, :D_FF], SWIGLU_LIMIT)
        u = jnp.clip(gu[:, D_FF:], -SWIGLU_LIMIT, SWIGLU_LIMIT)
        act = (u + 1.0) * (g * jax.nn.sigmoid(SWIGLU_ALPHA * g))
        o_ref[...] = jnp.dot(act.astype(BF16), wd_b[...], preferred_element_type=F32) + bd_ref[...]

    @pl.when(i >= nused_ref[0])
    def _():
        o_ref[...] = jnp.zeros_like(o_ref)


def _moe(blk_expert, blk_first, n_used, x_sorted, w_gate_up, b_gate_up, w_down, b_down):
    n_rows = blk_expert.shape[0] * MOE_ROWS
    grid_spec = pltpu.PrefetchScalarGridSpec(
        num_scalar_prefetch=3,
        grid=(n_rows // MOE_ROWS,),
        in_specs=[pl.BlockSpec((MOE_ROWS, D_MODEL // 2), lambda i, be, bf, nu: (i, 0)),
                  pl.BlockSpec((None, D_MODEL, 2 * D_FF), lambda i, be, bf, nu: (be[i], 0, 0)),
                  pl.BlockSpec((None, 1, 2 * D_FF), lambda i, be, bf, nu: (be[i], 0, 0)),
                  pl.BlockSpec((None, D_FF, D_MODEL), lambda i, be, bf, nu: (be[i], 0, 0)),
                  pl.BlockSpec((None, 1, D_MODEL), lambda i, be, bf, nu: (be[i], 0, 0))],
        out_specs=pl.BlockSpec((MOE_ROWS, D_MODEL), lambda i, be, bf, nu: (i, 0)),
        scratch_shapes=[pltpu.VMEM((D_MODEL, 2 * D_FF), BF16), pltpu.VMEM((D_FF, D_MODEL), BF16)])
    return pl.pallas_call(
        _moe_kernel,
        grid_spec=grid_spec,
        out_shape=jax.ShapeDtypeStruct((n_rows, D_MODEL), F32),
        compiler_params=_cparams("arbitrary"),
        name="moe",
    )(blk_expert, blk_first, n_used, x_sorted, w_gate_up, b_gate_up.reshape(N_EXPERTS, 1, -1),
      w_down, b_down.reshape(N_EXPERTS, 1, -1))


def _final_kernel(block_offset, seg_ref, x1_ref, mod_ref, lpos_ref, gate_ref, g_ref, rows_hbm, y_ref, buf, sems):
    i = pl.program_id(0)
    n = pl.num_programs(0)
    rows = buf.shape[1]

    def issue(blk, slot):
        _segment_copies(seg_ref, blk + block_offset, lambda loc, glob, size: pltpu.make_async_copy(
            rows_hbm.at[pl.ds(glob, size)], buf.at[slot, pl.ds(loc, size)], sems.at[slot]))

    @pl.when(i == 0)
    def _():
        issue(0, 0)

    @pl.when(i + 1 < n)
    def _():
        issue(i + 1, (i + 1) % 2)

    slot = i % 2
    pltpu.make_async_copy(rows_hbm.at[pl.ds(0, rows)], buf.at[slot], sems.at[slot]).wait()
    gate = gate_ref[...]
    gmat = _local_order_matrix(lpos_ref, lambda j: gate[j:j + 1, :], rows).astype(BF16)
    moe = lax.dot_general(gmat, buf[slot].astype(BF16), (((0,), (0,)), ((), ())), preferred_element_type=F32)
    x2 = x1_ref[...] + mod_ref[:, 5 * D_MODEL:6 * D_MODEL] * moe
    ms = jnp.mean(x2 * x2, axis=-1, keepdims=True)
    y_ref[...] = x2 * lax.rsqrt(ms + NORM_EPS) * g_ref[...]


def _final(seg, x1, mod3, lpos8, gate8, rows_out, g, blocks_per_mod, block_offset):
    t = x1.shape[0]
    tm = TOK_BLOCK
    mod_rows = mod3.shape[1]
    row = lambda n: pl.BlockSpec((tm, n), lambda i, s: (i, 0))
    col8 = pl.BlockSpec((8, tm), lambda i, s: (0, i + block_offset))
    grid_spec = pltpu.PrefetchScalarGridSpec(
        num_scalar_prefetch=1,
        grid=(t // tm,),
        in_specs=[row(D_MODEL),
                  pl.BlockSpec((None, mod_rows, 6 * D_MODEL), lambda i, s: (i // blocks_per_mod, 0, 0)),
                  col8, col8,
                  pl.BlockSpec((1, D_MODEL), lambda i, s: (0, 0)),
                  pl.BlockSpec(memory_space=pl.ANY)],
        out_specs=row(D_MODEL),
        scratch_shapes=[pltpu.VMEM((2, SEG_ROWS, D_MODEL), F32), pltpu.SemaphoreType.DMA((2,))])
    return pl.pallas_call(
        functools.partial(_final_kernel, block_offset),
        grid_spec=grid_spec,
        out_shape=jax.ShapeDtypeStruct((t, D_MODEL), F32),
        compiler_params=_cparams("arbitrary"),
        name="final",
    )(seg, x1, mod3, lpos8, gate8, g, rows_out)


def _rope_tables(pos, rot_dim, theta, head_dim):
    half = rot_dim // 2
    inv = theta ** (-jnp.arange(half, dtype=F32) * (2.0 / rot_dim))
    ang = pos.astype(F32)[:, None] * inv[None, :]
    cos, sin = jnp.cos(ang), jnp.sin(ang)
    n = pos.shape[0]
    rest = head_dim - rot_dim
    zh = jnp.zeros((n, half), F32)
    c = jnp.concatenate([cos, cos, jnp.ones((n, rest), F32)], axis=1)
    sa = jnp.concatenate([-sin, zh, jnp.zeros((n, rest), F32)], axis=1)
    sb = jnp.concatenate([zh, sin, jnp.zeros((n, rest), F32)], axis=1)
    rep = LANES // head_dim
    return tuple(jnp.tile(a, (1, rep)) for a in (c, sa, sb))


def _pack_w_in(w_in):
    offs = np.cumsum((0,) + IN_SPLITS)
    part = lambda j: w_in[:, offs[j]:offs[j + 1]]
    zero = lambda n: jnp.zeros((D_MODEL, n), w_in.dtype)
    group = ATT_HEADS // ATT_KV_HEADS
    cols = []
    wq = part(0)
    for h in range(ATT_HEADS):
        wh = wq[:, h * HEAD_DIM:(h + 1) * HEAD_DIM]
        cols += [wh, zero(HEAD_DIM)] if h // group == 0 else [zero(HEAD_DIM), wh]
    wiq = part(3)
    for h in range(IDX_HEADS):
        cols += [wiq[:, h * IDX_DIM:(h + 1) * IDX_DIM], zero(LANES - IDX_DIM)]
    cols += [part(1), part(2), part(4), part(5), zero(LANES - IDX_DIM - IDX_HEADS)]
    cols += [part(j) for j in range(6, 12)]
    return jnp.concatenate(cols, axis=1).astype(BF16)


def _heads_major(a, db, t, width):
    heads = a.shape[1] // width
    return a.reshape(db, t, heads, width).transpose(0, 2, 1, 3).reshape(db, heads * t, width)


def _pad_rows(a, db, t, rows):
    return jnp.pad(a.reshape(db, t, -1), ((0, 0), (0, rows - t), (0, 0)))


def kernel(x_prompt, x_sample, cache_k, cache_v, cache_ik, state_ret, page_table, c_prompt, c_sample, norm_mix_g, norm_ffn_g, norm_final_g, w_ada, b_ada, w_in, w_branch_a, w_branch_b, w_out, w_router, b_router, w_gate_up, b_gate_up, w_down, b_down):
    batch, seq, _ = x_prompt.shape
    db, dt, _ = x_sample.shape
    assert w_in.shape[0] == 1, "one layer"
    tp, ts = batch * seq, db * dt
    xp = x_prompt.reshape(tp, D_MODEL)
    xs = x_sample.reshape(ts, D_MODEL)

    mod = _adaln(jnp.concatenate([c_prompt, c_sample], axis=0), w_ada[0], b_ada[0])
    mod_p = mod[:batch].reshape(batch, 1, 6 * D_MODEL)
    mod_s = jnp.repeat(mod[batch:], dt, axis=0).reshape(ts // TOK_BLOCK, TOK_BLOCK, 6 * D_MODEL)
    bpm_p = seq // TOK_BLOCK
    tmp = PROMPT_TOK_BLOCK
    bpm_big = seq // tmp

    w_packed = _pack_w_in(w_in[0])
    pos_p = jnp.arange(seq)
    pos_s = PAST_LEN + (jnp.arange(TOK_BLOCK) % dt)
    g_mix = norm_mix_g[0].reshape(1, D_MODEL)
    outs_p = _inproj(xp, mod_p, g_mix, w_packed, _rope_tables(pos_p, ROPE_DIM, ROPE_THETA, HEAD_DIM),
                     _rope_tables(pos_p, RET_DK, RET_THETA, RET_DK), TOK_BLOCK, bpm_p, bpm_p)
    outs_s = _inproj(xs, mod_s, g_mix, w_packed, _rope_tables(pos_s, ROPE_DIM, ROPE_THETA, HEAD_DIM),
                     _rope_tables(pos_s, RET_DK, RET_THETA, RET_DK), TOK_BLOCK, 1, 1)
    (q_p, k_p, v_p, kb_p, vb_p, iq_p, ikw_p, ikb_p, rq_p, rk_p, rv_p, sg_p, sga_p, sgb_p) = outs_p
    (q_s, k_s, v_s, kb_s, vb_s, iq_s, ikw_s, ikb_s, rq_s, rk_s, rv_s, sg_s, sga_s, sgb_s) = outs_s

    tri = (jnp.arange(LANES)[:, None] < jnp.arange(LANES)[None, :]).astype(BF16)

    a_p = _dsa_prompt(q_p, iq_p, ikw_p, kb_p, vb_p, ikb_p, tri, batch, seq)
    group = ATT_HEADS // ATT_KV_HEADS
    q4 = q_s.reshape(db, dt, ATT_HEADS, LANES)
    qs = jnp.stack([q4[:, :, h, (h // group) * HEAD_DIM:(h // group + 1) * HEAD_DIM] for h in range(ATT_HEADS)],
                   axis=1).reshape(db, ATT_HEADS * dt, HEAD_DIM)
    iqs = _heads_major(iq_s, db, dt, LANES)[:, :, :IDX_DIM]
    ws = _heads_major(ikw_s[:, IDX_DIM:IDX_DIM + IDX_HEADS] * (IDX_HEADS ** -0.5), db, dt, 1)
    new_t = lambda a: jnp.pad(a.reshape(db, dt, ATT_KV_HEADS, HEAD_DIM).transpose(0, 2, 3, 1),
                              ((0, 0), (0, 0), (0, 0), (0, LANES - dt)))
    iknew_t = jnp.pad(ikb_s[:, :IDX_DIM].reshape(db, dt, IDX_DIM).transpose(0, 2, 1),
                      ((0, 0), (0, 0), (0, LANES - dt)))
    o_s = _dsa_sample(page_table, qs, iqs, ws, new_t(kb_s), new_t(vb_s), iknew_t,
                      cache_ik[0].transpose(0, 2, 1), cache_k[0].transpose(0, 2, 3, 1),
                      cache_v[0].transpose(0, 2, 3, 1), tri, dt)
    a_s = o_s.reshape(db, ATT_HEADS, dt, HEAD_DIM).transpose(0, 2, 1, 3).reshape(ts, ATT_OUT).astype(BF16)

    r_p, st_p = _retention(rq_p, rk_p, rv_p, sg_p, _retention_tables(float(RET_CHUNK)), None, batch,
                           seq // RET_CHUNK)
    pad = lambda a: _pad_rows(a, db, dt, RET_CHUNK).reshape(db * RET_CHUNK, -1)
    r_s, st_s = _retention(pad(rq_s), pad(rk_s), pad(rv_s), pad(sg_s), _retention_tables(float(dt)),
                           state_ret[0].reshape(db, RET_HEADS // 2, LANES, LANES), db, 1)
    r_s = r_s.reshape(db, RET_CHUNK, RET_OUT)[:, :dt].reshape(ts, RET_OUT)

    wr = jnp.pad(w_router[0], ((0, 0), (0, LANES - N_EXPERTS)))
    br = jnp.concatenate([b_router[0], jnp.full((LANES - N_EXPERTS,), -jnp.inf, F32)]).reshape(1, LANES)
    post_w = (w_branch_a[0].astype(BF16), w_branch_b[0].astype(BF16), w_out[0].astype(BF16),
              norm_ffn_g[0].reshape(1, D_MODEL), wr, br)
    x1_p, h2w_p, idx_p, gate_p, rank_p, cnt_p = _post(a_p, r_p, sga_p, sgb_p, xp, mod_p, *post_w,
                                                      jnp.zeros((1, LANES), F32), tmp, bpm_big)
    x1_s, h2w_s, idx_s, gate_s, rank_s, cnt_all = _post(a_s, r_s, sga_s, sgb_s, xs, mod_s, *post_w, cnt_p,
                                                        TOK_BLOCK, 1)

    del cnt_all
    n_tok, lb = tp + ts, TOK_BLOCK
    nb = n_tok // lb
    idx4 = jnp.concatenate([idx_p[:TOP_K], idx_s[:TOP_K]], axis=1)
    rank4 = jnp.concatenate([rank_p[:TOP_K], rank_s[:TOP_K]], axis=1)
    gate8 = jnp.concatenate([gate_p, gate_s], axis=1)
    hit = idx4[None] == jnp.arange(N_EXPERTS, dtype=I32)[:, None, None]
    bc = jnp.sum(hit.reshape(N_EXPERTS, TOP_K, nb, lb).astype(I32), axis=(1, 3)).T
    run = (bc + SEG_ALIGN - 1) // SEG_ALIGN * SEG_ALIGN
    padded = (jnp.sum(run, axis=0) + MOE_ROWS - 1) // MOE_ROWS * MOE_ROWS
    pend = jnp.cumsum(padded)
    pstart = pend - padded
    loc = jnp.cumsum(run, axis=1) - run
    glob = pstart[None, :] + jnp.cumsum(run, axis=0) - run
    carry = jnp.cumsum(bc, axis=0) - bc
    n_blocks = -(-(n_tok * TOP_K + nb * N_EXPERTS * (SEG_ALIGN - 1)) // MOE_ROWS) + N_EXPERTS
    n_rows = n_blocks * MOE_ROWS
    used = jnp.sum(run, axis=1, keepdims=True)
    runs = jnp.stack([loc, glob, run], axis=-1)
    seg_table = lambda spare: jnp.concatenate(
        [runs, jnp.concatenate([used, spare, SEG_ROWS - used], axis=1)[:, None, :]], axis=1).reshape(-1).astype(I32)
    spare_out = n_rows + (jnp.arange(nb, dtype=I32)[:, None] % 2) * (SEG_ROWS // 2)
    seg_out, seg_in = seg_table(spare_out), seg_table(jnp.zeros((nb, 1), I32))
    shift = jnp.repeat(loc - carry, lb, axis=0).T
    lpos4 = rank4 + jnp.sum(jnp.where(hit, shift[:, None, :], 0), axis=0)
    lpos8 = jnp.pad(lpos4, ((0, 8 - TOP_K), (0, 0))).astype(I32)
    blk_start = jnp.arange(n_blocks, dtype=I32) * MOE_ROWS
    blk_expert = jnp.minimum(jnp.sum((blk_start[:, None] >= pend[None, :]).astype(I32), axis=1), N_EXPERTS - 1)
    blk_first = jnp.concatenate([jnp.ones((1,), I32), (blk_expert[1:] != blk_expert[:-1]).astype(I32)])
    n_used = (pend[-1] // MOE_ROWS).astype(I32).reshape(1)
    x_sorted = jnp.zeros((n_rows + SEG_ROWS, D_MODEL // 2), jnp.uint32)
    x_sorted = _dispatch(seg_out, lpos8, jnp.concatenate([h2w_p, h2w_s], axis=0), x_sorted)
    rows_out = _moe(blk_expert, blk_first, n_used, x_sorted, w_gate_up[0], b_gate_up[0], w_down[0], b_down[0])

    g_final = norm_final_g.reshape(1, D_MODEL)
    y_p = _final(seg_in, x1_p, mod_p, lpos8, gate8, rows_out, g_final, bpm_p, 0)
    y_s = _final(seg_in, x1_s, mod_s, lpos8, gate8, rows_out, g_final, 1, tp // lb)

    kv_shape = lambda b, s: (1, b, s, ATT_KV_HEADS, HEAD_DIM)
    st_shape = lambda b: (1, b, RET_HEADS, RET_DK, RET_DV)
    return (y_p.reshape(batch, seq, D_MODEL), y_s.reshape(db, dt, D_MODEL),
            k_p.reshape(kv_shape(batch, seq)), v_p.reshape(kv_shape(batch, seq)),
            ikw_p[:, :IDX_DIM].reshape(1, batch, seq, IDX_DIM), st_p.reshape(st_shape(batch)),
            k_s.reshape(kv_shape(db, dt)), v_s.reshape(kv_shape(db, dt)),
            ikw_s[:, :IDX_DIM].reshape(1, db, dt, IDX_DIM), st_s.reshape(st_shape(db)))
```

```python
import functools

import jax
import jax.numpy as jnp
import numpy as np
from jax import lax
from jax.experimental import pallas as pl
from jax.experimental.pallas import tpu as pltpu

F32 = jnp.float32
BF16 = jnp.bfloat16
I32 = jnp.int32

D_MODEL = 1024
PAST_LEN = 8192
PAGE_SIZE = 128
ATT_HEADS = 8
ATT_KV_HEADS = 2
HEAD_DIM = 64
ROPE_DIM = HEAD_DIM // 4
ROPE_THETA = 500000.0
IDX_HEADS = 8
IDX_DIM = 64
IDX_ROPE_DIM = IDX_DIM // 4
TOPK_MAX = 256
RET_HEADS = 8
RET_DK = 64
RET_DV = 128
RET_THETA = 10000.0
RET_CHUNK = 128
N_EXPERTS = 32
TOP_K = 4
D_FF = D_MODEL
SWIGLU_LIMIT = 7.0
SWIGLU_ALPHA = 1.702
NORM_EPS = 1e-6
GN_EPS = 1e-5
ATT_OUT = ATT_HEADS * HEAD_DIM
RET_OUT = RET_HEADS * RET_DV
IN_SPLITS = (ATT_HEADS * HEAD_DIM, ATT_KV_HEADS * HEAD_DIM, ATT_KV_HEADS * HEAD_DIM,
             IDX_HEADS * IDX_DIM, IDX_DIM, IDX_HEADS,
             RET_HEADS * RET_DK, RET_HEADS * RET_DK, RET_OUT, RET_OUT, D_MODEL, D_MODEL)

LANES = 128
MASK_NEG = -1e30
FLT_MAX = 3.4028234663852886e38
SELECT_UNROLL = 4
CAUSAL_VARIANTS = 8
IDX_KEY_CHUNK = 256
VMEM_LIMIT = 56 * 1024 * 1024

TOK_BLOCK = 256
PROMPT_TOK_BLOCK = 512
Q_BLOCK = 128
IDX_Q_ROWS = 128
MOE_ROWS = 512
SEG_ALIGN = 8
SEG_BITS = 6
SEG_ROWS = TOP_K * TOK_BLOCK + N_EXPERTS * SEG_ALIGN

_W_GROUPS = (("q", 1024), ("iq", 1024), ("kvi", 384), ("rq", 512), ("rk", 512),
             ("rv", 1024), ("rg", 1024), ("ga", 1024), ("gb", 1024))
PROJ_COLS = 512
_W_OFF = {}
_off = 0
for _n, _w in _W_GROUPS:
    _W_OFF[_n] = (_off, _w)
    _off += _w
W_COLS = _off


def _cparams(*sem):
    return pltpu.CompilerParams(dimension_semantics=sem, vmem_limit_bytes=VMEM_LIMIT)


def _adaln_kernel(c_ref, w_ref, b_ref, o_ref):
    c = c_ref[...]
    s = c * jax.nn.sigmoid(c)
    o_ref[...] = jnp.dot(s, w_ref[...], preferred_element_type=F32, precision=lax.Precision.HIGHEST) + b_ref[...]


def _adaln(c_all, w_ada, b_ada):
    n = c_all.shape[0]
    nb = 1536
    return pl.pallas_call(
        _adaln_kernel,
        grid=(6 * D_MODEL // nb,),
        in_specs=[pl.BlockSpec((n, D_MODEL), lambda j: (0, 0)),
                  pl.BlockSpec((D_MODEL, nb), lambda j: (0, j)),
                  pl.BlockSpec((1, nb), lambda j: (0, j))],
        out_specs=pl.BlockSpec((n, nb), lambda j: (0, j)),
        out_shape=jax.ShapeDtypeStruct((n, 6 * D_MODEL), F32),
        compiler_params=_cparams("arbitrary"),
        name="adaln",
    )(c_all, w_ada, b_ada.reshape(1, -1))


def _rope_slab(z, c, sa, sb, half):
    return z * c + pltpu.roll(z, LANES - half, 1) * sa + pltpu.roll(z, half, 1) * sb


def _inproj_kernel(x_ref, mod_ref, g_ref, w_ref, ca_ref, saa_ref, sba_ref, cr_ref, sar_ref, sbr_ref,
                   q_ref, k_ref, v_ref, kb_ref, vb_ref, iq_ref, ikw_ref, ikb_ref,
                   rq_ref, rk_ref, rv_ref, sg_ref, sga_ref, sgb_ref):
    x = x_ref[...]
    ms = jnp.mean(x * x, axis=-1, keepdims=True)
    y = x * lax.rsqrt(ms + NORM_EPS) * g_ref[...]
    h = (y * (1.0 + mod_ref[:, D_MODEL:2 * D_MODEL]) + mod_ref[:, 0:D_MODEL]).astype(BF16)

    def slabs(name):
        c0, width = _W_OFF[name]
        step = min(width, PROJ_COLS)
        for j in range(width // step):
            z = jnp.dot(h, w_ref[:, c0 + j * step:c0 + (j + 1) * step], preferred_element_type=F32)
            for s in range(step // LANES):
                yield j * (step // LANES) + s, z[:, s * LANES:(s + 1) * LANES]

    ca, saa, sba = ca_ref[...], saa_ref[...], sba_ref[...]
    cr, sar, sbr = cr_ref[...], sar_ref[...], sbr_ref[...]
    att_half, ret_half = ROPE_DIM // 2, RET_DK // 2
    lane = lax.broadcasted_iota(I32, (x.shape[0], LANES), 1)
    sl = lambda s: slice(s * LANES, (s + 1) * LANES)

    for s, z in slabs("q"):
        q_ref[:, sl(s)] = (_rope_slab(z, ca, saa, sba, att_half) * 0.125).astype(BF16)
    for s, z in slabs("iq"):
        iq_ref[:, sl(s)] = (_rope_slab(z, ca, saa, sba, att_half) * 0.125).astype(BF16)
    (_, zk), (_, zv), (_, zi) = slabs("kvi")
    kk = _rope_slab(zk, ca, saa, sba, att_half)
    for j in range(ATT_KV_HEADS):
        k_ref[:, j, :] = kk[:, j * HEAD_DIM:(j + 1) * HEAD_DIM]
        v_ref[:, j, :] = zv[:, j * HEAD_DIM:(j + 1) * HEAD_DIM]
    kb_ref[...] = kk.astype(BF16)
    vb_ref[...] = zv.astype(BF16)
    zr = _rope_slab(zi, ca, saa, sba, att_half)
    ikw_ref[...] = jnp.where(lane < IDX_DIM, zr, zi)
    ikb_ref[...] = jnp.where(lane < IDX_DIM, zr, 0.0).astype(BF16)
    for s, z in slabs("rq"):
        rq_ref[:, sl(s)] = _rope_slab(z, cr, sar, sbr, ret_half).astype(BF16)
    for s, z in slabs("rk"):
        rk_ref[:, sl(s)] = (_rope_slab(z, cr, sar, sbr, ret_half) * 0.125).astype(BF16)
    for s, z in slabs("rv"):
        rv_ref[:, sl(s)] = z.astype(BF16)
    for s, z in slabs("rg"):
        sg_ref[:, sl(s)] = (z * jax.nn.sigmoid(z)).astype(BF16)
    for s, z in slabs("ga"):
        sga_ref[:, sl(s)] = jax.nn.sigmoid(z).astype(BF16)
    for s, z in slabs("gb"):
        sgb_ref[:, sl(s)] = jax.nn.sigmoid(z).astype(BF16)


def _inproj(x, mod3, g, w_packed, tabs_att, tabs_ret, tm, blocks_per_mod, tab_blocks):
    t = x.shape[0]
    nblk = t // tm
    mod_rows = mod3.shape[1]
    tab_spec = pl.BlockSpec((tm, LANES), lambda i: (i % tab_blocks, 0))
    row = lambda n: pl.BlockSpec((tm, n), lambda i: (i, 0))
    kv_spec = pl.BlockSpec((tm, ATT_KV_HEADS, HEAD_DIM), lambda i: (i, 0, 0))
    out_defs = [(1024, BF16), (128, F32), (128, F32), (128, BF16), (128, BF16), (1024, BF16), (128, F32),
                (128, BF16), (512, BF16), (512, BF16), (1024, BF16), (1024, BF16), (1024, BF16), (1024, BF16)]
    return pl.pallas_call(
        _inproj_kernel,
        grid=(nblk,),
        in_specs=[row(D_MODEL),
                  pl.BlockSpec((None, mod_rows, 6 * D_MODEL), lambda i: (i // blocks_per_mod, 0, 0)),
                  pl.BlockSpec((1, D_MODEL), lambda i: (0, 0)),
                  pl.BlockSpec((D_MODEL, W_COLS), lambda i: (0, 0), pipeline_mode=pl.Buffered(1))]
                 + [tab_spec] * 6,
        out_specs=[kv_spec if j in (1, 2) else row(n) for j, (n, _) in enumerate(out_defs)],
        out_shape=[jax.ShapeDtypeStruct((t, ATT_KV_HEADS, HEAD_DIM) if j in (1, 2) else (t, n), d)
                   for j, (n, d) in enumerate(out_defs)],
        compiler_params=_cparams("parallel"),
        name="inproj",
    )(x, mod3, g, w_packed, *tabs_att, *tabs_ret)


def _count(score_ref, n, pred):
    acc = jnp.zeros((score_ref.shape[0], LANES), F32)
    for c in range(n // LANES):
        acc = acc + jnp.where(pred(score_ref[:, c * LANES:(c + 1) * LANES]), 1.0, 0.0)
    return jnp.sum(acc, axis=1, keepdims=True)


def _kth_largest(score_ref, n, k):
    sc = score_ref[:, :n]
    finite = sc > -jnp.inf
    n_fin = jnp.sum(jnp.where(finite, 1.0, 0.0), axis=1, keepdims=True)
    n_pos = jnp.sum(jnp.where(sc > 0.0, 1.0, 0.0), axis=1, keepdims=True)
    n_nonneg = jnp.sum(jnp.where(sc >= 0.0, 1.0, 0.0), axis=1, keepdims=True)
    mx = jnp.max(sc, axis=1, keepdims=True)
    mn = jnp.min(jnp.where(finite, sc, jnp.inf), axis=1, keepdims=True)
    small = n_fin <= k
    positive = n_pos >= k
    at_zero = jnp.logical_and(jnp.logical_not(positive), n_nonneg >= k)
    lo = jnp.where(positive, 0.0, mn)
    hi = jnp.where(positive, mx + (jnp.abs(mx) * 2.0 ** -20 + 2.0 ** -100), 0.0)
    lo = jnp.where(at_zero, 0.0, lo)
    done = jnp.where(jnp.logical_or(small, at_zero), 1.0, 0.0)

    def cond(state):
        return jnp.min(state[2]) < 0.5

    def body(state):
        lo, hi, done = state
        for _ in range(SELECT_UNROLL):
            mid = 0.5 * lo + 0.5 * hi
            cnt = _count(score_ref, n, lambda s: s >= mid)
            stuck = jnp.logical_or(mid <= lo, mid >= hi)
            live = jnp.logical_and(done < 0.5, jnp.logical_not(stuck))
            ge = cnt >= k
            lo = jnp.where(jnp.logical_and(live, ge), mid, lo)
            hi = jnp.where(jnp.logical_and(live, jnp.logical_not(ge)), mid, hi)
            done = jnp.where(jnp.logical_or(stuck, cnt == k), 1.0, done)
        return lo, hi, done

    lo, _, _ = lax.while_loop(cond, body, (lo, hi, done))
    return jnp.where(small, -FLT_MAX, lo)


def _topk_bias(score_ref, bias_ref, tri_ref, n, k):
    rows = score_ref.shape[0]
    thr = _kth_largest(score_ref, n, k)
    n_ge = _count(score_ref, n, lambda s: s >= thr)
    has_ties = jnp.max(jnp.where(n_ge > k, 1.0, 0.0)) > 0.5

    @pl.when(jnp.logical_not(has_ties))
    def _():
        for c in range(n // LANES):
            sl = slice(c * LANES, (c + 1) * LANES)
            bias_ref[:, sl] = jnp.where(score_ref[:, sl] >= thr, 0.0, MASK_NEG)

    @pl.when(has_ties)
    def _():
        need = k - _count(score_ref, n, lambda s: s > thr)
        run = jnp.zeros((rows, 1), F32)
        for c in range(n // LANES):
            sl = slice(c * LANES, (c + 1) * LANES)
            sc = score_ref[:, sl]
            eq = sc == thr
            eqf = jnp.where(eq, 1.0, 0.0)
            before = jnp.dot(eqf.astype(BF16), tri_ref[...], preferred_element_type=F32) + run
            take = jnp.logical_or(sc > thr, jnp.logical_and(eq, before < need))
            bias_ref[:, sl] = jnp.where(take, 0.0, MASK_NEG)
            run = run + jnp.sum(eqf, axis=1, keepdims=True)


def _dsa_prompt_kernel(q_ref, iq_ref, ikw_ref, kb_ref, vb_ref, ikb_ref, tri_ref, o_ref, score_ref, bias_ref):
    qb, s_len = score_ref.shape
    i = pl.program_id(1)
    nqb = s_len // qb
    per_variant = nqb // CAUSAL_VARIANTS
    for v in range(CAUSAL_VARIANTS):
        pl.when(i // per_variant == v)(
            functools.partial(_dsa_prompt_body, q_ref, iq_ref, ikw_ref, kb_ref, vb_ref, ikb_ref, tri_ref, o_ref,
                              score_ref, bias_ref, (v + 1) * per_variant * qb, min(TOPK_MAX, s_len // 4)))


def _dsa_prompt_body(q_ref, iq_ref, ikw_ref, kb_ref, vb_ref, ikb_ref, tri_ref, o_ref, score_ref, bias_ref,
                     n_keys, topk):
    qb = score_ref.shape[0]
    i = pl.program_id(1)
    w = ikw_ref[:, IDX_DIM:IDX_DIM + IDX_HEADS] * (IDX_HEADS ** -0.5)
    nt = (((1,), (1,)), ((), ()))
    kc = IDX_KEY_CHUNK
    qr = IDX_Q_ROWS
    for c in range(n_keys // kc):
        ikc = ikb_ref[c * kc:(c + 1) * kc, :]
        kpos = c * kc + lax.broadcasted_iota(I32, (qr, kc), 1)
        for r0 in range(0, qb, qr):
            acc = jnp.zeros((qr, kc), F32)
            for h in range(IDX_HEADS):
                d = lax.dot_general(iq_ref[r0:r0 + qr, h * LANES:(h + 1) * LANES], ikc, nt,
                                    preferred_element_type=F32)
                acc = acc + jnp.maximum(d, 0.0) * w[r0:r0 + qr, h:h + 1]
            qpos = i * qb + r0 + lax.broadcasted_iota(I32, (qr, kc), 0)
            score_ref[r0:r0 + qr, c * kc:(c + 1) * kc] = jnp.where(kpos <= qpos, acc, -jnp.inf)

    _topk_bias(score_ref, bias_ref, tri_ref, n_keys, topk)

    kb = kb_ref[0:n_keys, :]
    vb = vb_ref[0:n_keys, :]
    bias = bias_ref[:, 0:n_keys]
    lane = lax.broadcasted_iota(I32, (qb, LANES), 1)
    heads = []
    for h in range(ATT_HEADS):
        s = lax.dot_general(q_ref[:, h * LANES:(h + 1) * LANES], kb, nt, preferred_element_type=F32) + bias
        m = jnp.max(s, axis=1, keepdims=True)
        p = jnp.exp(s - m)
        l = jnp.sum(p, axis=1, keepdims=True)
        heads.append(jnp.dot(p.astype(BF16), vb, preferred_element_type=F32) / l)
    group = ATT_HEADS // ATT_KV_HEADS
    for pp in range(ATT_HEADS // 2):
        a, b = heads[2 * pp], heads[2 * pp + 1]
        if (2 * pp) // group == 0:
            slab = jnp.where(lane < HEAD_DIM, a, pltpu.roll(b, HEAD_DIM, 1))
        else:
            slab = jnp.where(lane < HEAD_DIM, pltpu.roll(a, HEAD_DIM, 1), b)
        o_ref[:, pp * LANES:(pp + 1) * LANES] = slab.astype(BF16)


def _dsa_prompt(q, iq, ikw, kb, vb, ikb, tri, batch, seq):
    nqb = seq // Q_BLOCK
    qrow = lambda n: pl.BlockSpec((Q_BLOCK, n), lambda b, i: (b * nqb + i, 0))
    keys = pl.BlockSpec((seq, LANES), lambda b, i: (b, 0))
    return pl.pallas_call(
        _dsa_prompt_kernel,
        grid=(batch, nqb),
        in_specs=[qrow(1024), qrow(1024), qrow(LANES), keys, keys, keys,
                  pl.BlockSpec((LANES, LANES), lambda b, i: (0, 0))],
        out_specs=qrow(ATT_OUT),
        out_shape=jax.ShapeDtypeStruct((batch * seq, ATT_OUT), BF16),
        scratch_shapes=[pltpu.VMEM((Q_BLOCK, seq), F32), pltpu.VMEM((Q_BLOCK, seq), F32)],
        compiler_params=_cparams("parallel", "arbitrary"),
        name="dsa_prompt",
    )(q, iq, ikw, kb, vb, ikb, tri)


def _dsa_sample_kernel(pt_ref, qs_ref, iqs_ref, ws_ref, knew_ref, vnew_ref, iknew_ref, cik_hbm, ck_hbm, cv_hbm,
                       tri_ref, o_ref, ikbuf, kbuf, vbuf, ikt, kt, vt, sems, key_ref, bias_ref):
    db = pl.program_id(0)
    n_pages = ikbuf.shape[1]
    t = key_ref.shape[0]
    n_past = n_pages * PAGE_SIZE
    last = pl.num_programs(0) - 1

    def fetch(src, dst_of_page, sem, req):
        def body(p, carry):
            pltpu.make_async_copy(src.at[pt_ref[req, p]], dst_of_page(p), sem).start()
            return carry
        lax.fori_loop(0, n_pages, body, 0)

    def wait_all(src, dst, sem):
        pltpu.make_async_copy(src.at[pl.ds(0, n_pages)], dst, sem).wait()

    fetch_ik = lambda req, slot: fetch(cik_hbm, lambda p: ikbuf.at[slot, p], sems.at[slot], req)
    fetch_k = lambda req: fetch(ck_hbm, lambda p: kbuf.at[p], sems.at[2], req)
    fetch_v = lambda req: fetch(cv_hbm, lambda p: vbuf.at[p], sems.at[3], req)

    @pl.when(db == 0)
    def _():
        fetch_ik(0, 0)
        fetch_k(0)
        fetch_v(0)

    @pl.when(db < last)
    def _():
        fetch_ik(db + 1, (db + 1) % 2)

    slot = db % 2
    wait_all(cik_hbm, ikbuf.at[slot], sems.at[slot])

    nt = (((1,), (1,)), ((), ()))
    page = lambda p: slice(p * PAGE_SIZE, (p + 1) * PAGE_SIZE)
    for p in range(n_pages):
        ikt[:, page(p)] = ikbuf[slot, p].astype(BF16)
    iqs = iqs_ref[...]
    wcol = ws_ref[...]
    d_past = jnp.maximum(jnp.dot(iqs, ikt[...], preferred_element_type=F32), 0.0) * wcol
    d_new = jnp.maximum(jnp.dot(iqs, iknew_ref[...], preferred_element_type=F32), 0.0) * wcol
    s_past = d_past[0:t]
    s_new = d_new[0:t]
    for h in range(1, IDX_HEADS):
        s_past = s_past + d_past[h * t:(h + 1) * t]
        s_new = s_new + d_new[h * t:(h + 1) * t]
    row = lax.broadcasted_iota(I32, (t, LANES), 0)
    lane = lax.broadcasted_iota(I32, (t, LANES), 1)
    new_ok = lane <= row
    key_ref[:, 0:n_past] = s_past
    key_ref[:, n_past:n_past + LANES] = jnp.where(new_ok, s_new, -jnp.inf)
    _topk_bias(key_ref, bias_ref, tri_ref, n_past + LANES, min(TOPK_MAX, (n_past + t) // 4))

    def stage(buf, dst):
        for p in range(n_pages):
            for j in range(ATT_KV_HEADS):
                dst[j, :, page(p)] = buf[p, j].astype(BF16)

    wait_all(ck_hbm, kbuf, sems.at[2])
    stage(kbuf, kt)

    @pl.when(db < last)
    def _():
        fetch_k(db + 1)

    wait_all(cv_hbm, vbuf, sems.at[3])
    stage(vbuf, vt)

    @pl.when(db < last)
    def _():
        fetch_v(db + 1)

    rows_per_kv = qs_ref.shape[0] // ATT_KV_HEADS
    bias = jnp.concatenate([bias_ref[...]] * (rows_per_kv // t), axis=0)
    for j in range(ATT_KV_HEADS):
        qj = qs_ref[j * rows_per_kv:(j + 1) * rows_per_kv, :]
        sp = jnp.dot(qj, kt[j], preferred_element_type=F32) + bias[:, 0:n_past]
        sn = jnp.dot(qj, knew_ref[j], preferred_element_type=F32) + bias[:, n_past:n_past + LANES]
        m = jnp.maximum(jnp.max(sp, axis=1, keepdims=True), jnp.max(sn, axis=1, keepdims=True))
        pp = jnp.exp(sp - m)
        pn = jnp.exp(sn - m)
        l = jnp.sum(pp, axis=1, keepdims=True) + jnp.sum(pn, axis=1, keepdims=True)
        o = (lax.dot_general(pp.astype(BF16), vt[j], nt, preferred_element_type=F32)
             + lax.dot_general(pn.astype(BF16), vnew_ref[j], nt, preferred_element_type=F32))
        o_ref[j * rows_per_kv:(j + 1) * rows_per_kv, :] = o / l


def _dsa_sample(page_table, qs, iqs, ws, knew, vnew, iknew, cache_ik, cache_k, cache_v, tri, t):
    db, n_pages = page_table.shape
    rows = qs.shape[1]
    per_db = lambda r, n: pl.BlockSpec((None, r, n), lambda b, pt: (b, 0, 0))
    any_spec = pl.BlockSpec(memory_space=pl.ANY)
    n_keys = n_pages * PAGE_SIZE + LANES
    grid_spec = pltpu.PrefetchScalarGridSpec(
        num_scalar_prefetch=1,
        grid=(db,),
        in_specs=[per_db(rows, HEAD_DIM), per_db(rows, IDX_DIM), per_db(rows, 1),
                  pl.BlockSpec((None, ATT_KV_HEADS, HEAD_DIM, LANES), lambda b, pt: (b, 0, 0, 0)),
                  pl.BlockSpec((None, ATT_KV_HEADS, HEAD_DIM, LANES), lambda b, pt: (b, 0, 0, 0)),
                  per_db(IDX_DIM, LANES), any_spec, any_spec, any_spec,
                  pl.BlockSpec((LANES, LANES), lambda b, pt: (0, 0))],
        out_specs=per_db(rows, HEAD_DIM),
        scratch_shapes=[pltpu.VMEM((2, n_pages, IDX_DIM, PAGE_SIZE), F32),
                        pltpu.VMEM((n_pages, ATT_KV_HEADS, HEAD_DIM, PAGE_SIZE), F32),
                        pltpu.VMEM((n_pages, ATT_KV_HEADS, HEAD_DIM, PAGE_SIZE), F32),
                        pltpu.VMEM((IDX_DIM, n_pages * PAGE_SIZE), BF16),
                        pltpu.VMEM((ATT_KV_HEADS, HEAD_DIM, n_pages * PAGE_SIZE), BF16),
                        pltpu.VMEM((ATT_KV_HEADS, HEAD_DIM, n_pages * PAGE_SIZE), BF16),
                        pltpu.SemaphoreType.DMA((4,)),
                        pltpu.VMEM((t, n_keys), F32),
                        pltpu.VMEM((t, n_keys), F32)])
    return pl.pallas_call(
        _dsa_sample_kernel,
        grid_spec=grid_spec,
        out_shape=jax.ShapeDtypeStruct((db, rows, HEAD_DIM), F32),
        compiler_params=_cparams("arbitrary"),
        name="dsa_sample",
    )(page_table, qs, iqs, ws, knew, vnew, iknew, cache_ik, cache_k, cache_v, tri)


def _retention_kernel(has_init, rq_ref, rk_ref, rv_ref, sg_ref, decay_ref, qdec_ref, kdec_ref, gst_ref, *rest):
    if has_init:
        init_ref, o_ref, st_ref, state = rest
    else:
        o_ref, st_ref, state = rest
    c = pl.program_id(1)

    @pl.when(c == 0)
    def _():
        if has_init:
            state[...] = init_ref[...]
        else:
            state[...] = jnp.zeros_like(state)

    nt = (((1,), (1,)), ((), ()))
    tn = (((0,), (0,)), ((), ()))
    rows = rq_ref.shape[0]
    lane = lax.broadcasted_iota(I32, (rows, LANES), 1)
    for p in range(RET_HEADS // 2):
        sl = slice(p * LANES, (p + 1) * LANES)
        qp = rq_ref[:, sl]
        kp = rk_ref[:, sl]
        qd = (qp.astype(F32) * qdec_ref[:, sl]).astype(BF16)
        kd = (kp.astype(F32) * kdec_ref[:, sl]).astype(BF16)
        s_old = state[p]
        s_old_b = s_old.astype(BF16)
        s_new = s_old * gst_ref[p]
        for e in range(2):
            h = 2 * p + e
            hs = slice(h * LANES, (h + 1) * LANES)
            mine = jnp.where((lane >= e * RET_DK) & (lane < (e + 1) * RET_DK), 1.0, 0.0).astype(BF16)
            sc = lax.dot_general(qp * mine, kp, nt, preferred_element_type=F32) * decay_ref[h]
            vh = rv_ref[:, hs]
            o = jnp.dot(jnp.concatenate([sc.astype(BF16), qd * mine], axis=1),
                        jnp.concatenate([vh, s_old_b], axis=0), preferred_element_type=F32)
            s_new = s_new + lax.dot_general(kd * mine, vh, tn, preferred_element_type=F32)
            mu = jnp.mean(o, axis=-1, keepdims=True)
            var = jnp.mean(jnp.square(o - mu), axis=-1, keepdims=True)
            on = (o - mu) * lax.rsqrt(var + GN_EPS)
            o_ref[:, hs] = (on * sg_ref[:, hs].astype(F32)).astype(BF16)
        state[p] = s_new

    @pl.when(c == pl.num_programs(1) - 1)
    def _():
        st_ref[...] = state[...]


def _retention(rq, rk, rv, sg, tables, init, batch, n_chunks):
    decay, qdec, kdec, gst = tables
    cr = RET_CHUNK
    rowspec = lambda n: pl.BlockSpec((cr, n), lambda b, c: (b * n_chunks + c, 0))
    const = lambda shape: pl.BlockSpec(shape, lambda b, c: (0,) * len(shape))
    st_spec = pl.BlockSpec((None, RET_HEADS // 2, LANES, LANES), lambda b, c: (b, 0, 0, 0))
    in_specs = [rowspec(512), rowspec(512), rowspec(RET_OUT), rowspec(RET_OUT),
                const(decay.shape), const(qdec.shape), const(kdec.shape), const(gst.shape)]
    args = [rq, rk, rv, sg, decay, qdec, kdec, gst]
    if init is not None:
        in_specs.append(st_spec)
        args.append(init)
    return pl.pallas_call(
        functools.partial(_retention_kernel, init is not None),
        grid=(batch, n_chunks),
        in_specs=in_specs,
        out_specs=[rowspec(RET_OUT), st_spec],
        out_shape=[jax.ShapeDtypeStruct((batch * n_chunks * cr, RET_OUT), BF16),
                   jax.ShapeDtypeStruct((batch, RET_HEADS // 2, LANES, LANES), F32)],
        scratch_shapes=[pltpu.VMEM((RET_HEADS // 2, LANES, LANES), F32)],
        compiler_params=_cparams("parallel", "arbitrary"),
        name="retention",
    )(*args)


def _retention_tables(c_eff):
    lg = jnp.log(1.0 - 2.0 ** (-5.0 - jnp.arange(RET_HEADS, dtype=F32)))
    i = jnp.arange(RET_CHUNK, dtype=F32)
    diff = i[:, None] - i[None, :]
    decay = jnp.where(diff >= 0, jnp.exp(jnp.maximum(diff, 0.0)[None] * lg[:, None, None]), 0.0)
    q_decay = jnp.exp((i + 1.0)[:, None] * lg[None, :])
    k_decay = jnp.exp((c_eff - 1.0 - i)[:, None] * lg[None, :])
    qdec = jnp.repeat(q_decay, RET_DK, axis=1)
    kdec = jnp.repeat(k_decay, RET_DK, axis=1)
    g_state = jnp.exp(c_eff * lg)
    gst = jnp.broadcast_to(jnp.repeat(g_state, RET_DK).reshape(RET_HEADS // 2, LANES, 1),
                           (RET_HEADS // 2, LANES, LANES))
    return decay, qdec, kdec, gst


def _pack_bf16_pairs(lo, hi):
    return pltpu.pack_elementwise([lo, hi], packed_dtype=BF16)


def _unpack_bf16_pairs(words):
    return tuple(pltpu.unpack_elementwise(words, index=j, packed_dtype=BF16, unpacked_dtype=F32).astype(BF16)
                 for j in range(2))


def _post_kernel(a_ref, r_ref, sga_ref, sgb_ref, x_ref, mod_ref, wpa_ref, wpb_ref, wo_ref, g_ref, wr_ref, br_ref,
                 ltri_ref, cnt0_ref, x1_ref, h2w_ref, idx_ref, gate_ref, rank_ref, cnt_ref, cnt):
    @pl.when(pl.program_id(0) == 0)
    def _():
        cnt[...] = cnt0_ref[...]

    pa = jnp.dot(a_ref[...], wpa_ref[...], preferred_element_type=F32)
    pb = jnp.dot(r_ref[...], wpb_ref[...], preferred_element_type=F32)
    merged = sga_ref[...].astype(F32) * pa + sgb_ref[...].astype(F32) * pb
    gt1 = mod_ref[:, 2 * D_MODEL:3 * D_MODEL]
    x1 = x_ref[...] + gt1 * jnp.dot(merged.astype(BF16), wo_ref[...], preferred_element_type=F32)
    x1_ref[...] = x1
    ms = jnp.mean(x1 * x1, axis=-1, keepdims=True)
    y = x1 * lax.rsqrt(ms + NORM_EPS) * g_ref[...]
    h2 = y * (1.0 + mod_ref[:, 4 * D_MODEL:5 * D_MODEL]) + mod_ref[:, 3 * D_MODEL:4 * D_MODEL]
    half = D_MODEL // 2
    h2w_ref[...] = _pack_bf16_pairs(h2[:, :half], h2[:, half:])
    h_hi = h2.astype(BF16)
    h_lo = (h2 - h_hi.astype(F32)).astype(BF16)
    logits = (jnp.dot(h_hi, wr_ref[0], preferred_element_type=F32) + jnp.dot(h_hi, wr_ref[1], preferred_element_type=F32)
              + jnp.dot(h_lo, wr_ref[0], preferred_element_type=F32) + br_ref[...])
    lane = lax.broadcasted_iota(I32, logits.shape, 1).astype(F32)
    idx_out = jnp.zeros(logits.shape, F32)
    val_out = jnp.zeros(logits.shape, F32)
    chosen = []
    top = None
    for j in range(TOP_K):
        m = jnp.max(logits, axis=1, keepdims=True)
        am = jnp.min(jnp.where(logits == m, lane, float(LANES)), axis=1, keepdims=True)
        if j == 0:
            top = m
        idx_out = jnp.where(lane == j, am, idx_out)
        val_out = jnp.where(lane == j, jnp.exp(m - top), val_out)
        chosen.append(lane == am)
        logits = jnp.where(chosen[-1], -jnp.inf, logits)
    idx_ref[...] = jnp.transpose(idx_out)[0:8, :].astype(I32)
    gate_ref[...] = jnp.transpose(val_out / jnp.sum(val_out, axis=1, keepdims=True))[0:8, :]
    onehot = jnp.where(chosen[0] | chosen[1] | chosen[2] | chosen[3], 1.0, 0.0)
    before = jnp.dot(ltri_ref[...], onehot.astype(BF16), preferred_element_type=F32) + cnt[...]
    rank_out = jnp.zeros(logits.shape, F32)
    for j in range(TOP_K):
        rj = jnp.sum(jnp.where(chosen[j], before, 0.0), axis=1, keepdims=True)
        rank_out = jnp.where(lane == j, rj, rank_out)
    rank_ref[...] = jnp.transpose(rank_out)[0:8, :].astype(I32)
    cnt[...] = cnt[...] + jnp.sum(onehot, axis=0, keepdims=True)
    cnt_ref[...] = cnt[...]


def _post(a, r, sga, sgb, x, mod3, wpa, wpb, wo, g, wr, br, cnt0, tm, blocks_per_mod):
    t = x.shape[0]
    mod_rows = mod3.shape[1]
    ar = jnp.arange(tm)
    ltri = (ar[None, :] < ar[:, None]).astype(BF16)
    row = lambda n: pl.BlockSpec((tm, n), lambda i: (i, 0))
    col8 = pl.BlockSpec((8, tm), lambda i: (0, i))
    const = lambda a_: pl.BlockSpec(a_.shape, lambda i: (0,) * a_.ndim)
    return pl.pallas_call(
        _post_kernel,
        grid=(t // tm,),
        in_specs=[row(ATT_OUT), row(RET_OUT), row(D_MODEL), row(D_MODEL), row(D_MODEL),
                  pl.BlockSpec((None, mod_rows, 6 * D_MODEL), lambda i: (i // blocks_per_mod, 0, 0)),
                  const(wpa), const(wpb), const(wo), const(g), const(wr), const(br), const(ltri), const(cnt0)],
        out_specs=[row(D_MODEL), row(D_MODEL // 2), col8, col8, col8,
                   pl.BlockSpec((1, LANES), lambda i: (0, 0))],
        out_shape=[jax.ShapeDtypeStruct((t, D_MODEL), F32), jax.ShapeDtypeStruct((t, D_MODEL // 2), jnp.uint32),
                   jax.ShapeDtypeStruct((8, t), I32), jax.ShapeDtypeStruct((8, t), F32),
                   jax.ShapeDtypeStruct((8, t), I32), jax.ShapeDtypeStruct((1, LANES), F32)],
        scratch_shapes=[pltpu.VMEM((1, LANES), F32)],
        compiler_params=_cparams("arbitrary"),
        name="post",
    )(a, r, sga, sgb, x, mod3, wpa, wpb, wo, g, wr, br, ltri, cnt0)


def _segment_copies(seg_ref, blk, make_copy):
    for e in range(N_EXPERTS + 1):
        base = (blk * (N_EXPERTS + 1) + e) * 3
        loc, glob, n = seg_ref[base], seg_ref[base + 1], seg_ref[base + 2]
        for k in range(SEG_BITS):
            size = SEG_ALIGN << k

            @pl.when((n & size) != 0)
            def _():
                done = n & (size - 1)
                make_copy(pl.multiple_of(loc + done, SEG_ALIGN), pl.multiple_of(glob + done, SEG_ALIGN),
                          size).start()


def _local_order_matrix(lpos_ref, values, rows):
    lp = lpos_ref[...]
    r = lax.broadcasted_iota(I32, (rows, lp.shape[1]), 0)
    m = jnp.zeros((rows, lp.shape[1]), F32)
    for j in range(TOP_K):
        m = jnp.where(r == lp[j:j + 1, :], values(j), m)
    return m


def _dispatch_kernel(seg_ref, lpos_ref, h2w_ref, xs_in, xs_out, buf, sems):
    del xs_in
    i = pl.program_id(0)
    slot = i % 2
    rows = buf.shape[1]

    def wait_slot(s):
        pltpu.make_async_copy(buf.at[s], xs_out.at[pl.ds(0, rows)], sems.at[s]).wait()

    @pl.when(i >= 2)
    def _():
        wait_slot(slot)

    x = jnp.concatenate(_unpack_bf16_pairs(h2w_ref[...]), axis=1)
    perm = _local_order_matrix(lpos_ref, lambda j: 1.0, rows).astype(BF16)
    xs = jnp.dot(perm, x, preferred_element_type=F32)
    half = D_MODEL // 2
    buf[slot] = _pack_bf16_pairs(xs[:, :half], xs[:, half:])
    _segment_copies(seg_ref, i, lambda loc, glob, size: pltpu.make_async_copy(
        buf.at[slot, pl.ds(loc, size)], xs_out.at[pl.ds(glob, size)], sems.at[slot]))

    @pl.when(i == pl.num_programs(0) - 1)
    def _():
        wait_slot(slot)

        @pl.when(i >= 1)
        def _():
            wait_slot(1 - slot)


def _dispatch(seg, lpos8, h2w, x_sorted):
    t = h2w.shape[0]
    tm = TOK_BLOCK
    grid_spec = pltpu.PrefetchScalarGridSpec(
        num_scalar_prefetch=1,
        grid=(t // tm,),
        in_specs=[pl.BlockSpec((8, tm), lambda i, s: (0, i)),
                  pl.BlockSpec((tm, D_MODEL // 2), lambda i, s: (i, 0)), pl.BlockSpec(memory_space=pl.ANY)],
        out_specs=pl.BlockSpec(memory_space=pl.ANY),
        scratch_shapes=[pltpu.VMEM((2, SEG_ROWS, D_MODEL // 2), jnp.uint32), pltpu.SemaphoreType.DMA((2,))])
    return pl.pallas_call(
        _dispatch_kernel,
        grid_spec=grid_spec,
        out_shape=jax.ShapeDtypeStruct(x_sorted.shape, x_sorted.dtype),
        input_output_aliases={3: 0},
        compiler_params=_cparams("arbitrary"),
        name="dispatch",
    )(seg, lpos8, h2w, x_sorted)


def _moe_kernel(be_ref, first_ref, nused_ref, x_ref, wgu_ref, bgu_ref, wd_ref, bd_ref, o_ref, wgu_b, wd_b):
    i = pl.program_id(0)

    @pl.when(first_ref[i] == 1)
    def _():
        wgu_b[...] = wgu_ref[...].astype(BF16)
        wd_b[...] = wd_ref[...].astype(BF16)

    @pl.when(i < nused_ref[0])
    def _():
        x = jnp.concatenate(_unpack_bf16_pairs(x_ref[...]), axis=1)
        gu = jnp.dot(x, wgu_b[...], preferred_element_type=F32) + bgu_ref[...]
        g = jnp.minimum(gu[:, :D_FF], SWIGLU_LIMIT)
        u = jnp.clip(gu[:, D_FF:], -SWIGLU_LIMIT, SWIGLU_LIMIT)
        act = (u + 1.0) * (g * jax.nn.sigmoid(SWIGLU_ALPHA * g))
        o_ref[...] = jnp.dot(act.astype(BF16), wd_b[...], preferred_element_type=F32) + bd_ref[...]

    @pl.when(i >= nused_ref[0])
    def _():
        o_ref[...] = jnp.zeros_like(o_ref)


def _moe(blk_expert, blk_first, n_used, x_sorted, w_gate_up, b_gate_up, w_down, b_down):
    n_rows = blk_expert.shape[0] * MOE_ROWS
    grid_spec = pltpu.PrefetchScalarGridSpec(
        num_scalar_prefetch=3,
        grid=(n_rows // MOE_ROWS,),
        in_specs=[pl.BlockSpec((MOE_ROWS, D_MODEL // 2), lambda i, be, bf, nu: (i, 0)),
                  pl.BlockSpec((None, D_MODEL, 2 * D_FF), lambda i, be, bf, nu: (be[i], 0, 0)),
                  pl.BlockSpec((None, 1, 2 * D_FF), lambda i, be, bf, nu: (be[i], 0, 0)),
                  pl.BlockSpec((None, D_FF, D_MODEL), lambda i, be, bf, nu: (be[i], 0, 0)),
                  pl.BlockSpec((None, 1, D_MODEL), lambda i, be, bf, nu: (be[i], 0, 0))],
        out_specs=pl.BlockSpec((MOE_ROWS, D_MODEL), lambda i, be, bf, nu: (i, 0)),
        scratch_shapes=[pltpu.VMEM((D_MODEL, 2 * D_FF), BF16), pltpu.VMEM((D_FF, D_MODEL), BF16)])
    return pl.pallas_call(
        _moe_kernel,
        grid_spec=grid_spec,
        out_shape=jax.ShapeDtypeStruct((n_rows, D_MODEL), F32),
        compiler_params=_cparams("arbitrary"),
        name="moe",
    )(blk_expert, blk_first, n_used, x_sorted, w_gate_up, b_gate_up.reshape(N_EXPERTS, 1, -1),
      w_down, b_down.reshape(N_EXPERTS, 1, -1))


def _final_kernel(block_offset, seg_ref, x1_ref, mod_ref, lpos_ref, gate_ref, g_ref, rows_hbm, y_ref, buf, sems):
    i = pl.program_id(0)
    n = pl.num_programs(0)
    rows = buf.shape[1]

    def issue(blk, slot):
        _segment_copies(seg_ref, blk + block_offset, lambda loc, glob, size: pltpu.make_async_copy(
            rows_hbm.at[pl.ds(glob, size)], buf.at[slot, pl.ds(loc, size)], sems.at[slot]))

    @pl.when(i == 0)
    def _():
        issue(0, 0)

    @pl.when(i + 1 < n)
    def _():
        issue(i + 1, (i + 1) % 2)

    slot = i % 2
    pltpu.make_async_copy(rows_hbm.at[pl.ds(0, rows)], buf.at[slot], sems.at[slot]).wait()
    gate = gate_ref[...]
    gmat = _local_order_matrix(lpos_ref, lambda j: gate[j:j + 1, :], rows).astype(BF16)
    moe = lax.dot_general(gmat, buf[slot].astype(BF16), (((0,), (0,)), ((), ())), preferred_element_type=F32)
    x2 = x1_ref[...] + mod_ref[:, 5 * D_MODEL:6 * D_MODEL] * moe
    ms = jnp.mean(x2 * x2, axis=-1, keepdims=True)
    y_ref[...] = x2 * lax.rsqrt(ms + NORM_EPS) * g_ref[...]


def _final(seg, x1, mod3, lpos8, gate8, rows_out, g, blocks_per_mod, block_offset):
    t = x1.shape[0]
    tm = TOK_BLOCK
    mod_rows = mod3.shape[1]
    row = lambda n: pl.BlockSpec((tm, n), lambda i, s: (i, 0))
    col8 = pl.BlockSpec((8, tm), lambda i, s: (0, i + block_offset))
    grid_spec = pltpu.PrefetchScalarGridSpec(
        num_scalar_prefetch=1,
        grid=(t // tm,),
        in_specs=[row(D_MODEL),
                  pl.BlockSpec((None, mod_rows, 6 * D_MODEL), lambda i, s: (i // blocks_per_mod, 0, 0)),
                  col8, col8,
                  pl.BlockSpec((1, D_MODEL), lambda i, s: (0, 0)),
                  pl.BlockSpec(memory_space=pl.ANY)],
        out_specs=row(D_MODEL),
        scratch_shapes=[pltpu.VMEM((2, SEG_ROWS, D_MODEL), F32), pltpu.SemaphoreType.DMA((2,))])
    return pl.pallas_call(
        functools.partial(_final_kernel, block_offset),
        grid_spec=grid_spec,
        out_shape=jax.ShapeDtypeStruct((t, D_MODEL), F32),
        compiler_params=_cparams("arbitrary"),
        name="final",
    )(seg, x1, mod3, lpos8, gate8, g, rows_out)


def _rope_tables(pos, rot_dim, theta, head_dim):
    half = rot_dim // 2
    inv = theta ** (-jnp.arange(half, dtype=F32) * (2.0 / rot_dim))
    ang = pos.astype(F32)[:, None] * inv[None, :]
    cos, sin = jnp.cos(ang), jnp.sin(ang)
    n = pos.shape[0]
    rest = head_dim - rot_dim
    zh = jnp.zeros((n, half), F32)
    c = jnp.concatenate([cos, cos, jnp.ones((n, rest), F32)], axis=1)
    sa = jnp.concatenate([-sin, zh, jnp.zeros((n, rest), F32)], axis=1)
    sb = jnp.concatenate([zh, sin, jnp.zeros((n, rest), F32)], axis=1)
    rep = LANES // head_dim
    return tuple(jnp.tile(a, (1, rep)) for a in (c, sa, sb))


def _pack_w_in(w_in):
    offs = np.cumsum((0,) + IN_SPLITS)
    part = lambda j: w_in[:, offs[j]:offs[j + 1]]
    zero = lambda n: jnp.zeros((D_MODEL, n), w_in.dtype)
    group = ATT_HEADS // ATT_KV_HEADS
    cols = []
    wq = part(0)
    for h in range(ATT_HEADS):
        wh = wq[:, h * HEAD_DIM:(h + 1) * HEAD_DIM]
        cols += [wh, zero(HEAD_DIM)] if h // group == 0 else [zero(HEAD_DIM), wh]
    wiq = part(3)
    for h in range(IDX_HEADS):
        cols += [wiq[:, h * IDX_DIM:(h + 1) * IDX_DIM], zero(LANES - IDX_DIM)]
    cols += [part(1), part(2), part(4), part(5), zero(LANES - IDX_DIM - IDX_HEADS)]
    cols += [part(j) for j in range(6, 12)]
    return jnp.concatenate(cols, axis=1).astype(BF16)


def _heads_major(a, db, t, width):
    heads = a.shape[1] // width
    return a.reshape(db, t, heads, width).transpose(0, 2, 1, 3).reshape(db, heads * t, width)


def _pad_rows(a, db, t, rows):
    return jnp.pad(a.reshape(db, t, -1), ((0, 0), (0, rows - t), (0, 0)))


def kernel(x_prompt, x_sample, cache_k, cache_v, cache_ik, state_ret, page_table, c_prompt, c_sample, norm_mix_g, norm_ffn_g, norm_final_g, w_ada, b_ada, w_in, w_branch_a, w_branch_b, w_out, w_router, b_router, w_gate_up, b_gate_up, w_down, b_down):
    batch, seq, _ = x_prompt.shape
    db, dt, _ = x_sample.shape
    assert w_in.shape[0] == 1, "one layer"
    tp, ts = batch * seq, db * dt
    xp = x_prompt.reshape(tp, D_MODEL)
    xs = x_sample.reshape(ts, D_MODEL)

    mod = _adaln(jnp.concatenate([c_prompt, c_sample], axis=0), w_ada[0], b_ada[0])
    mod_p = mod[:batch].reshape(batch, 1, 6 * D_MODEL)
    mod_s = jnp.repeat(mod[batch:], dt, axis=0).reshape(ts // TOK_BLOCK, TOK_BLOCK, 6 * D_MODEL)
    bpm_p = seq // TOK_BLOCK
    tmp = PROMPT_TOK_BLOCK
    bpm_big = seq // tmp

    w_packed = _pack_w_in(w_in[0])
    pos_p = jnp.arange(seq)
    pos_s = PAST_LEN + (jnp.arange(TOK_BLOCK) % dt)
    g_mix = norm_mix_g[0].reshape(1, D_MODEL)
    outs_p = _inproj(xp, mod_p, g_mix, w_packed, _rope_tables(pos_p, ROPE_DIM, ROPE_THETA, HEAD_DIM),
                     _rope_tables(pos_p, RET_DK, RET_THETA, RET_DK), TOK_BLOCK, bpm_p, bpm_p)
    outs_s = _inproj(xs, mod_s, g_mix, w_packed, _rope_tables(pos_s, ROPE_DIM, ROPE_THETA, HEAD_DIM),
                     _rope_tables(pos_s, RET_DK, RET_THETA, RET_DK), TOK_BLOCK, 1, 1)
    (q_p, k_p, v_p, kb_p, vb_p, iq_p, ikw_p, ikb_p, rq_p, rk_p, rv_p, sg_p, sga_p, sgb_p) = outs_p
    (q_s, k_s, v_s, kb_s, vb_s, iq_s, ikw_s, ikb_s, rq_s, rk_s, rv_s, sg_s, sga_s, sgb_s) = outs_s

    tri = (jnp.arange(LANES)[:, None] < jnp.arange(LANES)[None, :]).astype(BF16)

    a_p = _dsa_prompt(q_p, iq_p, ikw_p, kb_p, vb_p, ikb_p, tri, batch, seq)
    group = ATT_HEADS // ATT_KV_HEADS
    q4 = q_s.reshape(db, dt, ATT_HEADS, LANES)
    qs = jnp.stack([q4[:, :, h, (h // group) * HEAD_DIM:(h // group + 1) * HEAD_DIM] for h in range(ATT_HEADS)],
                   axis=1).reshape(db, ATT_HEADS * dt, HEAD_DIM)
    iqs = _heads_major(iq_s, db, dt, LANES)[:, :, :IDX_DIM]
    ws = _heads_major(ikw_s[:, IDX_DIM:IDX_DIM + IDX_HEADS] * (IDX_HEADS ** -0.5), db, dt, 1)
    new_t = lambda a: jnp.pad(a.reshape(db, dt, ATT_KV_HEADS, HEAD_DIM).transpose(0, 2, 3, 1),
                              ((0, 0), (0, 0), (0, 0), (0, LANES - dt)))
    iknew_t = jnp.pad(ikb_s[:, :IDX_DIM].reshape(db, dt, IDX_DIM).transpose(0, 2, 1),
                      ((0, 0), (0, 0), (0, LANES - dt)))
    o_s = _dsa_sample(page_table, qs, iqs, ws, new_t(kb_s), new_t(vb_s), iknew_t,
                      cache_ik[0].transpose(0, 2, 1), cache_k[0].transpose(0, 2, 3, 1),
                      cache_v[0].transpose(0, 2, 3, 1), tri, dt)
    a_s = o_s.reshape(db, ATT_HEADS, dt, HEAD_DIM).transpose(0, 2, 1, 3).reshape(ts, ATT_OUT).astype(BF16)

    r_p, st_p = _retention(rq_p, rk_p, rv_p, sg_p, _retention_tables(float(RET_CHUNK)), None, batch,
                           seq // RET_CHUNK)
    pad = lambda a: _pad_rows(a, db, dt, RET_CHUNK).reshape(db * RET_CHUNK, -1)
    r_s, st_s = _retention(pad(rq_s), pad(rk_s), pad(rv_s), pad(sg_s), _retention_tables(float(dt)),
                           state_ret[0].reshape(db, RET_HEADS // 2, LANES, LANES), db, 1)
    r_s = r_s.reshape(db, RET_CHUNK, RET_OUT)[:, :dt].reshape(ts, RET_OUT)

    wr = jnp.pad(w_router[0], ((0, 0), (0, LANES - N_EXPERTS)))
    wr_hi = wr.astype(BF16)
    wr = jnp.stack([wr_hi, (wr - wr_hi.astype(F32)).astype(BF16)])
    br = jnp.concatenate([b_router[0], jnp.full((LANES - N_EXPERTS,), -jnp.inf, F32)]).reshape(1, LANES)
    post_w = (w_branch_a[0].astype(BF16), w_branch_b[0].astype(BF16), w_out[0].astype(BF16),
              norm_ffn_g[0].reshape(1, D_MODEL), wr, br)
    x1_p, h2w_p, idx_p, gate_p, rank_p, cnt_p = _post(a_p, r_p, sga_p, sgb_p, xp, mod_p, *post_w,
                                                      jnp.zeros((1, LANES), F32), tmp, bpm_big)
    x1_s, h2w_s, idx_s, gate_s, rank_s, cnt_all = _post(a_s, r_s, sga_s, sgb_s, xs, mod_s, *post_w, cnt_p,
                                                        TOK_BLOCK, 1)

    del cnt_all
    n_tok, lb = tp + ts, TOK_BLOCK
    nb = n_tok // lb
    idx4 = jnp.concatenate([idx_p[:TOP_K], idx_s[:TOP_K]], axis=1)
    rank4 = jnp.concatenate([rank_p[:TOP_K], rank_s[:TOP_K]], axis=1)
    gate8 = jnp.concatenate([gate_p, gate_s], axis=1)
    hit = idx4[None] == jnp.arange(N_EXPERTS, dtype=I32)[:, None, None]
    bc = jnp.sum(hit.reshape(N_EXPERTS, TOP_K, nb, lb).astype(I32), axis=(1, 3)).T
    run = (bc + SEG_ALIGN - 1) // SEG_ALIGN * SEG_ALIGN
    padded = (jnp.sum(run, axis=0) + MOE_ROWS - 1) // MOE_ROWS * MOE_ROWS
    pend = jnp.cumsum(padded)
    pstart = pend - padded
    loc = jnp.cumsum(run, axis=1) - run
    glob = pstart[None, :] + jnp.cumsum(run, axis=0) - run
    carry = jnp.cumsum(bc, axis=0) - bc
    n_blocks = -(-(n_tok * TOP_K + nb * N_EXPERTS * (SEG_ALIGN - 1)) // MOE_ROWS) + N_EXPERTS
    n_rows = n_blocks * MOE_ROWS
    used = jnp.sum(run, axis=1, keepdims=True)
    runs = jnp.stack([loc, glob, run], axis=-1)
    seg_table = lambda spare: jnp.concatenate(
        [runs, jnp.concatenate([used, spare, SEG_ROWS - used], axis=1)[:, None, :]], axis=1).reshape(-1).astype(I32)
    spare_out = n_rows + (jnp.arange(nb, dtype=I32)[:, None] % 2) * (SEG_ROWS // 2)
    seg_out, seg_in = seg_table(spare_out), seg_table(jnp.zeros((nb, 1), I32))
    shift = jnp.repeat(loc - carry, lb, axis=0).T
    lpos4 = rank4 + jnp.sum(jnp.where(hit, shift[:, None, :], 0), axis=0)
    lpos8 = jnp.pad(lpos4, ((0, 8 - TOP_K), (0, 0))).astype(I32)
    blk_start = jnp.arange(n_blocks, dtype=I32) * MOE_ROWS
    blk_expert = jnp.minimum(jnp.sum((blk_start[:, None] >= pend[None, :]).astype(I32), axis=1), N_EXPERTS - 1)
    blk_first = jnp.concatenate([jnp.ones((1,), I32), (blk_expert[1:] != blk_expert[:-1]).astype(I32)])
    n_used = (pend[-1] // MOE_ROWS).astype(I32).reshape(1)
    x_sorted = jnp.zeros((n_rows + SEG_ROWS, D_MODEL // 2), jnp.uint32)
    x_sorted = _dispatch(seg_out, lpos8, jnp.concatenate([h2w_p, h2w_s], axis=0), x_sorted)
    rows_out = _moe(blk_expert, blk_first, n_used, x_sorted, w_gate_up[0], b_gate_up[0], w_down[0], b_down[0])

    g_final = norm_final_g.reshape(1, D_MODEL)
    y_p = _final(seg_in, x1_p, mod_p, lpos8, gate8, rows_out, g_final, bpm_p, 0)
    y_s = _final(seg_in, x1_s, mod_s, lpos8, gate8, rows_out, g_final, 1, tp // lb)

    kv_shape = lambda b, s: (1, b, s, ATT_KV_HEADS, HEAD_DIM)
    st_shape = lambda b: (1, b, RET_HEADS, RET_DK, RET_DV)
    return (y_p.reshape(batch, seq, D_MODEL), y_s.reshape(db, dt, D_MODEL),
            k_p.reshape(kv_shape(batch, seq)), v_p.reshape(kv_shape(batch, seq)),
            ikw_p[:, :IDX_DIM].reshape(1, batch, seq, IDX_DIM), st_p.reshape(st_shape(batch)),
            k_s.reshape(kv_shape(db, dt)), v_s.reshape(kv_shape(db, dt)),
            ikw_s[:, :IDX_DIM].reshape(1, db, dt, IDX_DIM), st_s.reshape(st_shape(db)))
```

```python
import functools

import jax
import jax.numpy as jnp
import numpy as np
from jax import lax
from jax.experimental import pallas as pl
from jax.experimental.pallas import tpu as pltpu

F32 = jnp.float32
BF16 = jnp.bfloat16
I32 = jnp.int32

D_MODEL = 1024
PAST_LEN = 8192
PAGE_SIZE = 128
ATT_HEADS = 8
ATT_KV_HEADS = 2
HEAD_DIM = 64
ROPE_DIM = HEAD_DIM // 4
ROPE_THETA = 500000.0
IDX_HEADS = 8
IDX_DIM = 64
IDX_ROPE_DIM = IDX_DIM // 4
TOPK_MAX = 256
RET_HEADS = 8
RET_DK = 64
RET_DV = 128
RET_THETA = 10000.0
RET_CHUNK = 128
N_EXPERTS = 32
TOP_K = 4
D_FF = D_MODEL
SWIGLU_LIMIT = 7.0
SWIGLU_ALPHA = 1.702
NORM_EPS = 1e-6
GN_EPS = 1e-5
ATT_OUT = ATT_HEADS * HEAD_DIM
RET_OUT = RET_HEADS * RET_DV
IN_SPLITS = (ATT_HEADS * HEAD_DIM, ATT_KV_HEADS * HEAD_DIM, ATT_KV_HEADS * HEAD_DIM,
             IDX_HEADS * IDX_DIM, IDX_DIM, IDX_HEADS,
             RET_HEADS * RET_DK, RET_HEADS * RET_DK, RET_OUT, RET_OUT, D_MODEL, D_MODEL)

LANES = 128
MASK_NEG = -1e30
FLT_MAX = 3.4028234663852886e38
SELECT_UNROLL = 4
CAUSAL_VARIANTS = 8
IDX_KEY_CHUNK = 256
VMEM_LIMIT = 56 * 1024 * 1024

TOK_BLOCK = 256
PROMPT_TOK_BLOCK = 512
Q_BLOCK = 128
IDX_Q_ROWS = 128
MOE_ROWS = 512
SEG_ALIGN = 8
SEG_BITS = 6
SEG_ROWS = TOP_K * TOK_BLOCK + N_EXPERTS * SEG_ALIGN

_W_GROUPS = (("q", 1024), ("iq", 1024), ("kvi", 384), ("rq", 512), ("rk", 512),
             ("rv", 1024), ("rg", 1024), ("ga", 1024), ("gb", 1024))
PROJ_COLS = 512
_W_OFF = {}
_off = 0
for _n, _w in _W_GROUPS:
    _W_OFF[_n] = (_off, _w)
    _off += _w
W_COLS = _off


def _cparams(*sem):
    return pltpu.CompilerParams(dimension_semantics=sem, vmem_limit_bytes=VMEM_LIMIT)


def _adaln_kernel(c_ref, w_ref, b_ref, o_ref):
    c = c_ref[...]
    s = c * jax.nn.sigmoid(c)
    o_ref[...] = jnp.dot(s, w_ref[...], preferred_element_type=F32, precision=lax.Precision.HIGHEST) + b_ref[...]


def _adaln(c_all, w_ada, b_ada):
    n = c_all.shape[0]
    nb = 1536
    return pl.pallas_call(
        _adaln_kernel,
        grid=(6 * D_MODEL // nb,),
        in_specs=[pl.BlockSpec((n, D_MODEL), lambda j: (0, 0)),
                  pl.BlockSpec((D_MODEL, nb), lambda j: (0, j)),
                  pl.BlockSpec((1, nb), lambda j: (0, j))],
        out_specs=pl.BlockSpec((n, nb), lambda j: (0, j)),
        out_shape=jax.ShapeDtypeStruct((n, 6 * D_MODEL), F32),
        compiler_params=_cparams("arbitrary"),
        name="adaln",
    )(c_all, w_ada, b_ada.reshape(1, -1))


def _rope_slab(z, c, sa, sb, half):
    return z * c + pltpu.roll(z, LANES - half, 1) * sa + pltpu.roll(z, half, 1) * sb


def _inproj_kernel(x_ref, mod_ref, g_ref, w_ref, ca_ref, saa_ref, sba_ref, cr_ref, sar_ref, sbr_ref,
                   q_ref, k_ref, v_ref, kb_ref, vb_ref, iq_ref, ikw_ref, ikb_ref,
                   rq_ref, rk_ref, rv_ref, sg_ref, sga_ref, sgb_ref):
    x = x_ref[...]
    ms = jnp.mean(x * x, axis=-1, keepdims=True)
    y = x * lax.rsqrt(ms + NORM_EPS) * g_ref[...]
    h = (y * (1.0 + mod_ref[:, D_MODEL:2 * D_MODEL]) + mod_ref[:, 0:D_MODEL]).astype(BF16)

    def slabs(name):
        c0, width = _W_OFF[name]
        step = min(width, PROJ_COLS)
        for j in range(width // step):
            z = jnp.dot(h, w_ref[:, c0 + j * step:c0 + (j + 1) * step], preferred_element_type=F32)
            for s in range(step // LANES):
                yield j * (step // LANES) + s, z[:, s * LANES:(s + 1) * LANES]

    ca, saa, sba = ca_ref[...], saa_ref[...], sba_ref[...]
    cr, sar, sbr = cr_ref[...], sar_ref[...], sbr_ref[...]
    att_half, ret_half = ROPE_DIM // 2, RET_DK // 2
    lane = lax.broadcasted_iota(I32, (x.shape[0], LANES), 1)
    sl = lambda s: slice(s * LANES, (s + 1) * LANES)

    for s, z in slabs("q"):
        q_ref[:, sl(s)] = (_rope_slab(z, ca, saa, sba, att_half) * 0.125).astype(BF16)
    for s, z in slabs("iq"):
        iq_ref[:, sl(s)] = (_rope_slab(z, ca, saa, sba, att_half) * 0.125).astype(BF16)
    (_, zk), (_, zv), (_, zi) = slabs("kvi")
    kk = _rope_slab(zk, ca, saa, sba, att_half)
    for j in range(ATT_KV_HEADS):
        k_ref[:, j, :] = kk[:, j * HEAD_DIM:(j + 1) * HEAD_DIM]
        v_ref[:, j, :] = zv[:, j * HEAD_DIM:(j + 1) * HEAD_DIM]
    kb_ref[...] = kk.astype(BF16)
    vb_ref[...] = zv.astype(BF16)
    zr = _rope_slab(zi, ca, saa, sba, att_half)
    ikw_ref[...] = jnp.where(lane < IDX_DIM, zr, zi)
    ikb_ref[...] = jnp.where(lane < IDX_DIM, zr, 0.0).astype(BF16)
    for s, z in slabs("rq"):
        rq_ref[:, sl(s)] = _rope_slab(z, cr, sar, sbr, ret_half).astype(BF16)
    for s, z in slabs("rk"):
        rk_ref[:, sl(s)] = (_rope_slab(z, cr, sar, sbr, ret_half) * 0.125).astype(BF16)
    for s, z in slabs("rv"):
        rv_ref[:, sl(s)] = z.astype(BF16)
    for s, z in slabs("rg"):
        sg_ref[:, sl(s)] = (z * jax.nn.sigmoid(z)).astype(BF16)
    for s, z in slabs("ga"):
        sga_ref[:, sl(s)] = jax.nn.sigmoid(z).astype(BF16)
    for s, z in slabs("gb"):
        sgb_ref[:, sl(s)] = jax.nn.sigmoid(z).astype(BF16)


def _inproj(x, mod3, g, w_packed, tabs_att, tabs_ret, tm, blocks_per_mod, tab_blocks):
    t = x.shape[0]
    nblk = t // tm
    mod_rows = mod3.shape[1]
    tab_spec = pl.BlockSpec((tm, LANES), lambda i: (i % tab_blocks, 0))
    row = lambda n: pl.BlockSpec((tm, n), lambda i: (i, 0))
    kv_spec = pl.BlockSpec((tm, ATT_KV_HEADS, HEAD_DIM), lambda i: (i, 0, 0))
    out_defs = [(1024, BF16), (128, F32), (128, F32), (128, BF16), (128, BF16), (1024, BF16), (128, F32),
                (128, BF16), (512, BF16), (512, BF16), (1024, BF16), (1024, BF16), (1024, BF16), (1024, BF16)]
    return pl.pallas_call(
        _inproj_kernel,
        grid=(nblk,),
        in_specs=[row(D_MODEL),
                  pl.BlockSpec((None, mod_rows, 6 * D_MODEL), lambda i: (i // blocks_per_mod, 0, 0)),
                  pl.BlockSpec((1, D_MODEL), lambda i: (0, 0)),
                  pl.BlockSpec((D_MODEL, W_COLS), lambda i: (0, 0), pipeline_mode=pl.Buffered(1))]
                 + [tab_spec] * 6,
        out_specs=[kv_spec if j in (1, 2) else row(n) for j, (n, _) in enumerate(out_defs)],
        out_shape=[jax.ShapeDtypeStruct((t, ATT_KV_HEADS, HEAD_DIM) if j in (1, 2) else (t, n), d)
                   for j, (n, d) in enumerate(out_defs)],
        compiler_params=_cparams("parallel"),
        name="inproj",
    )(x, mod3, g, w_packed, *tabs_att, *tabs_ret)


def _count(score_ref, n, pred):
    acc = jnp.zeros((score_ref.shape[0], LANES), F32)
    for c in range(n // LANES):
        acc = acc + jnp.where(pred(score_ref[:, c * LANES:(c + 1) * LANES]), 1.0, 0.0)
    return jnp.sum(acc, axis=1, keepdims=True)


def _kth_largest(score_ref, n, k):
    sc = score_ref[:, :n]
    finite = sc > -jnp.inf
    n_fin = jnp.sum(jnp.where(finite, 1.0, 0.0), axis=1, keepdims=True)
    n_pos = jnp.sum(jnp.where(sc > 0.0, 1.0, 0.0), axis=1, keepdims=True)
    n_nonneg = jnp.sum(jnp.where(sc >= 0.0, 1.0, 0.0), axis=1, keepdims=True)
    mx = jnp.max(sc, axis=1, keepdims=True)
    mn = jnp.min(jnp.where(finite, sc, jnp.inf), axis=1, keepdims=True)
    small = n_fin <= k
    positive = n_pos >= k
    at_zero = jnp.logical_and(jnp.logical_not(positive), n_nonneg >= k)
    lo = jnp.where(positive, 0.0, mn)
    hi = jnp.where(positive, mx + (jnp.abs(mx) * 2.0 ** -20 + 2.0 ** -100), 0.0)
    lo = jnp.where(at_zero, 0.0, lo)
    done = jnp.where(jnp.logical_or(small, at_zero), 1.0, 0.0)

    def cond(state):
        return jnp.min(state[2]) < 0.5

    def body(state):
        lo, hi, done = state
        for _ in range(SELECT_UNROLL):
            mid = 0.5 * lo + 0.5 * hi
            cnt = _count(score_ref, n, lambda s: s >= mid)
            stuck = jnp.logical_or(mid <= lo, mid >= hi)
            live = jnp.logical_and(done < 0.5, jnp.logical_not(stuck))
            ge = cnt >= k
            lo = jnp.where(jnp.logical_and(live, ge), mid, lo)
            hi = jnp.where(jnp.logical_and(live, jnp.logical_not(ge)), mid, hi)
            done = jnp.where(jnp.logical_or(stuck, cnt == k), 1.0, done)
        return lo, hi, done

    lo, _, _ = lax.while_loop(cond, body, (lo, hi, done))
    return jnp.where(small, -FLT_MAX, lo)


def _topk_bias(score_ref, bias_ref, tri_ref, n, k):
    rows = score_ref.shape[0]
    thr = _kth_largest(score_ref, n, k)
    n_ge = _count(score_ref, n, lambda s: s >= thr)
    has_ties = jnp.max(jnp.where(n_ge > k, 1.0, 0.0)) > 0.5

    @pl.when(jnp.logical_not(has_ties))
    def _():
        for c in range(n // LANES):
            sl = slice(c * LANES, (c + 1) * LANES)
            bias_ref[:, sl] = jnp.where(score_ref[:, sl] >= thr, 0.0, MASK_NEG)

    @pl.when(has_ties)
    def _():
        need = k - _count(score_ref, n, lambda s: s > thr)
        run = jnp.zeros((rows, 1), F32)
        for c in range(n // LANES):
            sl = slice(c * LANES, (c + 1) * LANES)
            sc = score_ref[:, sl]
            eq = sc == thr
            eqf = jnp.where(eq, 1.0, 0.0)
            before = jnp.dot(eqf.astype(BF16), tri_ref[...], preferred_element_type=F32) + run
            take = jnp.logical_or(sc > thr, jnp.logical_and(eq, before < need))
            bias_ref[:, sl] = jnp.where(take, 0.0, MASK_NEG)
            run = run + jnp.sum(eqf, axis=1, keepdims=True)


def _dsa_prompt_kernel(q_ref, iq_ref, ikw_ref, kb_ref, vb_ref, ikb_ref, tri_ref, o_ref, score_ref, bias_ref):
    qb, s_len = score_ref.shape
    i = pl.program_id(1)
    nqb = s_len // qb
    per_variant = nqb // CAUSAL_VARIANTS
    for v in range(CAUSAL_VARIANTS):
        pl.when(i // per_variant == v)(
            functools.partial(_dsa_prompt_body, q_ref, iq_ref, ikw_ref, kb_ref, vb_ref, ikb_ref, tri_ref, o_ref,
                              score_ref, bias_ref, (v + 1) * per_variant * qb, min(TOPK_MAX, s_len // 4)))


def _dsa_prompt_body(q_ref, iq_ref, ikw_ref, kb_ref, vb_ref, ikb_ref, tri_ref, o_ref, score_ref, bias_ref,
                     n_keys, topk):
    qb = score_ref.shape[0]
    i = pl.program_id(1)
    w = ikw_ref[:, IDX_DIM:IDX_DIM + IDX_HEADS] * (IDX_HEADS ** -0.5)
    nt = (((1,), (1,)), ((), ()))
    kc = IDX_KEY_CHUNK
    qr = IDX_Q_ROWS
    for c in range(n_keys // kc):
        ikc = ikb_ref[c * kc:(c + 1) * kc, :]
        kpos = c * kc + lax.broadcasted_iota(I32, (qr, kc), 1)
        for r0 in range(0, qb, qr):
            acc = jnp.zeros((qr, kc), F32)
            for h in range(IDX_HEADS):
                d = lax.dot_general(iq_ref[r0:r0 + qr, h * LANES:(h + 1) * LANES], ikc, nt,
                                    preferred_element_type=F32)
                acc = acc + jnp.maximum(d, 0.0) * w[r0:r0 + qr, h:h + 1]
            qpos = i * qb + r0 + lax.broadcasted_iota(I32, (qr, kc), 0)
            score_ref[r0:r0 + qr, c * kc:(c + 1) * kc] = jnp.where(kpos <= qpos, acc, -jnp.inf)

    _topk_bias(score_ref, bias_ref, tri_ref, n_keys, topk)

    kb = kb_ref[0:n_keys, :]
    vb = vb_ref[0:n_keys, :]
    bias = bias_ref[:, 0:n_keys]
    lane = lax.broadcasted_iota(I32, (qb, LANES), 1)
    heads = []
    for h in range(ATT_HEADS):
        s = lax.dot_general(q_ref[:, h * LANES:(h + 1) * LANES], kb, nt, preferred_element_type=F32) + bias
        m = jnp.max(s, axis=1, keepdims=True)
        p = jnp.exp(s - m)
        l = jnp.sum(p, axis=1, keepdims=True)
        heads.append(jnp.dot(p.astype(BF16), vb, preferred_element_type=F32) / l)
    group = ATT_HEADS // ATT_KV_HEADS
    for pp in range(ATT_HEADS // 2):
        a, b = heads[2 * pp], heads[2 * pp + 1]
        if (2 * pp) // group == 0:
            slab = jnp.where(lane < HEAD_DIM, a, pltpu.roll(b, HEAD_DIM, 1))
        else:
            slab = jnp.where(lane < HEAD_DIM, pltpu.roll(a, HEAD_DIM, 1), b)
        o_ref[:, pp * LANES:(pp + 1) * LANES] = slab.astype(BF16)


def _dsa_prompt(q, iq, ikw, kb, vb, ikb, tri, batch, seq):
    nqb = seq // Q_BLOCK
    qrow = lambda n: pl.BlockSpec((Q_BLOCK, n), lambda b, i: (b * nqb + i, 0))
    keys = pl.BlockSpec((seq, LANES), lambda b, i: (b, 0))
    return pl.pallas_call(
        _dsa_prompt_kernel,
        grid=(batch, nqb),
        in_specs=[qrow(1024), qrow(1024), qrow(LANES), keys, keys, keys,
                  pl.BlockSpec((LANES, LANES), lambda b, i: (0, 0))],
        out_specs=qrow(ATT_OUT),
        out_shape=jax.ShapeDtypeStruct((batch * seq, ATT_OUT), BF16),
        scratch_shapes=[pltpu.VMEM((Q_BLOCK, seq), F32), pltpu.VMEM((Q_BLOCK, seq), F32)],
        compiler_params=_cparams("parallel", "arbitrary"),
        name="dsa_prompt",
    )(q, iq, ikw, kb, vb, ikb, tri)


def _dsa_sample_kernel(pt_ref, qs_ref, iqs_ref, ws_ref, knew_ref, vnew_ref, iknew_ref, cik_hbm, ck_hbm, cv_hbm,
                       tri_ref, o_ref, ikbuf, kbuf, vbuf, ikt, kt, vt, sems, key_ref, bias_ref):
    db = pl.program_id(0)
    n_pages = ikbuf.shape[1]
    t = key_ref.shape[0]
    n_past = n_pages * PAGE_SIZE
    last = pl.num_programs(0) - 1

    def fetch(src, dst_of_page, sem, req):
        def body(p, carry):
            pltpu.make_async_copy(src.at[pt_ref[req, p]], dst_of_page(p), sem).start()
            return carry
        lax.fori_loop(0, n_pages, body, 0)

    def wait_all(src, dst, sem):
        pltpu.make_async_copy(src.at[pl.ds(0, n_pages)], dst, sem).wait()

    fetch_ik = lambda req, slot: fetch(cik_hbm, lambda p: ikbuf.at[slot, p], sems.at[slot], req)
    fetch_k = lambda req: fetch(ck_hbm, lambda p: kbuf.at[p], sems.at[2], req)
    fetch_v = lambda req: fetch(cv_hbm, lambda p: vbuf.at[p], sems.at[3], req)

    @pl.when(db == 0)
    def _():
        fetch_ik(0, 0)
        fetch_k(0)
        fetch_v(0)

    @pl.when(db < last)
    def _():
        fetch_ik(db + 1, (db + 1) % 2)

    slot = db % 2
    wait_all(cik_hbm, ikbuf.at[slot], sems.at[slot])

    nt = (((1,), (1,)), ((), ()))
    page = lambda p: slice(p * PAGE_SIZE, (p + 1) * PAGE_SIZE)
    for p in range(n_pages):
        ikt[:, page(p)] = ikbuf[slot, p].astype(BF16)
    iqs = iqs_ref[...]
    wcol = ws_ref[...]
    d_past = jnp.maximum(jnp.dot(iqs, ikt[...], preferred_element_type=F32), 0.0) * wcol
    d_new = jnp.maximum(jnp.dot(iqs, iknew_ref[...], preferred_element_type=F32), 0.0) * wcol
    s_past = d_past[0:t]
    s_new = d_new[0:t]
    for h in range(1, IDX_HEADS):
        s_past = s_past + d_past[h * t:(h + 1) * t]
        s_new = s_new + d_new[h * t:(h + 1) * t]
    row = lax.broadcasted_iota(I32, (t, LANES), 0)
    lane = lax.broadcasted_iota(I32, (t, LANES), 1)
    new_ok = lane <= row
    key_ref[:, 0:n_past] = s_past
    key_ref[:, n_past:n_past + LANES] = jnp.where(new_ok, s_new, -jnp.inf)
    _topk_bias(key_ref, bias_ref, tri_ref, n_past + LANES, min(TOPK_MAX, (n_past + t) // 4))

    def stage(buf, dst):
        for p in range(n_pages):
            for j in range(ATT_KV_HEADS):
                dst[j, :, page(p)] = buf[p, j].astype(BF16)

    wait_all(ck_hbm, kbuf, sems.at[2])
    stage(kbuf, kt)

    @pl.when(db < last)
    def _():
        fetch_k(db + 1)

    wait_all(cv_hbm, vbuf, sems.at[3])
    stage(vbuf, vt)

    @pl.when(db < last)
    def _():
        fetch_v(db + 1)

    rows_per_kv = qs_ref.shape[0] // ATT_KV_HEADS
    bias = jnp.concatenate([bias_ref[...]] * (rows_per_kv // t), axis=0)
    for j in range(ATT_KV_HEADS):
        qj = qs_ref[j * rows_per_kv:(j + 1) * rows_per_kv, :]
        sp = jnp.dot(qj, kt[j], preferred_element_type=F32) + bias[:, 0:n_past]
        sn = jnp.dot(qj, knew_ref[j], preferred_element_type=F32) + bias[:, n_past:n_past + LANES]
        m = jnp.maximum(jnp.max(sp, axis=1, keepdims=True), jnp.max(sn, axis=1, keepdims=True))
        pp = jnp.exp(sp - m)
        pn = jnp.exp(sn - m)
        l = jnp.sum(pp, axis=1, keepdims=True) + jnp.sum(pn, axis=1, keepdims=True)
        o = (lax.dot_general(pp.astype(BF16), vt[j], nt, preferred_element_type=F32)
             + lax.dot_general(pn.astype(BF16), vnew_ref[j], nt, preferred_element_type=F32))
        o_ref[j * rows_per_kv:(j + 1) * rows_per_kv, :] = o / l


def _dsa_sample(page_table, qs, iqs, ws, knew, vnew, iknew, cache_ik, cache_k, cache_v, tri, t):
    db, n_pages = page_table.shape
    rows = qs.shape[1]
    per_db = lambda r, n: pl.BlockSpec((None, r, n), lambda b, pt: (b, 0, 0))
    any_spec = pl.BlockSpec(memory_space=pl.ANY)
    n_keys = n_pages * PAGE_SIZE + LANES
    grid_spec = pltpu.PrefetchScalarGridSpec(
        num_scalar_prefetch=1,
        grid=(db,),
        in_specs=[per_db(rows, HEAD_DIM), per_db(rows, IDX_DIM), per_db(rows, 1),
                  pl.BlockSpec((None, ATT_KV_HEADS, HEAD_DIM, LANES), lambda b, pt: (b, 0, 0, 0)),
                  pl.BlockSpec((None, ATT_KV_HEADS, HEAD_DIM, LANES), lambda b, pt: (b, 0, 0, 0)),
                  per_db(IDX_DIM, LANES), any_spec, any_spec, any_spec,
                  pl.BlockSpec((LANES, LANES), lambda b, pt: (0, 0))],
        out_specs=per_db(rows, HEAD_DIM),
        scratch_shapes=[pltpu.VMEM((2, n_pages, IDX_DIM, PAGE_SIZE), F32),
                        pltpu.VMEM((n_pages, ATT_KV_HEADS, HEAD_DIM, PAGE_SIZE), F32),
                        pltpu.VMEM((n_pages, ATT_KV_HEADS, HEAD_DIM, PAGE_SIZE), F32),
                        pltpu.VMEM((IDX_DIM, n_pages * PAGE_SIZE), BF16),
                        pltpu.VMEM((ATT_KV_HEADS, HEAD_DIM, n_pages * PAGE_SIZE), BF16),
                        pltpu.VMEM((ATT_KV_HEADS, HEAD_DIM, n_pages * PAGE_SIZE), BF16),
                        pltpu.SemaphoreType.DMA((4,)),
                        pltpu.VMEM((t, n_keys), F32),
                        pltpu.VMEM((t, n_keys), F32)])
    return pl.pallas_call(
        _dsa_sample_kernel,
        grid_spec=grid_spec,
        out_shape=jax.ShapeDtypeStruct((db, rows, HEAD_DIM), F32),
        compiler_params=_cparams("arbitrary"),
        name="dsa_sample",
    )(page_table, qs, iqs, ws, knew, vnew, iknew, cache_ik, cache_k, cache_v, tri)


def _retention_kernel(has_init, rq_ref, rk_ref, rv_ref, sg_ref, decay_ref, qdec_ref, kdec_ref, gst_ref, *rest):
    if has_init:
        init_ref, o_ref, st_ref, state = rest
    else:
        o_ref, st_ref, state = rest
    c = pl.program_id(1)

    @pl.when(c == 0)
    def _():
        if has_init:
            state[...] = init_ref[...]
        else:
            state[...] = jnp.zeros_like(state)

    nt = (((1,), (1,)), ((), ()))
    tn = (((0,), (0,)), ((), ()))
    rows = rq_ref.shape[0]
    lane = lax.broadcasted_iota(I32, (rows, LANES), 1)
    for p in range(RET_HEADS // 2):
        sl = slice(p * LANES, (p + 1) * LANES)
        qp = rq_ref[:, sl]
        kp = rk_ref[:, sl]
        qd = (qp.astype(F32) * qdec_ref[:, sl]).astype(BF16)
        kd = (kp.astype(F32) * kdec_ref[:, sl]).astype(BF16)
        s_old = state[p]
        s_old_b = s_old.astype(BF16)
        s_new = s_old * gst_ref[p]
        for e in range(2):
            h = 2 * p + e
            hs = slice(h * LANES, (h + 1) * LANES)
            mine = jnp.where((lane >= e * RET_DK) & (lane < (e + 1) * RET_DK), 1.0, 0.0).astype(BF16)
            sc = lax.dot_general(qp * mine, kp, nt, preferred_element_type=F32) * decay_ref[h]
            vh = rv_ref[:, hs]
            o = (jnp.dot(sc.astype(BF16), vh, preferred_element_type=F32)
                 + jnp.dot(qd * mine, s_old_b, preferred_element_type=F32))
            s_new = s_new + lax.dot_general(kd * mine, vh, tn, preferred_element_type=F32)
            mu = jnp.mean(o, axis=-1, keepdims=True)
            var = jnp.mean(jnp.square(o - mu), axis=-1, keepdims=True)
            on = (o - mu) * lax.rsqrt(var + GN_EPS)
            o_ref[:, hs] = (on * sg_ref[:, hs].astype(F32)).astype(BF16)
        state[p] = s_new

    @pl.when(c == pl.num_programs(1) - 1)
    def _():
        st_ref[...] = state[...]


def _retention(rq, rk, rv, sg, tables, init, batch, n_chunks):
    decay, qdec, kdec, gst = tables
    cr = RET_CHUNK
    rowspec = lambda n: pl.BlockSpec((cr, n), lambda b, c: (b * n_chunks + c, 0))
    const = lambda shape: pl.BlockSpec(shape, lambda b, c: (0,) * len(shape))
    st_spec = pl.BlockSpec((None, RET_HEADS // 2, LANES, LANES), lambda b, c: (b, 0, 0, 0))
    in_specs = [rowspec(512), rowspec(512), rowspec(RET_OUT), rowspec(RET_OUT),
                const(decay.shape), const(qdec.shape), const(kdec.shape), const(gst.shape)]
    args = [rq, rk, rv, sg, decay, qdec, kdec, gst]
    if init is not None:
        in_specs.append(st_spec)
        args.append(init)
    return pl.pallas_call(
        functools.partial(_retention_kernel, init is not None),
        grid=(batch, n_chunks),
        in_specs=in_specs,
        out_specs=[rowspec(RET_OUT), st_spec],
        out_shape=[jax.ShapeDtypeStruct((batch * n_chunks * cr, RET_OUT), BF16),
                   jax.ShapeDtypeStruct((batch, RET_HEADS // 2, LANES, LANES), F32)],
        scratch_shapes=[pltpu.VMEM((RET_HEADS // 2, LANES, LANES), F32)],
        compiler_params=_cparams("parallel", "arbitrary"),
        name="retention",
    )(*args)


def _retention_tables(c_eff):
    lg = jnp.log(1.0 - 2.0 ** (-5.0 - jnp.arange(RET_HEADS, dtype=F32)))
    i = jnp.arange(RET_CHUNK, dtype=F32)
    diff = i[:, None] - i[None, :]
    decay = jnp.where(diff >= 0, jnp.exp(jnp.maximum(diff, 0.0)[None] * lg[:, None, None]), 0.0)
    q_decay = jnp.exp((i + 1.0)[:, None] * lg[None, :])
    k_decay = jnp.exp((c_eff - 1.0 - i)[:, None] * lg[None, :])
    qdec = jnp.repeat(q_decay, RET_DK, axis=1)
    kdec = jnp.repeat(k_decay, RET_DK, axis=1)
    g_state = jnp.exp(c_eff * lg)
    gst = jnp.broadcast_to(jnp.repeat(g_state, RET_DK).reshape(RET_HEADS // 2, LANES, 1),
                           (RET_HEADS // 2, LANES, LANES))
    return decay, qdec, kdec, gst


def _pack_bf16_pairs(lo, hi):
    return pltpu.pack_elementwise([lo, hi], packed_dtype=BF16)


def _unpack_bf16_pairs(words):
    return tuple(pltpu.unpack_elementwise(words, index=j, packed_dtype=BF16, unpacked_dtype=F32).astype(BF16)
                 for j in range(2))


def _post_kernel(a_ref, r_ref, sga_ref, sgb_ref, x_ref, mod_ref, wpa_ref, wpb_ref, wo_ref, g_ref, wr_ref, br_ref,
                 ltri_ref, cnt0_ref, x1_ref, h2w_ref, idx_ref, gate_ref, rank_ref, cnt_ref, cnt):
    @pl.when(pl.program_id(0) == 0)
    def _():
        cnt[...] = cnt0_ref[...]

    pa = jnp.dot(a_ref[...], wpa_ref[...], preferred_element_type=F32)
    pb = jnp.dot(r_ref[...], wpb_ref[...], preferred_element_type=F32)
    merged = sga_ref[...].astype(F32) * pa + sgb_ref[...].astype(F32) * pb
    gt1 = mod_ref[:, 2 * D_MODEL:3 * D_MODEL]
    x1 = x_ref[...] + gt1 * jnp.dot(merged.astype(BF16), wo_ref[...], preferred_element_type=F32)
    x1_ref[...] = x1
    ms = jnp.mean(x1 * x1, axis=-1, keepdims=True)
    y = x1 * lax.rsqrt(ms + NORM_EPS) * g_ref[...]
    h2 = y * (1.0 + mod_ref[:, 4 * D_MODEL:5 * D_MODEL]) + mod_ref[:, 3 * D_MODEL:4 * D_MODEL]
    half = D_MODEL // 2
    h2w_ref[...] = _pack_bf16_pairs(h2[:, :half], h2[:, half:])
    h_hi = h2.astype(BF16)
    h_lo = (h2 - h_hi.astype(F32)).astype(BF16)
    logits = (jnp.dot(h_hi, wr_ref[0], preferred_element_type=F32) + jnp.dot(h_hi, wr_ref[1], preferred_element_type=F32)
              + jnp.dot(h_lo, wr_ref[0], preferred_element_type=F32) + br_ref[...])
    lane = lax.broadcasted_iota(I32, logits.shape, 1).astype(F32)
    idx_out = jnp.zeros(logits.shape, F32)
    val_out = jnp.zeros(logits.shape, F32)
    chosen = []
    top = None
    for j in range(TOP_K):
        m = jnp.max(logits, axis=1, keepdims=True)
        am = jnp.min(jnp.where(logits == m, lane, float(LANES)), axis=1, keepdims=True)
        if j == 0:
            top = m
        idx_out = jnp.where(lane == j, am, idx_out)
        val_out = jnp.where(lane == j, jnp.exp(m - top), val_out)
        chosen.append(lane == am)
        logits = jnp.where(chosen[-1], -jnp.inf, logits)
    idx_ref[...] = jnp.transpose(idx_out)[0:8, :].astype(I32)
    gate_ref[...] = jnp.transpose(val_out / jnp.sum(val_out, axis=1, keepdims=True))[0:8, :]
    onehot = jnp.where(chosen[0] | chosen[1] | chosen[2] | chosen[3], 1.0, 0.0)
    before = jnp.dot(ltri_ref[...], onehot.astype(BF16), preferred_element_type=F32) + cnt[...]
    rank_out = jnp.zeros(logits.shape, F32)
    for j in range(TOP_K):
        rj = jnp.sum(jnp.where(chosen[j], before, 0.0), axis=1, keepdims=True)
        rank_out = jnp.where(lane == j, rj, rank_out)
    rank_ref[...] = jnp.transpose(rank_out)[0:8, :].astype(I32)
    cnt[...] = cnt[...] + jnp.sum(onehot, axis=0, keepdims=True)
    cnt_ref[...] = cnt[...]


def _post(a, r, sga, sgb, x, mod3, wpa, wpb, wo, g, wr, br, cnt0, tm, blocks_per_mod):
    t = x.shape[0]
    mod_rows = mod3.shape[1]
    ar = jnp.arange(tm)
    ltri = (ar[None, :] < ar[:, None]).astype(BF16)
    row = lambda n: pl.BlockSpec((tm, n), lambda i: (i, 0))
    col8 = pl.BlockSpec((8, tm), lambda i: (0, i))
    const = lambda a_: pl.BlockSpec(a_.shape, lambda i: (0,) * a_.ndim)
    return pl.pallas_call(
        _post_kernel,
        grid=(t // tm,),
        in_specs=[row(ATT_OUT), row(RET_OUT), row(D_MODEL), row(D_MODEL), row(D_MODEL),
                  pl.BlockSpec((None, mod_rows, 6 * D_MODEL), lambda i: (i // blocks_per_mod, 0, 0)),
                  const(wpa), const(wpb), const(wo), const(g), const(wr), const(br), const(ltri), const(cnt0)],
        out_specs=[row(D_MODEL), row(D_MODEL // 2), col8, col8, col8,
                   pl.BlockSpec((1, LANES), lambda i: (0, 0))],
        out_shape=[jax.ShapeDtypeStruct((t, D_MODEL), F32), jax.ShapeDtypeStruct((t, D_MODEL // 2), jnp.uint32),
                   jax.ShapeDtypeStruct((8, t), I32), jax.ShapeDtypeStruct((8, t), F32),
                   jax.ShapeDtypeStruct((8, t), I32), jax.ShapeDtypeStruct((1, LANES), F32)],
        scratch_shapes=[pltpu.VMEM((1, LANES), F32)],
        compiler_params=_cparams("arbitrary"),
        name="post",
    )(a, r, sga, sgb, x, mod3, wpa, wpb, wo, g, wr, br, ltri, cnt0)


def _segment_copies(seg_ref, blk, make_copy):
    for e in range(N_EXPERTS + 1):
        base = (blk * (N_EXPERTS + 1) + e) * 3
        loc, glob, n = seg_ref[base], seg_ref[base + 1], seg_ref[base + 2]
        for k in range(SEG_BITS):
            size = SEG_ALIGN << k

            @pl.when((n & size) != 0)
            def _():
                done = n & (size - 1)
                make_copy(pl.multiple_of(loc + done, SEG_ALIGN), pl.multiple_of(glob + done, SEG_ALIGN),
                          size).start()


def _local_order_matrix(lpos_ref, values, rows):
    lp = lpos_ref[...]
    r = lax.broadcasted_iota(I32, (rows, lp.shape[1]), 0)
    m = jnp.zeros((rows, lp.shape[1]), F32)
    for j in range(TOP_K):
        m = jnp.where(r == lp[j:j + 1, :], values(j), m)
    return m


def _dispatch_kernel(seg_ref, lpos_ref, h2w_ref, xs_in, xs_out, buf, sems):
    del xs_in
    i = pl.program_id(0)
    slot = i % 2
    rows = buf.shape[1]

    def wait_slot(s):
        pltpu.make_async_copy(buf.at[s], xs_out.at[pl.ds(0, rows)], sems.at[s]).wait()

    @pl.when(i >= 2)
    def _():
        wait_slot(slot)

    x = jnp.concatenate(_unpack_bf16_pairs(h2w_ref[...]), axis=1)
    perm = _local_order_matrix(lpos_ref, lambda j: 1.0, rows).astype(BF16)
    xs = jnp.dot(perm, x, preferred_element_type=F32)
    half = D_MODEL // 2
    buf[slot] = _pack_bf16_pairs(xs[:, :half], xs[:, half:])
    _segment_copies(seg_ref, i, lambda loc, glob, size: pltpu.make_async_copy(
        buf.at[slot, pl.ds(loc, size)], xs_out.at[pl.ds(glob, size)], sems.at[slot]))

    @pl.when(i == pl.num_programs(0) - 1)
    def _():
        wait_slot(slot)

        @pl.when(i >= 1)
        def _():
            wait_slot(1 - slot)


def _dispatch(seg, lpos8, h2w, x_sorted):
    t = h2w.shape[0]
    tm = TOK_BLOCK
    grid_spec = pltpu.PrefetchScalarGridSpec(
        num_scalar_prefetch=1,
        grid=(t // tm,),
        in_specs=[pl.BlockSpec((8, tm), lambda i, s: (0, i)),
                  pl.BlockSpec((tm, D_MODEL // 2), lambda i, s: (i, 0)), pl.BlockSpec(memory_space=pl.ANY)],
        out_specs=pl.BlockSpec(memory_space=pl.ANY),
        scratch_shapes=[pltpu.VMEM((2, SEG_ROWS, D_MODEL // 2), jnp.uint32), pltpu.SemaphoreType.DMA((2,))])
    return pl.pallas_call(
        _dispatch_kernel,
        grid_spec=grid_spec,
        out_shape=jax.ShapeDtypeStruct(x_sorted.shape, x_sorted.dtype),
        input_output_aliases={3: 0},
        compiler_params=_cparams("arbitrary"),
        name="dispatch",
    )(seg, lpos8, h2w, x_sorted)


def _moe_kernel(be_ref, first_ref, nused_ref, x_ref, wgu_ref, bgu_ref, wd_ref, bd_ref, o_ref, wgu_b, wd_b):
    i = pl.program_id(0)

    @pl.when(first_ref[i] == 1)
    def _():
        wgu_b[...] = wgu_ref[...].astype(BF16)
        wd_b[...] = wd_ref[...].astype(BF16)

    @pl.when(i < nused_ref[0])
    def _():
        x = jnp.concatenate(_unpack_bf16_pairs(x_ref[...]), axis=1)
        gu = jnp.dot(x, wgu_b[...], preferred_element_type=F32) + bgu_ref[...]
        g = jnp.minimum(gu[:, :D_FF], SWIGLU_LIMIT)
        u = jnp.clip(gu[:, D_FF:], -SWIGLU_LIMIT, SWIGLU_LIMIT)
        act = (u + 1.0) * (g * jax.nn.sigmoid(SWIGLU_ALPHA * g))
        o_ref[...] = jnp.dot(act.astype(BF16), wd_b[...], preferred_element_type=F32) + bd_ref[...]

    @pl.when(i >= nused_ref[0])
    def _():
        o_ref[...] = jnp.zeros_like(o_ref)


def _moe(blk_expert, blk_first, n_used, x_sorted, w_gate_up, b_gate_up, w_down, b_down):
    n_rows = blk_expert.shape[0] * MOE_ROWS
    grid_spec = pltpu.PrefetchScalarGridSpec(
        num_scalar_prefetch=3,
        grid=(n_rows // MOE_ROWS,),
        in_specs=[pl.BlockSpec((MOE_ROWS, D_MODEL // 2), lambda i, be, bf, nu: (i, 0)),
                  pl.BlockSpec((None, D_MODEL, 2 * D_FF), lambda i, be, bf, nu: (be[i], 0, 0)),
                  pl.BlockSpec((None, 1, 2 * D_FF), lambda i, be, bf, nu: (be[i], 0, 0)),
                  pl.BlockSpec((None, D_FF, D_MODEL), lambda i, be, bf, nu: (be[i], 0, 0)),
                  pl.BlockSpec((None, 1, D_MODEL), lambda i, be, bf, nu: (be[i], 0, 0))],
        out_specs=pl.BlockSpec((MOE_ROWS, D_MODEL), lambda i, be, bf, nu: (i, 0)),
        scratch_shapes=[pltpu.VMEM((D_MODEL, 2 * D_FF), BF16), pltpu.VMEM((D_FF, D_MODEL), BF16)])
    return pl.pallas_call(
        _moe_kernel,
        grid_spec=grid_spec,
        out_shape=jax.ShapeDtypeStruct((n_rows, D_MODEL), F32),
        compiler_params=_cparams("arbitrary"),
        name="moe",
    )(blk_expert, blk_first, n_used, x_sorted, w_gate_up, b_gate_up.reshape(N_EXPERTS, 1, -1),
      w_down, b_down.reshape(N_EXPERTS, 1, -1))


def _final_kernel(block_offset, seg_ref, x1_ref, mod_ref, lpos_ref, gate_ref, g_ref, rows_hbm, y_ref, buf, sems):
    i = pl.program_id(0)
    n = pl.num_programs(0)
    rows = buf.shape[1]

    def issue(blk, slot):
        _segment_copies(seg_ref, blk + block_offset, lambda loc, glob, size: pltpu.make_async_copy(
            rows_hbm.at[pl.ds(glob, size)], buf.at[slot, pl.ds(loc, size)], sems.at[slot]))

    @pl.when(i == 0)
    def _():
        issue(0, 0)

    @pl.when(i + 1 < n)
    def _():
        issue(i + 1, (i + 1) % 2)

    slot = i % 2
    pltpu.make_async_copy(rows_hbm.at[pl.ds(0, rows)], buf.at[slot], sems.at[slot]).wait()
    gate = gate_ref[...]
    gmat = _local_order_matrix(lpos_ref, lambda j: gate[j:j + 1, :], rows).astype(BF16)
    moe = lax.dot_general(gmat, buf[slot].astype(BF16), (((0,), (0,)), ((), ())), preferred_element_type=F32)
    x2 = x1_ref[...] + mod_ref[:, 5 * D_MODEL:6 * D_MODEL] * moe
    ms = jnp.mean(x2 * x2, axis=-1, keepdims=True)
    y_ref[...] = x2 * lax.rsqrt(ms + NORM_EPS) * g_ref[...]


def _final(seg, x1, mod3, lpos8, gate8, rows_out, g, blocks_per_mod, block_offset):
    t = x1.shape[0]
    tm = TOK_BLOCK
    mod_rows = mod3.shape[1]
    row = lambda n: pl.BlockSpec((tm, n), lambda i, s: (i, 0))
    col8 = pl.BlockSpec((8, tm), lambda i, s: (0, i + block_offset))
    grid_spec = pltpu.PrefetchScalarGridSpec(
        num_scalar_prefetch=1,
        grid=(t // tm,),
        in_specs=[row(D_MODEL),
                  pl.BlockSpec((None, mod_rows, 6 * D_MODEL), lambda i, s: (i // blocks_per_mod, 0, 0)),
                  col8, col8,
                  pl.BlockSpec((1, D_MODEL), lambda i, s: (0, 0)),
                  pl.BlockSpec(memory_space=pl.ANY)],
        out_specs=row(D_MODEL),
        scratch_shapes=[pltpu.VMEM((2, SEG_ROWS, D_MODEL), F32), pltpu.SemaphoreType.DMA((2,))])
    return pl.pallas_call(
        functools.partial(_final_kernel, block_offset),
        grid_spec=grid_spec,
        out_shape=jax.ShapeDtypeStruct((t, D_MODEL), F32),
        compiler_params=_cparams("arbitrary"),
        name="final",
    )(seg, x1, mod3, lpos8, gate8, g, rows_out)


def _rope_tables(pos, rot_dim, theta, head_dim):
    half = rot_dim // 2
    inv = theta ** (-jnp.arange(half, dtype=F32) * (2.0 / rot_dim))
    ang = pos.astype(F32)[:, None] * inv[None, :]
    cos, sin = jnp.cos(ang), jnp.sin(ang)
    n = pos.shape[0]
    rest = head_dim - rot_dim
    zh = jnp.zeros((n, half), F32)
    c = jnp.concatenate([cos, cos, jnp.ones((n, rest), F32)], axis=1)
    sa = jnp.concatenate([-sin, zh, jnp.zeros((n, rest), F32)], axis=1)
    sb = jnp.concatenate([zh, sin, jnp.zeros((n, rest), F32)], axis=1)
    rep = LANES // head_dim
    return tuple(jnp.tile(a, (1, rep)) for a in (c, sa, sb))


def _pack_w_in(w_in):
    offs = np.cumsum((0,) + IN_SPLITS)
    part = lambda j: w_in[:, offs[j]:offs[j + 1]]
    zero = lambda n: jnp.zeros((D_MODEL, n), w_in.dtype)
    group = ATT_HEADS // ATT_KV_HEADS
    cols = []
    wq = part(0)
    for h in range(ATT_HEADS):
        wh = wq[:, h * HEAD_DIM:(h + 1) * HEAD_DIM]
        cols += [wh, zero(HEAD_DIM)] if h // group == 0 else [zero(HEAD_DIM), wh]
    wiq = part(3)
    for h in range(IDX_HEADS):
        cols += [wiq[:, h * IDX_DIM:(h + 1) * IDX_DIM], zero(LANES - IDX_DIM)]
    cols += [part(1), part(2), part(4), part(5), zero(LANES - IDX_DIM - IDX_HEADS)]
    cols += [part(j) for j in range(6, 12)]
    return jnp.concatenate(cols, axis=1).astype(BF16)


def _heads_major(a, db, t, width):
    heads = a.shape[1] // width
    return a.reshape(db, t, heads, width).transpose(0, 2, 1, 3).reshape(db, heads * t, width)


def _pad_rows(a, db, t, rows):
    return jnp.pad(a.reshape(db, t, -1), ((0, 0), (0, rows - t), (0, 0)))


def kernel(x_prompt, x_sample, cache_k, cache_v, cache_ik, state_ret, page_table, c_prompt, c_sample, norm_mix_g, norm_ffn_g, norm_final_g, w_ada, b_ada, w_in, w_branch_a, w_branch_b, w_out, w_router, b_router, w_gate_up, b_gate_up, w_down, b_down):
    batch, seq, _ = x_prompt.shape
    db, dt, _ = x_sample.shape
    assert w_in.shape[0] == 1, "one layer"
    tp, ts = batch * seq, db * dt
    xp = x_prompt.reshape(tp, D_MODEL)
    xs = x_sample.reshape(ts, D_MODEL)

    mod = _adaln(jnp.concatenate([c_prompt, c_sample], axis=0), w_ada[0], b_ada[0])
    mod_p = mod[:batch].reshape(batch, 1, 6 * D_MODEL)
    mod_s = jnp.repeat(mod[batch:], dt, axis=0).reshape(ts // TOK_BLOCK, TOK_BLOCK, 6 * D_MODEL)
    bpm_p = seq // TOK_BLOCK
    tmp = PROMPT_TOK_BLOCK
    bpm_big = seq // tmp

    w_packed = _pack_w_in(w_in[0])
    pos_p = jnp.arange(seq)
    pos_s = PAST_LEN + (jnp.arange(TOK_BLOCK) % dt)
    g_mix = norm_mix_g[0].reshape(1, D_MODEL)
    outs_p = _inproj(xp, mod_p, g_mix, w_packed, _rope_tables(pos_p, ROPE_DIM, ROPE_THETA, HEAD_DIM),
                     _rope_tables(pos_p, RET_DK, RET_THETA, RET_DK), TOK_BLOCK, bpm_p, bpm_p)
    outs_s = _inproj(xs, mod_s, g_mix, w_packed, _rope_tables(pos_s, ROPE_DIM, ROPE_THETA, HEAD_DIM),
                     _rope_tables(pos_s, RET_DK, RET_THETA, RET_DK), TOK_BLOCK, 1, 1)
    (q_p, k_p, v_p, kb_p, vb_p, iq_p, ikw_p, ikb_p, rq_p, rk_p, rv_p, sg_p, sga_p, sgb_p) = outs_p
    (q_s, k_s, v_s, kb_s, vb_s, iq_s, ikw_s, ikb_s, rq_s, rk_s, rv_s, sg_s, sga_s, sgb_s) = outs_s

    tri = (jnp.arange(LANES)[:, None] < jnp.arange(LANES)[None, :]).astype(BF16)

    a_p = _dsa_prompt(q_p, iq_p, ikw_p, kb_p, vb_p, ikb_p, tri, batch, seq)
    group = ATT_HEADS // ATT_KV_HEADS
    q4 = q_s.reshape(db, dt, ATT_HEADS, LANES)
    qs = jnp.stack([q4[:, :, h, (h // group) * HEAD_DIM:(h // group + 1) * HEAD_DIM] for h in range(ATT_HEADS)],
                   axis=1).reshape(db, ATT_HEADS * dt, HEAD_DIM)
    iqs = _heads_major(iq_s, db, dt, LANES)[:, :, :IDX_DIM]
    ws = _heads_major(ikw_s[:, IDX_DIM:IDX_DIM + IDX_HEADS] * (IDX_HEADS ** -0.5), db, dt, 1)
    new_t = lambda a: jnp.pad(a.reshape(db, dt, ATT_KV_HEADS, HEAD_DIM).transpose(0, 2, 3, 1),
                              ((0, 0), (0, 0), (0, 0), (0, LANES - dt)))
    iknew_t = jnp.pad(ikb_s[:, :IDX_DIM].reshape(db, dt, IDX_DIM).transpose(0, 2, 1),
                      ((0, 0), (0, 0), (0, LANES - dt)))
    o_s = _dsa_sample(page_table, qs, iqs, ws, new_t(kb_s), new_t(vb_s), iknew_t,
                      cache_ik[0].transpose(0, 2, 1), cache_k[0].transpose(0, 2, 3, 1),
                      cache_v[0].transpose(0, 2, 3, 1), tri, dt)
    a_s = o_s.reshape(db, ATT_HEADS, dt, HEAD_DIM).transpose(0, 2, 1, 3).reshape(ts, ATT_OUT).astype(BF16)

    r_p, st_p = _retention(rq_p, rk_p, rv_p, sg_p, _retention_tables(float(RET_CHUNK)), None, batch,
                           seq // RET_CHUNK)
    pad = lambda a: _pad_rows(a, db, dt, RET_CHUNK).reshape(db * RET_CHUNK, -1)
    r_s, st_s = _retention(pad(rq_s), pad(rk_s), pad(rv_s), pad(sg_s), _retention_tables(float(dt)),
                           state_ret[0].reshape(db, RET_HEADS // 2, LANES, LANES), db, 1)
    r_s = r_s.reshape(db, RET_CHUNK, RET_OUT)[:, :dt].reshape(ts, RET_OUT)

    wr = jnp.pad(w_router[0], ((0, 0), (0, LANES - N_EXPERTS)))
    wr_hi = wr.astype(BF16)
    wr = jnp.stack([wr_hi, (wr - wr_hi.astype(F32)).astype(BF16)])
    br = jnp.concatenate([b_router[0], jnp.full((LANES - N_EXPERTS,), -jnp.inf, F32)]).reshape(1, LANES)
    post_w = (w_branch_a[0].astype(BF16), w_branch_b[0].astype(BF16), w_out[0].astype(BF16),
              norm_ffn_g[0].reshape(1, D_MODEL), wr, br)
    x1_p, h2w_p, idx_p, gate_p, rank_p, cnt_p = _post(a_p, r_p, sga_p, sgb_p, xp, mod_p, *post_w,
                                                      jnp.zeros((1, LANES), F32), tmp, bpm_big)
    x1_s, h2w_s, idx_s, gate_s, rank_s, cnt_all = _post(a_s, r_s, sga_s, sgb_s, xs, mod_s, *post_w, cnt_p,
                                                        TOK_BLOCK, 1)

    del cnt_all
    n_tok, lb = tp + ts, TOK_BLOCK
    nb = n_tok // lb
    idx4 = jnp.concatenate([idx_p[:TOP_K], idx_s[:TOP_K]], axis=1)
    rank4 = jnp.concatenate([rank_p[:TOP_K], rank_s[:TOP_K]], axis=1)
    gate8 = jnp.concatenate([gate_p, gate_s], axis=1)
    hit = idx4[None] == jnp.arange(N_EXPERTS, dtype=I32)[:, None, None]
    bc = jnp.sum(hit.reshape(N_EXPERTS, TOP_K, nb, lb).astype(I32), axis=(1, 3)).T
    run = (bc + SEG_ALIGN - 1) // SEG_ALIGN * SEG_ALIGN
    padded = (jnp.sum(run, axis=0) + MOE_ROWS - 1) // MOE_ROWS * MOE_ROWS
    pend = jnp.cumsum(padded)
    pstart = pend - padded
    loc = jnp.cumsum(run, axis=1) - run
    glob = pstart[None, :] + jnp.cumsum(run, axis=0) - run
    carry = jnp.cumsum(bc, axis=0) - bc
    n_blocks = -(-(n_tok * TOP_K + nb * N_EXPERTS * (SEG_ALIGN - 1)) // MOE_ROWS) + N_EXPERTS
    n_rows = n_blocks * MOE_ROWS
    used = jnp.sum(run, axis=1, keepdims=True)
    runs = jnp.stack([loc, glob, run], axis=-1)
    seg_table = lambda spare: jnp.concatenate(
        [runs, jnp.concatenate([used, spare, SEG_ROWS - used], axis=1)[:, None, :]], axis=1).reshape(-1).astype(I32)
    spare_out = n_rows + (jnp.arange(nb, dtype=I32)[:, None] % 2) * (SEG_ROWS // 2)
    seg_out, seg_in = seg_table(spare_out), seg_table(jnp.zeros((nb, 1), I32))
    shift = jnp.repeat(loc - carry, lb, axis=0).T
    lpos4 = rank4 + jnp.sum(jnp.where(hit, shift[:, None, :], 0), axis=0)
    lpos8 = jnp.pad(lpos4, ((0, 8 - TOP_K), (0, 0))).astype(I32)
    blk_start = jnp.arange(n_blocks, dtype=I32) * MOE_ROWS
    blk_expert = jnp.minimum(jnp.sum((blk_start[:, None] >= pend[None, :]).astype(I32), axis=1), N_EXPERTS - 1)
    blk_first = jnp.concatenate([jnp.ones((1,), I32), (blk_expert[1:] != blk_expert[:-1]).astype(I32)])
    n_used = (pend[-1] // MOE_ROWS).astype(I32).reshape(1)
    x_sorted = jnp.zeros((n_rows + SEG_ROWS, D_MODEL // 2), jnp.uint32)
    x_sorted = _dispatch(seg_out, lpos8, jnp.concatenate([h2w_p, h2w_s], axis=0), x_sorted)
    rows_out = _moe(blk_expert, blk_first, n_used, x_sorted, w_gate_up[0], b_gate_up[0], w_down[0], b_down[0])

    g_final = norm_final_g.reshape(1, D_MODEL)
    y_p = _final(seg_in, x1_p, mod_p, lpos8, gate8, rows_out, g_final, bpm_p, 0)
    y_s = _final(seg_in, x1_s, mod_s, lpos8, gate8, rows_out, g_final, 1, tp // lb)

    kv_shape = lambda b, s: (1, b, s, ATT_KV_HEADS, HEAD_DIM)
    st_shape = lambda b: (1, b, RET_HEADS, RET_DK, RET_DV)
    return (y_p.reshape(batch, seq, D_MODEL), y_s.reshape(db, dt, D_MODEL),
            k_p.reshape(kv_shape(batch, seq)), v_p.reshape(kv_shape(batch, seq)),
            ikw_p[:, :IDX_DIM].reshape(1, batch, seq, IDX_DIM), st_p.reshape(st_shape(batch)),
            k_s.reshape(kv_shape(db, dt)), v_s.reshape(kv_shape(db, dt)),
            ikw_s[:, :IDX_DIM].reshape(1, db, dt, IDX_DIM), st_s.reshape(st_shape(db)))
```

```python
import functools

import jax
import jax.numpy as jnp
import numpy as np
from jax import lax
from jax.experimental import pallas as pl
from jax.experimental.pallas import tpu as pltpu

F32 = jnp.float32
BF16 = jnp.bfloat16
I32 = jnp.int32

D_MODEL = 1024
PAST_LEN = 8192
PAGE_SIZE = 128
ATT_HEADS = 8
ATT_KV_HEADS = 2
HEAD_DIM = 64
ROPE_DIM = HEAD_DIM // 4
ROPE_THETA = 500000.0
IDX_HEADS = 8
IDX_DIM = 64
IDX_ROPE_DIM = IDX_DIM // 4
TOPK_MAX = 256
RET_HEADS = 8
RET_DK = 64
RET_DV = 128
RET_THETA = 10000.0
RET_CHUNK = 128
N_EXPERTS = 32
TOP_K = 4
D_FF = D_MODEL
SWIGLU_LIMIT = 7.0
SWIGLU_ALPHA = 1.702
NORM_EPS = 1e-6
GN_EPS = 1e-5
ATT_OUT = ATT_HEADS * HEAD_DIM
RET_OUT = RET_HEADS * RET_DV
IN_SPLITS = (ATT_HEADS * HEAD_DIM, ATT_KV_HEADS * HEAD_DIM, ATT_KV_HEADS * HEAD_DIM,
             IDX_HEADS * IDX_DIM, IDX_DIM, IDX_HEADS,
             RET_HEADS * RET_DK, RET_HEADS * RET_DK, RET_OUT, RET_OUT, D_MODEL, D_MODEL)

LANES = 128
MASK_NEG = -1e30
FLT_MAX = 3.4028234663852886e38
SELECT_UNROLL = 4
CAUSAL_VARIANTS = 8
IDX_KEY_CHUNK = 256
VMEM_LIMIT = 56 * 1024 * 1024

TOK_BLOCK = 256
PROMPT_TOK_BLOCK = 512
Q_BLOCK = 128
IDX_Q_ROWS = 128
MOE_ROWS = 512
SEG_ALIGN = 8
SEG_BITS = 6
SEG_ROWS = TOP_K * TOK_BLOCK + N_EXPERTS * SEG_ALIGN

_W_GROUPS = (("q", 1024), ("iq", 1024), ("kvi", 384), ("rq", 512), ("rk", 512),
             ("rv", 1024), ("rg", 1024), ("ga", 1024), ("gb", 1024))
PROJ_COLS = 512
_W_OFF = {}
_off = 0
for _n, _w in _W_GROUPS:
    _W_OFF[_n] = (_off, _w)
    _off += _w
W_COLS = _off


def _cparams(*sem):
    return pltpu.CompilerParams(dimension_semantics=sem, vmem_limit_bytes=VMEM_LIMIT)


def _adaln_kernel(c_ref, w_ref, b_ref, o_ref):
    c = c_ref[...]
    s = c * jax.nn.sigmoid(c)
    o_ref[...] = jnp.dot(s, w_ref[...], preferred_element_type=F32, precision=lax.Precision.HIGHEST) + b_ref[...]


def _adaln(c_all, w_ada, b_ada):
    n = c_all.shape[0]
    nb = 1536
    return pl.pallas_call(
        _adaln_kernel,
        grid=(6 * D_MODEL // nb,),
        in_specs=[pl.BlockSpec((n, D_MODEL), lambda j: (0, 0)),
                  pl.BlockSpec((D_MODEL, nb), lambda j: (0, j)),
                  pl.BlockSpec((1, nb), lambda j: (0, j))],
        out_specs=pl.BlockSpec((n, nb), lambda j: (0, j)),
        out_shape=jax.ShapeDtypeStruct((n, 6 * D_MODEL), F32),
        compiler_params=_cparams("arbitrary"),
        name="adaln",
    )(c_all, w_ada, b_ada.reshape(1, -1))


def _rope_slab(z, c, sa, sb, half):
    return z * c + pltpu.roll(z, LANES - half, 1) * sa + pltpu.roll(z, half, 1) * sb


def _inproj_kernel(x_ref, mod_ref, g_ref, w_ref, ca_ref, saa_ref, sba_ref, cr_ref, sar_ref, sbr_ref,
                   q_ref, k_ref, v_ref, kb_ref, vb_ref, iq_ref, ikw_ref, ikb_ref,
                   rq_ref, rk_ref, rv_ref, sg_ref, sga_ref, sgb_ref):
    x = x_ref[...]
    ms = jnp.mean(x * x, axis=-1, keepdims=True)
    y = x * lax.rsqrt(ms + NORM_EPS) * g_ref[...]
    h = (y * (1.0 + mod_ref[:, D_MODEL:2 * D_MODEL]) + mod_ref[:, 0:D_MODEL]).astype(BF16)

    def slabs(name):
        c0, width = _W_OFF[name]
        step = min(width, PROJ_COLS)
        for j in range(width // step):
            z = jnp.dot(h, w_ref[:, c0 + j * step:c0 + (j + 1) * step], preferred_element_type=F32)
            for s in range(step // LANES):
                yield j * (step // LANES) + s, z[:, s * LANES:(s + 1) * LANES]

    ca, saa, sba = ca_ref[...], saa_ref[...], sba_ref[...]
    cr, sar, sbr = cr_ref[...], sar_ref[...], sbr_ref[...]
    att_half, ret_half = ROPE_DIM // 2, RET_DK // 2
    lane = lax.broadcasted_iota(I32, (x.shape[0], LANES), 1)
    sl = lambda s: slice(s * LANES, (s + 1) * LANES)

    for s, z in slabs("q"):
        q_ref[:, sl(s)] = (_rope_slab(z, ca, saa, sba, att_half) * 0.125).astype(BF16)
    for s, z in slabs("iq"):
        iq_ref[:, sl(s)] = (_rope_slab(z, ca, saa, sba, att_half) * 0.125).astype(BF16)
    (_, zk), (_, zv), (_, zi) = slabs("kvi")
    kk = _rope_slab(zk, ca, saa, sba, att_half)
    for j in range(ATT_KV_HEADS):
        k_ref[:, j, :] = kk[:, j * HEAD_DIM:(j + 1) * HEAD_DIM]
        v_ref[:, j, :] = zv[:, j * HEAD_DIM:(j + 1) * HEAD_DIM]
    kb_ref[...] = kk.astype(BF16)
    vb_ref[...] = zv.astype(BF16)
    zr = _rope_slab(zi, ca, saa, sba, att_half)
    ikw_ref[...] = jnp.where(lane < IDX_DIM, zr, zi)
    ikb_ref[...] = jnp.where(lane < IDX_DIM, zr, 0.0).astype(BF16)
    for s, z in slabs("rq"):
        rq_ref[:, sl(s)] = _rope_slab(z, cr, sar, sbr, ret_half).astype(BF16)
    for s, z in slabs("rk"):
        rk_ref[:, sl(s)] = (_rope_slab(z, cr, sar, sbr, ret_half) * 0.125).astype(BF16)
    for s, z in slabs("rv"):
        rv_ref[:, sl(s)] = z.astype(BF16)
    for s, z in slabs("rg"):
        sg_ref[:, sl(s)] = (z * jax.nn.sigmoid(z)).astype(BF16)
    for s, z in slabs("ga"):
        sga_ref[:, sl(s)] = jax.nn.sigmoid(z).astype(BF16)
    for s, z in slabs("gb"):
        sgb_ref[:, sl(s)] = jax.nn.sigmoid(z).astype(BF16)


def _inproj(x, mod3, g, w_packed, tabs_att, tabs_ret, tm, blocks_per_mod, tab_blocks):
    t = x.shape[0]
    nblk = t // tm
    mod_rows = mod3.shape[1]
    tab_spec = pl.BlockSpec((tm, LANES), lambda i: (i % tab_blocks, 0))
    row = lambda n: pl.BlockSpec((tm, n), lambda i: (i, 0))
    kv_spec = pl.BlockSpec((tm, ATT_KV_HEADS, HEAD_DIM), lambda i: (i, 0, 0))
    out_defs = [(1024, BF16), (128, F32), (128, F32), (128, BF16), (128, BF16), (1024, BF16), (128, F32),
                (128, BF16), (512, BF16), (512, BF16), (1024, BF16), (1024, BF16), (1024, BF16), (1024, BF16)]
    return pl.pallas_call(
        _inproj_kernel,
        grid=(nblk,),
        in_specs=[row(D_MODEL),
                  pl.BlockSpec((None, mod_rows, 6 * D_MODEL), lambda i: (i // blocks_per_mod, 0, 0)),
                  pl.BlockSpec((1, D_MODEL), lambda i: (0, 0)),
                  pl.BlockSpec((D_MODEL, W_COLS), lambda i: (0, 0), pipeline_mode=pl.Buffered(1))]
                 + [tab_spec] * 6,
        out_specs=[kv_spec if j in (1, 2) else row(n) for j, (n, _) in enumerate(out_defs)],
        out_shape=[jax.ShapeDtypeStruct((t, ATT_KV_HEADS, HEAD_DIM) if j in (1, 2) else (t, n), d)
                   for j, (n, d) in enumerate(out_defs)],
        compiler_params=_cparams("parallel"),
        name="inproj",
    )(x, mod3, g, w_packed, *tabs_att, *tabs_ret)


def _count(score_ref, n, pred):
    acc = jnp.zeros((score_ref.shape[0], LANES), F32)
    for c in range(n // LANES):
        acc = acc + jnp.where(pred(score_ref[:, c * LANES:(c + 1) * LANES]), 1.0, 0.0)
    return jnp.sum(acc, axis=1, keepdims=True)


def _kth_largest(score_ref, n, k, quarters):
    sc = score_ref[:, :n]
    finite = sc > -jnp.inf
    n_fin = jnp.sum(jnp.where(finite, 1.0, 0.0), axis=1, keepdims=True)
    n_pos = jnp.sum(jnp.where(sc > 0.0, 1.0, 0.0), axis=1, keepdims=True)
    n_nonneg = jnp.sum(jnp.where(sc >= 0.0, 1.0, 0.0), axis=1, keepdims=True)
    mx = jnp.max(sc, axis=1, keepdims=True)
    mn = jnp.min(jnp.where(finite, sc, jnp.inf), axis=1, keepdims=True)
    small = n_fin <= k
    positive = n_pos >= k
    at_zero = jnp.logical_and(jnp.logical_not(positive), n_nonneg >= k)
    lo = jnp.where(positive, 0.0, mn)
    hi = jnp.where(positive, mx + (jnp.abs(mx) * 2.0 ** -20 + 2.0 ** -100), 0.0)
    lo = jnp.where(at_zero, 0.0, lo)
    done = jnp.where(jnp.logical_or(small, at_zero), 1.0, 0.0)

    def cond(state):
        return jnp.min(state[2]) < 0.5

    def body(state):
        lo, hi, done = state
        for _ in range(SELECT_UNROLL):
            mid = 0.5 * lo + 0.5 * hi
            cands = (0.5 * lo + 0.5 * mid, mid, 0.5 * mid + 0.5 * hi) if quarters else (mid,)
            stuck = jnp.logical_or(mid <= lo, mid >= hi)
            live = jnp.logical_and(done < 0.5, jnp.logical_not(stuck))
            new_lo, hit, ges = lo, jnp.zeros_like(lo), []
            for c in cands:
                cnt = _count(score_ref, n, lambda s, c=c: s >= c)
                ges.append(cnt >= k)
                new_lo = jnp.where(ges[-1], c, new_lo)
                hit = jnp.where(ges[-1], jnp.where(cnt == k, 1.0, 0.0), hit)
            new_hi = hi
            for c, ge in zip(reversed(cands), reversed(ges)):
                new_hi = jnp.where(ge, new_hi, c)
            lo = jnp.where(live, new_lo, lo)
            hi = jnp.where(live, new_hi, hi)
            done = jnp.where(jnp.logical_or(stuck, hit > 0.5), 1.0, done)
        return lo, hi, done

    lo, _, _ = lax.while_loop(cond, body, (lo, hi, done))
    return jnp.where(small, -FLT_MAX, lo)


def _topk_bias(score_ref, bias_ref, tri_ref, n, k, quarters=False):
    rows = score_ref.shape[0]
    thr = _kth_largest(score_ref, n, k, quarters)
    n_ge = _count(score_ref, n, lambda s: s >= thr)
    has_ties = jnp.max(jnp.where(n_ge > k, 1.0, 0.0)) > 0.5

    @pl.when(jnp.logical_not(has_ties))
    def _():
        for c in range(n // LANES):
            sl = slice(c * LANES, (c + 1) * LANES)
            bias_ref[:, sl] = jnp.where(score_ref[:, sl] >= thr, 0.0, MASK_NEG)

    @pl.when(has_ties)
    def _():
        need = k - _count(score_ref, n, lambda s: s > thr)
        run = jnp.zeros((rows, 1), F32)
        for c in range(n // LANES):
            sl = slice(c * LANES, (c + 1) * LANES)
            sc = score_ref[:, sl]
            eq = sc == thr
            eqf = jnp.where(eq, 1.0, 0.0)
            before = jnp.dot(eqf.astype(BF16), tri_ref[...], preferred_element_type=F32) + run
            take = jnp.logical_or(sc > thr, jnp.logical_and(eq, before < need))
            bias_ref[:, sl] = jnp.where(take, 0.0, MASK_NEG)
            run = run + jnp.sum(eqf, axis=1, keepdims=True)


def _dsa_prompt_kernel(q_ref, iq_ref, ikw_ref, kb_ref, vb_ref, ikb_ref, tri_ref, o_ref, score_ref, bias_ref):
    qb, s_len = score_ref.shape
    i = pl.program_id(1)
    nqb = s_len // qb
    per_variant = nqb // CAUSAL_VARIANTS
    for v in range(CAUSAL_VARIANTS):
        pl.when(i // per_variant == v)(
            functools.partial(_dsa_prompt_body, q_ref, iq_ref, ikw_ref, kb_ref, vb_ref, ikb_ref, tri_ref, o_ref,
                              score_ref, bias_ref, (v + 1) * per_variant * qb, min(TOPK_MAX, s_len // 4)))


def _dsa_prompt_body(q_ref, iq_ref, ikw_ref, kb_ref, vb_ref, ikb_ref, tri_ref, o_ref, score_ref, bias_ref,
                     n_keys, topk):
    qb = score_ref.shape[0]
    i = pl.program_id(1)
    w = ikw_ref[:, IDX_DIM:IDX_DIM + IDX_HEADS] * (IDX_HEADS ** -0.5)
    nt = (((1,), (1,)), ((), ()))
    kc = IDX_KEY_CHUNK
    qr = IDX_Q_ROWS
    for c in range(n_keys // kc):
        ikc = ikb_ref[c * kc:(c + 1) * kc, :]
        kpos = c * kc + lax.broadcasted_iota(I32, (qr, kc), 1)
        for r0 in range(0, qb, qr):
            acc = jnp.zeros((qr, kc), F32)
            for h in range(IDX_HEADS):
                d = lax.dot_general(iq_ref[r0:r0 + qr, h * LANES:(h + 1) * LANES], ikc, nt,
                                    preferred_element_type=F32)
                acc = acc + jnp.maximum(d, 0.0) * w[r0:r0 + qr, h:h + 1]
            qpos = i * qb + r0 + lax.broadcasted_iota(I32, (qr, kc), 0)
            score_ref[r0:r0 + qr, c * kc:(c + 1) * kc] = jnp.where(kpos <= qpos, acc, -jnp.inf)

    _topk_bias(score_ref, bias_ref, tri_ref, n_keys, topk)

    kb = kb_ref[0:n_keys, :]
    vb = vb_ref[0:n_keys, :]
    bias = bias_ref[:, 0:n_keys]
    lane = lax.broadcasted_iota(I32, (qb, LANES), 1)
    heads = []
    for h in range(ATT_HEADS):
        s = lax.dot_general(q_ref[:, h * LANES:(h + 1) * LANES], kb, nt, preferred_element_type=F32) + bias
        m = jnp.max(s, axis=1, keepdims=True)
        p = jnp.exp(s - m)
        l = jnp.sum(p, axis=1, keepdims=True)
        heads.append(jnp.dot(p.astype(BF16), vb, preferred_element_type=F32) / l)
    group = ATT_HEADS // ATT_KV_HEADS
    for pp in range(ATT_HEADS // 2):
        a, b = heads[2 * pp], heads[2 * pp + 1]
        if (2 * pp) // group == 0:
            slab = jnp.where(lane < HEAD_DIM, a, pltpu.roll(b, HEAD_DIM, 1))
        else:
            slab = jnp.where(lane < HEAD_DIM, pltpu.roll(a, HEAD_DIM, 1), b)
        o_ref[:, pp * LANES:(pp + 1) * LANES] = slab.astype(BF16)


def _dsa_prompt(q, iq, ikw, kb, vb, ikb, tri, batch, seq):
    nqb = seq // Q_BLOCK
    qrow = lambda n: pl.BlockSpec((Q_BLOCK, n), lambda b, i: (b * nqb + i, 0))
    keys = pl.BlockSpec((seq, LANES), lambda b, i: (b, 0))
    return pl.pallas_call(
        _dsa_prompt_kernel,
        grid=(batch, nqb),
        in_specs=[qrow(1024), qrow(1024), qrow(LANES), keys, keys, keys,
                  pl.BlockSpec((LANES, LANES), lambda b, i: (0, 0))],
        out_specs=qrow(ATT_OUT),
        out_shape=jax.ShapeDtypeStruct((batch * seq, ATT_OUT), BF16),
        scratch_shapes=[pltpu.VMEM((Q_BLOCK, seq), F32), pltpu.VMEM((Q_BLOCK, seq), F32)],
        compiler_params=_cparams("parallel", "arbitrary"),
        name="dsa_prompt",
    )(q, iq, ikw, kb, vb, ikb, tri)


def _dsa_sample_kernel(pt_ref, qs_ref, iqs_ref, ws_ref, knew_ref, vnew_ref, iknew_ref, cik_hbm, ck_hbm, cv_hbm,
                       tri_ref, o_ref, ikbuf, kbuf, vbuf, ikt, kt, vt, sems, key_ref, bias_ref):
    db = pl.program_id(0)
    n_pages = ikbuf.shape[1]
    t = key_ref.shape[0]
    n_past = n_pages * PAGE_SIZE
    last = pl.num_programs(0) - 1

    def fetch(src, dst_of_page, sem, req):
        def body(p, carry):
            pltpu.make_async_copy(src.at[pt_ref[req, p]], dst_of_page(p), sem).start()
            return carry
        lax.fori_loop(0, n_pages, body, 0)

    def wait_all(src, dst, sem):
        pltpu.make_async_copy(src.at[pl.ds(0, n_pages)], dst, sem).wait()

    fetch_ik = lambda req, slot: fetch(cik_hbm, lambda p: ikbuf.at[slot, p], sems.at[slot], req)
    fetch_k = lambda req: fetch(ck_hbm, lambda p: kbuf.at[p], sems.at[2], req)
    fetch_v = lambda req: fetch(cv_hbm, lambda p: vbuf.at[p], sems.at[3], req)

    @pl.when(db == 0)
    def _():
        fetch_ik(0, 0)
        fetch_k(0)
        fetch_v(0)

    @pl.when(db < last)
    def _():
        fetch_ik(db + 1, (db + 1) % 2)

    slot = db % 2
    wait_all(cik_hbm, ikbuf.at[slot], sems.at[slot])

    nt = (((1,), (1,)), ((), ()))
    page = lambda p: slice(p * PAGE_SIZE, (p + 1) * PAGE_SIZE)
    for p in range(n_pages):
        ikt[:, page(p)] = ikbuf[slot, p].astype(BF16)
    iqs = iqs_ref[...]
    wcol = ws_ref[...]
    d_past = jnp.maximum(jnp.dot(iqs, ikt[...], preferred_element_type=F32), 0.0) * wcol
    d_new = jnp.maximum(jnp.dot(iqs, iknew_ref[...], preferred_element_type=F32), 0.0) * wcol
    s_past = d_past[0:t]
    s_new = d_new[0:t]
    for h in range(1, IDX_HEADS):
        s_past = s_past + d_past[h * t:(h + 1) * t]
        s_new = s_new + d_new[h * t:(h + 1) * t]
    row = lax.broadcasted_iota(I32, (t, LANES), 0)
    lane = lax.broadcasted_iota(I32, (t, LANES), 1)
    new_ok = lane <= row
    key_ref[:, 0:n_past] = s_past
    key_ref[:, n_past:n_past + LANES] = jnp.where(new_ok, s_new, -jnp.inf)
    _topk_bias(key_ref, bias_ref, tri_ref, n_past + LANES, min(TOPK_MAX, (n_past + t) // 4), quarters=True)

    def stage(buf, dst):
        for p in range(n_pages):
            for j in range(ATT_KV_HEADS):
                dst[j, :, page(p)] = buf[p, j].astype(BF16)

    wait_all(ck_hbm, kbuf, sems.at[2])
    stage(kbuf, kt)

    @pl.when(db < last)
    def _():
        fetch_k(db + 1)

    wait_all(cv_hbm, vbuf, sems.at[3])
    stage(vbuf, vt)

    @pl.when(db < last)
    def _():
        fetch_v(db + 1)

    rows_per_kv = qs_ref.shape[0] // ATT_KV_HEADS
    bias = jnp.concatenate([bias_ref[...]] * (rows_per_kv // t), axis=0)
    for j in range(ATT_KV_HEADS):
        qj = qs_ref[j * rows_per_kv:(j + 1) * rows_per_kv, :]
        sp = jnp.dot(qj, kt[j], preferred_element_type=F32) + bias[:, 0:n_past]
        sn = jnp.dot(qj, knew_ref[j], preferred_element_type=F32) + bias[:, n_past:n_past + LANES]
        m = jnp.maximum(jnp.max(sp, axis=1, keepdims=True), jnp.max(sn, axis=1, keepdims=True))
        pp = jnp.exp(sp - m)
        pn = jnp.exp(sn - m)
        l = jnp.sum(pp, axis=1, keepdims=True) + jnp.sum(pn, axis=1, keepdims=True)
        o = (lax.dot_general(pp.astype(BF16), vt[j], nt, preferred_element_type=F32)
             + lax.dot_general(pn.astype(BF16), vnew_ref[j], nt, preferred_element_type=F32))
        o_ref[j * rows_per_kv:(j + 1) * rows_per_kv, :] = o / l


def _dsa_sample(page_table, qs, iqs, ws, knew, vnew, iknew, cache_ik, cache_k, cache_v, tri, t):
    db, n_pages = page_table.shape
    rows = qs.shape[1]
    per_db = lambda r, n: pl.BlockSpec((None, r, n), lambda b, pt: (b, 0, 0))
    any_spec = pl.BlockSpec(memory_space=pl.ANY)
    n_keys = n_pages * PAGE_SIZE + LANES
    grid_spec = pltpu.PrefetchScalarGridSpec(
        num_scalar_prefetch=1,
        grid=(db,),
        in_specs=[per_db(rows, HEAD_DIM), per_db(rows, IDX_DIM), per_db(rows, 1),
                  pl.BlockSpec((None, ATT_KV_HEADS, HEAD_DIM, LANES), lambda b, pt: (b, 0, 0, 0)),
                  pl.BlockSpec((None, ATT_KV_HEADS, HEAD_DIM, LANES), lambda b, pt: (b, 0, 0, 0)),
                  per_db(IDX_DIM, LANES), any_spec, any_spec, any_spec,
                  pl.BlockSpec((LANES, LANES), lambda b, pt: (0, 0))],
        out_specs=per_db(rows, HEAD_DIM),
        scratch_shapes=[pltpu.VMEM((2, n_pages, IDX_DIM, PAGE_SIZE), F32),
                        pltpu.VMEM((n_pages, ATT_KV_HEADS, HEAD_DIM, PAGE_SIZE), F32),
                        pltpu.VMEM((n_pages, ATT_KV_HEADS, HEAD_DIM, PAGE_SIZE), F32),
                        pltpu.VMEM((IDX_DIM, n_pages * PAGE_SIZE), BF16),
                        pltpu.VMEM((ATT_KV_HEADS, HEAD_DIM, n_pages * PAGE_SIZE), BF16),
                        pltpu.VMEM((ATT_KV_HEADS, HEAD_DIM, n_pages * PAGE_SIZE), BF16),
                        pltpu.SemaphoreType.DMA((4,)),
                        pltpu.VMEM((t, n_keys), F32),
                        pltpu.VMEM((t, n_keys), F32)])
    return pl.pallas_call(
        _dsa_sample_kernel,
        grid_spec=grid_spec,
        out_shape=jax.ShapeDtypeStruct((db, rows, HEAD_DIM), F32),
        compiler_params=_cparams("arbitrary"),
        name="dsa_sample",
    )(page_table, qs, iqs, ws, knew, vnew, iknew, cache_ik, cache_k, cache_v, tri)


def _retention_kernel(has_init, rq_ref, rk_ref, rv_ref, sg_ref, decay_ref, qdec_ref, kdec_ref, gst_ref, *rest):
    if has_init:
        init_ref, o_ref, st_ref, state = rest
    else:
        o_ref, st_ref, state = rest
    c = pl.program_id(1)

    @pl.when(c == 0)
    def _():
        if has_init:
            state[...] = init_ref[...]
        else:
            state[...] = jnp.zeros_like(state)

    nt = (((1,), (1,)), ((), ()))
    tn = (((0,), (0,)), ((), ()))
    rows = rq_ref.shape[0]
    lane = lax.broadcasted_iota(I32, (rows, LANES), 1)
    for p in range(RET_HEADS // 2):
        sl = slice(p * LANES, (p + 1) * LANES)
        qp = rq_ref[:, sl]
        kp = rk_ref[:, sl]
        qd = (qp.astype(F32) * qdec_ref[:, sl]).astype(BF16)
        kd = (kp.astype(F32) * kdec_ref[:, sl]).astype(BF16)
        s_old = state[p]
        s_old_b = s_old.astype(BF16)
        s_new = s_old * gst_ref[p]
        for e in range(2):
            h = 2 * p + e
            hs = slice(h * LANES, (h + 1) * LANES)
            mine = jnp.where((lane >= e * RET_DK) & (lane < (e + 1) * RET_DK), 1.0, 0.0).astype(BF16)
            sc = lax.dot_general(qp * mine, kp, nt, preferred_element_type=F32) * decay_ref[h]
            vh = rv_ref[:, hs]
            o = (jnp.dot(sc.astype(BF16), vh, preferred_element_type=F32)
                 + jnp.dot(qd * mine, s_old_b, preferred_element_type=F32))
            s_new = s_new + lax.dot_general(kd * mine, vh, tn, preferred_element_type=F32)
            mu = jnp.mean(o, axis=-1, keepdims=True)
            var = jnp.mean(jnp.square(o - mu), axis=-1, keepdims=True)
            on = (o - mu) * lax.rsqrt(var + GN_EPS)
            o_ref[:, hs] = (on * sg_ref[:, hs].astype(F32)).astype(BF16)
        state[p] = s_new

    @pl.when(c == pl.num_programs(1) - 1)
    def _():
        st_ref[...] = state[...]


def _retention(rq, rk, rv, sg, tables, init, batch, n_chunks):
    decay, qdec, kdec, gst = tables
    cr = RET_CHUNK
    rowspec = lambda n: pl.BlockSpec((cr, n), lambda b, c: (b * n_chunks + c, 0))
    const = lambda shape: pl.BlockSpec(shape, lambda b, c: (0,) * len(shape))
    st_spec = pl.BlockSpec((None, RET_HEADS // 2, LANES, LANES), lambda b, c: (b, 0, 0, 0))
    in_specs = [rowspec(512), rowspec(512), rowspec(RET_OUT), rowspec(RET_OUT),
                const(decay.shape), const(qdec.shape), const(kdec.shape), const(gst.shape)]
    args = [rq, rk, rv, sg, decay, qdec, kdec, gst]
    if init is not None:
        in_specs.append(st_spec)
        args.append(init)
    return pl.pallas_call(
        functools.partial(_retention_kernel, init is not None),
        grid=(batch, n_chunks),
        in_specs=in_specs,
        out_specs=[rowspec(RET_OUT), st_spec],
        out_shape=[jax.ShapeDtypeStruct((batch * n_chunks * cr, RET_OUT), BF16),
                   jax.ShapeDtypeStruct((batch, RET_HEADS // 2, LANES, LANES), F32)],
        scratch_shapes=[pltpu.VMEM((RET_HEADS // 2, LANES, LANES), F32)],
        compiler_params=_cparams("parallel", "arbitrary"),
        name="retention",
    )(*args)


def _retention_tables(c_eff):
    lg = jnp.log(1.0 - 2.0 ** (-5.0 - jnp.arange(RET_HEADS, dtype=F32)))
    i = jnp.arange(RET_CHUNK, dtype=F32)
    diff = i[:, None] - i[None, :]
    decay = jnp.where(diff >= 0, jnp.exp(jnp.maximum(diff, 0.0)[None] * lg[:, None, None]), 0.0)
    q_decay = jnp.exp((i + 1.0)[:, None] * lg[None, :])
    k_decay = jnp.exp((c_eff - 1.0 - i)[:, None] * lg[None, :])
    qdec = jnp.repeat(q_decay, RET_DK, axis=1)
    kdec = jnp.repeat(k_decay, RET_DK, axis=1)
    g_state = jnp.exp(c_eff * lg)
    gst = jnp.broadcast_to(jnp.repeat(g_state, RET_DK).reshape(RET_HEADS // 2, LANES, 1),
                           (RET_HEADS // 2, LANES, LANES))
    return decay, qdec, kdec, gst


def _pack_bf16_pairs(lo, hi):
    return pltpu.pack_elementwise([lo, hi], packed_dtype=BF16)


def _unpack_bf16_pairs(words):
    return tuple(pltpu.unpack_elementwise(words, index=j, packed_dtype=BF16, unpacked_dtype=F32).astype(BF16)
                 for j in range(2))


def _post_kernel(a_ref, r_ref, sga_ref, sgb_ref, x_ref, mod_ref, wpa_ref, wpb_ref, wo_ref, g_ref, wr_ref, br_ref,
                 ltri_ref, cnt0_ref, x1_ref, h2w_ref, idx_ref, gate_ref, rank_ref, cnt_ref, cnt):
    @pl.when(pl.program_id(0) == 0)
    def _():
        cnt[...] = cnt0_ref[...]

    pa = jnp.dot(a_ref[...], wpa_ref[...], preferred_element_type=F32)
    pb = jnp.dot(r_ref[...], wpb_ref[...], preferred_element_type=F32)
    merged = sga_ref[...].astype(F32) * pa + sgb_ref[...].astype(F32) * pb
    gt1 = mod_ref[:, 2 * D_MODEL:3 * D_MODEL]
    x1 = x_ref[...] + gt1 * jnp.dot(merged.astype(BF16), wo_ref[...], preferred_element_type=F32)
    x1_ref[...] = x1
    ms = jnp.mean(x1 * x1, axis=-1, keepdims=True)
    y = x1 * lax.rsqrt(ms + NORM_EPS) * g_ref[...]
    h2 = y * (1.0 + mod_ref[:, 4 * D_MODEL:5 * D_MODEL]) + mod_ref[:, 3 * D_MODEL:4 * D_MODEL]
    half = D_MODEL // 2
    h2w_ref[...] = _pack_bf16_pairs(h2[:, :half], h2[:, half:])
    h_hi = h2.astype(BF16)
    h_lo = (h2 - h_hi.astype(F32)).astype(BF16)
    logits = (jnp.dot(h_hi, wr_ref[0], preferred_element_type=F32) + jnp.dot(h_hi, wr_ref[1], preferred_element_type=F32)
              + jnp.dot(h_lo, wr_ref[0], preferred_element_type=F32) + br_ref[...])
    lane = lax.broadcasted_iota(I32, logits.shape, 1).astype(F32)
    idx_out = jnp.zeros(logits.shape, F32)
    val_out = jnp.zeros(logits.shape, F32)
    chosen = []
    top = None
    for j in range(TOP_K):
        m = jnp.max(logits, axis=1, keepdims=True)
        am = jnp.min(jnp.where(logits == m, lane, float(LANES)), axis=1, keepdims=True)
        if j == 0:
            top = m
        idx_out = jnp.where(lane == j, am, idx_out)
        val_out = jnp.where(lane == j, jnp.exp(m - top), val_out)
        chosen.append(lane == am)
        logits = jnp.where(chosen[-1], -jnp.inf, logits)
    idx_ref[...] = jnp.transpose(idx_out)[0:8, :].astype(I32)
    gate_ref[...] = jnp.transpose(val_out / jnp.sum(val_out, axis=1, keepdims=True))[0:8, :]
    onehot = jnp.where(chosen[0] | chosen[1] | chosen[2] | chosen[3], 1.0, 0.0)
    before = jnp.dot(ltri_ref[...], onehot.astype(BF16), preferred_element_type=F32) + cnt[...]
    rank_out = jnp.zeros(logits.shape, F32)
    for j in range(TOP_K):
        rj = jnp.sum(jnp.where(chosen[j], before, 0.0), axis=1, keepdims=True)
        rank_out = jnp.where(lane == j, rj, rank_out)
    rank_ref[...] = jnp.transpose(rank_out)[0:8, :].astype(I32)
    cnt[...] = cnt[...] + jnp.sum(onehot, axis=0, keepdims=True)
    cnt_ref[...] = cnt[...]


def _post(a, r, sga, sgb, x, mod3, wpa, wpb, wo, g, wr, br, cnt0, tm, blocks_per_mod):
    t = x.shape[0]
    mod_rows = mod3.shape[1]
    ar = jnp.arange(tm)
    ltri = (ar[None, :] < ar[:, None]).astype(BF16)
    row = lambda n: pl.BlockSpec((tm, n), lambda i: (i, 0))
    col8 = pl.BlockSpec((8, tm), lambda i: (0, i))
    const = lambda a_: pl.BlockSpec(a_.shape, lambda i: (0,) * a_.ndim)
    return pl.pallas_call(
        _post_kernel,
        grid=(t // tm,),
        in_specs=[row(ATT_OUT), row(RET_OUT), row(D_MODEL), row(D_MODEL), row(D_MODEL),
                  pl.BlockSpec((None, mod_rows, 6 * D_MODEL), lambda i: (i // blocks_per_mod, 0, 0)),
                  const(wpa), const(wpb), const(wo), const(g), const(wr), const(br), const(ltri), const(cnt0)],
        out_specs=[row(D_MODEL), row(D_MODEL // 2), col8, col8, col8,
                   pl.BlockSpec((1, LANES), lambda i: (0, 0))],
        out_shape=[jax.ShapeDtypeStruct((t, D_MODEL), F32), jax.ShapeDtypeStruct((t, D_MODEL // 2), jnp.uint32),
                   jax.ShapeDtypeStruct((8, t), I32), jax.ShapeDtypeStruct((8, t), F32),
                   jax.ShapeDtypeStruct((8, t), I32), jax.ShapeDtypeStruct((1, LANES), F32)],
        scratch_shapes=[pltpu.VMEM((1, LANES), F32)],
        compiler_params=_cparams("arbitrary"),
        name="post",
    )(a, r, sga, sgb, x, mod3, wpa, wpb, wo, g, wr, br, ltri, cnt0)


def _segment_copies(seg_ref, blk, make_copy):
    for e in range(N_EXPERTS + 1):
        base = (blk * (N_EXPERTS + 1) + e) * 3
        loc, glob, n = seg_ref[base], seg_ref[base + 1], seg_ref[base + 2]
        for k in range(SEG_BITS):
            size = SEG_ALIGN << k

            @pl.when((n & size) != 0)
            def _():
                done = n & (size - 1)
                make_copy(pl.multiple_of(loc + done, SEG_ALIGN), pl.multiple_of(glob + done, SEG_ALIGN),
                          size).start()


def _local_order_matrix(lpos_ref, values, rows):
    lp = lpos_ref[...]
    r = lax.broadcasted_iota(I32, (rows, lp.shape[1]), 0)
    m = jnp.zeros((rows, lp.shape[1]), F32)
    for j in range(TOP_K):
        m = jnp.where(r == lp[j:j + 1, :], values(j), m)
    return m


def _dispatch_kernel(seg_ref, lpos_ref, h2w_ref, xs_in, xs_out, buf, sems):
    del xs_in
    i = pl.program_id(0)
    slot = i % 2
    rows = buf.shape[1]

    def wait_slot(s):
        pltpu.make_async_copy(buf.at[s], xs_out.at[pl.ds(0, rows)], sems.at[s]).wait()

    @pl.when(i >= 2)
    def _():
        wait_slot(slot)

    x = jnp.concatenate(_unpack_bf16_pairs(h2w_ref[...]), axis=1)
    perm = _local_order_matrix(lpos_ref, lambda j: 1.0, rows).astype(BF16)
    xs = jnp.dot(perm, x, preferred_element_type=F32)
    half = D_MODEL // 2
    buf[slot] = _pack_bf16_pairs(xs[:, :half], xs[:, half:])
    _segment_copies(seg_ref, i, lambda loc, glob, size: pltpu.make_async_copy(
        buf.at[slot, pl.ds(loc, size)], xs_out.at[pl.ds(glob, size)], sems.at[slot]))

    @pl.when(i == pl.num_programs(0) - 1)
    def _():
        wait_slot(slot)

        @pl.when(i >= 1)
        def _():
            wait_slot(1 - slot)


def _dispatch(seg, lpos8, h2w, x_sorted):
    t = h2w.shape[0]
    tm = TOK_BLOCK
    grid_spec = pltpu.PrefetchScalarGridSpec(
        num_scalar_prefetch=1,
        grid=(t // tm,),
        in_specs=[pl.BlockSpec((8, tm), lambda i, s: (0, i)),
                  pl.BlockSpec((tm, D_MODEL // 2), lambda i, s: (i, 0)), pl.BlockSpec(memory_space=pl.ANY)],
        out_specs=pl.BlockSpec(memory_space=pl.ANY),
        scratch_shapes=[pltpu.VMEM((2, SEG_ROWS, D_MODEL // 2), jnp.uint32), pltpu.SemaphoreType.DMA((2,))])
    return pl.pallas_call(
        _dispatch_kernel,
        grid_spec=grid_spec,
        out_shape=jax.ShapeDtypeStruct(x_sorted.shape, x_sorted.dtype),
        input_output_aliases={3: 0},
        compiler_params=_cparams("arbitrary"),
        name="dispatch",
    )(seg, lpos8, h2w, x_sorted)


def _moe_kernel(be_ref, first_ref, nused_ref, x_ref, wgu_ref, bgu_ref, wd_ref, bd_ref, o_ref, wgu_b, wd_b):
    i = pl.program_id(0)

    @pl.when(first_ref[i] == 1)
    def _():
        wgu_b[...] = wgu_ref[...].astype(BF16)
        wd_b[...] = wd_ref[...].astype(BF16)

    @pl.when(i < nused_ref[0])
    def _():
        x = jnp.concatenate(_unpack_bf16_pairs(x_ref[...]), axis=1)
        gu = jnp.dot(x, wgu_b[...], preferred_element_type=F32) + bgu_ref[...]
        g = jnp.minimum(gu[:, :D_FF], SWIGLU_LIMIT)
        u = jnp.clip(gu[:, D_FF:], -SWIGLU_LIMIT, SWIGLU_LIMIT)
        act = (u + 1.0) * (g * jax.nn.sigmoid(SWIGLU_ALPHA * g))
        o_ref[...] = jnp.dot(act.astype(BF16), wd_b[...], preferred_element_type=F32) + bd_ref[...]

    @pl.when(i >= nused_ref[0])
    def _():
        o_ref[...] = jnp.zeros_like(o_ref)


def _moe(blk_expert, blk_first, n_used, x_sorted, w_gate_up, b_gate_up, w_down, b_down):
    n_rows = blk_expert.shape[0] * MOE_ROWS
    grid_spec = pltpu.PrefetchScalarGridSpec(
        num_scalar_prefetch=3,
        grid=(n_rows // MOE_ROWS,),
        in_specs=[pl.BlockSpec((MOE_ROWS, D_MODEL // 2), lambda i, be, bf, nu: (i, 0)),
                  pl.BlockSpec((None, D_MODEL, 2 * D_FF), lambda i, be, bf, nu: (be[i], 0, 0)),
                  pl.BlockSpec((None, 1, 2 * D_FF), lambda i, be, bf, nu: (be[i], 0, 0)),
                  pl.BlockSpec((None, D_FF, D_MODEL), lambda i, be, bf, nu: (be[i], 0, 0)),
                  pl.BlockSpec((None, 1, D_MODEL), lambda i, be, bf, nu: (be[i], 0, 0))],
        out_specs=pl.BlockSpec((MOE_ROWS, D_MODEL), lambda i, be, bf, nu: (i, 0)),
        scratch_shapes=[pltpu.VMEM((D_MODEL, 2 * D_FF), BF16), pltpu.VMEM((D_FF, D_MODEL), BF16)])
    return pl.pallas_call(
        _moe_kernel,
        grid_spec=grid_spec,
        out_shape=jax.ShapeDtypeStruct((n_rows, D_MODEL), F32),
        compiler_params=_cparams("arbitrary"),
        name="moe",
    )(blk_expert, blk_first, n_used, x_sorted, w_gate_up, b_gate_up.reshape(N_EXPERTS, 1, -1),
      w_down, b_down.reshape(N_EXPERTS, 1, -1))


def _final_kernel(block_offset, seg_ref, x1_ref, mod_ref, lpos_ref, gate_ref, g_ref, rows_hbm, y_ref, buf, sems):
    i = pl.program_id(0)
    n = pl.num_programs(0)
    rows = buf.shape[1]

    def issue(blk, slot):
        _segment_copies(seg_ref, blk + block_offset, lambda loc, glob, size: pltpu.make_async_copy(
            rows_hbm.at[pl.ds(glob, size)], buf.at[slot, pl.ds(loc, size)], sems.at[slot]))

    @pl.when(i == 0)
    def _():
        issue(0, 0)

    @pl.when(i + 1 < n)
    def _():
        issue(i + 1, (i + 1) % 2)

    slot = i % 2
    pltpu.make_async_copy(rows_hbm.at[pl.ds(0, rows)], buf.at[slot], sems.at[slot]).wait()
    gate = gate_ref[...]
    gmat = _local_order_matrix(lpos_ref, lambda j: gate[j:j + 1, :], rows).astype(BF16)
    moe = lax.dot_general(gmat, buf[slot].astype(BF16), (((0,), (0,)), ((), ())), preferred_element_type=F32)
    x2 = x1_ref[...] + mod_ref[:, 5 * D_MODEL:6 * D_MODEL] * moe
    ms = jnp.mean(x2 * x2, axis=-1, keepdims=True)
    y_ref[...] = x2 * lax.rsqrt(ms + NORM_EPS) * g_ref[...]


def _final(seg, x1, mod3, lpos8, gate8, rows_out, g, blocks_per_mod, block_offset):
    t = x1.shape[0]
    tm = TOK_BLOCK
    mod_rows = mod3.shape[1]
    row = lambda n: pl.BlockSpec((tm, n), lambda i, s: (i, 0))
    col8 = pl.BlockSpec((8, tm), lambda i, s: (0, i + block_offset))
    grid_spec = pltpu.PrefetchScalarGridSpec(
        num_scalar_prefetch=1,
        grid=(t // tm,),
        in_specs=[row(D_MODEL),
                  pl.BlockSpec((None, mod_rows, 6 * D_MODEL), lambda i, s: (i // blocks_per_mod, 0, 0)),
                  col8, col8,
                  pl.BlockSpec((1, D_MODEL), lambda i, s: (0, 0)),
                  pl.BlockSpec(memory_space=pl.ANY)],
        out_specs=row(D_MODEL),
        scratch_shapes=[pltpu.VMEM((2, SEG_ROWS, D_MODEL), F32), pltpu.SemaphoreType.DMA((2,))])
    return pl.pallas_call(
        functools.partial(_final_kernel, block_offset),
        grid_spec=grid_spec,
        out_shape=jax.ShapeDtypeStruct((t, D_MODEL), F32),
        compiler_params=_cparams("arbitrary"),
        name="final",
    )(seg, x1, mod3, lpos8, gate8, g, rows_out)


def _rope_tables(pos, rot_dim, theta, head_dim):
    half = rot_dim // 2
    inv = theta ** (-jnp.arange(half, dtype=F32) * (2.0 / rot_dim))
    ang = pos.astype(F32)[:, None] * inv[None, :]
    cos, sin = jnp.cos(ang), jnp.sin(ang)
    n = pos.shape[0]
    rest = head_dim - rot_dim
    zh = jnp.zeros((n, half), F32)
    c = jnp.concatenate([cos, cos, jnp.ones((n, rest), F32)], axis=1)
    sa = jnp.concatenate([-sin, zh, jnp.zeros((n, rest), F32)], axis=1)
    sb = jnp.concatenate([zh, sin, jnp.zeros((n, rest), F32)], axis=1)
    rep = LANES // head_dim
    return tuple(jnp.tile(a, (1, rep)) for a in (c, sa, sb))


def _pack_w_in(w_in):
    offs = np.cumsum((0,) + IN_SPLITS)
    part = lambda j: w_in[:, offs[j]:offs[j + 1]]
    zero = lambda n: jnp.zeros((D_MODEL, n), w_in.dtype)
    group = ATT_HEADS // ATT_KV_HEADS
    cols = []
    wq = part(0)
    for h in range(ATT_HEADS):
        wh = wq[:, h * HEAD_DIM:(h + 1) * HEAD_DIM]
        cols += [wh, zero(HEAD_DIM)] if h // group == 0 else [zero(HEAD_DIM), wh]
    wiq = part(3)
    for h in range(IDX_HEADS):
        cols += [wiq[:, h * IDX_DIM:(h + 1) * IDX_DIM], zero(LANES - IDX_DIM)]
    cols += [part(1), part(2), part(4), part(5), zero(LANES - IDX_DIM - IDX_HEADS)]
    cols += [part(j) for j in range(6, 12)]
    return jnp.concatenate(cols, axis=1).astype(BF16)


def _heads_major(a, db, t, width):
    heads = a.shape[1] // width
    return a.reshape(db, t, heads, width).transpose(0, 2, 1, 3).reshape(db, heads * t, width)


def _pad_rows(a, db, t, rows):
    return jnp.pad(a.reshape(db, t, -1), ((0, 0), (0, rows - t), (0, 0)))


def kernel(x_prompt, x_sample, cache_k, cache_v, cache_ik, state_ret, page_table, c_prompt, c_sample, norm_mix_g, norm_ffn_g, norm_final_g, w_ada, b_ada, w_in, w_branch_a, w_branch_b, w_out, w_router, b_router, w_gate_up, b_gate_up, w_down, b_down):
    batch, seq, _ = x_prompt.shape
    db, dt, _ = x_sample.shape
    assert w_in.shape[0] == 1, "one layer"
    tp, ts = batch * seq, db * dt
    xp = x_prompt.reshape(tp, D_MODEL)
    xs = x_sample.reshape(ts, D_MODEL)

    mod = _adaln(jnp.concatenate([c_prompt, c_sample], axis=0), w_ada[0], b_ada[0])
    mod_p = mod[:batch].reshape(batch, 1, 6 * D_MODEL)
    mod_s = jnp.repeat(mod[batch:], dt, axis=0).reshape(ts // TOK_BLOCK, TOK_BLOCK, 6 * D_MODEL)
    bpm_p = seq // TOK_BLOCK
    tmp = PROMPT_TOK_BLOCK
    bpm_big = seq // tmp

    w_packed = _pack_w_in(w_in[0].astype(BF16))
    pos_p = jnp.arange(seq)
    pos_s = PAST_LEN + (jnp.arange(TOK_BLOCK) % dt)
    g_mix = norm_mix_g[0].reshape(1, D_MODEL)
    outs_p = _inproj(xp, mod_p, g_mix, w_packed, _rope_tables(pos_p, ROPE_DIM, ROPE_THETA, HEAD_DIM),
                     _rope_tables(pos_p, RET_DK, RET_THETA, RET_DK), TOK_BLOCK, bpm_p, bpm_p)
    outs_s = _inproj(xs, mod_s, g_mix, w_packed, _rope_tables(pos_s, ROPE_DIM, ROPE_THETA, HEAD_DIM),
                     _rope_tables(pos_s, RET_DK, RET_THETA, RET_DK), TOK_BLOCK, 1, 1)
    (q_p, k_p, v_p, kb_p, vb_p, iq_p, ikw_p, ikb_p, rq_p, rk_p, rv_p, sg_p, sga_p, sgb_p) = outs_p
    (q_s, k_s, v_s, kb_s, vb_s, iq_s, ikw_s, ikb_s, rq_s, rk_s, rv_s, sg_s, sga_s, sgb_s) = outs_s

    tri = (jnp.arange(LANES)[:, None] < jnp.arange(LANES)[None, :]).astype(BF16)

    a_p = _dsa_prompt(q_p, iq_p, ikw_p, kb_p, vb_p, ikb_p, tri, batch, seq)
    group = ATT_HEADS // ATT_KV_HEADS
    q4 = q_s.reshape(db, dt, ATT_HEADS, LANES)
    qs = jnp.stack([q4[:, :, h, (h // group) * HEAD_DIM:(h // group + 1) * HEAD_DIM] for h in range(ATT_HEADS)],
                   axis=1).reshape(db, ATT_HEADS * dt, HEAD_DIM)
    iqs = _heads_major(iq_s, db, dt, LANES)[:, :, :IDX_DIM]
    ws = _heads_major(ikw_s[:, IDX_DIM:IDX_DIM + IDX_HEADS] * (IDX_HEADS ** -0.5), db, dt, 1)
    new_t = lambda a: jnp.pad(a.reshape(db, dt, ATT_KV_HEADS, HEAD_DIM).transpose(0, 2, 3, 1),
                              ((0, 0), (0, 0), (0, 0), (0, LANES - dt)))
    iknew_t = jnp.pad(ikb_s[:, :IDX_DIM].reshape(db, dt, IDX_DIM).transpose(0, 2, 1),
                      ((0, 0), (0, 0), (0, LANES - dt)))
    o_s = _dsa_sample(page_table, qs, iqs, ws, new_t(kb_s), new_t(vb_s), iknew_t,
                      cache_ik[0].transpose(0, 2, 1), cache_k[0].transpose(0, 2, 3, 1),
                      cache_v[0].transpose(0, 2, 3, 1), tri, dt)
    a_s = o_s.reshape(db, ATT_HEADS, dt, HEAD_DIM).transpose(0, 2, 1, 3).reshape(ts, ATT_OUT).astype(BF16)

    r_p, st_p = _retention(rq_p, rk_p, rv_p, sg_p, _retention_tables(float(RET_CHUNK)), None, batch,
                           seq // RET_CHUNK)
    pad = lambda a: _pad_rows(a, db, dt, RET_CHUNK).reshape(db * RET_CHUNK, -1)
    r_s, st_s = _retention(pad(rq_s), pad(rk_s), pad(rv_s), pad(sg_s), _retention_tables(float(dt)),
                           state_ret[0].reshape(db, RET_HEADS // 2, LANES, LANES), db, 1)
    r_s = r_s.reshape(db, RET_CHUNK, RET_OUT)[:, :dt].reshape(ts, RET_OUT)

    wr = jnp.pad(w_router[0], ((0, 0), (0, LANES - N_EXPERTS)))
    wr_hi = wr.astype(BF16)
    wr = jnp.stack([wr_hi, (wr - wr_hi.astype(F32)).astype(BF16)])
    br = jnp.concatenate([b_router[0], jnp.full((LANES - N_EXPERTS,), -jnp.inf, F32)]).reshape(1, LANES)
    post_w = (w_branch_a[0].astype(BF16), w_branch_b[0].astype(BF16), w_out[0].astype(BF16),
              norm_ffn_g[0].reshape(1, D_MODEL), wr, br)
    x1_p, h2w_p, idx_p, gate_p, rank_p, cnt_p = _post(a_p, r_p, sga_p, sgb_p, xp, mod_p, *post_w,
                                                      jnp.zeros((1, LANES), F32), tmp, bpm_big)
    x1_s, h2w_s, idx_s, gate_s, rank_s, cnt_all = _post(a_s, r_s, sga_s, sgb_s, xs, mod_s, *post_w, cnt_p,
                                                        TOK_BLOCK, 1)

    del cnt_all
    n_tok, lb = tp + ts, TOK_BLOCK
    nb = n_tok // lb
    idx4 = jnp.concatenate([idx_p[:TOP_K], idx_s[:TOP_K]], axis=1)
    rank4 = jnp.concatenate([rank_p[:TOP_K], rank_s[:TOP_K]], axis=1)
    gate8 = jnp.concatenate([gate_p, gate_s], axis=1)
    hit = idx4[None] == jnp.arange(N_EXPERTS, dtype=I32)[:, None, None]
    bc = jnp.sum(hit.reshape(N_EXPERTS, TOP_K, nb, lb).astype(I32), axis=(1, 3)).T
    run = (bc + SEG_ALIGN - 1) // SEG_ALIGN * SEG_ALIGN
    padded = (jnp.sum(run, axis=0) + MOE_ROWS - 1) // MOE_ROWS * MOE_ROWS
    pend = jnp.cumsum(padded)
    pstart = pend - padded
    loc = jnp.cumsum(run, axis=1) - run
    glob = pstart[None, :] + jnp.cumsum(run, axis=0) - run
    carry = jnp.cumsum(bc, axis=0) - bc
    n_blocks = -(-(n_tok * TOP_K + nb * N_EXPERTS * (SEG_ALIGN - 1)) // MOE_ROWS) + N_EXPERTS
    n_rows = n_blocks * MOE_ROWS
    used = jnp.sum(run, axis=1, keepdims=True)
    runs = jnp.stack([loc, glob, run], axis=-1)
    seg_table = lambda spare: jnp.concatenate(
        [runs, jnp.concatenate([used, spare, SEG_ROWS - used], axis=1)[:, None, :]], axis=1).reshape(-1).astype(I32)
    spare_out = n_rows + (jnp.arange(nb, dtype=I32)[:, None] % 2) * (SEG_ROWS // 2)
    seg_out, seg_in = seg_table(spare_out), seg_table(jnp.zeros((nb, 1), I32))
    shift = jnp.repeat(loc - carry, lb, axis=0).T
    lpos4 = rank4 + jnp.sum(jnp.where(hit, shift[:, None, :], 0), axis=0)
    lpos8 = jnp.pad(lpos4, ((0, 8 - TOP_K), (0, 0))).astype(I32)
    blk_start = jnp.arange(n_blocks, dtype=I32) * MOE_ROWS
    blk_expert = jnp.minimum(jnp.sum((blk_start[:, None] >= pend[None, :]).astype(I32), axis=1), N_EXPERTS - 1)
    blk_first = jnp.concatenate([jnp.ones((1,), I32), (blk_expert[1:] != blk_expert[:-1]).astype(I32)])
    n_used = (pend[-1] // MOE_ROWS).astype(I32).reshape(1)
    x_sorted = jnp.zeros((n_rows + SEG_ROWS, D_MODEL // 2), jnp.uint32)
    x_sorted = _dispatch(seg_out, lpos8, jnp.concatenate([h2w_p, h2w_s], axis=0), x_sorted)
    rows_out = _moe(blk_expert, blk_first, n_used, x_sorted, w_gate_up[0], b_gate_up[0], w_down[0], b_down[0])

    g_final = norm_final_g.reshape(1, D_MODEL)
    y_p = _final(seg_in, x1_p, mod_p, lpos8, gate8, rows_out, g_final, bpm_p, 0)
    y_s = _final(seg_in, x1_s, mod_s, lpos8, gate8, rows_out, g_final, 1, tp // lb)

    kv_shape = lambda b, s: (1, b, s, ATT_KV_HEADS, HEAD_DIM)
    st_shape = lambda b: (1, b, RET_HEADS, RET_DK, RET_DV)
    return (y_p.reshape(batch, seq, D_MODEL), y_s.reshape(db, dt, D_MODEL),
            k_p.reshape(kv_shape(batch, seq)), v_p.reshape(kv_shape(batch, seq)),
            ikw_p[:, :IDX_DIM].reshape(1, batch, seq, IDX_DIM), st_p.reshape(st_shape(batch)),
            k_s.reshape(kv_shape(db, dt)), v_s.reshape(kv_shape(db, dt)),
            ikw_s[:, :IDX_DIM].reshape(1, db, dt, IDX_DIM), st_s.reshape(st_shape(db)))
```

```python
import functools

import jax
import jax.numpy as jnp
import numpy as np
from jax import lax
from jax.experimental import pallas as pl
from jax.experimental.pallas import tpu as pltpu

F32 = jnp.float32
BF16 = jnp.bfloat16
I32 = jnp.int32

D_MODEL = 1024
PAST_LEN = 8192
PAGE_SIZE = 128
ATT_HEADS = 8
ATT_KV_HEADS = 2
HEAD_DIM = 64
ROPE_DIM = HEAD_DIM // 4
ROPE_THETA = 500000.0
IDX_HEADS = 8
IDX_DIM = 64
IDX_ROPE_DIM = IDX_DIM // 4
TOPK_MAX = 256
RET_HEADS = 8
RET_DK = 64
RET_DV = 128
RET_THETA = 10000.0
RET_CHUNK = 128
N_EXPERTS = 32
TOP_K = 4
D_FF = D_MODEL
SWIGLU_LIMIT = 7.0
SWIGLU_ALPHA = 1.702
NORM_EPS = 1e-6
GN_EPS = 1e-5
ATT_OUT = ATT_HEADS * HEAD_DIM
RET_OUT = RET_HEADS * RET_DV
IN_SPLITS = (ATT_HEADS * HEAD_DIM, ATT_KV_HEADS * HEAD_DIM, ATT_KV_HEADS * HEAD_DIM,
             IDX_HEADS * IDX_DIM, IDX_DIM, IDX_HEADS,
             RET_HEADS * RET_DK, RET_HEADS * RET_DK, RET_OUT, RET_OUT, D_MODEL, D_MODEL)

LANES = 128
MASK_NEG = -1e30
FLT_MAX = 3.4028234663852886e38
SELECT_UNROLL = 4
QUARTER_SEARCH_MAX_KEYS = 768
CAUSAL_VARIANTS = 8
IDX_KEY_CHUNK = 256
VMEM_LIMIT = 56 * 1024 * 1024

TOK_BLOCK = 256
PROMPT_TOK_BLOCK = 512
Q_BLOCK = 128
IDX_Q_ROWS = 128
MOE_ROWS = 512
SEG_ALIGN = 8
SEG_BITS = 6
SEG_ROWS = TOP_K * TOK_BLOCK + N_EXPERTS * SEG_ALIGN

_W_GROUPS = (("q", 1024), ("iq", 1024), ("kvi", 384), ("rq", 512), ("rk", 512),
             ("rv", 1024), ("rg", 1024), ("ga", 1024), ("gb", 1024))
PROJ_COLS = 512
_W_OFF = {}
_off = 0
for _n, _w in _W_GROUPS:
    _W_OFF[_n] = (_off, _w)
    _off += _w
W_COLS = _off


def _cparams(*sem):
    return pltpu.CompilerParams(dimension_semantics=sem, vmem_limit_bytes=VMEM_LIMIT)


def _adaln_kernel(c_ref, w_ref, b_ref, o_ref):
    c = c_ref[...]
    s = c * jax.nn.sigmoid(c)
    o_ref[...] = jnp.dot(s, w_ref[...], preferred_element_type=F32, precision=lax.Precision.HIGHEST) + b_ref[...]


def _adaln(c_all, w_ada, b_ada):
    n = c_all.shape[0]
    nb = 1536
    return pl.pallas_call(
        _adaln_kernel,
        grid=(6 * D_MODEL // nb,),
        in_specs=[pl.BlockSpec((n, D_MODEL), lambda j: (0, 0)),
                  pl.BlockSpec((D_MODEL, nb), lambda j: (0, j)),
                  pl.BlockSpec((1, nb), lambda j: (0, j))],
        out_specs=pl.BlockSpec((n, nb), lambda j: (0, j)),
        out_shape=jax.ShapeDtypeStruct((n, 6 * D_MODEL), F32),
        compiler_params=_cparams("arbitrary"),
        name="adaln",
    )(c_all, w_ada, b_ada.reshape(1, -1))


def _rope_slab(z, c, sa, sb, half):
    return z * c + pltpu.roll(z, LANES - half, 1) * sa + pltpu.roll(z, half, 1) * sb


def _inproj_kernel(x_ref, mod_ref, g_ref, w_ref, ca_ref, saa_ref, sba_ref, cr_ref, sar_ref, sbr_ref,
                   q_ref, k_ref, v_ref, kb_ref, vb_ref, iq_ref, ikw_ref, ikb_ref,
                   rq_ref, rk_ref, rv_ref, sg_ref, sga_ref, sgb_ref):
    x = x_ref[...]
    ms = jnp.mean(x * x, axis=-1, keepdims=True)
    y = x * lax.rsqrt(ms + NORM_EPS) * g_ref[...]
    h = (y * (1.0 + mod_ref[:, D_MODEL:2 * D_MODEL]) + mod_ref[:, 0:D_MODEL]).astype(BF16)

    def slabs(name):
        c0, width = _W_OFF[name]
        step = min(width, PROJ_COLS)
        for j in range(width // step):
            z = jnp.dot(h, w_ref[:, c0 + j * step:c0 + (j + 1) * step], preferred_element_type=F32)
            for s in range(step // LANES):
                yield j * (step // LANES) + s, z[:, s * LANES:(s + 1) * LANES]

    ca, saa, sba = ca_ref[...], saa_ref[...], sba_ref[...]
    cr, sar, sbr = cr_ref[...], sar_ref[...], sbr_ref[...]
    att_half, ret_half = ROPE_DIM // 2, RET_DK // 2
    lane = lax.broadcasted_iota(I32, (x.shape[0], LANES), 1)
    sl = lambda s: slice(s * LANES, (s + 1) * LANES)

    for s, z in slabs("q"):
        q_ref[:, sl(s)] = (_rope_slab(z, ca, saa, sba, att_half) * 0.125).astype(BF16)
    for s, z in slabs("iq"):
        iq_ref[:, sl(s)] = (_rope_slab(z, ca, saa, sba, att_half) * 0.125).astype(BF16)
    (_, zk), (_, zv), (_, zi) = slabs("kvi")
    kk = _rope_slab(zk, ca, saa, sba, att_half)
    for j in range(ATT_KV_HEADS):
        k_ref[:, j, :] = kk[:, j * HEAD_DIM:(j + 1) * HEAD_DIM]
        v_ref[:, j, :] = zv[:, j * HEAD_DIM:(j + 1) * HEAD_DIM]
    kb_ref[...] = kk.astype(BF16)
    vb_ref[...] = zv.astype(BF16)
    zr = _rope_slab(zi, ca, saa, sba, att_half)
    ikw_ref[...] = jnp.where(lane < IDX_DIM, zr, zi)
    ikb_ref[...] = jnp.where(lane < IDX_DIM, zr, 0.0).astype(BF16)
    for s, z in slabs("rq"):
        rq_ref[:, sl(s)] = _rope_slab(z, cr, sar, sbr, ret_half).astype(BF16)
    for s, z in slabs("rk"):
        rk_ref[:, sl(s)] = (_rope_slab(z, cr, sar, sbr, ret_half) * 0.125).astype(BF16)
    for s, z in slabs("rv"):
        rv_ref[:, sl(s)] = z.astype(BF16)
    for s, z in slabs("rg"):
        sg_ref[:, sl(s)] = (z * jax.nn.sigmoid(z)).astype(BF16)
    for s, z in slabs("ga"):
        sga_ref[:, sl(s)] = jax.nn.sigmoid(z).astype(BF16)
    for s, z in slabs("gb"):
        sgb_ref[:, sl(s)] = jax.nn.sigmoid(z).astype(BF16)


def _inproj(x, mod3, g, w_packed, tabs_att, tabs_ret, tm, blocks_per_mod, tab_blocks):
    t = x.shape[0]
    nblk = t // tm
    mod_rows = mod3.shape[1]
    tab_spec = pl.BlockSpec((tm, LANES), lambda i: (i % tab_blocks, 0))
    row = lambda n: pl.BlockSpec((tm, n), lambda i: (i, 0))
    kv_spec = pl.BlockSpec((tm, ATT_KV_HEADS, HEAD_DIM), lambda i: (i, 0, 0))
    out_defs = [(1024, BF16), (128, F32), (128, F32), (128, BF16), (128, BF16), (1024, BF16), (128, F32),
                (128, BF16), (512, BF16), (512, BF16), (1024, BF16), (1024, BF16), (1024, BF16), (1024, BF16)]
    return pl.pallas_call(
        _inproj_kernel,
        grid=(nblk,),
        in_specs=[row(D_MODEL),
                  pl.BlockSpec((None, mod_rows, 6 * D_MODEL), lambda i: (i // blocks_per_mod, 0, 0)),
                  pl.BlockSpec((1, D_MODEL), lambda i: (0, 0)),
                  pl.BlockSpec((D_MODEL, W_COLS), lambda i: (0, 0), pipeline_mode=pl.Buffered(1))]
                 + [tab_spec] * 6,
        out_specs=[kv_spec if j in (1, 2) else row(n) for j, (n, _) in enumerate(out_defs)],
        out_shape=[jax.ShapeDtypeStruct((t, ATT_KV_HEADS, HEAD_DIM) if j in (1, 2) else (t, n), d)
                   for j, (n, d) in enumerate(out_defs)],
        compiler_params=_cparams("parallel"),
        name="inproj",
    )(x, mod3, g, w_packed, *tabs_att, *tabs_ret)


def _count(score_ref, n, pred):
    acc = jnp.zeros((score_ref.shape[0], LANES), F32)
    for c in range(n // LANES):
        acc = acc + jnp.where(pred(score_ref[:, c * LANES:(c + 1) * LANES]), 1.0, 0.0)
    return jnp.sum(acc, axis=1, keepdims=True)


def _kth_largest(score_ref, n, k, quarters):
    sc = score_ref[:, :n]
    finite = sc > -jnp.inf
    n_fin = jnp.sum(jnp.where(finite, 1.0, 0.0), axis=1, keepdims=True)
    n_pos = jnp.sum(jnp.where(sc > 0.0, 1.0, 0.0), axis=1, keepdims=True)
    n_nonneg = jnp.sum(jnp.where(sc >= 0.0, 1.0, 0.0), axis=1, keepdims=True)
    mx = jnp.max(sc, axis=1, keepdims=True)
    mn = jnp.min(jnp.where(finite, sc, jnp.inf), axis=1, keepdims=True)
    small = n_fin <= k
    positive = n_pos >= k
    at_zero = jnp.logical_and(jnp.logical_not(positive), n_nonneg >= k)
    lo = jnp.where(positive, 0.0, mn)
    hi = jnp.where(positive, mx + (jnp.abs(mx) * 2.0 ** -20 + 2.0 ** -100), 0.0)
    lo = jnp.where(at_zero, 0.0, lo)
    done = jnp.where(jnp.logical_or(small, at_zero), 1.0, 0.0)

    def cond(state):
        return jnp.min(state[2]) < 0.5

    def body(state):
        lo, hi, done = state
        for _ in range(SELECT_UNROLL):
            mid = 0.5 * lo + 0.5 * hi
            cands = (0.5 * lo + 0.5 * mid, mid, 0.5 * mid + 0.5 * hi) if quarters else (mid,)
            stuck = jnp.logical_or(mid <= lo, mid >= hi)
            live = jnp.logical_and(done < 0.5, jnp.logical_not(stuck))
            new_lo, hit, ges = lo, jnp.zeros_like(lo), []
            for c in cands:
                cnt = _count(score_ref, n, lambda s, c=c: s >= c)
                ges.append(cnt >= k)
                new_lo = jnp.where(ges[-1], c, new_lo)
                hit = jnp.where(ges[-1], jnp.where(cnt == k, 1.0, 0.0), hit)
            new_hi = hi
            for c, ge in zip(reversed(cands), reversed(ges)):
                new_hi = jnp.where(ge, new_hi, c)
            lo = jnp.where(live, new_lo, lo)
            hi = jnp.where(live, new_hi, hi)
            done = jnp.where(jnp.logical_or(stuck, hit > 0.5), 1.0, done)
        return lo, hi, done

    lo, _, _ = lax.while_loop(cond, body, (lo, hi, done))
    return jnp.where(small, -FLT_MAX, lo)


def _topk_bias(score_ref, bias_ref, tri_ref, n, k, quarters=False):
    rows = score_ref.shape[0]
    thr = _kth_largest(score_ref, n, k, quarters)
    n_ge = _count(score_ref, n, lambda s: s >= thr)
    has_ties = jnp.max(jnp.where(n_ge > k, 1.0, 0.0)) > 0.5

    @pl.when(jnp.logical_not(has_ties))
    def _():
        for c in range(n // LANES):
            sl = slice(c * LANES, (c + 1) * LANES)
            bias_ref[:, sl] = jnp.where(score_ref[:, sl] >= thr, 0.0, MASK_NEG)

    @pl.when(has_ties)
    def _():
        need = k - _count(score_ref, n, lambda s: s > thr)
        run = jnp.zeros((rows, 1), F32)
        for c in range(n // LANES):
            sl = slice(c * LANES, (c + 1) * LANES)
            sc = score_ref[:, sl]
            eq = sc == thr
            eqf = jnp.where(eq, 1.0, 0.0)
            before = jnp.dot(eqf.astype(BF16), tri_ref[...], preferred_element_type=F32) + run
            take = jnp.logical_or(sc > thr, jnp.logical_and(eq, before < need))
            bias_ref[:, sl] = jnp.where(take, 0.0, MASK_NEG)
            run = run + jnp.sum(eqf, axis=1, keepdims=True)


def _dsa_prompt_kernel(q_ref, iq_ref, ikw_ref, kb_ref, vb_ref, ikb_ref, tri_ref, o_ref, score_ref, bias_ref):
    qb, s_len = score_ref.shape
    i = pl.program_id(1)
    nqb = s_len // qb
    per_variant = nqb // CAUSAL_VARIANTS
    for v in range(CAUSAL_VARIANTS):
        pl.when(i // per_variant == v)(
            functools.partial(_dsa_prompt_body, q_ref, iq_ref, ikw_ref, kb_ref, vb_ref, ikb_ref, tri_ref, o_ref,
                              score_ref, bias_ref, (v + 1) * per_variant * qb, min(TOPK_MAX, s_len // 4)))


def _dsa_prompt_body(q_ref, iq_ref, ikw_ref, kb_ref, vb_ref, ikb_ref, tri_ref, o_ref, score_ref, bias_ref,
                     n_keys, topk):
    qb = score_ref.shape[0]
    i = pl.program_id(1)
    w = ikw_ref[:, IDX_DIM:IDX_DIM + IDX_HEADS] * (IDX_HEADS ** -0.5)
    nt = (((1,), (1,)), ((), ()))
    kc = IDX_KEY_CHUNK
    qr = IDX_Q_ROWS
    for c in range(n_keys // kc):
        ikc = ikb_ref[c * kc:(c + 1) * kc, :]
        kpos = c * kc + lax.broadcasted_iota(I32, (qr, kc), 1)
        for r0 in range(0, qb, qr):
            acc = jnp.zeros((qr, kc), F32)
            for h in range(IDX_HEADS):
                d = lax.dot_general(iq_ref[r0:r0 + qr, h * LANES:(h + 1) * LANES], ikc, nt,
                                    preferred_element_type=F32)
                acc = acc + jnp.maximum(d, 0.0) * w[r0:r0 + qr, h:h + 1]
            qpos = i * qb + r0 + lax.broadcasted_iota(I32, (qr, kc), 0)
            score_ref[r0:r0 + qr, c * kc:(c + 1) * kc] = jnp.where(kpos <= qpos, acc, -jnp.inf)

    _topk_bias(score_ref, bias_ref, tri_ref, n_keys, topk, quarters=n_keys <= QUARTER_SEARCH_MAX_KEYS)

    kb = kb_ref[0:n_keys, :]
    vb = vb_ref[0:n_keys, :]
    bias = bias_ref[:, 0:n_keys]
    lane = lax.broadcasted_iota(I32, (qb, LANES), 1)
    heads = []
    for h in range(ATT_HEADS):
        s = lax.dot_general(q_ref[:, h * LANES:(h + 1) * LANES], kb, nt, preferred_element_type=F32) + bias
        m = jnp.max(s, axis=1, keepdims=True)
        p = jnp.exp(s - m)
        l = jnp.sum(p, axis=1, keepdims=True)
        heads.append(jnp.dot(p.astype(BF16), vb, preferred_element_type=F32) / l)
    group = ATT_HEADS // ATT_KV_HEADS
    for pp in range(ATT_HEADS // 2):
        a, b = heads[2 * pp], heads[2 * pp + 1]
        if (2 * pp) // group == 0:
            slab = jnp.where(lane < HEAD_DIM, a, pltpu.roll(b, HEAD_DIM, 1))
        else:
            slab = jnp.where(lane < HEAD_DIM, pltpu.roll(a, HEAD_DIM, 1), b)
        o_ref[:, pp * LANES:(pp + 1) * LANES] = slab.astype(BF16)


def _dsa_prompt(q, iq, ikw, kb, vb, ikb, tri, batch, seq):
    nqb = seq // Q_BLOCK
    qrow = lambda n: pl.BlockSpec((Q_BLOCK, n), lambda b, i: (b * nqb + i, 0))
    keys = pl.BlockSpec((seq, LANES), lambda b, i: (b, 0))
    return pl.pallas_call(
        _dsa_prompt_kernel,
        grid=(batch, nqb),
        in_specs=[qrow(1024), qrow(1024), qrow(LANES), keys, keys, keys,
                  pl.BlockSpec((LANES, LANES), lambda b, i: (0, 0))],
        out_specs=qrow(ATT_OUT),
        out_shape=jax.ShapeDtypeStruct((batch * seq, ATT_OUT), BF16),
        scratch_shapes=[pltpu.VMEM((Q_BLOCK, seq), F32), pltpu.VMEM((Q_BLOCK, seq), F32)],
        compiler_params=_cparams("parallel", "arbitrary"),
        name="dsa_prompt",
    )(q, iq, ikw, kb, vb, ikb, tri)


def _dsa_sample_kernel(pt_ref, qs_ref, iqs_ref, ws_ref, knew_ref, vnew_ref, iknew_ref, cik_hbm, ck_hbm, cv_hbm,
                       tri_ref, o_ref, ikbuf, kbuf, vbuf, ikt, kt, vt, sems, key_ref, bias_ref):
    db = pl.program_id(0)
    n_pages = ikbuf.shape[1]
    t = key_ref.shape[0]
    n_past = n_pages * PAGE_SIZE
    last = pl.num_programs(0) - 1

    def fetch(src, dst_of_page, sem, req):
        def body(p, carry):
            pltpu.make_async_copy(src.at[pt_ref[req, p]], dst_of_page(p), sem).start()
            return carry
        lax.fori_loop(0, n_pages, body, 0)

    def wait_all(src, dst, sem):
        pltpu.make_async_copy(src.at[pl.ds(0, n_pages)], dst, sem).wait()

    fetch_ik = lambda req, slot: fetch(cik_hbm, lambda p: ikbuf.at[slot, p], sems.at[slot], req)
    fetch_k = lambda req: fetch(ck_hbm, lambda p: kbuf.at[p], sems.at[2], req)
    fetch_v = lambda req: fetch(cv_hbm, lambda p: vbuf.at[p], sems.at[3], req)

    @pl.when(db == 0)
    def _():
        fetch_ik(0, 0)
        fetch_k(0)
        fetch_v(0)

    @pl.when(db < last)
    def _():
        fetch_ik(db + 1, (db + 1) % 2)

    slot = db % 2
    wait_all(cik_hbm, ikbuf.at[slot], sems.at[slot])

    nt = (((1,), (1,)), ((), ()))
    page = lambda p: slice(p * PAGE_SIZE, (p + 1) * PAGE_SIZE)
    for p in range(n_pages):
        ikt[:, page(p)] = ikbuf[slot, p].astype(BF16)
    iqs = iqs_ref[...]
    wcol = ws_ref[...]
    d_past = jnp.maximum(jnp.dot(iqs, ikt[...], preferred_element_type=F32), 0.0) * wcol
    d_new = jnp.maximum(jnp.dot(iqs, iknew_ref[...], preferred_element_type=F32), 0.0) * wcol
    s_past = d_past[0:t]
    s_new = d_new[0:t]
    for h in range(1, IDX_HEADS):
        s_past = s_past + d_past[h * t:(h + 1) * t]
        s_new = s_new + d_new[h * t:(h + 1) * t]
    row = lax.broadcasted_iota(I32, (t, LANES), 0)
    lane = lax.broadcasted_iota(I32, (t, LANES), 1)
    new_ok = lane <= row
    key_ref[:, 0:n_past] = s_past
    key_ref[:, n_past:n_past + LANES] = jnp.where(new_ok, s_new, -jnp.inf)
    _topk_bias(key_ref, bias_ref, tri_ref, n_past + LANES, min(TOPK_MAX, (n_past + t) // 4), quarters=True)

    def stage(buf, dst):
        for p in range(n_pages):
            for j in range(ATT_KV_HEADS):
                dst[j, :, page(p)] = buf[p, j].astype(BF16)

    wait_all(ck_hbm, kbuf, sems.at[2])
    stage(kbuf, kt)

    @pl.when(db < last)
    def _():
        fetch_k(db + 1)

    wait_all(cv_hbm, vbuf, sems.at[3])
    stage(vbuf, vt)

    @pl.when(db < last)
    def _():
        fetch_v(db + 1)

    rows_per_kv = qs_ref.shape[0] // ATT_KV_HEADS
    bias = jnp.concatenate([bias_ref[...]] * (rows_per_kv // t), axis=0)
    for j in range(ATT_KV_HEADS):
        qj = qs_ref[j * rows_per_kv:(j + 1) * rows_per_kv, :]
        sp = jnp.dot(qj, kt[j], preferred_element_type=F32) + bias[:, 0:n_past]
        sn = jnp.dot(qj, knew_ref[j], preferred_element_type=F32) + bias[:, n_past:n_past + LANES]
        m = jnp.maximum(jnp.max(sp, axis=1, keepdims=True), jnp.max(sn, axis=1, keepdims=True))
        pp = jnp.exp(sp - m)
        pn = jnp.exp(sn - m)
        l = jnp.sum(pp, axis=1, keepdims=True) + jnp.sum(pn, axis=1, keepdims=True)
        o = (lax.dot_general(pp.astype(BF16), vt[j], nt, preferred_element_type=F32)
             + lax.dot_general(pn.astype(BF16), vnew_ref[j], nt, preferred_element_type=F32))
        o_ref[j * rows_per_kv:(j + 1) * rows_per_kv, :] = o / l


def _dsa_sample(page_table, qs, iqs, ws, knew, vnew, iknew, cache_ik, cache_k, cache_v, tri, t):
    db, n_pages = page_table.shape
    rows = qs.shape[1]
    per_db = lambda r, n: pl.BlockSpec((None, r, n), lambda b, pt: (b, 0, 0))
    any_spec = pl.BlockSpec(memory_space=pl.ANY)
    n_keys = n_pages * PAGE_SIZE + LANES
    grid_spec = pltpu.PrefetchScalarGridSpec(
        num_scalar_prefetch=1,
        grid=(db,),
        in_specs=[per_db(rows, HEAD_DIM), per_db(rows, IDX_DIM), per_db(rows, 1),
                  pl.BlockSpec((None, ATT_KV_HEADS, HEAD_DIM, LANES), lambda b, pt: (b, 0, 0, 0)),
                  pl.BlockSpec((None, ATT_KV_HEADS, HEAD_DIM, LANES), lambda b, pt: (b, 0, 0, 0)),
                  per_db(IDX_DIM, LANES), any_spec, any_spec, any_spec,
                  pl.BlockSpec((LANES, LANES), lambda b, pt: (0, 0))],
        out_specs=per_db(rows, HEAD_DIM),
        scratch_shapes=[pltpu.VMEM((2, n_pages, IDX_DIM, PAGE_SIZE), F32),
                        pltpu.VMEM((n_pages, ATT_KV_HEADS, HEAD_DIM, PAGE_SIZE), F32),
                        pltpu.VMEM((n_pages, ATT_KV_HEADS, HEAD_DIM, PAGE_SIZE), F32),
                        pltpu.VMEM((IDX_DIM, n_pages * PAGE_SIZE), BF16),
                        pltpu.VMEM((ATT_KV_HEADS, HEAD_DIM, n_pages * PAGE_SIZE), BF16),
                        pltpu.VMEM((ATT_KV_HEADS, HEAD_DIM, n_pages * PAGE_SIZE), BF16),
                        pltpu.SemaphoreType.DMA((4,)),
                        pltpu.VMEM((t, n_keys), F32),
                        pltpu.VMEM((t, n_keys), F32)])
    return pl.pallas_call(
        _dsa_sample_kernel,
        grid_spec=grid_spec,
        out_shape=jax.ShapeDtypeStruct((db, rows, HEAD_DIM), F32),
        compiler_params=_cparams("arbitrary"),
        name="dsa_sample",
    )(page_table, qs, iqs, ws, knew, vnew, iknew, cache_ik, cache_k, cache_v, tri)


def _retention_kernel(has_init, rq_ref, rk_ref, rv_ref, sg_ref, decay_ref, qdec_ref, kdec_ref, gst_ref, *rest):
    if has_init:
        init_ref, o_ref, st_ref, state = rest
    else:
        o_ref, st_ref, state = rest
    c = pl.program_id(1)

    @pl.when(c == 0)
    def _():
        if has_init:
            state[...] = init_ref[...]
        else:
            state[...] = jnp.zeros_like(state)

    nt = (((1,), (1,)), ((), ()))
    tn = (((0,), (0,)), ((), ()))
    rows = rq_ref.shape[0]
    lane = lax.broadcasted_iota(I32, (rows, LANES), 1)
    for p in range(RET_HEADS // 2):
        sl = slice(p * LANES, (p + 1) * LANES)
        qp = rq_ref[:, sl]
        kp = rk_ref[:, sl]
        qd = (qp.astype(F32) * qdec_ref[:, sl]).astype(BF16)
        kd = (kp.astype(F32) * kdec_ref[:, sl]).astype(BF16)
        s_old = state[p]
        s_old_b = s_old.astype(BF16)
        s_new = s_old * gst_ref[p]
        for e in range(2):
            h = 2 * p + e
            hs = slice(h * LANES, (h + 1) * LANES)
            mine = jnp.where((lane >= e * RET_DK) & (lane < (e + 1) * RET_DK), 1.0, 0.0).astype(BF16)
            sc = lax.dot_general(qp * mine, kp, nt, preferred_element_type=F32) * decay_ref[h]
            vh = rv_ref[:, hs]
            o = (jnp.dot(sc.astype(BF16), vh, preferred_element_type=F32)
                 + jnp.dot(qd * mine, s_old_b, preferred_element_type=F32))
            s_new = s_new + lax.dot_general(kd * mine, vh, tn, preferred_element_type=F32)
            mu = jnp.mean(o, axis=-1, keepdims=True)
            var = jnp.mean(jnp.square(o - mu), axis=-1, keepdims=True)
            on = (o - mu) * lax.rsqrt(var + GN_EPS)
            o_ref[:, hs] = (on * sg_ref[:, hs].astype(F32)).astype(BF16)
        state[p] = s_new

    @pl.when(c == pl.num_programs(1) - 1)
    def _():
        st_ref[...] = state[...]


def _retention(rq, rk, rv, sg, tables, init, batch, n_chunks):
    decay, qdec, kdec, gst = tables
    cr = RET_CHUNK
    rowspec = lambda n: pl.BlockSpec((cr, n), lambda b, c: (b * n_chunks + c, 0))
    const = lambda shape: pl.BlockSpec(shape, lambda b, c: (0,) * len(shape))
    st_spec = pl.BlockSpec((None, RET_HEADS // 2, LANES, LANES), lambda b, c: (b, 0, 0, 0))
    in_specs = [rowspec(512), rowspec(512), rowspec(RET_OUT), rowspec(RET_OUT),
                const(decay.shape), const(qdec.shape), const(kdec.shape), const(gst.shape)]
    args = [rq, rk, rv, sg, decay, qdec, kdec, gst]
    if init is not None:
        in_specs.append(st_spec)
        args.append(init)
    return pl.pallas_call(
        functools.partial(_retention_kernel, init is not None),
        grid=(batch, n_chunks),
        in_specs=in_specs,
        out_specs=[rowspec(RET_OUT), st_spec],
        out_shape=[jax.ShapeDtypeStruct((batch * n_chunks * cr, RET_OUT), BF16),
                   jax.ShapeDtypeStruct((batch, RET_HEADS // 2, LANES, LANES), F32)],
        scratch_shapes=[pltpu.VMEM((RET_HEADS // 2, LANES, LANES), F32)],
        compiler_params=_cparams("parallel", "arbitrary"),
        name="retention",
    )(*args)


def _retention_tables(c_eff):
    lg = jnp.log(1.0 - 2.0 ** (-5.0 - jnp.arange(RET_HEADS, dtype=F32)))
    i = jnp.arange(RET_CHUNK, dtype=F32)
    diff = i[:, None] - i[None, :]
    decay = jnp.where(diff >= 0, jnp.exp(jnp.maximum(diff, 0.0)[None] * lg[:, None, None]), 0.0)
    q_decay = jnp.exp((i + 1.0)[:, None] * lg[None, :])
    k_decay = jnp.exp((c_eff - 1.0 - i)[:, None] * lg[None, :])
    qdec = jnp.repeat(q_decay, RET_DK, axis=1)
    kdec = jnp.repeat(k_decay, RET_DK, axis=1)
    g_state = jnp.exp(c_eff * lg)
    gst = jnp.broadcast_to(jnp.repeat(g_state, RET_DK).reshape(RET_HEADS // 2, LANES, 1),
                           (RET_HEADS // 2, LANES, LANES))
    return decay, qdec, kdec, gst


def _pack_bf16_pairs(lo, hi):
    return pltpu.pack_elementwise([lo, hi], packed_dtype=BF16)


def _unpack_bf16_pairs(words):
    return tuple(pltpu.unpack_elementwise(words, index=j, packed_dtype=BF16, unpacked_dtype=F32).astype(BF16)
                 for j in range(2))


def _post_kernel(a_ref, r_ref, sga_ref, sgb_ref, x_ref, mod_ref, wpa_ref, wpb_ref, wo_ref, g_ref, wr_ref, br_ref,
                 ltri_ref, cnt0_ref, x1_ref, h2w_ref, idx_ref, gate_ref, rank_ref, cnt_ref, cnt):
    @pl.when(pl.program_id(0) == 0)
    def _():
        cnt[...] = cnt0_ref[...]

    pa = jnp.dot(a_ref[...], wpa_ref[...], preferred_element_type=F32)
    pb = jnp.dot(r_ref[...], wpb_ref[...], preferred_element_type=F32)
    merged = sga_ref[...].astype(F32) * pa + sgb_ref[...].astype(F32) * pb
    gt1 = mod_ref[:, 2 * D_MODEL:3 * D_MODEL]
    x1 = x_ref[...] + gt1 * jnp.dot(merged.astype(BF16), wo_ref[...], preferred_element_type=F32)
    x1_ref[...] = x1
    ms = jnp.mean(x1 * x1, axis=-1, keepdims=True)
    y = x1 * lax.rsqrt(ms + NORM_EPS) * g_ref[...]
    h2 = y * (1.0 + mod_ref[:, 4 * D_MODEL:5 * D_MODEL]) + mod_ref[:, 3 * D_MODEL:4 * D_MODEL]
    half = D_MODEL // 2
    h2w_ref[...] = _pack_bf16_pairs(h2[:, :half], h2[:, half:])
    h_hi = h2.astype(BF16)
    h_lo = (h2 - h_hi.astype(F32)).astype(BF16)
    logits = (jnp.dot(h_hi, wr_ref[0], preferred_element_type=F32) + jnp.dot(h_hi, wr_ref[1], preferred_element_type=F32)
              + jnp.dot(h_lo, wr_ref[0], preferred_element_type=F32) + br_ref[...])
    lane = lax.broadcasted_iota(I32, logits.shape, 1).astype(F32)
    idx_out = jnp.zeros(logits.shape, F32)
    val_out = jnp.zeros(logits.shape, F32)
    chosen = []
    top = None
    for j in range(TOP_K):
        m = jnp.max(logits, axis=1, keepdims=True)
        am = jnp.min(jnp.where(logits == m, lane, float(LANES)), axis=1, keepdims=True)
        if j == 0:
            top = m
        idx_out = jnp.where(lane == j, am, idx_out)
        val_out = jnp.where(lane == j, jnp.exp(m - top), val_out)
        chosen.append(lane == am)
        logits = jnp.where(chosen[-1], -jnp.inf, logits)
    idx_ref[...] = jnp.transpose(idx_out)[0:8, :].astype(I32)
    gate_ref[...] = jnp.transpose(val_out / jnp.sum(val_out, axis=1, keepdims=True))[0:8, :]
    onehot = jnp.where(chosen[0] | chosen[1] | chosen[2] | chosen[3], 1.0, 0.0)
    before = jnp.dot(ltri_ref[...], onehot.astype(BF16), preferred_element_type=F32) + cnt[...]
    rank_out = jnp.zeros(logits.shape, F32)
    for j in range(TOP_K):
        rj = jnp.sum(jnp.where(chosen[j], before, 0.0), axis=1, keepdims=True)
        rank_out = jnp.where(lane == j, rj, rank_out)
    rank_ref[...] = jnp.transpose(rank_out)[0:8, :].astype(I32)
    cnt[...] = cnt[...] + jnp.sum(onehot, axis=0, keepdims=True)
    cnt_ref[...] = cnt[...]


def _post(a, r, sga, sgb, x, mod3, wpa, wpb, wo, g, wr, br, cnt0, tm, blocks_per_mod):
    t = x.shape[0]
    mod_rows = mod3.shape[1]
    ar = jnp.arange(tm)
    ltri = (ar[None, :] < ar[:, None]).astype(BF16)
    row = lambda n: pl.BlockSpec((tm, n), lambda i: (i, 0))
    col8 = pl.BlockSpec((8, tm), lambda i: (0, i))
    const = lambda a_: pl.BlockSpec(a_.shape, lambda i: (0,) * a_.ndim)
    return pl.pallas_call(
        _post_kernel,
        grid=(t // tm,),
        in_specs=[row(ATT_OUT), row(RET_OUT), row(D_MODEL), row(D_MODEL), row(D_MODEL),
                  pl.BlockSpec((None, mod_rows, 6 * D_MODEL), lambda i: (i // blocks_per_mod, 0, 0)),
                  const(wpa), const(wpb), const(wo), const(g), const(wr), const(br), const(ltri), const(cnt0)],
        out_specs=[row(D_MODEL), row(D_MODEL // 2), col8, col8, col8,
                   pl.BlockSpec((1, LANES), lambda i: (0, 0))],
        out_shape=[jax.ShapeDtypeStruct((t, D_MODEL), F32), jax.ShapeDtypeStruct((t, D_MODEL // 2), jnp.uint32),
                   jax.ShapeDtypeStruct((8, t), I32), jax.ShapeDtypeStruct((8, t), F32),
                   jax.ShapeDtypeStruct((8, t), I32), jax.ShapeDtypeStruct((1, LANES), F32)],
        scratch_shapes=[pltpu.VMEM((1, LANES), F32)],
        compiler_params=_cparams("arbitrary"),
        name="post",
    )(a, r, sga, sgb, x, mod3, wpa, wpb, wo, g, wr, br, ltri, cnt0)


def _segment_copies(seg_ref, blk, make_copy):
    for e in range(N_EXPERTS + 1):
        base = (blk * (N_EXPERTS + 1) + e) * 3
        loc, glob, n = seg_ref[base], seg_ref[base + 1], seg_ref[base + 2]
        for k in range(SEG_BITS):
            size = SEG_ALIGN << k

            @pl.when((n & size) != 0)
            def _():
                done = n & (size - 1)
                make_copy(pl.multiple_of(loc + done, SEG_ALIGN), pl.multiple_of(glob + done, SEG_ALIGN),
                          size).start()


def _local_order_matrix(lpos_ref, values, rows):
    lp = lpos_ref[...]
    r = lax.broadcasted_iota(I32, (rows, lp.shape[1]), 0)
    m = jnp.zeros((rows, lp.shape[1]), F32)
    for j in range(TOP_K):
        m = jnp.where(r == lp[j:j + 1, :], values(j), m)
    return m


def _dispatch_kernel(seg_ref, lpos_ref, h2w_ref, xs_in, xs_out, buf, sems):
    del xs_in
    i = pl.program_id(0)
    slot = i % 2
    rows = buf.shape[1]

    def wait_slot(s):
        pltpu.make_async_copy(buf.at[s], xs_out.at[pl.ds(0, rows)], sems.at[s]).wait()

    @pl.when(i >= 2)
    def _():
        wait_slot(slot)

    x = jnp.concatenate(_unpack_bf16_pairs(h2w_ref[...]), axis=1)
    perm = _local_order_matrix(lpos_ref, lambda j: 1.0, rows).astype(BF16)
    xs = jnp.dot(perm, x, preferred_element_type=F32)
    half = D_MODEL // 2
    buf[slot] = _pack_bf16_pairs(xs[:, :half], xs[:, half:])
    _segment_copies(seg_ref, i, lambda loc, glob, size: pltpu.make_async_copy(
        buf.at[slot, pl.ds(loc, size)], xs_out.at[pl.ds(glob, size)], sems.at[slot]))

    @pl.when(i == pl.num_programs(0) - 1)
    def _():
        wait_slot(slot)

        @pl.when(i >= 1)
        def _():
            wait_slot(1 - slot)


def _dispatch(seg, lpos8, h2w, x_sorted):
    t = h2w.shape[0]
    tm = TOK_BLOCK
    grid_spec = pltpu.PrefetchScalarGridSpec(
        num_scalar_prefetch=1,
        grid=(t // tm,),
        in_specs=[pl.BlockSpec((8, tm), lambda i, s: (0, i)),
                  pl.BlockSpec((tm, D_MODEL // 2), lambda i, s: (i, 0)), pl.BlockSpec(memory_space=pl.ANY)],
        out_specs=pl.BlockSpec(memory_space=pl.ANY),
        scratch_shapes=[pltpu.VMEM((2, SEG_ROWS, D_MODEL // 2), jnp.uint32), pltpu.SemaphoreType.DMA((2,))])
    return pl.pallas_call(
        _dispatch_kernel,
        grid_spec=grid_spec,
        out_shape=jax.ShapeDtypeStruct(x_sorted.shape, x_sorted.dtype),
        input_output_aliases={3: 0},
        compiler_params=_cparams("arbitrary"),
        name="dispatch",
    )(seg, lpos8, h2w, x_sorted)


def _moe_kernel(be_ref, first_ref, nused_ref, x_ref, wgu_ref, bgu_ref, wd_ref, bd_ref, o_ref, wgu_b, wd_b):
    i = pl.program_id(0)

    @pl.when(first_ref[i] == 1)
    def _():
        wgu_b[...] = wgu_ref[...].astype(BF16)
        wd_b[...] = wd_ref[...].astype(BF16)

    @pl.when(i < nused_ref[0])
    def _():
        x = jnp.concatenate(_unpack_bf16_pairs(x_ref[...]), axis=1)
        gu = jnp.dot(x, wgu_b[...], preferred_element_type=F32) + bgu_ref[...]
        g = jnp.minimum(gu[:, :D_FF], SWIGLU_LIMIT)
        u = jnp.clip(gu[:, D_FF:], -SWIGLU_LIMIT, SWIGLU_LIMIT)
        act = (u + 1.0) * (g * jax.nn.sigmoid(SWIGLU_ALPHA * g))
        o_ref[...] = jnp.dot(act.astype(BF16), wd_b[...], preferred_element_type=F32) + bd_ref[...]

    @pl.when(i >= nused_ref[0])
    def _():
        o_ref[...] = jnp.zeros_like(o_ref)


def _moe(blk_expert, blk_first, n_used, x_sorted, w_gate_up, b_gate_up, w_down, b_down):
    n_rows = blk_expert.shape[0] * MOE_ROWS
    grid_spec = pltpu.PrefetchScalarGridSpec(
        num_scalar_prefetch=3,
        grid=(n_rows // MOE_ROWS,),
        in_specs=[pl.BlockSpec((MOE_ROWS, D_MODEL // 2), lambda i, be, bf, nu: (i, 0)),
                  pl.BlockSpec((None, D_MODEL, 2 * D_FF), lambda i, be, bf, nu: (be[i], 0, 0)),
                  pl.BlockSpec((None, 1, 2 * D_FF), lambda i, be, bf, nu: (be[i], 0, 0)),
                  pl.BlockSpec((None, D_FF, D_MODEL), lambda i, be, bf, nu: (be[i], 0, 0)),
                  pl.BlockSpec((None, 1, D_MODEL), lambda i, be, bf, nu: (be[i], 0, 0))],
        out_specs=pl.BlockSpec((MOE_ROWS, D_MODEL), lambda i, be, bf, nu: (i, 0)),
        scratch_shapes=[pltpu.VMEM((D_MODEL, 2 * D_FF), BF16), pltpu.VMEM((D_FF, D_MODEL), BF16)])
    return pl.pallas_call(
        _moe_kernel,
        grid_spec=grid_spec,
        out_shape=jax.ShapeDtypeStruct((n_rows, D_MODEL), F32),
        compiler_params=_cparams("arbitrary"),
        name="moe",
    )(blk_expert, blk_first, n_used, x_sorted, w_gate_up, b_gate_up.reshape(N_EXPERTS, 1, -1),
      w_down, b_down.reshape(N_EXPERTS, 1, -1))


def _final_kernel(block_offset, seg_ref, x1_ref, mod_ref, lpos_ref, gate_ref, g_ref, rows_hbm, y_ref, buf, sems):
    i = pl.program_id(0)
    n = pl.num_programs(0)
    rows = buf.shape[1]

    def issue(blk, slot):
        _segment_copies(seg_ref, blk + block_offset, lambda loc, glob, size: pltpu.make_async_copy(
            rows_hbm.at[pl.ds(glob, size)], buf.at[slot, pl.ds(loc, size)], sems.at[slot]))

    @pl.when(i == 0)
    def _():
        issue(0, 0)

    @pl.when(i + 1 < n)
    def _():
        issue(i + 1, (i + 1) % 2)

    slot = i % 2
    pltpu.make_async_copy(rows_hbm.at[pl.ds(0, rows)], buf.at[slot], sems.at[slot]).wait()
    gate = gate_ref[...]
    gmat = _local_order_matrix(lpos_ref, lambda j: gate[j:j + 1, :], rows).astype(BF16)
    moe = lax.dot_general(gmat, buf[slot].astype(BF16), (((0,), (0,)), ((), ())), preferred_element_type=F32)
    x2 = x1_ref[...] + mod_ref[:, 5 * D_MODEL:6 * D_MODEL] * moe
    ms = jnp.mean(x2 * x2, axis=-1, keepdims=True)
    y_ref[...] = x2 * lax.rsqrt(ms + NORM_EPS) * g_ref[...]


def _final(seg, x1, mod3, lpos8, gate8, rows_out, g, blocks_per_mod, block_offset):
    t = x1.shape[0]
    tm = TOK_BLOCK
    mod_rows = mod3.shape[1]
    row = lambda n: pl.BlockSpec((tm, n), lambda i, s: (i, 0))
    col8 = pl.BlockSpec((8, tm), lambda i, s: (0, i + block_offset))
    grid_spec = pltpu.PrefetchScalarGridSpec(
        num_scalar_prefetch=1,
        grid=(t // tm,),
        in_specs=[row(D_MODEL),
                  pl.BlockSpec((None, mod_rows, 6 * D_MODEL), lambda i, s: (i // blocks_per_mod, 0, 0)),
                  col8, col8,
                  pl.BlockSpec((1, D_MODEL), lambda i, s: (0, 0)),
                  pl.BlockSpec(memory_space=pl.ANY)],
        out_specs=row(D_MODEL),
        scratch_shapes=[pltpu.VMEM((2, SEG_ROWS, D_MODEL), F32), pltpu.SemaphoreType.DMA((2,))])
    return pl.pallas_call(
        functools.partial(_final_kernel, block_offset),
        grid_spec=grid_spec,
        out_shape=jax.ShapeDtypeStruct((t, D_MODEL), F32),
        compiler_params=_cparams("arbitrary"),
        name="final",
    )(seg, x1, mod3, lpos8, gate8, g, rows_out)


def _rope_tables(pos, rot_dim, theta, head_dim):
    half = rot_dim // 2
    inv = theta ** (-jnp.arange(half, dtype=F32) * (2.0 / rot_dim))
    ang = pos.astype(F32)[:, None] * inv[None, :]
    cos, sin = jnp.cos(ang), jnp.sin(ang)
    n = pos.shape[0]
    rest = head_dim - rot_dim
    zh = jnp.zeros((n, half), F32)
    c = jnp.concatenate([cos, cos, jnp.ones((n, rest), F32)], axis=1)
    sa = jnp.concatenate([-sin, zh, jnp.zeros((n, rest), F32)], axis=1)
    sb = jnp.concatenate([zh, sin, jnp.zeros((n, rest), F32)], axis=1)
    rep = LANES // head_dim
    return tuple(jnp.tile(a, (1, rep)) for a in (c, sa, sb))


def _pack_w_in(w_in):
    offs = np.cumsum((0,) + IN_SPLITS)
    part = lambda j: w_in[:, offs[j]:offs[j + 1]]
    zero = lambda n: jnp.zeros((D_MODEL, n), w_in.dtype)
    group = ATT_HEADS // ATT_KV_HEADS
    cols = []
    wq = part(0)
    for h in range(ATT_HEADS):
        wh = wq[:, h * HEAD_DIM:(h + 1) * HEAD_DIM]
        cols += [wh, zero(HEAD_DIM)] if h // group == 0 else [zero(HEAD_DIM), wh]
    wiq = part(3)
    for h in range(IDX_HEADS):
        cols += [wiq[:, h * IDX_DIM:(h + 1) * IDX_DIM], zero(LANES - IDX_DIM)]
    cols += [part(1), part(2), part(4), part(5), zero(LANES - IDX_DIM - IDX_HEADS)]
    cols += [part(j) for j in range(6, 12)]
    return jnp.concatenate(cols, axis=1).astype(BF16)


def _heads_major(a, db, t, width):
    heads = a.shape[1] // width
    return a.reshape(db, t, heads, width).transpose(0, 2, 1, 3).reshape(db, heads * t, width)


def _pad_rows(a, db, t, rows):
    return jnp.pad(a.reshape(db, t, -1), ((0, 0), (0, rows - t), (0, 0)))


def kernel(x_prompt, x_sample, cache_k, cache_v, cache_ik, state_ret, page_table, c_prompt, c_sample, norm_mix_g, norm_ffn_g, norm_final_g, w_ada, b_ada, w_in, w_branch_a, w_branch_b, w_out, w_router, b_router, w_gate_up, b_gate_up, w_down, b_down):
    batch, seq, _ = x_prompt.shape
    db, dt, _ = x_sample.shape
    assert w_in.shape[0] == 1, "one layer"
    tp, ts = batch * seq, db * dt
    xp = x_prompt.reshape(tp, D_MODEL)
    xs = x_sample.reshape(ts, D_MODEL)

    mod = _adaln(jnp.concatenate([c_prompt, c_sample], axis=0), w_ada[0], b_ada[0])
    mod_p = mod[:batch].reshape(batch, 1, 6 * D_MODEL)
    mod_s = jnp.repeat(mod[batch:], dt, axis=0).reshape(ts // TOK_BLOCK, TOK_BLOCK, 6 * D_MODEL)
    bpm_p = seq // TOK_BLOCK
    tmp = PROMPT_TOK_BLOCK
    bpm_big = seq // tmp

    w_packed = _pack_w_in(w_in[0].astype(BF16))
    pos_p = jnp.arange(seq)
    pos_s = PAST_LEN + (jnp.arange(TOK_BLOCK) % dt)
    g_mix = norm_mix_g[0].reshape(1, D_MODEL)
    outs_p = _inproj(xp, mod_p, g_mix, w_packed, _rope_tables(pos_p, ROPE_DIM, ROPE_THETA, HEAD_DIM),
                     _rope_tables(pos_p, RET_DK, RET_THETA, RET_DK), TOK_BLOCK, bpm_p, bpm_p)
    outs_s = _inproj(xs, mod_s, g_mix, w_packed, _rope_tables(pos_s, ROPE_DIM, ROPE_THETA, HEAD_DIM),
                     _rope_tables(pos_s, RET_DK, RET_THETA, RET_DK), TOK_BLOCK, 1, 1)
    (q_p, k_p, v_p, kb_p, vb_p, iq_p, ikw_p, ikb_p, rq_p, rk_p, rv_p, sg_p, sga_p, sgb_p) = outs_p
    (q_s, k_s, v_s, kb_s, vb_s, iq_s, ikw_s, ikb_s, rq_s, rk_s, rv_s, sg_s, sga_s, sgb_s) = outs_s

    tri = (jnp.arange(LANES)[:, None] < jnp.arange(LANES)[None, :]).astype(BF16)

    a_p = _dsa_prompt(q_p, iq_p, ikw_p, kb_p, vb_p, ikb_p, tri, batch, seq)
    group = ATT_HEADS // ATT_KV_HEADS
    q4 = q_s.reshape(db, dt, ATT_HEADS, LANES)
    qs = jnp.stack([q4[:, :, h, (h // group) * HEAD_DIM:(h // group + 1) * HEAD_DIM] for h in range(ATT_HEADS)],
                   axis=1).reshape(db, ATT_HEADS * dt, HEAD_DIM)
    iqs = _heads_major(iq_s, db, dt, LANES)[:, :, :IDX_DIM]
    ws = _heads_major(ikw_s[:, IDX_DIM:IDX_DIM + IDX_HEADS] * (IDX_HEADS ** -0.5), db, dt, 1)
    new_t = lambda a: jnp.pad(a.reshape(db, dt, ATT_KV_HEADS, HEAD_DIM).transpose(0, 2, 3, 1),
                              ((0, 0), (0, 0), (0, 0), (0, LANES - dt)))
    iknew_t = jnp.pad(ikb_s[:, :IDX_DIM].reshape(db, dt, IDX_DIM).transpose(0, 2, 1),
                      ((0, 0), (0, 0), (0, LANES - dt)))
    o_s = _dsa_sample(page_table, qs, iqs, ws, new_t(kb_s), new_t(vb_s), iknew_t,
                      cache_ik[0].transpose(0, 2, 1), cache_k[0].transpose(0, 2, 3, 1),
                      cache_v[0].transpose(0, 2, 3, 1), tri, dt)
    a_s = o_s.reshape(db, ATT_HEADS, dt, HEAD_DIM).transpose(0, 2, 1, 3).reshape(ts, ATT_OUT).astype(BF16)

    r_p, st_p = _retention(rq_p, rk_p, rv_p, sg_p, _retention_tables(float(RET_CHUNK)), None, batch,
                           seq // RET_CHUNK)
    pad = lambda a: _pad_rows(a, db, dt, RET_CHUNK).reshape(db * RET_CHUNK, -1)
    r_s, st_s = _retention(pad(rq_s), pad(rk_s), pad(rv_s), pad(sg_s), _retention_tables(float(dt)),
                           state_ret[0].reshape(db, RET_HEADS // 2, LANES, LANES), db, 1)
    r_s = r_s.reshape(db, RET_CHUNK, RET_OUT)[:, :dt].reshape(ts, RET_OUT)

    wr = jnp.pad(w_router[0], ((0, 0), (0, LANES - N_EXPERTS)))
    wr_hi = wr.astype(BF16)
    wr = jnp.stack([wr_hi, (wr - wr_hi.astype(F32)).astype(BF16)])
    br = jnp.concatenate([b_router[0], jnp.full((LANES - N_EXPERTS,), -jnp.inf, F32)]).reshape(1, LANES)
    post_w = (w_branch_a[0].astype(BF16), w_branch_b[0].astype(BF16), w_out[0].astype(BF16),
              norm_ffn_g[0].reshape(1, D_MODEL), wr, br)
    x1_p, h2w_p, idx_p, gate_p, rank_p, cnt_p = _post(a_p, r_p, sga_p, sgb_p, xp, mod_p, *post_w,
                                                      jnp.zeros((1, LANES), F32), tmp, bpm_big)
    x1_s, h2w_s, idx_s, gate_s, rank_s, cnt_all = _post(a_s, r_s, sga_s, sgb_s, xs, mod_s, *post_w, cnt_p,
                                                        TOK_BLOCK, 1)

    del cnt_all
    n_tok, lb = tp + ts, TOK_BLOCK
    nb = n_tok // lb
    idx4 = jnp.concatenate([idx_p[:TOP_K], idx_s[:TOP_K]], axis=1)
    rank4 = jnp.concatenate([rank_p[:TOP_K], rank_s[:TOP_K]], axis=1)
    gate8 = jnp.concatenate([gate_p, gate_s], axis=1)
    hit = idx4[None] == jnp.arange(N_EXPERTS, dtype=I32)[:, None, None]
    bc = jnp.sum(hit.reshape(N_EXPERTS, TOP_K, nb, lb).astype(I32), axis=(1, 3)).T
    run = (bc + SEG_ALIGN - 1) // SEG_ALIGN * SEG_ALIGN
    padded = (jnp.sum(run, axis=0) + MOE_ROWS - 1) // MOE_ROWS * MOE_ROWS
    pend = jnp.cumsum(padded)
    pstart = pend - padded
    loc = jnp.cumsum(run, axis=1) - run
    glob = pstart[None, :] + jnp.cumsum(run, axis=0) - run
    carry = jnp.cumsum(bc, axis=0) - bc
    n_blocks = -(-(n_tok * TOP_K + nb * N_EXPERTS * (SEG_ALIGN - 1)) // MOE_ROWS) + N_EXPERTS
    n_rows = n_blocks * MOE_ROWS
    used = jnp.sum(run, axis=1, keepdims=True)
    runs = jnp.stack([loc, glob, run], axis=-1)
    seg_table = lambda spare: jnp.concatenate(
        [runs, jnp.concatenate([used, spare, SEG_ROWS - used], axis=1)[:, None, :]], axis=1).reshape(-1).astype(I32)
    spare_out = n_rows + (jnp.arange(nb, dtype=I32)[:, None] % 2) * (SEG_ROWS // 2)
    seg_out, seg_in = seg_table(spare_out), seg_table(jnp.zeros((nb, 1), I32))
    shift = jnp.repeat(loc - carry, lb, axis=0).T
    lpos4 = rank4 + jnp.sum(jnp.where(hit, shift[:, None, :], 0), axis=0)
    lpos8 = jnp.pad(lpos4, ((0, 8 - TOP_K), (0, 0))).astype(I32)
    blk_start = jnp.arange(n_blocks, dtype=I32) * MOE_ROWS
    blk_expert = jnp.minimum(jnp.sum((blk_start[:, None] >= pend[None, :]).astype(I32), axis=1), N_EXPERTS - 1)
    blk_first = jnp.concatenate([jnp.ones((1,), I32), (blk_expert[1:] != blk_expert[:-1]).astype(I32)])
    n_used = (pend[-1] // MOE_ROWS).astype(I32).reshape(1)
    x_sorted = jnp.zeros((n_rows + SEG_ROWS, D_MODEL // 2), jnp.uint32)
    x_sorted = _dispatch(seg_out, lpos8, jnp.concatenate([h2w_p, h2w_s], axis=0), x_sorted)
    rows_out = _moe(blk_expert, blk_first, n_used, x_sorted, w_gate_up[0], b_gate_up[0], w_down[0], b_down[0])

    g_final = norm_final_g.reshape(1, D_MODEL)
    y_p = _final(seg_in, x1_p, mod_p, lpos8, gate8, rows_out, g_final, bpm_p, 0)
    y_s = _final(seg_in, x1_s, mod_s, lpos8, gate8, rows_out, g_final, 1, tp // lb)

    kv_shape = lambda b, s: (1, b, s, ATT_KV_HEADS, HEAD_DIM)
    st_shape = lambda b: (1, b, RET_HEADS, RET_DK, RET_DV)
    return (y_p.reshape(batch, seq, D_MODEL), y_s.reshape(db, dt, D_MODEL),
            k_p.reshape(kv_shape(batch, seq)), v_p.reshape(kv_shape(batch, seq)),
            ikw_p[:, :IDX_DIM].reshape(1, batch, seq, IDX_DIM), st_p.reshape(st_shape(batch)),
            k_s.reshape(kv_shape(db, dt)), v_s.reshape(kv_shape(db, dt)),
            ikw_s[:, :IDX_DIM].reshape(1, db, dt, IDX_DIM), st_s.reshape(st_shape(db)))
```

```python
import functools

import jax
import jax.numpy as jnp
import numpy as np
from jax import lax
from jax.experimental import pallas as pl
from jax.experimental.pallas import tpu as pltpu

F32 = jnp.float32
BF16 = jnp.bfloat16
I32 = jnp.int32

D_MODEL = 1024
PAST_LEN = 8192
PAGE_SIZE = 128
ATT_HEADS = 8
ATT_KV_HEADS = 2
HEAD_DIM = 64
ROPE_DIM = HEAD_DIM // 4
ROPE_THETA = 500000.0
IDX_HEADS = 8
IDX_DIM = 64
IDX_ROPE_DIM = IDX_DIM // 4
TOPK_MAX = 256
RET_HEADS = 8
RET_DK = 64
RET_DV = 128
RET_THETA = 10000.0
RET_CHUNK = 128
N_EXPERTS = 32
TOP_K = 4
D_FF = D_MODEL
SWIGLU_LIMIT = 7.0
SWIGLU_ALPHA = 1.702
NORM_EPS = 1e-6
GN_EPS = 1e-5
ATT_OUT = ATT_HEADS * HEAD_DIM
RET_OUT = RET_HEADS * RET_DV
IN_SPLITS = (ATT_HEADS * HEAD_DIM, ATT_KV_HEADS * HEAD_DIM, ATT_KV_HEADS * HEAD_DIM,
             IDX_HEADS * IDX_DIM, IDX_DIM, IDX_HEADS,
             RET_HEADS * RET_DK, RET_HEADS * RET_DK, RET_OUT, RET_OUT, D_MODEL, D_MODEL)

LANES = 128
MASK_NEG = -1e30
FLT_MAX = 3.4028234663852886e38
SELECT_UNROLL = 4
CAUSAL_VARIANTS = 8
IDX_KEY_CHUNK = 256
VMEM_LIMIT = 56 * 1024 * 1024

TOK_BLOCK = 256
PROMPT_TOK_BLOCK = 512
Q_BLOCK = 128
IDX_Q_ROWS = 128
MOE_ROWS = 512
SEG_ALIGN = 8
SEG_BITS = 6
SEG_ROWS = TOP_K * TOK_BLOCK + N_EXPERTS * SEG_ALIGN

_W_GROUPS = (("q", 1024), ("iq", 1024), ("kvi", 384), ("rq", 512), ("rk", 512),
             ("rv", 1024), ("rg", 1024), ("ga", 1024), ("gb", 1024))
PROJ_COLS = 512
_W_OFF = {}
_off = 0
for _n, _w in _W_GROUPS:
    _W_OFF[_n] = (_off, _w)
    _off += _w
W_COLS = _off


def _cparams(*sem):
    return pltpu.CompilerParams(dimension_semantics=sem, vmem_limit_bytes=VMEM_LIMIT)


def _adaln_kernel(c_ref, w_ref, b_ref, o_ref):
    c = c_ref[...]
    s = c * jax.nn.sigmoid(c)
    o_ref[...] = jnp.dot(s, w_ref[...], preferred_element_type=F32, precision=lax.Precision.HIGHEST) + b_ref[...]


def _adaln(c_all, w_ada, b_ada):
    n = c_all.shape[0]
    nb = 1536
    return pl.pallas_call(
        _adaln_kernel,
        grid=(6 * D_MODEL // nb,),
        in_specs=[pl.BlockSpec((n, D_MODEL), lambda j: (0, 0)),
                  pl.BlockSpec((D_MODEL, nb), lambda j: (0, j)),
                  pl.BlockSpec((1, nb), lambda j: (0, j))],
        out_specs=pl.BlockSpec((n, nb), lambda j: (0, j)),
        out_shape=jax.ShapeDtypeStruct((n, 6 * D_MODEL), F32),
        compiler_params=_cparams("arbitrary"),
        name="adaln",
    )(c_all, w_ada, b_ada.reshape(1, -1))


def _rope_slab(z, c, sa, sb, half):
    return z * c + pltpu.roll(z, LANES - half, 1) * sa + pltpu.roll(z, half, 1) * sb


def _inproj_kernel(x_ref, mod_ref, g_ref, w_ref, ca_ref, saa_ref, sba_ref, cr_ref, sar_ref, sbr_ref,
                   q_ref, k_ref, v_ref, kb_ref, vb_ref, iq_ref, ikw_ref, ikb_ref,
                   rq_ref, rk_ref, rv_ref, sg_ref, sga_ref, sgb_ref):
    x = x_ref[...]
    ms = jnp.mean(x * x, axis=-1, keepdims=True)
    y = x * lax.rsqrt(ms + NORM_EPS) * g_ref[...]
    h = (y * (1.0 + mod_ref[:, D_MODEL:2 * D_MODEL]) + mod_ref[:, 0:D_MODEL]).astype(BF16)

    def slabs(name):
        c0, width = _W_OFF[name]
        step = min(width, PROJ_COLS)
        for j in range(width // step):
            z = jnp.dot(h, w_ref[:, c0 + j * step:c0 + (j + 1) * step], preferred_element_type=F32)
            for s in range(step // LANES):
                yield j * (step // LANES) + s, z[:, s * LANES:(s + 1) * LANES]

    ca, saa, sba = ca_ref[...], saa_ref[...], sba_ref[...]
    cr, sar, sbr = cr_ref[...], sar_ref[...], sbr_ref[...]
    att_half, ret_half = ROPE_DIM // 2, RET_DK // 2
    lane = lax.broadcasted_iota(I32, (x.shape[0], LANES), 1)
    sl = lambda s: slice(s * LANES, (s + 1) * LANES)

    for s, z in slabs("q"):
        q_ref[:, sl(s)] = (_rope_slab(z, ca, saa, sba, att_half) * 0.125).astype(BF16)
    for s, z in slabs("iq"):
        iq_ref[:, sl(s)] = (_rope_slab(z, ca, saa, sba, att_half) * 0.125).astype(BF16)
    (_, zk), (_, zv), (_, zi) = slabs("kvi")
    kk = _rope_slab(zk, ca, saa, sba, att_half)
    for j in range(ATT_KV_HEADS):
        k_ref[:, j, :] = kk[:, j * HEAD_DIM:(j + 1) * HEAD_DIM]
        v_ref[:, j, :] = zv[:, j * HEAD_DIM:(j + 1) * HEAD_DIM]
    kb_ref[...] = kk.astype(BF16)
    vb_ref[...] = zv.astype(BF16)
    zr = _rope_slab(zi, ca, saa, sba, att_half)
    ikw_ref[...] = jnp.where(lane < IDX_DIM, zr, zi)
    ikb_ref[...] = jnp.where(lane < IDX_DIM, zr, 0.0).astype(BF16)
    for s, z in slabs("rq"):
        rq_ref[:, sl(s)] = _rope_slab(z, cr, sar, sbr, ret_half).astype(BF16)
    for s, z in slabs("rk"):
        rk_ref[:, sl(s)] = (_rope_slab(z, cr, sar, sbr, ret_half) * 0.125).astype(BF16)
    for s, z in slabs("rv"):
        rv_ref[:, sl(s)] = z.astype(BF16)
    for s, z in slabs("rg"):
        sg_ref[:, sl(s)] = (z * jax.nn.sigmoid(z)).astype(BF16)
    for s, z in slabs("ga"):
        sga_ref[:, sl(s)] = jax.nn.sigmoid(z).astype(BF16)
    for s, z in slabs("gb"):
        sgb_ref[:, sl(s)] = jax.nn.sigmoid(z).astype(BF16)


def _inproj(x, mod3, g, w_packed, tabs_att, tabs_ret, tm, blocks_per_mod, tab_blocks):
    t = x.shape[0]
    nblk = t // tm
    mod_rows = mod3.shape[1]
    tab_spec = pl.BlockSpec((tm, LANES), lambda i: (i % tab_blocks, 0))
    row = lambda n: pl.BlockSpec((tm, n), lambda i: (i, 0))
    kv_spec = pl.BlockSpec((tm, ATT_KV_HEADS, HEAD_DIM), lambda i: (i, 0, 0))
    out_defs = [(1024, BF16), (128, F32), (128, F32), (128, BF16), (128, BF16), (1024, BF16), (128, F32),
                (128, BF16), (512, BF16), (512, BF16), (1024, BF16), (1024, BF16), (1024, BF16), (1024, BF16)]
    return pl.pallas_call(
        _inproj_kernel,
        grid=(nblk,),
        in_specs=[row(D_MODEL),
                  pl.BlockSpec((None, mod_rows, 6 * D_MODEL), lambda i: (i // blocks_per_mod, 0, 0)),
                  pl.BlockSpec((1, D_MODEL), lambda i: (0, 0)),
                  pl.BlockSpec((D_MODEL, W_COLS), lambda i: (0, 0), pipeline_mode=pl.Buffered(1))]
                 + [tab_spec] * 6,
        out_specs=[kv_spec if j in (1, 2) else row(n) for j, (n, _) in enumerate(out_defs)],
        out_shape=[jax.ShapeDtypeStruct((t, ATT_KV_HEADS, HEAD_DIM) if j in (1, 2) else (t, n), d)
                   for j, (n, d) in enumerate(out_defs)],
        compiler_params=_cparams("parallel"),
        name="inproj",
    )(x, mod3, g, w_packed, *tabs_att, *tabs_ret)


def _count(score_ref, n, pred):
    acc = jnp.zeros((score_ref.shape[0], LANES), F32)
    for c in range(n // LANES):
        acc = acc + jnp.where(pred(score_ref[:, c * LANES:(c + 1) * LANES]), 1.0, 0.0)
    return jnp.sum(acc, axis=1, keepdims=True)


def _kth_largest(score_ref, n, k, quarters):
    sc = score_ref[:, :n]
    finite = sc > -jnp.inf
    n_fin = jnp.sum(jnp.where(finite, 1.0, 0.0), axis=1, keepdims=True)
    n_pos = jnp.sum(jnp.where(sc > 0.0, 1.0, 0.0), axis=1, keepdims=True)
    n_nonneg = jnp.sum(jnp.where(sc >= 0.0, 1.0, 0.0), axis=1, keepdims=True)
    mx = jnp.max(sc, axis=1, keepdims=True)
    mn = jnp.min(jnp.where(finite, sc, jnp.inf), axis=1, keepdims=True)
    small = n_fin <= k
    positive = n_pos >= k
    at_zero = jnp.logical_and(jnp.logical_not(positive), n_nonneg >= k)
    lo = jnp.where(positive, 0.0, mn)
    hi = jnp.where(positive, mx + (jnp.abs(mx) * 2.0 ** -20 + 2.0 ** -100), 0.0)
    lo = jnp.where(at_zero, 0.0, lo)
    done = jnp.where(jnp.logical_or(small, at_zero), 1.0, 0.0)

    def cond(state):
        return jnp.min(state[2]) < 0.5

    def body(state):
        lo, hi, done = state
        for _ in range(SELECT_UNROLL):
            mid = 0.5 * lo + 0.5 * hi
            stuck = jnp.logical_or(mid <= lo, mid >= hi)
            live = jnp.logical_and(done < 0.5, jnp.logical_not(stuck))
            if not quarters:
                cnt = _count(score_ref, n, lambda s: s >= mid)
                ge = cnt >= k
                lo = jnp.where(jnp.logical_and(live, ge), mid, lo)
                hi = jnp.where(jnp.logical_and(live, jnp.logical_not(ge)), mid, hi)
                done = jnp.where(jnp.logical_or(stuck, cnt == k), 1.0, done)
                continue
            cands = (0.5 * lo + 0.5 * mid, mid, 0.5 * mid + 0.5 * hi)
            new_lo, hit, ges = lo, jnp.zeros_like(lo), []
            for c in cands:
                cnt = _count(score_ref, n, lambda s, c=c: s >= c)
                ges.append(cnt >= k)
                new_lo = jnp.where(ges[-1], c, new_lo)
                hit = jnp.where(ges[-1], jnp.where(cnt == k, 1.0, 0.0), hit)
            new_hi = hi
            for c, ge in zip(reversed(cands), reversed(ges)):
                new_hi = jnp.where(ge, new_hi, c)
            lo = jnp.where(live, new_lo, lo)
            hi = jnp.where(live, new_hi, hi)
            done = jnp.where(jnp.logical_or(stuck, hit > 0.5), 1.0, done)
        return lo, hi, done

    lo, _, _ = lax.while_loop(cond, body, (lo, hi, done))
    return jnp.where(small, -FLT_MAX, lo)


def _topk_bias(score_ref, bias_ref, tri_ref, n, k, quarters=False):
    rows = score_ref.shape[0]
    thr = _kth_largest(score_ref, n, k, quarters)
    n_ge = _count(score_ref, n, lambda s: s >= thr)
    has_ties = jnp.max(jnp.where(n_ge > k, 1.0, 0.0)) > 0.5

    @pl.when(jnp.logical_not(has_ties))
    def _():
        for c in range(n // LANES):
            sl = slice(c * LANES, (c + 1) * LANES)
            bias_ref[:, sl] = jnp.where(score_ref[:, sl] >= thr, 0.0, MASK_NEG)

    @pl.when(has_ties)
    def _():
        need = k - _count(score_ref, n, lambda s: s > thr)
        run = jnp.zeros((rows, 1), F32)
        for c in range(n // LANES):
            sl = slice(c * LANES, (c + 1) * LANES)
            sc = score_ref[:, sl]
            eq = sc == thr
            eqf = jnp.where(eq, 1.0, 0.0)
            before = jnp.dot(eqf.astype(BF16), tri_ref[...], preferred_element_type=F32) + run
            take = jnp.logical_or(sc > thr, jnp.logical_and(eq, before < need))
            bias_ref[:, sl] = jnp.where(take, 0.0, MASK_NEG)
            run = run + jnp.sum(eqf, axis=1, keepdims=True)


def _dsa_prompt_kernel(q_ref, iq_ref, ikw_ref, kb_ref, vb_ref, ikb_ref, tri_ref, o_ref, score_ref, bias_ref):
    qb, s_len = score_ref.shape
    i = pl.program_id(1)
    nqb = s_len // qb
    per_variant = nqb // CAUSAL_VARIANTS
    for v in range(CAUSAL_VARIANTS):
        pl.when(i // per_variant == v)(
            functools.partial(_dsa_prompt_body, q_ref, iq_ref, ikw_ref, kb_ref, vb_ref, ikb_ref, tri_ref, o_ref,
                              score_ref, bias_ref, (v + 1) * per_variant * qb, min(TOPK_MAX, s_len // 4)))


def _dsa_prompt_body(q_ref, iq_ref, ikw_ref, kb_ref, vb_ref, ikb_ref, tri_ref, o_ref, score_ref, bias_ref,
                     n_keys, topk):
    qb = score_ref.shape[0]
    i = pl.program_id(1)
    w = ikw_ref[:, IDX_DIM:IDX_DIM + IDX_HEADS] * (IDX_HEADS ** -0.5)
    nt = (((1,), (1,)), ((), ()))
    kc = IDX_KEY_CHUNK
    qr = IDX_Q_ROWS
    for c in range(n_keys // kc):
        ikc = ikb_ref[c * kc:(c + 1) * kc, :]
        kpos = c * kc + lax.broadcasted_iota(I32, (qr, kc), 1)
        for r0 in range(0, qb, qr):
            acc = jnp.zeros((qr, kc), F32)
            for h in range(IDX_HEADS):
                d = lax.dot_general(iq_ref[r0:r0 + qr, h * LANES:(h + 1) * LANES], ikc, nt,
                                    preferred_element_type=F32)
                acc = acc + jnp.maximum(d, 0.0) * w[r0:r0 + qr, h:h + 1]
            qpos = i * qb + r0 + lax.broadcasted_iota(I32, (qr, kc), 0)
            score_ref[r0:r0 + qr, c * kc:(c + 1) * kc] = jnp.where(kpos <= qpos, acc, -jnp.inf)

    _topk_bias(score_ref, bias_ref, tri_ref, n_keys, topk)

    kb = kb_ref[0:n_keys, :]
    vb = vb_ref[0:n_keys, :]
    bias = bias_ref[:, 0:n_keys]
    lane = lax.broadcasted_iota(I32, (qb, LANES), 1)
    heads = []
    for h in range(ATT_HEADS):
        s = lax.dot_general(q_ref[:, h * LANES:(h + 1) * LANES], kb, nt, preferred_element_type=F32) + bias
        m = jnp.max(s, axis=1, keepdims=True)
        p = jnp.exp(s - m)
        l = jnp.sum(p, axis=1, keepdims=True)
        heads.append(jnp.dot(p.astype(BF16), vb, preferred_element_type=F32) / l)
    group = ATT_HEADS // ATT_KV_HEADS
    for pp in range(ATT_HEADS // 2):
        a, b = heads[2 * pp], heads[2 * pp + 1]
        if (2 * pp) // group == 0:
            slab = jnp.where(lane < HEAD_DIM, a, pltpu.roll(b, HEAD_DIM, 1))
        else:
            slab = jnp.where(lane < HEAD_DIM, pltpu.roll(a, HEAD_DIM, 1), b)
        o_ref[:, pp * LANES:(pp + 1) * LANES] = slab.astype(BF16)


def _dsa_prompt(q, iq, ikw, kb, vb, ikb, tri, batch, seq):
    nqb = seq // Q_BLOCK
    qrow = lambda n: pl.BlockSpec((Q_BLOCK, n), lambda b, i: (b * nqb + i, 0))
    keys = pl.BlockSpec((seq, LANES), lambda b, i: (b, 0))
    return pl.pallas_call(
        _dsa_prompt_kernel,
        grid=(batch, nqb),
        in_specs=[qrow(1024), qrow(1024), qrow(LANES), keys, keys, keys,
                  pl.BlockSpec((LANES, LANES), lambda b, i: (0, 0))],
        out_specs=qrow(ATT_OUT),
        out_shape=jax.ShapeDtypeStruct((batch * seq, ATT_OUT), BF16),
        scratch_shapes=[pltpu.VMEM((Q_BLOCK, seq), F32), pltpu.VMEM((Q_BLOCK, seq), F32)],
        compiler_params=_cparams("parallel", "arbitrary"),
        name="dsa_prompt",
    )(q, iq, ikw, kb, vb, ikb, tri)


def _dsa_sample_kernel(pt_ref, qs_ref, iqs_ref, ws_ref, knew_ref, vnew_ref, iknew_ref, cik_hbm, ck_hbm, cv_hbm,
                       tri_ref, o_ref, ikbuf, kbuf, vbuf, ikt, kt, vt, sems, key_ref, bias_ref):
    db = pl.program_id(0)
    n_pages = ikbuf.shape[1]
    t = key_ref.shape[0]
    n_past = n_pages * PAGE_SIZE
    last = pl.num_programs(0) - 1

    def fetch(src, dst_of_page, sem, req):
        def body(p, carry):
            pltpu.make_async_copy(src.at[pt_ref[req, p]], dst_of_page(p), sem).start()
            return carry
        lax.fori_loop(0, n_pages, body, 0)

    def wait_all(src, dst, sem):
        pltpu.make_async_copy(src.at[pl.ds(0, n_pages)], dst, sem).wait()

    fetch_ik = lambda req, slot: fetch(cik_hbm, lambda p: ikbuf.at[slot, p], sems.at[slot], req)
    fetch_k = lambda req: fetch(ck_hbm, lambda p: kbuf.at[p], sems.at[2], req)
    fetch_v = lambda req: fetch(cv_hbm, lambda p: vbuf.at[p], sems.at[3], req)

    @pl.when(db == 0)
    def _():
        fetch_ik(0, 0)
        fetch_k(0)
        fetch_v(0)

    @pl.when(db < last)
    def _():
        fetch_ik(db + 1, (db + 1) % 2)

    slot = db % 2
    wait_all(cik_hbm, ikbuf.at[slot], sems.at[slot])

    nt = (((1,), (1,)), ((), ()))
    page = lambda p: slice(p * PAGE_SIZE, (p + 1) * PAGE_SIZE)
    for p in range(n_pages):
        ikt[:, page(p)] = ikbuf[slot, p].astype(BF16)
    iqs = iqs_ref[...]
    wcol = ws_ref[...]
    d_past = jnp.maximum(jnp.dot(iqs, ikt[...], preferred_element_type=F32), 0.0) * wcol
    d_new = jnp.maximum(jnp.dot(iqs, iknew_ref[...], preferred_element_type=F32), 0.0) * wcol
    s_past = d_past[0:t]
    s_new = d_new[0:t]
    for h in range(1, IDX_HEADS):
        s_past = s_past + d_past[h * t:(h + 1) * t]
        s_new = s_new + d_new[h * t:(h + 1) * t]
    row = lax.broadcasted_iota(I32, (t, LANES), 0)
    lane = lax.broadcasted_iota(I32, (t, LANES), 1)
    new_ok = lane <= row
    key_ref[:, 0:n_past] = s_past
    key_ref[:, n_past:n_past + LANES] = jnp.where(new_ok, s_new, -jnp.inf)
    _topk_bias(key_ref, bias_ref, tri_ref, n_past + LANES, min(TOPK_MAX, (n_past + t) // 4), quarters=True)

    def stage(buf, dst):
        for p in range(n_pages):
            for j in range(ATT_KV_HEADS):
                dst[j, :, page(p)] = buf[p, j].astype(BF16)

    wait_all(ck_hbm, kbuf, sems.at[2])
    stage(kbuf, kt)

    @pl.when(db < last)
    def _():
        fetch_k(db + 1)

    wait_all(cv_hbm, vbuf, sems.at[3])
    stage(vbuf, vt)

    @pl.when(db < last)
    def _():
        fetch_v(db + 1)

    rows_per_kv = qs_ref.shape[0] // ATT_KV_HEADS
    bias = jnp.concatenate([bias_ref[...]] * (rows_per_kv // t), axis=0)
    for j in range(ATT_KV_HEADS):
        qj = qs_ref[j * rows_per_kv:(j + 1) * rows_per_kv, :]
        sp = jnp.dot(qj, kt[j], preferred_element_type=F32) + bias[:, 0:n_past]
        sn = jnp.dot(qj, knew_ref[j], preferred_element_type=F32) + bias[:, n_past:n_past + LANES]
        m = jnp.maximum(jnp.max(sp, axis=1, keepdims=True), jnp.max(sn, axis=1, keepdims=True))
        pp = jnp.exp(sp - m)
        pn = jnp.exp(sn - m)
        l = jnp.sum(pp, axis=1, keepdims=True) + jnp.sum(pn, axis=1, keepdims=True)
        o = (lax.dot_general(pp.astype(BF16), vt[j], nt, preferred_element_type=F32)
             + lax.dot_general(pn.astype(BF16), vnew_ref[j], nt, preferred_element_type=F32))
        o_ref[j * rows_per_kv:(j + 1) * rows_per_kv, :] = o / l


def _dsa_sample(page_table, qs, iqs, ws, knew, vnew, iknew, cache_ik, cache_k, cache_v, tri, t):
    db, n_pages = page_table.shape
    rows = qs.shape[1]
    per_db = lambda r, n: pl.BlockSpec((None, r, n), lambda b, pt: (b, 0, 0))
    any_spec = pl.BlockSpec(memory_space=pl.ANY)
    n_keys = n_pages * PAGE_SIZE + LANES
    grid_spec = pltpu.PrefetchScalarGridSpec(
        num_scalar_prefetch=1,
        grid=(db,),
        in_specs=[per_db(rows, HEAD_DIM), per_db(rows, IDX_DIM), per_db(rows, 1),
                  pl.BlockSpec((None, ATT_KV_HEADS, HEAD_DIM, LANES), lambda b, pt: (b, 0, 0, 0)),
                  pl.BlockSpec((None, ATT_KV_HEADS, HEAD_DIM, LANES), lambda b, pt: (b, 0, 0, 0)),
                  per_db(IDX_DIM, LANES), any_spec, any_spec, any_spec,
                  pl.BlockSpec((LANES, LANES), lambda b, pt: (0, 0))],
        out_specs=per_db(rows, HEAD_DIM),
        scratch_shapes=[pltpu.VMEM((2, n_pages, IDX_DIM, PAGE_SIZE), F32),
                        pltpu.VMEM((n_pages, ATT_KV_HEADS, HEAD_DIM, PAGE_SIZE), F32),
                        pltpu.VMEM((n_pages, ATT_KV_HEADS, HEAD_DIM, PAGE_SIZE), F32),
                        pltpu.VMEM((IDX_DIM, n_pages * PAGE_SIZE), BF16),
                        pltpu.VMEM((ATT_KV_HEADS, HEAD_DIM, n_pages * PAGE_SIZE), BF16),
                        pltpu.VMEM((ATT_KV_HEADS, HEAD_DIM, n_pages * PAGE_SIZE), BF16),
                        pltpu.SemaphoreType.DMA((4,)),
                        pltpu.VMEM((t, n_keys), F32),
                        pltpu.VMEM((t, n_keys), F32)])
    return pl.pallas_call(
        _dsa_sample_kernel,
        grid_spec=grid_spec,
        out_shape=jax.ShapeDtypeStruct((db, rows, HEAD_DIM), F32),
        compiler_params=_cparams("arbitrary"),
        name="dsa_sample",
    )(page_table, qs, iqs, ws, knew, vnew, iknew, cache_ik, cache_k, cache_v, tri)


def _retention_kernel(has_init, rq_ref, rk_ref, rv_ref, sg_ref, decay_ref, qdec_ref, kdec_ref, gst_ref, *rest):
    if has_init:
        init_ref, o_ref, st_ref, state = rest
    else:
        o_ref, st_ref, state = rest
    c = pl.program_id(1)

    @pl.when(c == 0)
    def _():
        if has_init:
            state[...] = init_ref[...]
        else:
            state[...] = jnp.zeros_like(state)

    nt = (((1,), (1,)), ((), ()))
    tn = (((0,), (0,)), ((), ()))
    rows = rq_ref.shape[0]
    lane = lax.broadcasted_iota(I32, (rows, LANES), 1)
    for p in range(RET_HEADS // 2):
        sl = slice(p * LANES, (p + 1) * LANES)
        qp = rq_ref[:, sl]
        kp = rk_ref[:, sl]
        qd = (qp.astype(F32) * qdec_ref[:, sl]).astype(BF16)
        kd = (kp.astype(F32) * kdec_ref[:, sl]).astype(BF16)
        s_old = state[p]
        s_old_b = s_old.astype(BF16)
        s_new = s_old * gst_ref[p]
        for e in range(2):
            h = 2 * p + e
            hs = slice(h * LANES, (h + 1) * LANES)
            mine = jnp.where((lane >= e * RET_DK) & (lane < (e + 1) * RET_DK), 1.0, 0.0).astype(BF16)
            sc = lax.dot_general(qp * mine, kp, nt, preferred_element_type=F32) * decay_ref[h]
            vh = rv_ref[:, hs]
            o = (jnp.dot(sc.astype(BF16), vh, preferred_element_type=F32)
                 + jnp.dot(qd * mine, s_old_b, preferred_element_type=F32))
            s_new = s_new + lax.dot_general(kd * mine, vh, tn, preferred_element_type=F32)
            mu = jnp.mean(o, axis=-1, keepdims=True)
            var = jnp.mean(jnp.square(o - mu), axis=-1, keepdims=True)
            on = (o - mu) * lax.rsqrt(var + GN_EPS)
            o_ref[:, hs] = (on * sg_ref[:, hs].astype(F32)).astype(BF16)
        state[p] = s_new

    @pl.when(c == pl.num_programs(1) - 1)
    def _():
        st_ref[...] = state[...]


def _retention(rq, rk, rv, sg, tables, init, batch, n_chunks):
    decay, qdec, kdec, gst = tables
    cr = RET_CHUNK
    rowspec = lambda n: pl.BlockSpec((cr, n), lambda b, c: (b * n_chunks + c, 0))
    const = lambda shape: pl.BlockSpec(shape, lambda b, c: (0,) * len(shape))
    st_spec = pl.BlockSpec((None, RET_HEADS // 2, LANES, LANES), lambda b, c: (b, 0, 0, 0))
    in_specs = [rowspec(512), rowspec(512), rowspec(RET_OUT), rowspec(RET_OUT),
                const(decay.shape), const(qdec.shape), const(kdec.shape), const(gst.shape)]
    args = [rq, rk, rv, sg, decay, qdec, kdec, gst]
    if init is not None:
        in_specs.append(st_spec)
        args.append(init)
    return pl.pallas_call(
        functools.partial(_retention_kernel, init is not None),
        grid=(batch, n_chunks),
        in_specs=in_specs,
        out_specs=[rowspec(RET_OUT), st_spec],
        out_shape=[jax.ShapeDtypeStruct((batch * n_chunks * cr, RET_OUT), BF16),
                   jax.ShapeDtypeStruct((batch, RET_HEADS // 2, LANES, LANES), F32)],
        scratch_shapes=[pltpu.VMEM((RET_HEADS // 2, LANES, LANES), F32)],
        compiler_params=_cparams("parallel", "arbitrary"),
        name="retention",
    )(*args)


def _retention_tables(c_eff):
    lg = jnp.log(1.0 - 2.0 ** (-5.0 - jnp.arange(RET_HEADS, dtype=F32)))
    i = jnp.arange(RET_CHUNK, dtype=F32)
    diff = i[:, None] - i[None, :]
    decay = jnp.where(diff >= 0, jnp.exp(jnp.maximum(diff, 0.0)[None] * lg[:, None, None]), 0.0)
    q_decay = jnp.exp((i + 1.0)[:, None] * lg[None, :])
    k_decay = jnp.exp((c_eff - 1.0 - i)[:, None] * lg[None, :])
    qdec = jnp.repeat(q_decay, RET_DK, axis=1)
    kdec = jnp.repeat(k_decay, RET_DK, axis=1)
    g_state = jnp.exp(c_eff * lg)
    gst = jnp.broadcast_to(jnp.repeat(g_state, RET_DK).reshape(RET_HEADS // 2, LANES, 1),
                           (RET_HEADS // 2, LANES, LANES))
    return decay, qdec, kdec, gst


def _pack_bf16_pairs(lo, hi):
    return pltpu.pack_elementwise([lo, hi], packed_dtype=BF16)


def _unpack_bf16_pairs(words):
    return tuple(pltpu.unpack_elementwise(words, index=j, packed_dtype=BF16, unpacked_dtype=F32).astype(BF16)
                 for j in range(2))


def _post_kernel(a_ref, r_ref, sga_ref, sgb_ref, x_ref, mod_ref, wpa_ref, wpb_ref, wo_ref, g_ref, wr_ref, br_ref,
                 ltri_ref, cnt0_ref, x1_ref, h2w_ref, idx_ref, gate_ref, rank_ref, cnt_ref, cnt):
    @pl.when(pl.program_id(0) == 0)
    def _():
        cnt[...] = cnt0_ref[...]

    pa = jnp.dot(a_ref[...], wpa_ref[...], preferred_element_type=F32)
    pb = jnp.dot(r_ref[...], wpb_ref[...], preferred_element_type=F32)
    merged = sga_ref[...].astype(F32) * pa + sgb_ref[...].astype(F32) * pb
    gt1 = mod_ref[:, 2 * D_MODEL:3 * D_MODEL]
    x1 = x_ref[...] + gt1 * jnp.dot(merged.astype(BF16), wo_ref[...], preferred_element_type=F32)
    x1_ref[...] = x1
    ms = jnp.mean(x1 * x1, axis=-1, keepdims=True)
    y = x1 * lax.rsqrt(ms + NORM_EPS) * g_ref[...]
    h2 = y * (1.0 + mod_ref[:, 4 * D_MODEL:5 * D_MODEL]) + mod_ref[:, 3 * D_MODEL:4 * D_MODEL]
    half = D_MODEL // 2
    h2w_ref[...] = _pack_bf16_pairs(h2[:, :half], h2[:, half:])
    h_hi = h2.astype(BF16)
    h_lo = (h2 - h_hi.astype(F32)).astype(BF16)
    logits = (jnp.dot(h_hi, wr_ref[0], preferred_element_type=F32) + jnp.dot(h_hi, wr_ref[1], preferred_element_type=F32)
              + jnp.dot(h_lo, wr_ref[0], preferred_element_type=F32) + br_ref[...])
    lane = lax.broadcasted_iota(I32, logits.shape, 1).astype(F32)
    idx_out = jnp.zeros(logits.shape, F32)
    val_out = jnp.zeros(logits.shape, F32)
    chosen = []
    top = None
    for j in range(TOP_K):
        m = jnp.max(logits, axis=1, keepdims=True)
        am = jnp.min(jnp.where(logits == m, lane, float(LANES)), axis=1, keepdims=True)
        if j == 0:
            top = m
        idx_out = jnp.where(lane == j, am, idx_out)
        val_out = jnp.where(lane == j, jnp.exp(m - top), val_out)
        chosen.append(lane == am)
        logits = jnp.where(chosen[-1], -jnp.inf, logits)
    idx_ref[...] = jnp.transpose(idx_out)[0:8, :].astype(I32)
    gate_ref[...] = jnp.transpose(val_out / jnp.sum(val_out, axis=1, keepdims=True))[0:8, :]
    onehot = jnp.where(chosen[0] | chosen[1] | chosen[2] | chosen[3], 1.0, 0.0)
    before = jnp.dot(ltri_ref[...], onehot.astype(BF16), preferred_element_type=F32) + cnt[...]
    rank_out = jnp.zeros(logits.shape, F32)
    for j in range(TOP_K):
        rj = jnp.sum(jnp.where(chosen[j], before, 0.0), axis=1, keepdims=True)
        rank_out = jnp.where(lane == j, rj, rank_out)
    rank_ref[...] = jnp.transpose(rank_out)[0:8, :].astype(I32)
    cnt[...] = cnt[...] + jnp.sum(onehot, axis=0, keepdims=True)
    cnt_ref[...] = cnt[...]


def _post(a, r, sga, sgb, x, mod3, wpa, wpb, wo, g, wr, br, cnt0, tm, blocks_per_mod):
    t = x.shape[0]
    mod_rows = mod3.shape[1]
    ar = jnp.arange(tm)
    ltri = (ar[None, :] < ar[:, None]).astype(BF16)
    row = lambda n: pl.BlockSpec((tm, n), lambda i: (i, 0))
    col8 = pl.BlockSpec((8, tm), lambda i: (0, i))
    const = lambda a_: pl.BlockSpec(a_.shape, lambda i: (0,) * a_.ndim)
    return pl.pallas_call(
        _post_kernel,
        grid=(t // tm,),
        in_specs=[row(ATT_OUT), row(RET_OUT), row(D_MODEL), row(D_MODEL), row(D_MODEL),
                  pl.BlockSpec((None, mod_rows, 6 * D_MODEL), lambda i: (i // blocks_per_mod, 0, 0)),
                  const(wpa), const(wpb), const(wo), const(g), const(wr), const(br), const(ltri), const(cnt0)],
        out_specs=[row(D_MODEL), row(D_MODEL // 2), col8, col8, col8,
                   pl.BlockSpec((1, LANES), lambda i: (0, 0))],
        out_shape=[jax.ShapeDtypeStruct((t, D_MODEL), F32), jax.ShapeDtypeStruct((t, D_MODEL // 2), jnp.uint32),
                   jax.ShapeDtypeStruct((8, t), I32), jax.ShapeDtypeStruct((8, t), F32),
                   jax.ShapeDtypeStruct((8, t), I32), jax.ShapeDtypeStruct((1, LANES), F32)],
        scratch_shapes=[pltpu.VMEM((1, LANES), F32)],
        compiler_params=_cparams("arbitrary"),
        name="post",
    )(a, r, sga, sgb, x, mod3, wpa, wpb, wo, g, wr, br, ltri, cnt0)


def _segment_copies(seg_ref, blk, make_copy):
    for e in range(N_EXPERTS + 1):
        base = (blk * (N_EXPERTS + 1) + e) * 3
        loc, glob, n = seg_ref[base], seg_ref[base + 1], seg_ref[base + 2]
        for k in range(SEG_BITS):
            size = SEG_ALIGN << k

            @pl.when((n & size) != 0)
            def _():
                done = n & (size - 1)
                make_copy(pl.multiple_of(loc + done, SEG_ALIGN), pl.multiple_of(glob + done, SEG_ALIGN),
                          size).start()


def _local_order_matrix(lpos_ref, values, rows):
    lp = lpos_ref[...]
    r = lax.broadcasted_iota(I32, (rows, lp.shape[1]), 0)
    m = jnp.zeros((rows, lp.shape[1]), F32)
    for j in range(TOP_K):
        m = jnp.where(r == lp[j:j + 1, :], values(j), m)
    return m


def _dispatch_kernel(seg_ref, lpos_ref, h2w_ref, xs_in, xs_out, buf, sems):
    del xs_in
    i = pl.program_id(0)
    slot = i % 2
    rows = buf.shape[1]

    def wait_slot(s):
        pltpu.make_async_copy(buf.at[s], xs_out.at[pl.ds(0, rows)], sems.at[s]).wait()

    @pl.when(i >= 2)
    def _():
        wait_slot(slot)

    x = jnp.concatenate(_unpack_bf16_pairs(h2w_ref[...]), axis=1)
    perm = _local_order_matrix(lpos_ref, lambda j: 1.0, rows).astype(BF16)
    xs = jnp.dot(perm, x, preferred_element_type=F32)
    half = D_MODEL // 2
    buf[slot] = _pack_bf16_pairs(xs[:, :half], xs[:, half:])
    _segment_copies(seg_ref, i, lambda loc, glob, size: pltpu.make_async_copy(
        buf.at[slot, pl.ds(loc, size)], xs_out.at[pl.ds(glob, size)], sems.at[slot]))

    @pl.when(i == pl.num_programs(0) - 1)
    def _():
        wait_slot(slot)

        @pl.when(i >= 1)
        def _():
            wait_slot(1 - slot)


def _dispatch(seg, lpos8, h2w, x_sorted):
    t = h2w.shape[0]
    tm = TOK_BLOCK
    grid_spec = pltpu.PrefetchScalarGridSpec(
        num_scalar_prefetch=1,
        grid=(t // tm,),
        in_specs=[pl.BlockSpec((8, tm), lambda i, s: (0, i)),
                  pl.BlockSpec((tm, D_MODEL // 2), lambda i, s: (i, 0)), pl.BlockSpec(memory_space=pl.ANY)],
        out_specs=pl.BlockSpec(memory_space=pl.ANY),
        scratch_shapes=[pltpu.VMEM((2, SEG_ROWS, D_MODEL // 2), jnp.uint32), pltpu.SemaphoreType.DMA((2,))])
    return pl.pallas_call(
        _dispatch_kernel,
        grid_spec=grid_spec,
        out_shape=jax.ShapeDtypeStruct(x_sorted.shape, x_sorted.dtype),
        input_output_aliases={3: 0},
        compiler_params=_cparams("arbitrary"),
        name="dispatch",
    )(seg, lpos8, h2w, x_sorted)


def _moe_kernel(be_ref, first_ref, nused_ref, x_ref, wgu_ref, bgu_ref, wd_ref, bd_ref, o_ref, wgu_b, wd_b):
    i = pl.program_id(0)

    @pl.when(first_ref[i] == 1)
    def _():
        wgu_b[...] = wgu_ref[...].astype(BF16)
        wd_b[...] = wd_ref[...].astype(BF16)

    @pl.when(i < nused_ref[0])
    def _():
        x = jnp.concatenate(_unpack_bf16_pairs(x_ref[...]), axis=1)
        gu = jnp.dot(x, wgu_b[...], preferred_element_type=F32) + bgu_ref[...]
        g = jnp.minimum(gu[:, :D_FF], SWIGLU_LIMIT)
        u = jnp.clip(gu[:, D_FF:], -SWIGLU_LIMIT, SWIGLU_LIMIT)
        act = (u + 1.0) * (g * jax.nn.sigmoid(SWIGLU_ALPHA * g))
        o_ref[...] = jnp.dot(act.astype(BF16), wd_b[...], preferred_element_type=F32) + bd_ref[...]

    @pl.when(i >= nused_ref[0])
    def _():
        o_ref[...] = jnp.zeros_like(o_ref)


def _moe(blk_expert, blk_first, n_used, x_sorted, w_gate_up, b_gate_up, w_down, b_down):
    n_rows = blk_expert.shape[0] * MOE_ROWS
    grid_spec = pltpu.PrefetchScalarGridSpec(
        num_scalar_prefetch=3,
        grid=(n_rows // MOE_ROWS,),
        in_specs=[pl.BlockSpec((MOE_ROWS, D_MODEL // 2), lambda i, be, bf, nu: (i, 0)),
                  pl.BlockSpec((None, D_MODEL, 2 * D_FF), lambda i, be, bf, nu: (be[i], 0, 0)),
                  pl.BlockSpec((None, 1, 2 * D_FF), lambda i, be, bf, nu: (be[i], 0, 0)),
                  pl.BlockSpec((None, D_FF, D_MODEL), lambda i, be, bf, nu: (be[i], 0, 0)),
                  pl.BlockSpec((None, 1, D_MODEL), lambda i, be, bf, nu: (be[i], 0, 0))],
        out_specs=pl.BlockSpec((MOE_ROWS, D_MODEL), lambda i, be, bf, nu: (i, 0)),
        scratch_shapes=[pltpu.VMEM((D_MODEL, 2 * D_FF), BF16), pltpu.VMEM((D_FF, D_MODEL), BF16)])
    return pl.pallas_call(
        _moe_kernel,
        grid_spec=grid_spec,
        out_shape=jax.ShapeDtypeStruct((n_rows, D_MODEL), F32),
        compiler_params=_cparams("arbitrary"),
        name="moe",
    )(blk_expert, blk_first, n_used, x_sorted, w_gate_up, b_gate_up.reshape(N_EXPERTS, 1, -1),
      w_down, b_down.reshape(N_EXPERTS, 1, -1))


def _final_kernel(block_offset, seg_ref, x1_ref, mod_ref, lpos_ref, gate_ref, g_ref, rows_hbm, y_ref, buf, sems):
    i = pl.program_id(0)
    n = pl.num_programs(0)
    rows = buf.shape[1]

    def issue(blk, slot):
        _segment_copies(seg_ref, blk + block_offset, lambda loc, glob, size: pltpu.make_async_copy(
            rows_hbm.at[pl.ds(glob, size)], buf.at[slot, pl.ds(loc, size)], sems.at[slot]))

    @pl.when(i == 0)
    def _():
        issue(0, 0)

    @pl.when(i + 1 < n)
    def _():
        issue(i + 1, (i + 1) % 2)

    slot = i % 2
    pltpu.make_async_copy(rows_hbm.at[pl.ds(0, rows)], buf.at[slot], sems.at[slot]).wait()
    gate = gate_ref[...]
    gmat = _local_order_matrix(lpos_ref, lambda j: gate[j:j + 1, :], rows).astype(BF16)
    moe = lax.dot_general(gmat, buf[slot].astype(BF16), (((0,), (0,)), ((), ())), preferred_element_type=F32)
    x2 = x1_ref[...] + mod_ref[:, 5 * D_MODEL:6 * D_MODEL] * moe
    ms = jnp.mean(x2 * x2, axis=-1, keepdims=True)
    y_ref[...] = x2 * lax.rsqrt(ms + NORM_EPS) * g_ref[...]


def _final(seg, x1, mod3, lpos8, gate8, rows_out, g, blocks_per_mod, block_offset):
    t = x1.shape[0]
    tm = TOK_BLOCK
    mod_rows = mod3.shape[1]
    row = lambda n: pl.BlockSpec((tm, n), lambda i, s: (i, 0))
    col8 = pl.BlockSpec((8, tm), lambda i, s: (0, i + block_offset))
    grid_spec = pltpu.PrefetchScalarGridSpec(
        num_scalar_prefetch=1,
        grid=(t // tm,),
        in_specs=[row(D_MODEL),
                  pl.BlockSpec((None, mod_rows, 6 * D_MODEL), lambda i, s: (i // blocks_per_mod, 0, 0)),
                  col8, col8,
                  pl.BlockSpec((1, D_MODEL), lambda i, s: (0, 0)),
                  pl.BlockSpec(memory_space=pl.ANY)],
        out_specs=row(D_MODEL),
        scratch_shapes=[pltpu.VMEM((2, SEG_ROWS, D_MODEL), F32), pltpu.SemaphoreType.DMA((2,))])
    return pl.pallas_call(
        functools.partial(_final_kernel, block_offset),
        grid_spec=grid_spec,
        out_shape=jax.ShapeDtypeStruct((t, D_MODEL), F32),
        compiler_params=_cparams("arbitrary"),
        name="final",
    )(seg, x1, mod3, lpos8, gate8, g, rows_out)


def _rope_tables(pos, rot_dim, theta, head_dim):
    half = rot_dim // 2
    inv = theta ** (-jnp.arange(half, dtype=F32) * (2.0 / rot_dim))
    ang = pos.astype(F32)[:, None] * inv[None, :]
    cos, sin = jnp.cos(ang), jnp.sin(ang)
    n = pos.shape[0]
    rest = head_dim - rot_dim
    zh = jnp.zeros((n, half), F32)
    c = jnp.concatenate([cos, cos, jnp.ones((n, rest), F32)], axis=1)
    sa = jnp.concatenate([-sin, zh, jnp.zeros((n, rest), F32)], axis=1)
    sb = jnp.concatenate([zh, sin, jnp.zeros((n, rest), F32)], axis=1)
    rep = LANES // head_dim
    return tuple(jnp.tile(a, (1, rep)) for a in (c, sa, sb))


def _pack_w_in(w_in):
    offs = np.cumsum((0,) + IN_SPLITS)
    part = lambda j: w_in[:, offs[j]:offs[j + 1]]
    zero = lambda n: jnp.zeros((D_MODEL, n), w_in.dtype)
    group = ATT_HEADS // ATT_KV_HEADS
    cols = []
    wq = part(0)
    for h in range(ATT_HEADS):
        wh = wq[:, h * HEAD_DIM:(h + 1) * HEAD_DIM]
        cols += [wh, zero(HEAD_DIM)] if h // group == 0 else [zero(HEAD_DIM), wh]
    wiq = part(3)
    for h in range(IDX_HEADS):
        cols += [wiq[:, h * IDX_DIM:(h + 1) * IDX_DIM], zero(LANES - IDX_DIM)]
    cols += [part(1), part(2), part(4), part(5), zero(LANES - IDX_DIM - IDX_HEADS)]
    cols += [part(j) for j in range(6, 12)]
    return jnp.concatenate(cols, axis=1).astype(BF16)


def _heads_major(a, db, t, width):
    heads = a.shape[1] // width
    return a.reshape(db, t, heads, width).transpose(0, 2, 1, 3).reshape(db, heads * t, width)


def _pad_rows(a, db, t, rows):
    return jnp.pad(a.reshape(db, t, -1), ((0, 0), (0, rows - t), (0, 0)))


def kernel(x_prompt, x_sample, cache_k, cache_v, cache_ik, state_ret, page_table, c_prompt, c_sample, norm_mix_g, norm_ffn_g, norm_final_g, w_ada, b_ada, w_in, w_branch_a, w_branch_b, w_out, w_router, b_router, w_gate_up, b_gate_up, w_down, b_down):
    batch, seq, _ = x_prompt.shape
    db, dt, _ = x_sample.shape
    assert w_in.shape[0] == 1, "one layer"
    tp, ts = batch * seq, db * dt
    xp = x_prompt.reshape(tp, D_MODEL)
    xs = x_sample.reshape(ts, D_MODEL)

    mod = _adaln(jnp.concatenate([c_prompt, c_sample], axis=0), w_ada[0], b_ada[0])
    mod_p = mod[:batch].reshape(batch, 1, 6 * D_MODEL)
    mod_s = jnp.repeat(mod[batch:], dt, axis=0).reshape(ts // TOK_BLOCK, TOK_BLOCK, 6 * D_MODEL)
    bpm_p = seq // TOK_BLOCK
    tmp = PROMPT_TOK_BLOCK
    bpm_big = seq // tmp

    w_packed = _pack_w_in(w_in[0].astype(BF16))
    pos_p = jnp.arange(seq)
    pos_s = PAST_LEN + (jnp.arange(TOK_BLOCK) % dt)
    g_mix = norm_mix_g[0].reshape(1, D_MODEL)
    outs_p = _inproj(xp, mod_p, g_mix, w_packed, _rope_tables(pos_p, ROPE_DIM, ROPE_THETA, HEAD_DIM),
                     _rope_tables(pos_p, RET_DK, RET_THETA, RET_DK), TOK_BLOCK, bpm_p, bpm_p)
    outs_s = _inproj(xs, mod_s, g_mix, w_packed, _rope_tables(pos_s, ROPE_DIM, ROPE_THETA, HEAD_DIM),
                     _rope_tables(pos_s, RET_DK, RET_THETA, RET_DK), TOK_BLOCK, 1, 1)
    (q_p, k_p, v_p, kb_p, vb_p, iq_p, ikw_p, ikb_p, rq_p, rk_p, rv_p, sg_p, sga_p, sgb_p) = outs_p
    (q_s, k_s, v_s, kb_s, vb_s, iq_s, ikw_s, ikb_s, rq_s, rk_s, rv_s, sg_s, sga_s, sgb_s) = outs_s

    tri = (jnp.arange(LANES)[:, None] < jnp.arange(LANES)[None, :]).astype(BF16)

    a_p = _dsa_prompt(q_p, iq_p, ikw_p, kb_p, vb_p, ikb_p, tri, batch, seq)
    group = ATT_HEADS // ATT_KV_HEADS
    q4 = q_s.reshape(db, dt, ATT_HEADS, LANES)
    qs = jnp.stack([q4[:, :, h, (h // group) * HEAD_DIM:(h // group + 1) * HEAD_DIM] for h in range(ATT_HEADS)],
                   axis=1).reshape(db, ATT_HEADS * dt, HEAD_DIM)
    iqs = _heads_major(iq_s, db, dt, LANES)[:, :, :IDX_DIM]
    ws = _heads_major(ikw_s[:, IDX_DIM:IDX_DIM + IDX_HEADS] * (IDX_HEADS ** -0.5), db, dt, 1)
    new_t = lambda a: jnp.pad(a.reshape(db, dt, ATT_KV_HEADS, HEAD_DIM).transpose(0, 2, 3, 1),
                              ((0, 0), (0, 0), (0, 0), (0, LANES - dt)))
    iknew_t = jnp.pad(ikb_s[:, :IDX_DIM].reshape(db, dt, IDX_DIM).transpose(0, 2, 1),
                      ((0, 0), (0, 0), (0, LANES - dt)))
    o_s = _dsa_sample(page_table, qs, iqs, ws, new_t(kb_s), new_t(vb_s), iknew_t,
                      cache_ik[0].transpose(0, 2, 1), cache_k[0].transpose(0, 2, 3, 1),
                      cache_v[0].transpose(0, 2, 3, 1), tri, dt)
    a_s = o_s.reshape(db, ATT_HEADS, dt, HEAD_DIM).transpose(0, 2, 1, 3).reshape(ts, ATT_OUT).astype(BF16)

    r_p, st_p = _retention(rq_p, rk_p, rv_p, sg_p, _retention_tables(float(RET_CHUNK)), None, batch,
                           seq // RET_CHUNK)
    pad = lambda a: _pad_rows(a, db, dt, RET_CHUNK).reshape(db * RET_CHUNK, -1)
    r_s, st_s = _retention(pad(rq_s), pad(rk_s), pad(rv_s), pad(sg_s), _retention_tables(float(dt)),
                           state_ret[0].reshape(db, RET_HEADS // 2, LANES, LANES), db, 1)
    r_s = r_s.reshape(db, RET_CHUNK, RET_OUT)[:, :dt].reshape(ts, RET_OUT)

    wr = jnp.pad(w_router[0], ((0, 0), (0, LANES - N_EXPERTS)))
    wr_hi = wr.astype(BF16)
    wr = jnp.stack([wr_hi, (wr - wr_hi.astype(F32)).astype(BF16)])
    br = jnp.concatenate([b_router[0], jnp.full((LANES - N_EXPERTS,), -jnp.inf, F32)]).reshape(1, LANES)
    post_w = (w_branch_a[0].astype(BF16), w_branch_b[0].astype(BF16), w_out[0].astype(BF16),
              norm_ffn_g[0].reshape(1, D_MODEL), wr, br)
    x1_p, h2w_p, idx_p, gate_p, rank_p, cnt_p = _post(a_p, r_p, sga_p, sgb_p, xp, mod_p, *post_w,
                                                      jnp.zeros((1, LANES), F32), tmp, bpm_big)
    x1_s, h2w_s, idx_s, gate_s, rank_s, cnt_all = _post(a_s, r_s, sga_s, sgb_s, xs, mod_s, *post_w, cnt_p,
                                                        TOK_BLOCK, 1)

    del cnt_all
    n_tok, lb = tp + ts, TOK_BLOCK
    nb = n_tok // lb
    idx4 = jnp.concatenate([idx_p[:TOP_K], idx_s[:TOP_K]], axis=1)
    rank4 = jnp.concatenate([rank_p[:TOP_K], rank_s[:TOP_K]], axis=1)
    gate8 = jnp.concatenate([gate_p, gate_s], axis=1)
    hit = idx4[None] == jnp.arange(N_EXPERTS, dtype=I32)[:, None, None]
    bc = jnp.sum(hit.reshape(N_EXPERTS, TOP_K, nb, lb).astype(I32), axis=(1, 3)).T
    run = (bc + SEG_ALIGN - 1) // SEG_ALIGN * SEG_ALIGN
    padded = (jnp.sum(run, axis=0) + MOE_ROWS - 1) // MOE_ROWS * MOE_ROWS
    pend = jnp.cumsum(padded)
    pstart = pend - padded
    loc = jnp.cumsum(run, axis=1) - run
    glob = pstart[None, :] + jnp.cumsum(run, axis=0) - run
    carry = jnp.cumsum(bc, axis=0) - bc
    n_blocks = -(-(n_tok * TOP_K + nb * N_EXPERTS * (SEG_ALIGN - 1)) // MOE_ROWS) + N_EXPERTS
    n_rows = n_blocks * MOE_ROWS
    used = jnp.sum(run, axis=1, keepdims=True)
    runs = jnp.stack([loc, glob, run], axis=-1)
    seg_table = lambda spare: jnp.concatenate(
        [runs, jnp.concatenate([used, spare, SEG_ROWS - used], axis=1)[:, None, :]], axis=1).reshape(-1).astype(I32)
    spare_out = n_rows + (jnp.arange(nb, dtype=I32)[:, None] % 2) * (SEG_ROWS // 2)
    seg_out, seg_in = seg_table(spare_out), seg_table(jnp.zeros((nb, 1), I32))
    shift = jnp.repeat(loc - carry, lb, axis=0).T
    lpos4 = rank4 + jnp.sum(jnp.where(hit, shift[:, None, :], 0), axis=0)
    lpos8 = jnp.pad(lpos4, ((0, 8 - TOP_K), (0, 0))).astype(I32)
    blk_start = jnp.arange(n_blocks, dtype=I32) * MOE_ROWS
    blk_expert = jnp.minimum(jnp.sum((blk_start[:, None] >= pend[None, :]).astype(I32), axis=1), N_EXPERTS - 1)
    blk_first = jnp.concatenate([jnp.ones((1,), I32), (blk_expert[1:] != blk_expert[:-1]).astype(I32)])
    n_used = (pend[-1] // MOE_ROWS).astype(I32).reshape(1)
    x_sorted = jnp.zeros((n_rows + SEG_ROWS, D_MODEL // 2), jnp.uint32)
    x_sorted = _dispatch(seg_out, lpos8, jnp.concatenate([h2w_p, h2w_s], axis=0), x_sorted)
    rows_out = _moe(blk_expert, blk_first, n_used, x_sorted, w_gate_up[0], b_gate_up[0], w_down[0], b_down[0])

    g_final = norm_final_g.reshape(1, D_MODEL)
    y_p = _final(seg_in, x1_p, mod_p, lpos8, gate8, rows_out, g_final, bpm_p, 0)
    y_s = _final(seg_in, x1_s, mod_s, lpos8, gate8, rows_out, g_final, 1, tp // lb)

    kv_shape = lambda b, s: (1, b, s, ATT_KV_HEADS, HEAD_DIM)
    st_shape = lambda b: (1, b, RET_HEADS, RET_DK, RET_DV)
    return (y_p.reshape(batch, seq, D_MODEL), y_s.reshape(db, dt, D_MODEL),
            k_p.reshape(kv_shape(batch, seq)), v_p.reshape(kv_shape(batch, seq)),
            ikw_p[:, :IDX_DIM].reshape(1, batch, seq, IDX_DIM), st_p.reshape(st_shape(batch)),
            k_s.reshape(kv_shape(db, dt)), v_s.reshape(kv_shape(db, dt)),
            ikw_s[:, :IDX_DIM].reshape(1, db, dt, IDX_DIM), st_s.reshape(st_shape(db)))
```

```python
import functools

import jax
import jax.numpy as jnp
import numpy as np
from jax import lax
from jax.experimental import pallas as pl
from jax.experimental.pallas import tpu as pltpu

F32 = jnp.float32
BF16 = jnp.bfloat16
I32 = jnp.int32

D_MODEL = 1024
PAST_LEN = 8192
PAGE_SIZE = 128
ATT_HEADS = 8
ATT_KV_HEADS = 2
HEAD_DIM = 64
ROPE_DIM = HEAD_DIM // 4
ROPE_THETA = 500000.0
IDX_HEADS = 8
IDX_DIM = 64
IDX_ROPE_DIM = IDX_DIM // 4
TOPK_MAX = 256
RET_HEADS = 8
RET_DK = 64
RET_DV = 128
RET_THETA = 10000.0
RET_CHUNK = 128
N_EXPERTS = 32
TOP_K = 4
D_FF = D_MODEL
SWIGLU_LIMIT = 7.0
SWIGLU_ALPHA = 1.702
NORM_EPS = 1e-6
GN_EPS = 1e-5
ATT_OUT = ATT_HEADS * HEAD_DIM
RET_OUT = RET_HEADS * RET_DV
IN_SPLITS = (ATT_HEADS * HEAD_DIM, ATT_KV_HEADS * HEAD_DIM, ATT_KV_HEADS * HEAD_DIM,
             IDX_HEADS * IDX_DIM, IDX_DIM, IDX_HEADS,
             RET_HEADS * RET_DK, RET_HEADS * RET_DK, RET_OUT, RET_OUT, D_MODEL, D_MODEL)

LANES = 128
MASK_NEG = -1e30
FLT_MAX = 3.4028234663852886e38
SELECT_UNROLL = 4
CAUSAL_VARIANTS = 8
IDX_KEY_CHUNK = 256
VMEM_LIMIT = 56 * 1024 * 1024

TOK_BLOCK = 256
PROMPT_TOK_BLOCK = 512
Q_BLOCK = 128
IDX_Q_ROWS = 128
MOE_ROWS = 512
SEG_ALIGN = 8
SEG_BITS = 6
SEG_ROWS = TOP_K * TOK_BLOCK + N_EXPERTS * SEG_ALIGN

_W_GROUPS = (("q", 1024), ("iq", 1024), ("kvi", 384), ("rq", 512), ("rk", 512),
             ("rv", 1024), ("rg", 1024), ("ga", 1024), ("gb", 1024))
PROJ_COLS = 512
_W_OFF = {}
_off = 0
for _n, _w in _W_GROUPS:
    _W_OFF[_n] = (_off, _w)
    _off += _w
W_COLS = _off


def _cparams(*sem):
    return pltpu.CompilerParams(dimension_semantics=sem, vmem_limit_bytes=VMEM_LIMIT)


def _adaln_kernel(c_ref, w_ref, b_ref, o_ref):
    c = c_ref[...]
    s = c * jax.nn.sigmoid(c)
    o_ref[...] = jnp.dot(s, w_ref[...], preferred_element_type=F32, precision=lax.Precision.HIGHEST) + b_ref[...]


def _adaln(c_all, w_ada, b_ada):
    n = c_all.shape[0]
    nb = 1536
    return pl.pallas_call(
        _adaln_kernel,
        grid=(6 * D_MODEL // nb,),
        in_specs=[pl.BlockSpec((n, D_MODEL), lambda j: (0, 0)),
                  pl.BlockSpec((D_MODEL, nb), lambda j: (0, j)),
                  pl.BlockSpec((1, nb), lambda j: (0, j))],
        out_specs=pl.BlockSpec((n, nb), lambda j: (0, j)),
        out_shape=jax.ShapeDtypeStruct((n, 6 * D_MODEL), F32),
        compiler_params=_cparams("arbitrary"),
        name="adaln",
    )(c_all, w_ada, b_ada.reshape(1, -1))


def _rope_slab(z, c, sa, sb, half):
    return z * c + pltpu.roll(z, LANES - half, 1) * sa + pltpu.roll(z, half, 1) * sb


def _inproj_kernel(x_ref, mod_ref, g_ref, w_ref, ca_ref, saa_ref, sba_ref, cr_ref, sar_ref, sbr_ref,
                   q_ref, k_ref, v_ref, kb_ref, vb_ref, iq_ref, ikw_ref, ikb_ref,
                   rq_ref, rk_ref, rv_ref, sg_ref, sga_ref, sgb_ref):
    x = x_ref[...]
    ms = jnp.mean(x * x, axis=-1, keepdims=True)
    y = x * lax.rsqrt(ms + NORM_EPS) * g_ref[...]
    h = (y * (1.0 + mod_ref[:, D_MODEL:2 * D_MODEL]) + mod_ref[:, 0:D_MODEL]).astype(BF16)

    def slabs(name):
        c0, width = _W_OFF[name]
        step = min(width, PROJ_COLS)
        for j in range(width // step):
            z = jnp.dot(h, w_ref[:, c0 + j * step:c0 + (j + 1) * step], preferred_element_type=F32)
            for s in range(step // LANES):
                yield j * (step // LANES) + s, z[:, s * LANES:(s + 1) * LANES]

    ca, saa, sba = ca_ref[...], saa_ref[...], sba_ref[...]
    cr, sar, sbr = cr_ref[...], sar_ref[...], sbr_ref[...]
    att_half, ret_half = ROPE_DIM // 2, RET_DK // 2
    lane = lax.broadcasted_iota(I32, (x.shape[0], LANES), 1)
    sl = lambda s: slice(s * LANES, (s + 1) * LANES)

    for s, z in slabs("q"):
        q_ref[:, sl(s)] = (_rope_slab(z, ca, saa, sba, att_half) * 0.125).astype(BF16)
    for s, z in slabs("iq"):
        iq_ref[:, sl(s)] = (_rope_slab(z, ca, saa, sba, att_half) * 0.125).astype(BF16)
    (_, zk), (_, zv), (_, zi) = slabs("kvi")
    kk = _rope_slab(zk, ca, saa, sba, att_half)
    for j in range(ATT_KV_HEADS):
        k_ref[:, j, :] = kk[:, j * HEAD_DIM:(j + 1) * HEAD_DIM]
        v_ref[:, j, :] = zv[:, j * HEAD_DIM:(j + 1) * HEAD_DIM]
    kb_ref[...] = kk.astype(BF16)
    vb_ref[...] = zv.astype(BF16)
    zr = _rope_slab(zi, ca, saa, sba, att_half)
    ikw_ref[...] = jnp.where(lane < IDX_DIM, zr, zi)
    ikb_ref[...] = jnp.where(lane < IDX_DIM, zr, 0.0).astype(BF16)
    for s, z in slabs("rq"):
        rq_ref[:, sl(s)] = _rope_slab(z, cr, sar, sbr, ret_half).astype(BF16)
    for s, z in slabs("rk"):
        rk_ref[:, sl(s)] = (_rope_slab(z, cr, sar, sbr, ret_half) * 0.125).astype(BF16)
    for s, z in slabs("rv"):
        rv_ref[:, sl(s)] = z.astype(BF16)
    for s, z in slabs("rg"):
        sg_ref[:, sl(s)] = (z * jax.nn.sigmoid(z)).astype(BF16)
    for s, z in slabs("ga"):
        sga_ref[:, sl(s)] = jax.nn.sigmoid(z).astype(BF16)
    for s, z in slabs("gb"):
        sgb_ref[:, sl(s)] = jax.nn.sigmoid(z).astype(BF16)


def _inproj(x, mod3, g, w_packed, tabs_att, tabs_ret, tm, blocks_per_mod, tab_blocks):
    t = x.shape[0]
    nblk = t // tm
    mod_rows = mod3.shape[1]
    tab_spec = pl.BlockSpec((tm, LANES), lambda i: (i % tab_blocks, 0))
    row = lambda n: pl.BlockSpec((tm, n), lambda i: (i, 0))
    kv_spec = pl.BlockSpec((tm, ATT_KV_HEADS, HEAD_DIM), lambda i: (i, 0, 0))
    out_defs = [(1024, BF16), (128, F32), (128, F32), (128, BF16), (128, BF16), (1024, BF16), (128, F32),
                (128, BF16), (512, BF16), (512, BF16), (1024, BF16), (1024, BF16), (1024, BF16), (1024, BF16)]
    return pl.pallas_call(
        _inproj_kernel,
        grid=(nblk,),
        in_specs=[row(D_MODEL),
                  pl.BlockSpec((None, mod_rows, 6 * D_MODEL), lambda i: (i // blocks_per_mod, 0, 0)),
                  pl.BlockSpec((1, D_MODEL), lambda i: (0, 0)),
                  pl.BlockSpec((D_MODEL, W_COLS), lambda i: (0, 0), pipeline_mode=pl.Buffered(1))]
                 + [tab_spec] * 6,
        out_specs=[kv_spec if j in (1, 2) else row(n) for j, (n, _) in enumerate(out_defs)],
        out_shape=[jax.ShapeDtypeStruct((t, ATT_KV_HEADS, HEAD_DIM) if j in (1, 2) else (t, n), d)
                   for j, (n, d) in enumerate(out_defs)],
        compiler_params=_cparams("parallel"),
        name="inproj",
    )(x, mod3, g, w_packed, *tabs_att, *tabs_ret)


def _count(score_ref, n, pred):
    acc = jnp.zeros((score_ref.shape[0], LANES), F32)
    for c in range(n // LANES):
        acc = acc + jnp.where(pred(score_ref[:, c * LANES:(c + 1) * LANES]), 1.0, 0.0)
    return jnp.sum(acc, axis=1, keepdims=True)


def _kth_largest(score_ref, n, k, quarters):
    sc = score_ref[:, :n]
    finite = sc > -jnp.inf
    n_fin = jnp.sum(jnp.where(finite, 1.0, 0.0), axis=1, keepdims=True)
    n_pos = jnp.sum(jnp.where(sc > 0.0, 1.0, 0.0), axis=1, keepdims=True)
    n_nonneg = jnp.sum(jnp.where(sc >= 0.0, 1.0, 0.0), axis=1, keepdims=True)
    mx = jnp.max(sc, axis=1, keepdims=True)
    mn = jnp.min(jnp.where(finite, sc, jnp.inf), axis=1, keepdims=True)
    small = n_fin <= k
    positive = n_pos >= k
    at_zero = jnp.logical_and(jnp.logical_not(positive), n_nonneg >= k)
    lo = jnp.where(positive, 0.0, mn)
    hi = jnp.where(positive, mx + (jnp.abs(mx) * 2.0 ** -20 + 2.0 ** -100), 0.0)
    lo = jnp.where(at_zero, 0.0, lo)
    done = jnp.where(jnp.logical_or(small, at_zero), 1.0, 0.0)

    def cond(state):
        return jnp.min(state[2]) < 0.5

    def body(state):
        lo, hi, done = state
        for _ in range(SELECT_UNROLL):
            mid = 0.5 * lo + 0.5 * hi
            stuck = jnp.logical_or(mid <= lo, mid >= hi)
            live = jnp.logical_and(done < 0.5, jnp.logical_not(stuck))
            if not quarters:
                cnt = _count(score_ref, n, lambda s: s >= mid)
                ge = cnt >= k
                lo = jnp.where(jnp.logical_and(live, ge), mid, lo)
                hi = jnp.where(jnp.logical_and(live, jnp.logical_not(ge)), mid, hi)
                done = jnp.where(jnp.logical_or(stuck, cnt == k), 1.0, done)
                continue
            cands = (0.5 * lo + 0.5 * mid, mid, 0.5 * mid + 0.5 * hi)
            new_lo, hit, ges = lo, jnp.zeros_like(lo), []
            for c in cands:
                cnt = _count(score_ref, n, lambda s, c=c: s >= c)
                ges.append(cnt >= k)
                new_lo = jnp.where(ges[-1], c, new_lo)
                hit = jnp.where(ges[-1], jnp.where(cnt == k, 1.0, 0.0), hit)
            new_hi = hi
            for c, ge in zip(reversed(cands), reversed(ges)):
                new_hi = jnp.where(ge, new_hi, c)
            lo = jnp.where(live, new_lo, lo)
            hi = jnp.where(live, new_hi, hi)
            done = jnp.where(jnp.logical_or(stuck, hit > 0.5), 1.0, done)
        return lo, hi, done

    lo, _, _ = lax.while_loop(cond, body, (lo, hi, done))
    return jnp.where(small, -FLT_MAX, lo)


def _topk_bias(score_ref, bias_ref, tri_ref, n, k, quarters=False):
    rows = score_ref.shape[0]
    thr = _kth_largest(score_ref, n, k, quarters)
    n_ge = _count(score_ref, n, lambda s: s >= thr)
    has_ties = jnp.max(jnp.where(n_ge > k, 1.0, 0.0)) > 0.5

    @pl.when(jnp.logical_not(has_ties))
    def _():
        for c in range(n // LANES):
            sl = slice(c * LANES, (c + 1) * LANES)
            bias_ref[:, sl] = jnp.where(score_ref[:, sl] >= thr, 0.0, MASK_NEG)

    @pl.when(has_ties)
    def _():
        need = k - _count(score_ref, n, lambda s: s > thr)
        run = jnp.zeros((rows, 1), F32)
        for c in range(n // LANES):
            sl = slice(c * LANES, (c + 1) * LANES)
            sc = score_ref[:, sl]
            eq = sc == thr
            eqf = jnp.where(eq, 1.0, 0.0)
            before = jnp.dot(eqf.astype(BF16), tri_ref[...], preferred_element_type=F32) + run
            take = jnp.logical_or(sc > thr, jnp.logical_and(eq, before < need))
            bias_ref[:, sl] = jnp.where(take, 0.0, MASK_NEG)
            run = run + jnp.sum(eqf, axis=1, keepdims=True)


def _dsa_prompt_kernel(q_ref, iq_ref, ikw_ref, kb_ref, vb_ref, ikb_ref, tri_ref, o_ref, score_ref, bias_ref):
    qb, s_len = score_ref.shape
    i = pl.program_id(1)
    nqb = s_len // qb
    per_variant = nqb // CAUSAL_VARIANTS
    for v in range(CAUSAL_VARIANTS):
        pl.when(i // per_variant == v)(
            functools.partial(_dsa_prompt_body, q_ref, iq_ref, ikw_ref, kb_ref, vb_ref, ikb_ref, tri_ref, o_ref,
                              score_ref, bias_ref, (v + 1) * per_variant * qb, min(TOPK_MAX, s_len // 4)))


def _dsa_prompt_body(q_ref, iq_ref, ikw_ref, kb_ref, vb_ref, ikb_ref, tri_ref, o_ref, score_ref, bias_ref,
                     n_keys, topk):
    qb = score_ref.shape[0]
    i = pl.program_id(1)
    w = ikw_ref[:, IDX_DIM:IDX_DIM + IDX_HEADS] * (IDX_HEADS ** -0.5)
    nt = (((1,), (1,)), ((), ()))
    kc = IDX_KEY_CHUNK
    qr = IDX_Q_ROWS
    for c in range(n_keys // kc):
        ikc = ikb_ref[c * kc:(c + 1) * kc, :]
        kpos = c * kc + lax.broadcasted_iota(I32, (qr, kc), 1)
        for r0 in range(0, qb, qr):
            acc = jnp.zeros((qr, kc), F32)
            for h in range(IDX_HEADS):
                d = lax.dot_general(iq_ref[r0:r0 + qr, h * LANES:(h + 1) * LANES], ikc, nt,
                                    preferred_element_type=F32)
                acc = acc + jnp.maximum(d, 0.0) * w[r0:r0 + qr, h:h + 1]
            qpos = i * qb + r0 + lax.broadcasted_iota(I32, (qr, kc), 0)
            score_ref[r0:r0 + qr, c * kc:(c + 1) * kc] = jnp.where(kpos <= qpos, acc, -jnp.inf)

    _topk_bias(score_ref, bias_ref, tri_ref, n_keys, topk)

    kb = kb_ref[0:n_keys, :]
    vb = vb_ref[0:n_keys, :]
    bias = bias_ref[:, 0:n_keys]
    lane = lax.broadcasted_iota(I32, (qb, LANES), 1)
    heads = []
    for h in range(ATT_HEADS):
        s = lax.dot_general(q_ref[:, h * LANES:(h + 1) * LANES], kb, nt, preferred_element_type=F32) + bias
        m = jnp.max(s, axis=1, keepdims=True)
        p = jnp.exp(s - m)
        l = jnp.sum(p, axis=1, keepdims=True)
        heads.append(jnp.dot(p.astype(BF16), vb, preferred_element_type=F32) / l)
    group = ATT_HEADS // ATT_KV_HEADS
    for pp in range(ATT_HEADS // 2):
        a, b = heads[2 * pp], heads[2 * pp + 1]
        if (2 * pp) // group == 0:
            slab = jnp.where(lane < HEAD_DIM, a, pltpu.roll(b, HEAD_DIM, 1))
        else:
            slab = jnp.where(lane < HEAD_DIM, pltpu.roll(a, HEAD_DIM, 1), b)
        o_ref[:, pp * LANES:(pp + 1) * LANES] = slab.astype(BF16)


def _dsa_prompt(q, iq, ikw, kb, vb, ikb, tri, batch, seq):
    nqb = seq // Q_BLOCK
    qrow = lambda n: pl.BlockSpec((Q_BLOCK, n), lambda b, i: (b * nqb + i, 0))
    keys = pl.BlockSpec((seq, LANES), lambda b, i: (b, 0))
    return pl.pallas_call(
        _dsa_prompt_kernel,
        grid=(batch, nqb),
        in_specs=[qrow(1024), qrow(1024), qrow(LANES), keys, keys, keys,
                  pl.BlockSpec((LANES, LANES), lambda b, i: (0, 0))],
        out_specs=qrow(ATT_OUT),
        out_shape=jax.ShapeDtypeStruct((batch * seq, ATT_OUT), BF16),
        scratch_shapes=[pltpu.VMEM((Q_BLOCK, seq), F32), pltpu.VMEM((Q_BLOCK, seq), F32)],
        compiler_params=_cparams("parallel", "arbitrary"),
        name="dsa_prompt",
    )(q, iq, ikw, kb, vb, ikb, tri)


def _dsa_sample_kernel(pt_ref, qs_ref, iqs_ref, ws_ref, knew_ref, vnew_ref, iknew_ref, cik_hbm, ck_hbm, cv_hbm,
                       tri_ref, o_ref, ikbuf, kbuf, vbuf, ikt, kt, vt, sems, key_ref, bias_ref):
    db = pl.program_id(0)
    n_pages = ikbuf.shape[1]
    t = key_ref.shape[0]
    n_past = n_pages * PAGE_SIZE
    last = pl.num_programs(0) - 1

    def fetch(src, dst_of_page, sem, req):
        def body(p, carry):
            pltpu.make_async_copy(src.at[pt_ref[req, p]], dst_of_page(p), sem).start()
            return carry
        lax.fori_loop(0, n_pages, body, 0)

    def wait_all(src, dst, sem):
        pltpu.make_async_copy(src.at[pl.ds(0, n_pages)], dst, sem).wait()

    fetch_ik = lambda req, slot: fetch(cik_hbm, lambda p: ikbuf.at[slot, p], sems.at[slot], req)
    fetch_k = lambda req: fetch(ck_hbm, lambda p: kbuf.at[p], sems.at[2], req)
    fetch_v = lambda req: fetch(cv_hbm, lambda p: vbuf.at[p], sems.at[3], req)

    @pl.when(db == 0)
    def _():
        fetch_ik(0, 0)
        fetch_k(0)
        fetch_v(0)

    @pl.when(db < last)
    def _():
        fetch_ik(db + 1, (db + 1) % 2)

    slot = db % 2
    wait_all(cik_hbm, ikbuf.at[slot], sems.at[slot])

    nt = (((1,), (1,)), ((), ()))
    page = lambda p: slice(p * PAGE_SIZE, (p + 1) * PAGE_SIZE)
    for p in range(n_pages):
        ikt[:, page(p)] = ikbuf[slot, p].astype(BF16)
    iqs = iqs_ref[...]
    wcol = ws_ref[...]
    d_past = jnp.maximum(jnp.dot(iqs, ikt[...], preferred_element_type=F32), 0.0) * wcol
    d_new = jnp.maximum(jnp.dot(iqs, iknew_ref[...], preferred_element_type=F32), 0.0) * wcol
    s_past = d_past[0:t]
    s_new = d_new[0:t]
    for h in range(1, IDX_HEADS):
        s_past = s_past + d_past[h * t:(h + 1) * t]
        s_new = s_new + d_new[h * t:(h + 1) * t]
    row = lax.broadcasted_iota(I32, (t, LANES), 0)
    lane = lax.broadcasted_iota(I32, (t, LANES), 1)
    new_ok = lane <= row
    key_ref[:, 0:n_past] = s_past
    key_ref[:, n_past:n_past + LANES] = jnp.where(new_ok, s_new, -jnp.inf)
    _topk_bias(key_ref, bias_ref, tri_ref, n_past + LANES, min(TOPK_MAX, (n_past + t) // 4), quarters=True)

    def stage(buf, dst):
        for p in range(n_pages):
            for j in range(ATT_KV_HEADS):
                dst[j, :, page(p)] = buf[p, j].astype(BF16)

    wait_all(ck_hbm, kbuf, sems.at[2])
    stage(kbuf, kt)

    @pl.when(db < last)
    def _():
        fetch_k(db + 1)

    wait_all(cv_hbm, vbuf, sems.at[3])
    stage(vbuf, vt)

    @pl.when(db < last)
    def _():
        fetch_v(db + 1)

    rows_per_kv = qs_ref.shape[0] // ATT_KV_HEADS
    bias = jnp.concatenate([bias_ref[...]] * (rows_per_kv // t), axis=0)
    for j in range(ATT_KV_HEADS):
        qj = qs_ref[j * rows_per_kv:(j + 1) * rows_per_kv, :]
        sp = jnp.dot(qj, kt[j], preferred_element_type=F32) + bias[:, 0:n_past]
        sn = jnp.dot(qj, knew_ref[j], preferred_element_type=F32) + bias[:, n_past:n_past + LANES]
        m = jnp.maximum(jnp.max(sp, axis=1, keepdims=True), jnp.max(sn, axis=1, keepdims=True))
        pp = jnp.exp(sp - m)
        pn = jnp.exp(sn - m)
        l = jnp.sum(pp, axis=1, keepdims=True) + jnp.sum(pn, axis=1, keepdims=True)
        o = (lax.dot_general(pp.astype(BF16), vt[j], nt, preferred_element_type=F32)
             + lax.dot_general(pn.astype(BF16), vnew_ref[j], nt, preferred_element_type=F32))
        o_ref[j * rows_per_kv:(j + 1) * rows_per_kv, :] = o / l


def _dsa_sample(page_table, qs, iqs, ws, knew, vnew, iknew, cache_ik, cache_k, cache_v, tri, t):
    db, n_pages = page_table.shape
    rows = qs.shape[1]
    per_db = lambda r, n: pl.BlockSpec((None, r, n), lambda b, pt: (b, 0, 0))
    any_spec = pl.BlockSpec(memory_space=pl.ANY)
    n_keys = n_pages * PAGE_SIZE + LANES
    grid_spec = pltpu.PrefetchScalarGridSpec(
        num_scalar_prefetch=1,
        grid=(db,),
        in_specs=[per_db(rows, HEAD_DIM), per_db(rows, IDX_DIM), per_db(rows, 1),
                  pl.BlockSpec((None, ATT_KV_HEADS, HEAD_DIM, LANES), lambda b, pt: (b, 0, 0, 0)),
                  pl.BlockSpec((None, ATT_KV_HEADS, HEAD_DIM, LANES), lambda b, pt: (b, 0, 0, 0)),
                  per_db(IDX_DIM, LANES), any_spec, any_spec, any_spec,
                  pl.BlockSpec((LANES, LANES), lambda b, pt: (0, 0))],
        out_specs=per_db(rows, HEAD_DIM),
        scratch_shapes=[pltpu.VMEM((2, n_pages, IDX_DIM, PAGE_SIZE), F32),
                        pltpu.VMEM((n_pages, ATT_KV_HEADS, HEAD_DIM, PAGE_SIZE), F32),
                        pltpu.VMEM((n_pages, ATT_KV_HEADS, HEAD_DIM, PAGE_SIZE), F32),
                        pltpu.VMEM((IDX_DIM, n_pages * PAGE_SIZE), BF16),
                        pltpu.VMEM((ATT_KV_HEADS, HEAD_DIM, n_pages * PAGE_SIZE), BF16),
                        pltpu.VMEM((ATT_KV_HEADS, HEAD_DIM, n_pages * PAGE_SIZE), BF16),
                        pltpu.SemaphoreType.DMA((4,)),
                        pltpu.VMEM((t, n_keys), F32),
                        pltpu.VMEM((t, n_keys), F32)])
    return pl.pallas_call(
        _dsa_sample_kernel,
        grid_spec=grid_spec,
        out_shape=jax.ShapeDtypeStruct((db, rows, HEAD_DIM), F32),
        compiler_params=_cparams("arbitrary"),
        name="dsa_sample",
    )(page_table, qs, iqs, ws, knew, vnew, iknew, cache_ik, cache_k, cache_v, tri)


def _retention_kernel(has_init, rq_ref, rk_ref, rv_ref, sg_ref, decay_ref, qdec_ref, kdec_ref, gst_ref, *rest):
    if has_init:
        init_ref, o_ref, st_ref, state = rest
    else:
        o_ref, st_ref, state = rest
    c = pl.program_id(1)

    @pl.when(c == 0)
    def _():
        if has_init:
            state[...] = init_ref[...]
        else:
            state[...] = jnp.zeros_like(state)

    nt = (((1,), (1,)), ((), ()))
    tn = (((0,), (0,)), ((), ()))
    rows = rq_ref.shape[0]
    lane = lax.broadcasted_iota(I32, (rows, LANES), 1)
    for p in range(RET_HEADS // 2):
        sl = slice(p * LANES, (p + 1) * LANES)
        qp = rq_ref[:, sl]
        kp = rk_ref[:, sl]
        qd = (qp.astype(F32) * qdec_ref[:, sl]).astype(BF16)
        kd = (kp.astype(F32) * kdec_ref[:, sl]).astype(BF16)
        s_old = state[p]
        s_old_b = s_old.astype(BF16)
        s_new = s_old * gst_ref[p]
        for e in range(2):
            h = 2 * p + e
            hs = slice(h * LANES, (h + 1) * LANES)
            mine = jnp.where((lane >= e * RET_DK) & (lane < (e + 1) * RET_DK), 1.0, 0.0).astype(BF16)
            sc = lax.dot_general(qp * mine, kp, nt, preferred_element_type=F32) * decay_ref[h]
            vh = rv_ref[:, hs]
            o = (jnp.dot(sc.astype(BF16), vh, preferred_element_type=F32)
                 + jnp.dot(qd * mine, s_old_b, preferred_element_type=F32))
            s_new = s_new + lax.dot_general(kd * mine, vh, tn, preferred_element_type=F32)
            mu = jnp.mean(o, axis=-1, keepdims=True)
            var = jnp.mean(jnp.square(o - mu), axis=-1, keepdims=True)
            on = (o - mu) * lax.rsqrt(var + GN_EPS)
            o_ref[:, hs] = (on * sg_ref[:, hs].astype(F32)).astype(BF16)
        state[p] = s_new

    @pl.when(c == pl.num_programs(1) - 1)
    def _():
        st_ref[...] = state[...]


def _retention(rq, rk, rv, sg, tables, init, batch, n_chunks):
    decay, qdec, kdec, gst = tables
    cr = RET_CHUNK
    rowspec = lambda n: pl.BlockSpec((cr, n), lambda b, c: (b * n_chunks + c, 0))
    const = lambda shape: pl.BlockSpec(shape, lambda b, c: (0,) * len(shape))
    st_spec = pl.BlockSpec((None, RET_HEADS // 2, LANES, LANES), lambda b, c: (b, 0, 0, 0))
    in_specs = [rowspec(512), rowspec(512), rowspec(RET_OUT), rowspec(RET_OUT),
                const(decay.shape), const(qdec.shape), const(kdec.shape), const(gst.shape)]
    args = [rq, rk, rv, sg, decay, qdec, kdec, gst]
    if init is not None:
        in_specs.append(st_spec)
        args.append(init)
    return pl.pallas_call(
        functools.partial(_retention_kernel, init is not None),
        grid=(batch, n_chunks),
        in_specs=in_specs,
        out_specs=[rowspec(RET_OUT), st_spec],
        out_shape=[jax.ShapeDtypeStruct((batch * n_chunks * cr, RET_OUT), BF16),
                   jax.ShapeDtypeStruct((batch, RET_HEADS // 2, LANES, LANES), F32)],
        scratch_shapes=[pltpu.VMEM((RET_HEADS // 2, LANES, LANES), F32)],
        compiler_params=_cparams("parallel", "arbitrary"),
        name="retention",
    )(*args)


def _retention_tables(c_eff):
    lg = jnp.log(1.0 - 2.0 ** (-5.0 - jnp.arange(RET_HEADS, dtype=F32)))
    i = jnp.arange(RET_CHUNK, dtype=F32)
    diff = i[:, None] - i[None, :]
    decay = jnp.where(diff >= 0, jnp.exp(jnp.maximum(diff, 0.0)[None] * lg[:, None, None]), 0.0)
    q_decay = jnp.exp((i + 1.0)[:, None] * lg[None, :])
    k_decay = jnp.exp((c_eff - 1.0 - i)[:, None] * lg[None, :])
    qdec = jnp.repeat(q_decay, RET_DK, axis=1)
    kdec = jnp.repeat(k_decay, RET_DK, axis=1)
    g_state = jnp.exp(c_eff * lg)
    gst = jnp.broadcast_to(jnp.repeat(g_state, RET_DK).reshape(RET_HEADS // 2, LANES, 1),
                           (RET_HEADS // 2, LANES, LANES))
    return decay, qdec, kdec, gst


def _pack_bf16_pairs(lo, hi):
    return pltpu.pack_elementwise([lo, hi], packed_dtype=BF16)


def _unpack_bf16_pairs(words):
    return tuple(pltpu.unpack_elementwise(words, index=j, packed_dtype=BF16, unpacked_dtype=F32).astype(BF16)
                 for j in range(2))


def _post_kernel(a_ref, r_ref, sga_ref, sgb_ref, x_ref, mod_ref, wpa_ref, wpb_ref, wo_ref, g_ref, wr_ref, br_ref,
                 ltri_ref, cnt0_ref, x1_ref, h2w_ref, idx_ref, gate_ref, rank_ref, cnt_ref, cnt):
    @pl.when(pl.program_id(0) == 0)
    def _():
        cnt[...] = cnt0_ref[...]

    pa = jnp.dot(a_ref[...], wpa_ref[...], preferred_element_type=F32)
    pb = jnp.dot(r_ref[...], wpb_ref[...], preferred_element_type=F32)
    merged = sga_ref[...].astype(F32) * pa + sgb_ref[...].astype(F32) * pb
    gt1 = mod_ref[:, 2 * D_MODEL:3 * D_MODEL]
    x1 = x_ref[...] + gt1 * jnp.dot(merged.astype(BF16), wo_ref[...], preferred_element_type=F32)
    x1_ref[...] = x1
    ms = jnp.mean(x1 * x1, axis=-1, keepdims=True)
    y = x1 * lax.rsqrt(ms + NORM_EPS) * g_ref[...]
    h2 = y * (1.0 + mod_ref[:, 4 * D_MODEL:5 * D_MODEL]) + mod_ref[:, 3 * D_MODEL:4 * D_MODEL]
    half = D_MODEL // 2
    h2w_ref[...] = _pack_bf16_pairs(h2[:, :half], h2[:, half:])
    h_hi = h2.astype(BF16)
    h_lo = (h2 - h_hi.astype(F32)).astype(BF16)
    logits = (jnp.dot(h_hi, wr_ref[0], preferred_element_type=F32) + jnp.dot(h_hi, wr_ref[1], preferred_element_type=F32)
              + jnp.dot(h_lo, wr_ref[0], preferred_element_type=F32) + br_ref[...])
    lane = lax.broadcasted_iota(I32, logits.shape, 1).astype(F32)
    idx_out = jnp.zeros(logits.shape, F32)
    val_out = jnp.zeros(logits.shape, F32)
    chosen = []
    top = None
    for j in range(TOP_K):
        m = jnp.max(logits, axis=1, keepdims=True)
        am = jnp.min(jnp.where(logits == m, lane, float(LANES)), axis=1, keepdims=True)
        if j == 0:
            top = m
        idx_out = jnp.where(lane == j, am, idx_out)
        val_out = jnp.where(lane == j, jnp.exp(m - top), val_out)
        chosen.append(lane == am)
        logits = jnp.where(chosen[-1], -jnp.inf, logits)
    idx_ref[...] = jnp.transpose(idx_out)[0:8, :].astype(I32)
    gate_ref[...] = jnp.transpose(val_out / jnp.sum(val_out, axis=1, keepdims=True))[0:8, :]
    onehot = jnp.where(chosen[0] | chosen[1] | chosen[2] | chosen[3], 1.0, 0.0)
    before = jnp.dot(ltri_ref[...], onehot.astype(BF16), preferred_element_type=F32) + cnt[...]
    rank_out = jnp.zeros(logits.shape, F32)
    for j in range(TOP_K):
        rj = jnp.sum(jnp.where(chosen[j], before, 0.0), axis=1, keepdims=True)
        rank_out = jnp.where(lane == j, rj, rank_out)
    rank_ref[...] = jnp.transpose(rank_out)[0:8, :].astype(I32)
    cnt[...] = cnt[...] + jnp.sum(onehot, axis=0, keepdims=True)
    cnt_ref[...] = cnt[...]


def _post(a, r, sga, sgb, x, mod3, wpa, wpb, wo, g, wr, br, cnt0, tm, blocks_per_mod):
    t = x.shape[0]
    mod_rows = mod3.shape[1]
    ar = jnp.arange(tm)
    ltri = (ar[None, :] < ar[:, None]).astype(BF16)
    row = lambda n: pl.BlockSpec((tm, n), lambda i: (i, 0))
    col8 = pl.BlockSpec((8, tm), lambda i: (0, i))
    const = lambda a_: pl.BlockSpec(a_.shape, lambda i: (0,) * a_.ndim)
    return pl.pallas_call(
        _post_kernel,
        grid=(t // tm,),
        in_specs=[row(ATT_OUT), row(RET_OUT), row(D_MODEL), row(D_MODEL), row(D_MODEL),
                  pl.BlockSpec((None, mod_rows, 6 * D_MODEL), lambda i: (i // blocks_per_mod, 0, 0)),
                  const(wpa), const(wpb), const(wo), const(g), const(wr), const(br), const(ltri), const(cnt0)],
        out_specs=[row(D_MODEL), row(D_MODEL // 2), col8, col8, col8,
                   pl.BlockSpec((1, LANES), lambda i: (0, 0))],
        out_shape=[jax.ShapeDtypeStruct((t, D_MODEL), F32), jax.ShapeDtypeStruct((t, D_MODEL // 2), jnp.uint32),
                   jax.ShapeDtypeStruct((8, t), I32), jax.ShapeDtypeStruct((8, t), F32),
                   jax.ShapeDtypeStruct((8, t), I32), jax.ShapeDtypeStruct((1, LANES), F32)],
        scratch_shapes=[pltpu.VMEM((1, LANES), F32)],
        compiler_params=_cparams("arbitrary"),
        name="post",
    )(a, r, sga, sgb, x, mod3, wpa, wpb, wo, g, wr, br, ltri, cnt0)


def _segment_copies(seg_ref, blk, make_copy):
    for e in range(N_EXPERTS + 1):
        base = (blk * (N_EXPERTS + 1) + e) * 3
        loc, glob, n = seg_ref[base], seg_ref[base + 1], seg_ref[base + 2]
        for k in range(SEG_BITS):
            size = SEG_ALIGN << k

            @pl.when((n & size) != 0)
            def _():
                done = n & (size - 1)
                make_copy(pl.multiple_of(loc + done, SEG_ALIGN), pl.multiple_of(glob + done, SEG_ALIGN),
                          size).start()


def _local_order_matrix(lpos_ref, values, rows):
    lp = lpos_ref[...]
    r = lax.broadcasted_iota(I32, (rows, lp.shape[1]), 0)
    m = jnp.zeros((rows, lp.shape[1]), F32)
    for j in range(TOP_K):
        m = jnp.where(r == lp[j:j + 1, :], values(j), m)
    return m


def _dispatch_kernel(seg_ref, lpos_ref, h2w_ref, xs_in, xs_out, buf, sems):
    del xs_in
    i = pl.program_id(0)
    slot = i % 2
    rows = buf.shape[1]

    def wait_slot(s):
        pltpu.make_async_copy(buf.at[s], xs_out.at[pl.ds(0, rows)], sems.at[s]).wait()

    @pl.when(i >= 2)
    def _():
        wait_slot(slot)

    x = jnp.concatenate(_unpack_bf16_pairs(h2w_ref[...]), axis=1)
    perm = _local_order_matrix(lpos_ref, lambda j: 1.0, rows).astype(BF16)
    xs = jnp.dot(perm, x, preferred_element_type=F32)
    half = D_MODEL // 2
    buf[slot] = _pack_bf16_pairs(xs[:, :half], xs[:, half:])
    _segment_copies(seg_ref, i, lambda loc, glob, size: pltpu.make_async_copy(
        buf.at[slot, pl.ds(loc, size)], xs_out.at[pl.ds(glob, size)], sems.at[slot]))

    @pl.when(i == pl.num_programs(0) - 1)
    def _():
        wait_slot(slot)

        @pl.when(i >= 1)
        def _():
            wait_slot(1 - slot)


def _dispatch(seg, lpos8, h2w, x_sorted):
    t = h2w.shape[0]
    tm = TOK_BLOCK
    grid_spec = pltpu.PrefetchScalarGridSpec(
        num_scalar_prefetch=1,
        grid=(t // tm,),
        in_specs=[pl.BlockSpec((8, tm), lambda i, s: (0, i)),
                  pl.BlockSpec((tm, D_MODEL // 2), lambda i, s: (i, 0)), pl.BlockSpec(memory_space=pl.ANY)],
        out_specs=pl.BlockSpec(memory_space=pl.ANY),
        scratch_shapes=[pltpu.VMEM((2, SEG_ROWS, D_MODEL // 2), jnp.uint32), pltpu.SemaphoreType.DMA((2,))])
    return pl.pallas_call(
        _dispatch_kernel,
        grid_spec=grid_spec,
        out_shape=jax.ShapeDtypeStruct(x_sorted.shape, x_sorted.dtype),
        input_output_aliases={3: 0},
        compiler_params=_cparams("arbitrary"),
        name="dispatch",
    )(seg, lpos8, h2w, x_sorted)


def _moe_kernel(be_ref, first_ref, nused_ref, next_ref, has_next_ref, x_ref, wgu_hbm, bgu_ref, wd_hbm, bd_ref, o_ref,
                wgu_b, wd_b, wgu_f, wd_f, sems):
    i = pl.program_id(0)

    def fetch(e):
        return (pltpu.make_async_copy(wgu_hbm.at[e], wgu_f, sems.at[0]),
                pltpu.make_async_copy(wd_hbm.at[e], wd_f, sems.at[1]))

    @pl.when(first_ref[i] == 1)
    def _():
        @pl.when(i == 0)
        def _():
            for cp in fetch(be_ref[0]):
                cp.start()

        for cp in fetch(be_ref[i]):
            cp.wait()
        wgu_b[...] = wgu_f[...].astype(BF16)
        wd_b[...] = wd_f[...].astype(BF16)

        @pl.when(has_next_ref[i] == 1)
        def _():
            for cp in fetch(next_ref[i]):
                cp.start()

    @pl.when(i < nused_ref[0])
    def _():
        x = jnp.concatenate(_unpack_bf16_pairs(x_ref[...]), axis=1)
        gu = jnp.dot(x, wgu_b[...], preferred_element_type=F32) + bgu_ref[...]
        g = jnp.minimum(gu[:, :D_FF], SWIGLU_LIMIT)
        u = jnp.clip(gu[:, D_FF:], -SWIGLU_LIMIT, SWIGLU_LIMIT)
        act = (u + 1.0) * (g * jax.nn.sigmoid(SWIGLU_ALPHA * g))
        o_ref[...] = jnp.dot(act.astype(BF16), wd_b[...], preferred_element_type=F32) + bd_ref[...]

    @pl.when(i >= nused_ref[0])
    def _():
        o_ref[...] = jnp.zeros_like(o_ref)


def _moe(blk_expert, blk_first, n_used, next_expert, has_next, x_sorted, w_gate_up, b_gate_up, w_down, b_down):
    n_rows = blk_expert.shape[0] * MOE_ROWS
    grid_spec = pltpu.PrefetchScalarGridSpec(
        num_scalar_prefetch=5,
        grid=(n_rows // MOE_ROWS,),
        in_specs=[pl.BlockSpec((MOE_ROWS, D_MODEL // 2), lambda i, be, bf, nu, ne, hn: (i, 0)),
                  pl.BlockSpec(memory_space=pl.ANY),
                  pl.BlockSpec((None, 1, 2 * D_FF), lambda i, be, bf, nu, ne, hn: (be[i], 0, 0)),
                  pl.BlockSpec(memory_space=pl.ANY),
                  pl.BlockSpec((None, 1, D_MODEL), lambda i, be, bf, nu, ne, hn: (be[i], 0, 0))],
        out_specs=pl.BlockSpec((MOE_ROWS, D_MODEL), lambda i, be, bf, nu, ne, hn: (i, 0)),
        scratch_shapes=[pltpu.VMEM((D_MODEL, 2 * D_FF), BF16), pltpu.VMEM((D_FF, D_MODEL), BF16),
                        pltpu.VMEM((D_MODEL, 2 * D_FF), F32), pltpu.VMEM((D_FF, D_MODEL), F32),
                        pltpu.SemaphoreType.DMA((2,))])
    return pl.pallas_call(
        _moe_kernel,
        grid_spec=grid_spec,
        out_shape=jax.ShapeDtypeStruct((n_rows, D_MODEL), F32),
        compiler_params=_cparams("arbitrary"),
        name="moe",
    )(blk_expert, blk_first, n_used, next_expert, has_next, x_sorted, w_gate_up,
      b_gate_up.reshape(N_EXPERTS, 1, -1), w_down, b_down.reshape(N_EXPERTS, 1, -1))


def _final_kernel(block_offset, seg_ref, x1_ref, mod_ref, lpos_ref, gate_ref, g_ref, rows_hbm, y_ref, buf, sems):
    i = pl.program_id(0)
    n = pl.num_programs(0)
    rows = buf.shape[1]

    def issue(blk, slot):
        _segment_copies(seg_ref, blk + block_offset, lambda loc, glob, size: pltpu.make_async_copy(
            rows_hbm.at[pl.ds(glob, size)], buf.at[slot, pl.ds(loc, size)], sems.at[slot]))

    @pl.when(i == 0)
    def _():
        issue(0, 0)

    @pl.when(i + 1 < n)
    def _():
        issue(i + 1, (i + 1) % 2)

    slot = i % 2
    pltpu.make_async_copy(rows_hbm.at[pl.ds(0, rows)], buf.at[slot], sems.at[slot]).wait()
    gate = gate_ref[...]
    gmat = _local_order_matrix(lpos_ref, lambda j: gate[j:j + 1, :], rows).astype(BF16)
    moe = lax.dot_general(gmat, buf[slot].astype(BF16), (((0,), (0,)), ((), ())), preferred_element_type=F32)
    x2 = x1_ref[...] + mod_ref[:, 5 * D_MODEL:6 * D_MODEL] * moe
    ms = jnp.mean(x2 * x2, axis=-1, keepdims=True)
    y_ref[...] = x2 * lax.rsqrt(ms + NORM_EPS) * g_ref[...]


def _final(seg, x1, mod3, lpos8, gate8, rows_out, g, blocks_per_mod, block_offset):
    t = x1.shape[0]
    tm = TOK_BLOCK
    mod_rows = mod3.shape[1]
    row = lambda n: pl.BlockSpec((tm, n), lambda i, s: (i, 0))
    col8 = pl.BlockSpec((8, tm), lambda i, s: (0, i + block_offset))
    grid_spec = pltpu.PrefetchScalarGridSpec(
        num_scalar_prefetch=1,
        grid=(t // tm,),
        in_specs=[row(D_MODEL),
                  pl.BlockSpec((None, mod_rows, 6 * D_MODEL), lambda i, s: (i // blocks_per_mod, 0, 0)),
                  col8, col8,
                  pl.BlockSpec((1, D_MODEL), lambda i, s: (0, 0)),
                  pl.BlockSpec(memory_space=pl.ANY)],
        out_specs=row(D_MODEL),
        scratch_shapes=[pltpu.VMEM((2, SEG_ROWS, D_MODEL), F32), pltpu.SemaphoreType.DMA((2,))])
    return pl.pallas_call(
        functools.partial(_final_kernel, block_offset),
        grid_spec=grid_spec,
        out_shape=jax.ShapeDtypeStruct((t, D_MODEL), F32),
        compiler_params=_cparams("arbitrary"),
        name="final",
    )(seg, x1, mod3, lpos8, gate8, g, rows_out)


def _rope_tables(pos, rot_dim, theta, head_dim):
    half = rot_dim // 2
    inv = theta ** (-jnp.arange(half, dtype=F32) * (2.0 / rot_dim))
    ang = pos.astype(F32)[:, None] * inv[None, :]
    cos, sin = jnp.cos(ang), jnp.sin(ang)
    n = pos.shape[0]
    rest = head_dim - rot_dim
    zh = jnp.zeros((n, half), F32)
    c = jnp.concatenate([cos, cos, jnp.ones((n, rest), F32)], axis=1)
    sa = jnp.concatenate([-sin, zh, jnp.zeros((n, rest), F32)], axis=1)
    sb = jnp.concatenate([zh, sin, jnp.zeros((n, rest), F32)], axis=1)
    rep = LANES // head_dim
    return tuple(jnp.tile(a, (1, rep)) for a in (c, sa, sb))


def _pack_w_in(w_in):
    offs = np.cumsum((0,) + IN_SPLITS)
    part = lambda j: w_in[:, offs[j]:offs[j + 1]]
    zero = lambda n: jnp.zeros((D_MODEL, n), w_in.dtype)
    group = ATT_HEADS // ATT_KV_HEADS
    cols = []
    wq = part(0)
    for h in range(ATT_HEADS):
        wh = wq[:, h * HEAD_DIM:(h + 1) * HEAD_DIM]
        cols += [wh, zero(HEAD_DIM)] if h // group == 0 else [zero(HEAD_DIM), wh]
    wiq = part(3)
    for h in range(IDX_HEADS):
        cols += [wiq[:, h * IDX_DIM:(h + 1) * IDX_DIM], zero(LANES - IDX_DIM)]
    cols += [part(1), part(2), part(4), part(5), zero(LANES - IDX_DIM - IDX_HEADS)]
    cols += [part(j) for j in range(6, 12)]
    return jnp.concatenate(cols, axis=1).astype(BF16)


def _heads_major(a, db, t, width):
    heads = a.shape[1] // width
    return a.reshape(db, t, heads, width).transpose(0, 2, 1, 3).reshape(db, heads * t, width)


def _pad_rows(a, db, t, rows):
    return jnp.pad(a.reshape(db, t, -1), ((0, 0), (0, rows - t), (0, 0)))


def kernel(x_prompt, x_sample, cache_k, cache_v, cache_ik, state_ret, page_table, c_prompt, c_sample, norm_mix_g, norm_ffn_g, norm_final_g, w_ada, b_ada, w_in, w_branch_a, w_branch_b, w_out, w_router, b_router, w_gate_up, b_gate_up, w_down, b_down):
    batch, seq, _ = x_prompt.shape
    db, dt, _ = x_sample.shape
    assert w_in.shape[0] == 1, "one layer"
    tp, ts = batch * seq, db * dt
    xp = x_prompt.reshape(tp, D_MODEL)
    xs = x_sample.reshape(ts, D_MODEL)

    mod = _adaln(jnp.concatenate([c_prompt, c_sample], axis=0), w_ada[0], b_ada[0])
    mod_p = mod[:batch].reshape(batch, 1, 6 * D_MODEL)
    mod_s = jnp.repeat(mod[batch:], dt, axis=0).reshape(ts // TOK_BLOCK, TOK_BLOCK, 6 * D_MODEL)
    bpm_p = seq // TOK_BLOCK
    tmp = PROMPT_TOK_BLOCK
    bpm_big = seq // tmp

    w_packed = _pack_w_in(w_in[0].astype(BF16))
    pos_p = jnp.arange(seq)
    pos_s = PAST_LEN + (jnp.arange(TOK_BLOCK) % dt)
    g_mix = norm_mix_g[0].reshape(1, D_MODEL)
    outs_p = _inproj(xp, mod_p, g_mix, w_packed, _rope_tables(pos_p, ROPE_DIM, ROPE_THETA, HEAD_DIM),
                     _rope_tables(pos_p, RET_DK, RET_THETA, RET_DK), TOK_BLOCK, bpm_p, bpm_p)
    outs_s = _inproj(xs, mod_s, g_mix, w_packed, _rope_tables(pos_s, ROPE_DIM, ROPE_THETA, HEAD_DIM),
                     _rope_tables(pos_s, RET_DK, RET_THETA, RET_DK), TOK_BLOCK, 1, 1)
    (q_p, k_p, v_p, kb_p, vb_p, iq_p, ikw_p, ikb_p, rq_p, rk_p, rv_p, sg_p, sga_p, sgb_p) = outs_p
    (q_s, k_s, v_s, kb_s, vb_s, iq_s, ikw_s, ikb_s, rq_s, rk_s, rv_s, sg_s, sga_s, sgb_s) = outs_s

    tri = (jnp.arange(LANES)[:, None] < jnp.arange(LANES)[None, :]).astype(BF16)

    a_p = _dsa_prompt(q_p, iq_p, ikw_p, kb_p, vb_p, ikb_p, tri, batch, seq)
    group = ATT_HEADS // ATT_KV_HEADS
    q4 = q_s.reshape(db, dt, ATT_HEADS, LANES)
    qs = jnp.stack([q4[:, :, h, (h // group) * HEAD_DIM:(h // group + 1) * HEAD_DIM] for h in range(ATT_HEADS)],
                   axis=1).reshape(db, ATT_HEADS * dt, HEAD_DIM)
    iqs = _heads_major(iq_s, db, dt, LANES)[:, :, :IDX_DIM]
    ws = _heads_major(ikw_s[:, IDX_DIM:IDX_DIM + IDX_HEADS] * (IDX_HEADS ** -0.5), db, dt, 1)
    new_t = lambda a: jnp.pad(a.reshape(db, dt, ATT_KV_HEADS, HEAD_DIM).transpose(0, 2, 3, 1),
                              ((0, 0), (0, 0), (0, 0), (0, LANES - dt)))
    iknew_t = jnp.pad(ikb_s[:, :IDX_DIM].reshape(db, dt, IDX_DIM).transpose(0, 2, 1),
                      ((0, 0), (0, 0), (0, LANES - dt)))
    o_s = _dsa_sample(page_table, qs, iqs, ws, new_t(kb_s), new_t(vb_s), iknew_t,
                      cache_ik[0].transpose(0, 2, 1), cache_k[0].transpose(0, 2, 3, 1),
                      cache_v[0].transpose(0, 2, 3, 1), tri, dt)
    a_s = o_s.reshape(db, ATT_HEADS, dt, HEAD_DIM).transpose(0, 2, 1, 3).reshape(ts, ATT_OUT).astype(BF16)

    r_p, st_p = _retention(rq_p, rk_p, rv_p, sg_p, _retention_tables(float(RET_CHUNK)), None, batch,
                           seq // RET_CHUNK)
    pad = lambda a: _pad_rows(a, db, dt, RET_CHUNK).reshape(db * RET_CHUNK, -1)
    r_s, st_s = _retention(pad(rq_s), pad(rk_s), pad(rv_s), pad(sg_s), _retention_tables(float(dt)),
                           state_ret[0].reshape(db, RET_HEADS // 2, LANES, LANES), db, 1)
    r_s = r_s.reshape(db, RET_CHUNK, RET_OUT)[:, :dt].reshape(ts, RET_OUT)

    wr = jnp.pad(w_router[0], ((0, 0), (0, LANES - N_EXPERTS)))
    wr_hi = wr.astype(BF16)
    wr = jnp.stack([wr_hi, (wr - wr_hi.astype(F32)).astype(BF16)])
    br = jnp.concatenate([b_router[0], jnp.full((LANES - N_EXPERTS,), -jnp.inf, F32)]).reshape(1, LANES)
    post_w = (w_branch_a[0].astype(BF16), w_branch_b[0].astype(BF16), w_out[0].astype(BF16),
              norm_ffn_g[0].reshape(1, D_MODEL), wr, br)
    x1_p, h2w_p, idx_p, gate_p, rank_p, cnt_p = _post(a_p, r_p, sga_p, sgb_p, xp, mod_p, *post_w,
                                                      jnp.zeros((1, LANES), F32), tmp, bpm_big)
    x1_s, h2w_s, idx_s, gate_s, rank_s, cnt_all = _post(a_s, r_s, sga_s, sgb_s, xs, mod_s, *post_w, cnt_p,
                                                        TOK_BLOCK, 1)

    del cnt_all
    n_tok, lb = tp + ts, TOK_BLOCK
    nb = n_tok // lb
    idx4 = jnp.concatenate([idx_p[:TOP_K], idx_s[:TOP_K]], axis=1)
    rank4 = jnp.concatenate([rank_p[:TOP_K], rank_s[:TOP_K]], axis=1)
    gate8 = jnp.concatenate([gate_p, gate_s], axis=1)
    hit = idx4[None] == jnp.arange(N_EXPERTS, dtype=I32)[:, None, None]
    bc = jnp.sum(hit.reshape(N_EXPERTS, TOP_K, nb, lb).astype(I32), axis=(1, 3)).T
    run = (bc + SEG_ALIGN - 1) // SEG_ALIGN * SEG_ALIGN
    padded = (jnp.sum(run, axis=0) + MOE_ROWS - 1) // MOE_ROWS * MOE_ROWS
    pend = jnp.cumsum(padded)
    pstart = pend - padded
    loc = jnp.cumsum(run, axis=1) - run
    glob = pstart[None, :] + jnp.cumsum(run, axis=0) - run
    carry = jnp.cumsum(bc, axis=0) - bc
    n_blocks = -(-(n_tok * TOP_K + nb * N_EXPERTS * (SEG_ALIGN - 1)) // MOE_ROWS) + N_EXPERTS
    n_rows = n_blocks * MOE_ROWS
    used = jnp.sum(run, axis=1, keepdims=True)
    runs = jnp.stack([loc, glob, run], axis=-1)
    seg_table = lambda spare: jnp.concatenate(
        [runs, jnp.concatenate([used, spare, SEG_ROWS - used], axis=1)[:, None, :]], axis=1).reshape(-1).astype(I32)
    spare_out = n_rows + (jnp.arange(nb, dtype=I32)[:, None] % 2) * (SEG_ROWS // 2)
    seg_out, seg_in = seg_table(spare_out), seg_table(jnp.zeros((nb, 1), I32))
    shift = jnp.repeat(loc - carry, lb, axis=0).T
    lpos4 = rank4 + jnp.sum(jnp.where(hit, shift[:, None, :], 0), axis=0)
    lpos8 = jnp.pad(lpos4, ((0, 8 - TOP_K), (0, 0))).astype(I32)
    blk_start = jnp.arange(n_blocks, dtype=I32) * MOE_ROWS
    blk_expert = jnp.minimum(jnp.sum((blk_start[:, None] >= pend[None, :]).astype(I32), axis=1), N_EXPERTS - 1)
    blk_first = jnp.concatenate([jnp.ones((1,), I32), (blk_expert[1:] != blk_expert[:-1]).astype(I32)])
    n_used = (pend[-1] // MOE_ROWS).astype(I32).reshape(1)
    x_sorted = jnp.zeros((n_rows + SEG_ROWS, D_MODEL // 2), jnp.uint32)
    x_sorted = _dispatch(seg_out, lpos8, jnp.concatenate([h2w_p, h2w_s], axis=0), x_sorted)
    n_used_blocks = pend[-1] // MOE_ROWS
    blk_first = jnp.where(jnp.arange(n_blocks) < n_used_blocks, blk_first, 0).astype(I32)
    after = (pend[blk_expert] // MOE_ROWS).astype(I32)
    has_next = jnp.logical_and(blk_first == 1, after < n_used_blocks).astype(I32)
    next_expert = blk_expert[jnp.minimum(after, n_blocks - 1)].astype(I32)
    rows_out = _moe(blk_expert, blk_first, n_used, next_expert, has_next, x_sorted, w_gate_up[0], b_gate_up[0],
                    w_down[0], b_down[0])

    g_final = norm_final_g.reshape(1, D_MODEL)
    y_p = _final(seg_in, x1_p, mod_p, lpos8, gate8, rows_out, g_final, bpm_p, 0)
    y_s = _final(seg_in, x1_s, mod_s, lpos8, gate8, rows_out, g_final, 1, tp // lb)

    kv_shape = lambda b, s: (1, b, s, ATT_KV_HEADS, HEAD_DIM)
    st_shape = lambda b: (1, b, RET_HEADS, RET_DK, RET_DV)
    return (y_p.reshape(batch, seq, D_MODEL), y_s.reshape(db, dt, D_MODEL),
            k_p.reshape(kv_shape(batch, seq)), v_p.reshape(kv_shape(batch, seq)),
            ikw_p[:, :IDX_DIM].reshape(1, batch, seq, IDX_DIM), st_p.reshape(st_shape(batch)),
            k_s.reshape(kv_shape(db, dt)), v_s.reshape(kv_shape(db, dt)),
            ikw_s[:, :IDX_DIM].reshape(1, db, dt, IDX_DIM), st_s.reshape(st_shape(db)))
```
